```python
import math
import jax, jax.numpy as jnp
from jax import lax
import numpy as np

D_MODEL = 2048
BATCH = 8
SEQ = 2048
DEPTH = 2

A_HEADS = 8
A_HEAD_DIM = 64
A_WIDTH = A_HEADS * A_HEAD_DIM
A_PATTERNS = ((128, 1), (512, 4), (2048, 16))
REL_BUCKETS = 32
REL_MAX_DIST = 1024
B_HEADS = 8
B_NOPE = 64
B_ROPE = 32
B_V = 64
B_Q_LORA = 512
B_KV_LORA = 256
B_QBLOCK = 128
ROPE_THETA = 10000.0
C_CH = 512
C_KERNEL = 31
D_HEADS = 8
D_HEAD_DIM = 64
D_INNER = D_HEADS * D_HEAD_DIM
D_STATE = 128
D_GROUPS = 2
D_CONV = 5
D_CHUNK = 128
N_BRANCH = 4
BRANCH_W = 512
IN_SPLITS = (A_WIDTH, A_WIDTH, A_WIDTH,
             B_Q_LORA, B_KV_LORA, B_ROPE,
             2 * C_CH,
             D_INNER, D_INNER, D_GROUPS * D_STATE, D_GROUPS * D_STATE, 2 * D_HEADS)
IN_COLS = sum(IN_SPLITS)
N_GROUPS = 4
EXP_PER_GROUP = 8
N_EXPERTS = N_GROUPS * EXP_PER_GROUP
TOP_K = 2
D_FF = 512
MOE_BLOCK = 128
ALPHA = (2 * DEPTH) ** 0.25
BETA = (8 * DEPTH) ** -0.25
EPS = 1e-5
NEG_INF = -1e30

kernel_name = 'hybrid_gated_bidir_encoder'


def layernorm(x, g, b):
    xf = x.astype(jnp.float32)
    mu = jnp.mean(xf, axis=-1, keepdims=True)
    var = jnp.mean(jnp.square(xf - mu), axis=-1, keepdims=True)
    return ((xf - mu) * lax.rsqrt(var + EPS) * g + b).astype(x.dtype)


def rmsnorm(x, g):
    xf = x.astype(jnp.float32)
    return (xf * lax.rsqrt(jnp.mean(jnp.square(xf), axis=-1, keepdims=True) + EPS) * g).astype(x.dtype)


def depthwise_conv(x, w, b):
    width = w.shape[0]
    y = lax.conv_general_dilated(x, w[:, None, :].astype(x.dtype), (1,), ((width // 2, width // 2),),
                                 dimension_numbers=('NWC', 'WIO', 'NWC'), feature_group_count=x.shape[-1])
    return y + b


def t5_bucket(rel):
    half = REL_BUCKETS // 2
    max_exact = half // 2
    n = np.abs(rel)
    large = max_exact + (np.log(np.maximum(n, 1) / max_exact) / np.log(REL_MAX_DIST / max_exact)
                         * (half - max_exact)).astype(np.int32)
    large = np.minimum(large, half - 1)
    return (rel > 0).astype(np.int32) * half + np.where(n < max_exact, n, large)


def dilated_window_attention(q, k, v, rel_bias, band, dilation):
    Bsz, S, H, Dh = q.shape
    L = S // dilation
    nb = -(-L // band)
    Lp = nb * band

    def to_sub(t):
        return t.reshape(Bsz, L, dilation, H, Dh).transpose(0, 2, 1, 3, 4).reshape(Bsz * dilation, L, H, Dh)

    def banded(t):
        tp = jnp.pad(t, ((0, 0), (band, Lp - L + band), (0, 0), (0, 0))).reshape(-1, nb + 2, band, H, Dh)
        return jnp.concatenate([tp[:, :-2], tp[:, 1:-1], tp[:, 2:]], axis=2)

    qs, ks, vs = to_sub(q), to_sub(k), to_sub(v)
    qb = jnp.pad(qs, ((0, 0), (0, Lp - L), (0, 0), (0, 0))).reshape(-1, nb, band, H, Dh)
    kb, vb = banded(ks), banded(vs)
    qi = np.arange(band)[:, None]
    ki = np.arange(3 * band)[None, :]
    rel = ki - band - qi
    bias = jnp.transpose(rel_bias[t5_bucket(rel * dilation)], (2, 0, 1)).astype(jnp.float32)
    kpos = np.arange(nb)[:, None] * band + ki - band
    valid = (np.abs(rel) <= band)[None] & ((kpos >= 0) & (kpos < L))[:, None, :]
    s = jnp.einsum('znqhd,znkhd->znhqk', qb, kb).astype(jnp.float32) * (Dh ** -0.5) + bias
    s = jnp.where(valid[None, :, None], s, NEG_INF)
    m = jnp.max(s, axis=-1, keepdims=True)
    p = jnp.exp(s - m)
    den = jnp.sum(p, axis=-1)
    o = jnp.einsum('znhqk,znkhd->znqhd', p, vb) / jnp.swapaxes(den, 2, 3)[..., None]
    lse = jnp.swapaxes(m[..., 0] + jnp.log(den), 2, 3)
    o = o.reshape(Bsz, dilation, Lp, H, Dh)[:, :, :L].transpose(0, 2, 1, 3, 4).reshape(Bsz, S, H, Dh)
    lse = lse.reshape(Bsz, dilation, Lp, H)[:, :, :L].transpose(0, 2, 1, 3).reshape(Bsz, S, H)
    return o, lse


def dilated_mixture_attention(q, k, v, rel_bias):
    outs, lses = [], []
    for window, dilation in A_PATTERNS:
        o, lse = dilated_window_attention(q, k, v, rel_bias, window // (2 * dilation), dilation)
        outs.append(o)
        lses.append(lse)
    w = jax.nn.softmax(jnp.stack(lses, axis=0), axis=0)
    return sum(w[i][..., None] * outs[i] for i in range(len(outs)))


def rope(t, cos, sin):
    half = t.shape[-1] // 2
    t1, t2 = t[..., :half], t[..., half:]
    return jnp.concatenate([t1 * cos - t2 * sin, t2 * cos + t1 * sin], axis=-1).astype(t.dtype)


def mla(cq, ckv, kr, g_cq, w_uq, g_ckv, w_ukv):
    Bsz, S, _ = cq.shape
    inv_freq = ROPE_THETA ** (-jnp.arange(0, B_ROPE, 2, dtype=jnp.float32) / B_ROPE)
    ang = jnp.arange(S, dtype=jnp.float32)[:, None] * inv_freq[None]
    cos, sin = jnp.cos(ang), jnp.sin(ang)
    q = (rmsnorm(cq, g_cq) @ w_uq).reshape(Bsz, S, B_HEADS, B_NOPE + B_ROPE)
    q_nope = q[..., :B_NOPE]
    q_rope = rope(q[..., B_NOPE:], cos[:, None], sin[:, None])
    kv = (rmsnorm(ckv, g_ckv) @ w_ukv).reshape(Bsz, S, B_HEADS, B_NOPE + B_V)
    k_nope, v = kv[..., :B_NOPE], kv[..., B_NOPE:]
    k_rope = rope(kr, cos, sin)
    scale = (B_NOPE + B_ROPE) ** -0.5
    nqb = S // B_QBLOCK

    def to_blocks(t):
        return jnp.moveaxis(t.reshape(Bsz, nqb, B_QBLOCK, *t.shape[2:]), 1, 0)

    def attend(blk):
        qn, qr = blk
        s = jnp.einsum('bqhd,bkhd->bhqk', qn, k_nope) + jnp.einsum('bqhd,bkd->bhqk', qr, k_rope)
        p = jax.nn.softmax(s.astype(jnp.float32) * scale, axis=-1)
        return jnp.einsum('bhqk,bkhd->bqhd', p, v)

    o = lax.map(attend, (to_blocks(q_nope), to_blocks(q_rope)))
    return jnp.moveaxis(o, 0, 1).reshape(Bsz, S, B_HEADS * B_V)


def conformer_conv(u, w_dw, b_dw, ln_g, ln_b):
    a, gate = jnp.split(u, 2, axis=-1)
    h = a * jax.nn.sigmoid(gate)
    h = depthwise_conv(h, w_dw, b_dw)
    return jax.nn.silu(layernorm(h, ln_g, ln_b))


def ssd_scan(x, dt, A, Bm, Cm):
    Bsz, S, H, P = x.shape
    Q, G, N = D_CHUNK, D_GROUPS, D_STATE
    Hg = H // G
    nc = S // Q
    xc = x.reshape(Bsz, nc, Q, G, Hg, P)
    dtc = dt.reshape(Bsz, nc, Q, G, Hg)
    Bc = Bm.reshape(Bsz, nc, Q, G, N)
    Cc = Cm.reshape(Bsz, nc, Q, G, N)
    cs = jnp.cumsum(dtc * A.reshape(G, Hg), axis=2)
    csq = jnp.moveaxis(cs, 2, -1)
    tril = np.tril(np.ones((Q, Q), dtype=bool))
    decay = jnp.exp(jnp.where(tril, csq[..., :, None] - csq[..., None, :], -jnp.inf))
    cb = jnp.einsum('bcign,bcjgn->bcgij', Cc, Bc)
    xdt = xc * dtc[..., None]
    y_intra = jnp.einsum('bcghij,bcjghp->bcighp', cb[:, :, :, None] * decay, xdt)
    decay_end = jnp.exp(cs[:, :, -1:] - cs)
    states = jnp.einsum('bcjgn,bcjghp->bcghnp', Bc, xdt * decay_end[..., None])
    chunk_decay = jnp.exp(cs[:, :, -1])

    def step(h, inp):
        dec, st = inp
        return dec[..., None, None] * h + st, h

    h0 = jnp.zeros((Bsz, G, Hg, N, P), dtype=states.dtype)
    _, h_prev = lax.scan(step, h0, (jnp.moveaxis(chunk_decay, 1, 0), jnp.moveaxis(states, 1, 0)))
    h_prev = jnp.moveaxis(h_prev, 0, 1)
    y_inter = jnp.einsum('bcign,bcghnp->bcighp', Cc, h_prev) * jnp.exp(cs)[..., None]
    return (y_intra + y_inter).reshape(Bsz, S, H, P)


def ssd_mixer(z, xs, bm, cm, dt_raw, w_conv, b_conv, a_log_f, a_log_b, dt_bias_f, dt_bias_b, d_skip, g_norm):
    Bsz, S, _ = xs.shape
    xbc = jax.nn.silu(depthwise_conv(jnp.concatenate([xs, bm, cm], axis=-1), w_conv, b_conv))
    xs, bm, cm = jnp.split(xbc, [D_INNER, D_INNER + D_GROUPS * D_STATE], axis=-1)
    x = xs.reshape(Bsz, S, D_HEADS, D_HEAD_DIM)
    bm = bm.reshape(Bsz, S, D_GROUPS, D_STATE)
    cm = cm.reshape(Bsz, S, D_GROUPS, D_STATE)
    dtf = dt_raw.astype(jnp.float32)
    dt_f = jax.nn.softplus(dtf[..., :D_HEADS] + dt_bias_f)
    dt_b = jax.nn.softplus(dtf[..., D_HEADS:] + dt_bias_b)
    flip = lambda t: t[:, ::-1]
    y_f = ssd_scan(x, dt_f, -jnp.exp(a_log_f.astype(jnp.float32)), bm, cm)
    y_b = flip(ssd_scan(flip(x), flip(dt_b), -jnp.exp(a_log_b.astype(jnp.float32)), flip(bm), flip(cm)))
    y = (y_f + y_b + x * d_skip[:, None]).reshape(Bsz, S, D_INNER)
    return rmsnorm(y * jax.nn.silu(z), g_norm)


def hybrid_mixer(h, w_in, rel_bias, g_cq, w_uq, g_ckv, w_ukv, w_dw_c, b_dw_c, ln_c_g, ln_c_b,
                 w_conv_d, b_conv_d, a_log_f, a_log_b, dt_bias_f, dt_bias_b, d_skip, g_norm_d,
                 w_br, w_gate, b_gate, w_out):
    Bsz, S, _ = h.shape
    cuts = np.cumsum(IN_SPLITS)[:-1].tolist()
    qa, ka, va, cq, ckv, kr, glu, z, xs, bm, cm, dt = jnp.split(h @ w_in, cuts, axis=-1)
    heads = lambda t: t.reshape(Bsz, S, A_HEADS, A_HEAD_DIM)
    y_a = dilated_mixture_attention(heads(qa), heads(ka), heads(va), rel_bias).reshape(Bsz, S, A_WIDTH)
    y_b = mla(cq, ckv, kr, g_cq, w_uq, g_ckv, w_ukv)
    y_c = conformer_conv(glu, w_dw_c, b_dw_c, ln_c_g, ln_c_b)
    y_d = ssd_mixer(z, xs, bm, cm, dt, w_conv_d, b_conv_d, a_log_f, a_log_b, dt_bias_f, dt_bias_b,
                    d_skip, g_norm_d)
    branches = (y_a, y_b, y_c, y_d)
    merged = sum(jax.nn.sigmoid(h @ w_gate[i] + b_gate[i]) * (branches[i] @ w_br[i]) for i in range(N_BRANCH))
    return merged @ w_out


def hier_moe(h, w_rg, b_rg, w_re, b_re, w_e_gate, w_e_up, w_e_down):
    Bsz, S, D = h.shape
    T = Bsz * S
    A_ROWS = T * TOP_K
    xt = h.reshape(T, D)
    g_prob = jax.nn.softmax((xt @ w_rg + b_rg).astype(jnp.float32), axis=-1)
    g_w, g_idx = lax.top_k(g_prob, 1)
    e_logits = (xt @ w_re + b_re).astype(jnp.float32).reshape(T, N_GROUPS, EXP_PER_GROUP)
    e_in = jnp.take_along_axis(e_logits, g_idx[:, :, None], axis=1)[:, 0]
    e_w, e_loc = lax.top_k(jax.nn.softmax(e_in, axis=-1), TOP_K)
    wts = g_w * e_w / jnp.sum(e_w, axis=-1, keepdims=True)
    expert = g_idx * EXP_PER_GROUP + e_loc
    flat_e = expert.reshape(-1)
    flat_tok = jnp.repeat(jnp.arange(T, dtype=jnp.int32), TOP_K)
    order = jnp.argsort(flat_e)
    se, stok, sw = flat_e[order], flat_tok[order], wts.reshape(-1)[order]
    counts = jnp.bincount(flat_e, length=N_EXPERTS)
    padded = (counts + MOE_BLOCK - 1) // MOE_BLOCK * MOE_BLOCK
    starts = jnp.cumsum(counts) - counts
    pends = jnp.cumsum(padded)
    pstarts = pends - padded
    dest = pstarts[se] + (jnp.arange(A_ROWS, dtype=jnp.int32) - starts[se])
    n_rows = A_ROWS + N_EXPERTS * MOE_BLOCK
    n_blocks = n_rows // MOE_BLOCK
    row_tok = jnp.full((n_rows,), T, dtype=jnp.int32).at[dest].set(stok)
    blk_exp = jnp.minimum(jnp.searchsorted(pends, jnp.arange(n_blocks, dtype=jnp.int32) * MOE_BLOCK,
                                           side='right'), N_EXPERTS - 1)
    xrows = jnp.concatenate([xt, jnp.zeros((1, D), xt.dtype)], axis=0)[row_tok].reshape(n_blocks, MOE_BLOCK, D)

    def expert_block(args):
        xb, e = args
        return (jax.nn.silu(xb @ w_e_gate[e]) * (xb @ w_e_up[e])) @ w_e_down[e]

    yrows = lax.map(expert_block, (xrows, blk_exp)).reshape(n_rows, D)
    y = jax.ops.segment_sum(yrows[dest] * sw[:, None].astype(yrows.dtype), stok, num_segments=T)
    return y.reshape(Bsz, S, D)


def setup_inputs(seed: int = 0) -> dict:
    key = jax.random.key(seed)
    ks = iter(jax.random.split(key, 48))
    L, D = DEPTH, D_MODEL
    nrm = lambda shape, scale: jax.random.normal(next(ks), shape, jnp.float32) * scale
    gain = lambda shape: 1.0 + nrm(shape, 0.02)

    def dt_bias(shape):
        dt = jnp.exp(jax.random.uniform(next(ks), shape, jnp.float32, math.log(1e-3), math.log(1e-1)))
        return dt + jnp.log(-jnp.expm1(-dt))

    return {
        'x': nrm((BATCH, SEQ, D), 1.0),
        'ln_in_g': gain((D,)),
        'ln_in_b': nrm((D,), 0.02),
        'rel_bias': nrm((REL_BUCKETS, A_HEADS), 0.2),
        'w_in': nrm((L, D, IN_COLS), D ** -0.5),
        'g_cq': gain((L, B_Q_LORA)),
        'w_uq': nrm((L, B_Q_LORA, B_HEADS * (B_NOPE + B_ROPE)), B_Q_LORA ** -0.5),
        'g_ckv': gain((L, B_KV_LORA)),
        'w_ukv': nrm((L, B_KV_LORA, B_HEADS * (B_NOPE + B_V)), B_KV_LORA ** -0.5),
        'w_dw_c': nrm((L, C_KERNEL, C_CH), C_KERNEL ** -0.5),
        'b_dw_c': nrm((L, C_CH), 0.02),
        'ln_c_g': gain((L, C_CH)),
        'ln_c_b': nrm((L, C_CH), 0.02),
        'w_conv_d': nrm((L, D_CONV, D_INNER + 2 * D_GROUPS * D_STATE), D_CONV ** -0.5),
        'b_conv_d': nrm((L, D_INNER + 2 * D_GROUPS * D_STATE), 0.02),
        'a_log_f': jnp.log(jax.random.uniform(next(ks), (L, D_HEADS), jnp.float32, 1.0, 16.0)),
        'a_log_b': jnp.log(jax.random.uniform(next(ks), (L, D_HEADS), jnp.float32, 1.0, 16.0)),
        'dt_bias_f': dt_bias((L, D_HEADS)),
        'dt_bias_b': dt_bias((L, D_HEADS)),
        'd_skip': gain((L, D_HEADS)),
        'g_norm_d': gain((L, D_INNER)),
        'w_br': nrm((L, N_BRANCH, BRANCH_W, D), BETA * BRANCH_W ** -0.5),
        'w_gate': nrm((L, N_BRANCH, D, D), D ** -0.5),
        'b_gate': nrm((L, N_BRANCH, D), 0.02),
        'w_out': nrm((L, D, D), BETA * D ** -0.5),
        'ln1_g': gain((L, D)),
        'ln1_b': nrm((L, D), 0.02),
        'w_rg': nrm((L, D, N_GROUPS), D ** -0.5),
        'b_rg': nrm((L, N_GROUPS), 0.01),
        'w_re': nrm((L, D, N_EXPERTS), D ** -0.5),
        'b_re': nrm((L, N_EXPERTS), 0.01),
        'w_e_gate': nrm((L, N_EXPERTS, D, D_FF), D ** -0.5),
        'w_e_up': nrm((L, N_EXPERTS, D, D_FF), D ** -0.5),
        'w_e_down': nrm((L, N_EXPERTS, D_FF, D), BETA * D_FF ** -0.5),
        'ln2_g': gain((L, D)),
        'ln2_b': nrm((L, D), 0.02),
    }


def reference(x, ln_in_g, ln_in_b, rel_bias, w_in, g_cq, w_uq, g_ckv, w_ukv, w_dw_c, b_dw_c, ln_c_g, ln_c_b,
              w_conv_d, b_conv_d, a_log_f, a_log_b, dt_bias_f, dt_bias_b, d_skip, g_norm_d, w_br, w_gate, b_gate,
              w_out, ln1_g, ln1_b, w_rg, b_rg, w_re, b_re, w_e_gate, w_e_up, w_e_down, ln2_g, ln2_b):
    h = layernorm(x, ln_in_g, ln_in_b)
    for l in range(DEPTH):
        mix = hybrid_mixer(h, w_in[l], rel_bias, g_cq[l], w_uq[l], g_ckv[l], w_ukv[l], w_dw_c[l], b_dw_c[l],
                           ln_c_g[l], ln_c_b[l], w_conv_d[l], b_conv_d[l], a_log_f[l], a_log_b[l],
                           dt_bias_f[l], dt_bias_b[l], d_skip[l], g_norm_d[l], w_br[l], w_gate[l], b_gate[l],
                           w_out[l])
        h = layernorm(ALPHA * h + mix, ln1_g[l], ln1_b[l])
        moe = hier_moe(h, w_rg[l], b_rg[l], w_re[l], b_re[l], w_e_gate[l], w_e_up[l], w_e_down[l])
        h = layernorm(ALPHA * h + moe, ln2_g[l], ln2_b[l])
    return h
```

```python
import functools

import numpy as np
import jax
import jax.numpy as jnp
from jax import lax
from jax.experimental import pallas as pl
from jax.experimental.pallas import tpu as pltpu

F32 = jnp.float32
BF16 = jnp.bfloat16

D_MODEL = 2048
DEPTH = 2
A_HEADS = 8
A_HEAD_DIM = 64
A_WIDTH = A_HEADS * A_HEAD_DIM
A_PATTERNS = ((128, 1), (512, 4), (2048, 16))
A_BAND = 64
REL_BUCKETS = 32
REL_MAX_DIST = 1024
B_HEADS = 8
B_NOPE = 64
B_ROPE = 32
B_V = 64
B_Q_LORA = 512
B_KV_LORA = 256
ROPE_THETA = 10000.0
C_CH = 512
C_KERNEL = 31
D_HEADS = 8
D_HEAD_DIM = 64
D_INNER = D_HEADS * D_HEAD_DIM
D_STATE = 128
D_GROUPS = 2
D_CONV = 5
D_CHUNK = 128
N_BRANCH = 4
BRANCH_W = 512
N_GROUPS = 4
EXP_PER_GROUP = 8
N_EXPERTS = N_GROUPS * EXP_PER_GROUP
TOP_K = 2
D_FF = 512
MOE_BLOCK = 128
ALPHA = (2 * DEPTH) ** 0.25
EPS = 1e-5
NEG_INF = -1e30

LANES = 128
R_GLU, R_XBC, R_CQ, R_Z, R_CKV, R_KR, R_DT = 0, 1024, 2048, 2560, 3072, 3328, 3456
R_WIDTH = 3584
VMEM_LIMIT = 56 * 1024 * 1024


def _cp(*sem):
    return pltpu.CompilerParams(dimension_semantics=sem, vmem_limit_bytes=VMEM_LIMIT)


def _dot(a, b):
    return jnp.dot(a, b, preferred_element_type=F32)


def _dot_nt(a, b):
    return lax.dot_general(a, b, (((1,), (1,)), ((), ())), preferred_element_type=F32)


def _split3(x):
    hi = x.astype(BF16)
    r1 = x - hi.astype(F32)
    mid = r1.astype(BF16)
    lo = (r1 - mid.astype(F32)).astype(BF16)
    return hi, mid, lo


def _layernorm_rows(x, g, b):
    mu = jnp.mean(x, axis=-1, keepdims=True)
    xc = x - mu
    var = jnp.mean(xc * xc, axis=-1, keepdims=True)
    return xc * lax.rsqrt(var + EPS) * g + b


def _sigmoid(x):
    return 1.0 / (1.0 + jnp.exp(-x))


def _silu(x):
    return x * _sigmoid(x)


def _ln_kernel(x_ref, g_ref, b_ref, of_ref, ob_ref):
    y = _layernorm_rows(x_ref[...], g_ref[...], b_ref[...])
    of_ref[...] = y
    ob_ref[...] = y.astype(BF16)


def _layernorm(x, g, b, tm=256):
    T, D = x.shape
    return pl.pallas_call(
        _ln_kernel,
        grid=(T // tm,),
        in_specs=[pl.BlockSpec((tm, D), lambda i: (i, 0)),
                  pl.BlockSpec((1, D), lambda i: (0, 0)),
                  pl.BlockSpec((1, D), lambda i: (0, 0))],
        out_specs=[pl.BlockSpec((tm, D), lambda i: (i, 0)),
                   pl.BlockSpec((tm, D), lambda i: (i, 0))],
        out_shape=[jax.ShapeDtypeStruct((T, D), F32), jax.ShapeDtypeStruct((T, D), BF16)],
        compiler_params=_cp("parallel"),
        name="ln_in",
    )(x, g.reshape(1, D), b.reshape(1, D))


def _mm_kernel(x_ref, w_ref, o_ref):
    o_ref[...] = _dot(x_ref[...], w_ref[...]).astype(o_ref.dtype)


def _matmul(x, w, out_dtype, tm, tn, name):
    M, K = x.shape
    N = w.shape[1]
    return pl.pallas_call(
        _mm_kernel,
        grid=(N // tn, M // tm),
        in_specs=[pl.BlockSpec((tm, K), lambda j, i: (i, 0)),
                  pl.BlockSpec((K, tn), lambda j, i: (0, j))],
        out_specs=pl.BlockSpec((tm, tn), lambda j, i: (i, j)),
        out_shape=jax.ShapeDtypeStruct((M, N), out_dtype),
        compiler_params=_cp("parallel", "parallel"),
        name=name,
    )(x, w)


def _t5_bucket(rel):
    half = REL_BUCKETS // 2
    max_exact = half // 2
    n = np.abs(rel)
    large = max_exact + (np.log(np.maximum(n, 1) / max_exact) / np.log(REL_MAX_DIST / max_exact)
                         * (half - max_exact)).astype(np.int32)
    large = np.minimum(large, half - 1)
    return (rel > 0).astype(np.int32) * half + np.where(n < max_exact, n, large)


def _a_window(L):
    return min(2 * LANES, L)


def _a_bias_tiles(rel_bias, d, L):
    W = _a_window(L)
    offs = (0,) if L == LANES else (0, -A_BAND, -2 * A_BAND)
    qi = np.arange(LANES)[:, None]
    kj = np.arange(W)[None, :]
    tiles = []
    for off in offs:
        rel = kj - qi + off
        valid = np.abs(rel) <= A_BAND
        b = jnp.transpose(rel_bias[_t5_bucket(rel * d)], (2, 0, 1)).astype(F32)
        tiles.append(jnp.where(valid[None], b, NEG_INF))
    return jnp.stack(tiles, axis=0)


def _attn_a_kernel(q_ref, k_ref, v_ref, bias_ref, *rest, L, nqb, final):
    W = _a_window(L)
    i = pl.program_id(2)
    if nqb == 1:
        ws = 0
    else:
        ws = pl.multiple_of(jnp.clip(i * LANES - A_BAND, 0, L - W), A_BAND)
    q = q_ref[...]
    kw = k_ref[pl.ds(ws, W), :]
    vw = v_ref[pl.ds(ws, W), :]
    lane = lax.broadcasted_iota(jnp.int32, (LANES, LANES), 1)
    lse_tile = jnp.zeros((LANES, LANES), F32)
    outs = []
    for h in range(A_HEADS):
        sl = slice(h * A_HEAD_DIM, (h + 1) * A_HEAD_DIM)
        s = _dot_nt(q[:, sl], kw[:, sl]) * (A_HEAD_DIM ** -0.5) + bias_ref[h]
        m = jnp.max(s, axis=-1, keepdims=True)
        p = jnp.exp(s - m)
        den = jnp.sum(p, axis=-1, keepdims=True)
        o = _dot(p.astype(BF16), vw[:, sl]) / den
        lse = m + jnp.log(den)
        outs.append(o)
        lse_tile = jnp.where(lane == h, lse, lse_tile)
    if not final:
        o_ref, lse_ref = rest
        for h in range(A_HEADS):
            o_ref[:, h * A_HEAD_DIM:(h + 1) * A_HEAD_DIM] = outs[h]
        lse_ref[...] = lse_tile
    else:
        o1_ref, l1_ref, o2_ref, l2_ref, y_ref = rest
        l1 = l1_ref[...]
        l2 = l2_ref[...]
        mx = jnp.maximum(jnp.maximum(l1, l2), lse_tile)
        e1 = jnp.exp(l1 - mx)
        e2 = jnp.exp(l2 - mx)
        e3 = jnp.exp(lse_tile - mx)
        tot = e1 + e2 + e3
        w1 = e1 / tot
        w2 = e2 / tot
        w3 = e3 / tot
        for h in range(A_HEADS):
            sl = slice(h * A_HEAD_DIM, (h + 1) * A_HEAD_DIM)
            y = (w1[:, h:h + 1] * o1_ref[:, sl] + w2[:, h:h + 1] * o2_ref[:, sl]
                 + w3[:, h:h + 1] * outs[h])
            y_ref[:, sl] = y.astype(y_ref.dtype)


def _attn_a_call(qkv, bias, B, S, d, prev=None):
    L = S // d
    nqb = L // LANES
    W = _a_window(L)
    nvar = bias.shape[0]
    qkv_v = qkv.reshape(B, L, d * 3 * A_WIDTH)

    def var_of(i):
        if nvar == 1:
            return 0
        return jnp.where(i == 0, 0, jnp.where(i == nqb - 1, 2, 1))

    in_specs = [
        pl.BlockSpec((None, LANES, A_WIDTH), lambda b, r, i: (b, i, 3 * r)),
        pl.BlockSpec((None, L, A_WIDTH), lambda b, r, i: (b, 0, 3 * r + 1)),
        pl.BlockSpec((None, L, A_WIDTH), lambda b, r, i: (b, 0, 3 * r + 2)),
        pl.BlockSpec((None, A_HEADS, LANES, W), lambda b, r, i: (var_of(i), 0, 0, 0)),
    ]
    args = [qkv_v, qkv_v, qkv_v, bias]
    final = prev is not None
    if final:
        assert d == 1
        for o_p, l_p in ((prev[0], prev[1]), (prev[2], prev[3])):
            in_specs.append(pl.BlockSpec((None, LANES, A_WIDTH), lambda b, r, i: (b, i, 0)))
            in_specs.append(pl.BlockSpec((None, LANES, LANES), lambda b, r, i: (b, i, 0)))
            args += [o_p.reshape(B, S, A_WIDTH), l_p.reshape(B, S, LANES)]
        out_specs = pl.BlockSpec((None, LANES, A_WIDTH), lambda b, r, i: (b, i, 0))
        out_shape = jax.ShapeDtypeStruct((B, S, A_WIDTH), BF16)
    else:
        out_specs = [pl.BlockSpec((None, LANES, A_WIDTH), lambda b, r, i: (b, i, r)),
                     pl.BlockSpec((None, LANES, LANES), lambda b, r, i: (b, i, r))]
        out_shape = [jax.ShapeDtypeStruct((B, L, d * A_WIDTH), F32),
                     jax.ShapeDtypeStruct((B, L, d * LANES), F32)]
    out = pl.pallas_call(
        functools.partial(_attn_a_kernel, L=L, nqb=nqb, final=final),
        grid=(B, d, nqb),
        in_specs=in_specs,
        out_specs=out_specs,
        out_shape=out_shape,
        compiler_params=_cp("parallel", "parallel", "arbitrary"),
        name=f"attn_a_d{d}",
    )(*args)
    if final:
        return out.reshape(B * S, A_WIDTH)
    return out[0].reshape(B * S, A_WIDTH), out[1].reshape(B * S, LANES)


def _mixer_a(qkv, rel_bias, B, S):
    (_, d1), (_, d4), (_, d16) = A_PATTERNS
    o16, l16 = _attn_a_call(qkv, _a_bias_tiles(rel_bias, d16, S // d16), B, S, d16)
    o4, l4 = _attn_a_call(qkv, _a_bias_tiles(rel_bias, d4, S // d4), B, S, d4)
    return _attn_a_call(qkv, _a_bias_tiles(rel_bias, d1, S // d1), B, S, d1, prev=(o16, l16, o4, l4))


def _mla_proj_kernel(cq_ref, ckv_ref, kr_ref, gq_ref, gkv_ref, wqm_ref, wqs_ref, wk_ref, wv_ref,
                     ek_ref, cosq_ref, sinq_ref, csk_ref, q_ref, k_ref, v_ref):
    cq = cq_ref[...]
    xq = (cq * lax.rsqrt(jnp.mean(cq * cq, axis=-1, keepdims=True) + EPS) * gq_ref[...]).astype(BF16)
    ckv = ckv_ref[...]
    xkv = (ckv * lax.rsqrt(jnp.mean(ckv * ckv, axis=-1, keepdims=True) + EPS) * gkv_ref[...]).astype(BF16)
    qm = _dot(xq, wqm_ref[...])
    qs = _dot(xq, wqs_ref[...])
    cosq = cosq_ref[...]
    sinq = sinq_ref[...]
    t = kr_ref[...] * csk_ref[...]
    t_hi = t.astype(BF16)
    t_lo = (t - t_hi.astype(F32)).astype(BF16)
    kk = _dot(xkv, wk_ref[...]) + _dot(t_hi, ek_ref[...]) + _dot(t_lo, ek_ref[...])
    for h in range(B_HEADS):
        sl = slice(h * LANES, (h + 1) * LANES)
        q_ref[:, sl] = (qm[:, sl] * cosq + qs[:, sl] * sinq).astype(BF16)
    k_ref[...] = kk.astype(BF16)
    v_ref[...] = _dot(xkv, wv_ref[...]).astype(BF16)


def _mla_attn_kernel(q_ref, k_ref, v_ref, o_ref):
    scale = (B_NOPE + B_ROPE) ** -0.5
    v = v_ref[...]
    accs = []
    for hh in range(2):
        sl = slice(hh * LANES, (hh + 1) * LANES)
        s = _dot_nt(q_ref[:, sl], k_ref[:, sl]) * scale
        m = jnp.max(s, axis=-1, keepdims=True)
        p = jnp.exp(s - m)
        den = jnp.sum(p, axis=-1, keepdims=True)
        accs.append(_dot(p.astype(BF16), v) / den)
    lane = lax.broadcasted_iota(jnp.int32, accs[0].shape, 1)
    o_ref[...] = jnp.where(lane < B_V, accs[0], accs[1]).astype(o_ref.dtype)


def _mla_tables(S):
    inv_freq = ROPE_THETA ** (-jnp.arange(0, B_ROPE, 2, dtype=F32) / B_ROPE)
    ang = jnp.arange(S, dtype=F32)[:, None] * inv_freq[None]
    cos, sin = jnp.cos(ang), jnp.sin(ang)
    cos2 = jnp.concatenate([cos, cos], axis=-1)
    sin2 = jnp.concatenate([sin, sin], axis=-1)
    ones = jnp.ones((S, B_NOPE), F32)
    zn = jnp.zeros((S, B_NOPE), F32)
    zp = jnp.zeros((S, LANES - B_NOPE - B_ROPE), F32)
    cosq = jnp.concatenate([ones, cos2, zp], axis=-1)
    sinq = jnp.concatenate([zn, sin2, zp], axis=-1)
    csk = jnp.concatenate([cos2, sin2, jnp.zeros((S, LANES - 2 * B_ROPE), F32)], axis=-1)
    return cosq, sinq, csk


def _swap_cols(w):
    half = w.shape[-1] // 2
    return jnp.concatenate([-w[..., half:], w[..., :half]], axis=-1)


def _mla_weights(w_uq, w_ukv):
    dq = B_NOPE + B_ROPE
    wq = w_uq.reshape(B_Q_LORA, B_HEADS, dq)
    zpad = jnp.zeros((B_Q_LORA, B_HEADS, LANES - dq), F32)
    wqm = jnp.concatenate([wq, zpad], axis=-1).reshape(B_Q_LORA, B_HEADS * LANES)
    wqs = jnp.concatenate([jnp.zeros((B_Q_LORA, B_HEADS, B_NOPE), F32), _swap_cols(wq[..., B_NOPE:]), zpad],
                          axis=-1).reshape(B_Q_LORA, B_HEADS * LANES)
    wkv = w_ukv.reshape(B_KV_LORA, B_HEADS, B_NOPE + B_V)
    wk = jnp.concatenate([wkv[..., :B_NOPE], jnp.zeros((B_KV_LORA, B_HEADS, LANES - B_NOPE), F32)],
                         axis=-1).reshape(B_KV_LORA, B_HEADS * LANES)
    wv = wkv[..., B_NOPE:].reshape(B_KV_LORA, B_HEADS * B_V)
    ek = np.zeros((LANES, B_HEADS, LANES), np.float32)
    for j in range(B_ROPE):
        ek[j, :, B_NOPE + j] = 1.0
        ek[B_ROPE + j, :, B_NOPE + j] = 1.0
    ek = jnp.asarray(ek.reshape(LANES, B_HEADS * LANES))
    return wqm.astype(BF16), wqs.astype(BF16), wk.astype(BF16), wv.astype(BF16), ek.astype(BF16)


def _mixer_b(rest, g_cq, w_uq, g_ckv, w_ukv, B, S, tm=512, tq=256):
    T = B * S
    wqm, wqs, wk, wv, ek = _mla_weights(w_uq, w_ukv)
    cosq, sinq, csk = _mla_tables(S)
    nst = S // tm
    QW = B_HEADS * LANES
    const = lambda i: (0, 0)
    pos = lambda i: (i % nst, 0)
    q, k, v = pl.pallas_call(
        _mla_proj_kernel,
        grid=(T // tm,),
        in_specs=[pl.BlockSpec((tm, B_Q_LORA), lambda i: (i, R_CQ // B_Q_LORA)),
                  pl.BlockSpec((tm, B_KV_LORA), lambda i: (i, R_CKV // B_KV_LORA)),
                  pl.BlockSpec((tm, LANES), lambda i: (i, R_KR // LANES)),
                  pl.BlockSpec((1, B_Q_LORA), const),
                  pl.BlockSpec((1, B_KV_LORA), const),
                  pl.BlockSpec((B_Q_LORA, QW), const),
                  pl.BlockSpec((B_Q_LORA, QW), const),
                  pl.BlockSpec((B_KV_LORA, QW), const),
                  pl.BlockSpec((B_KV_LORA, B_HEADS * B_V), const),
                  pl.BlockSpec((LANES, QW), const),
                  pl.BlockSpec((tm, LANES), pos),
                  pl.BlockSpec((tm, LANES), pos),
                  pl.BlockSpec((tm, LANES), pos)],
        out_specs=[pl.BlockSpec((tm, QW), lambda i: (i, 0)),
                   pl.BlockSpec((tm, QW), lambda i: (i, 0)),
                   pl.BlockSpec((tm, B_HEADS * B_V), lambda i: (i, 0))],
        out_shape=[jax.ShapeDtypeStruct((T, QW), BF16), jax.ShapeDtypeStruct((T, QW), BF16),
                   jax.ShapeDtypeStruct((T, B_HEADS * B_V), BF16)],
        compiler_params=_cp("parallel"),
        name="mla_proj",
    )(rest, rest, rest, g_cq.reshape(1, -1), g_ckv.reshape(1, -1), wqm, wqs, wk, wv, ek, cosq, sinq, csk)
    y = pl.pallas_call(
        _mla_attn_kernel,
        grid=(B, B_HEADS // 2, S // tq),
        in_specs=[pl.BlockSpec((None, tq, 2 * LANES), lambda b, hp, i: (b, i, hp)),
                  pl.BlockSpec((None, S, 2 * LANES), lambda b, hp, i: (b, 0, hp)),
                  pl.BlockSpec((None, S, 2 * B_V), lambda b, hp, i: (b, 0, hp))],
        out_specs=pl.BlockSpec((None, tq, 2 * B_V), lambda b, hp, i: (b, i, hp)),
        out_shape=jax.ShapeDtypeStruct((B, S, B_HEADS * B_V), BF16),
        compiler_params=_cp("parallel", "parallel", "arbitrary"),
        name="mla_attn",
    )(q.reshape(B, S, QW), k.reshape(B, S, QW), v.reshape(B, S, B_HEADS * B_V))
    return y.reshape(T, B_HEADS * B_V)


C_PAD = 16
C_ROWS = 128


def _depthwise_taps(win_ref, w_ref, bias, ls, first, ntaps, rows):
    SUB = 8
    acc = jnp.broadcast_to(bias, (rows, LANES))
    span = rows + ((first + ntaps - 1) // SUB) * SUB
    for ph in range(SUB):
        taps = [j for j in range(ntaps) if (first + j) % SUB == ph]
        if not taps:
            continue
        xb = win_ref[ph:ph + span, ls]
        for j in taps:
            a = (first + j) // SUB * SUB
            acc = acc + w_ref[j:j + 1, ls] * xb[a:a + rows]
    return acc


def _fill_window(win_ref, load_rows, r0, i, nblk, rows, pad):
    width = win_ref.shape[1]
    win_ref[pad:pad + rows, :] = load_rows(r0, rows)

    @pl.when(i > 0)
    def _():
        win_ref[0:pad, :] = load_rows(r0 - pad, pad)

    @pl.when(i == 0)
    def _():
        win_ref[0:pad, :] = jnp.zeros((pad, width), F32)

    @pl.when(i < nblk - 1)
    def _():
        win_ref[pad + rows:pad + rows + pad, :] = load_rows(r0 + rows, pad)

    @pl.when(i == nblk - 1)
    def _():
        win_ref[pad + rows:pad + rows + pad, :] = jnp.zeros((pad, width), F32)


def _conv_c_kernel(glu_ref, w_ref, b_ref, g_ref, beta_ref, o_ref, win_ref, acc_ref, *, S):
    i = pl.program_id(1)
    r0 = pl.multiple_of(i * C_ROWS, C_ROWS)

    def glu_rows(start, n):
        rs = pl.ds(pl.multiple_of(start, C_PAD), n)
        return glu_ref[rs, 0:C_CH] * _sigmoid(glu_ref[rs, C_CH:2 * C_CH])

    _fill_window(win_ref, glu_rows, r0, i, S // C_ROWS, C_ROWS, C_PAD)
    first = C_PAD - C_KERNEL // 2
    for lb in range(C_CH // LANES):
        ls = slice(lb * LANES, (lb + 1) * LANES)
        acc_ref[:, ls] = _depthwise_taps(win_ref, w_ref, b_ref[:, ls], ls, first, C_KERNEL, C_ROWS)
    y = _layernorm_rows(acc_ref[...], g_ref[...], beta_ref[...])
    o_ref[...] = _silu(y).astype(o_ref.dtype)


def _mixer_c(rest, w_dw, b_dw, ln_g, ln_b, B, S):
    T = B * S
    const = lambda b, i: (0, 0)
    y = pl.pallas_call(
        functools.partial(_conv_c_kernel, S=S),
        grid=(B, S // C_ROWS),
        in_specs=[pl.BlockSpec((None, S, 2 * C_CH), lambda b, i: (b, 0, R_GLU // (2 * C_CH))),
                  pl.BlockSpec((C_KERNEL, C_CH), const),
                  pl.BlockSpec((1, C_CH), const),
                  pl.BlockSpec((1, C_CH), const),
                  pl.BlockSpec((1, C_CH), const)],
        out_specs=pl.BlockSpec((None, C_ROWS, C_CH), lambda b, i: (b, i, 0)),
        out_shape=jax.ShapeDtypeStruct((B, S, C_CH), BF16),
        scratch_shapes=[pltpu.VMEM((C_ROWS + 2 * C_PAD, C_CH), F32),
                        pltpu.VMEM((C_ROWS, C_CH), F32)],
        compiler_params=_cp("parallel", "parallel"),
        name="conformer_conv",
    )(rest.reshape(B, S, R_WIDTH), w_dw, b_dw.reshape(1, -1), ln_g.reshape(1, -1), ln_b.reshape(1, -1))
    return y.reshape(T, C_CH)


D_PAD = 8
XBC_W = D_INNER + 2 * D_GROUPS * D_STATE
N_PAIR = D_HEADS // 2


def _pair_expand(v, first):
    lane = lax.broadcasted_iota(jnp.int32, (v.shape[0], LANES), 1)
    lo = jnp.broadcast_to(v[:, first:first + 1], (v.shape[0], LANES))
    hi = jnp.broadcast_to(v[:, first + 1:first + 2], (v.shape[0], LANES))
    return jnp.where(lane < D_HEAD_DIM, lo, hi)


def _ssd_kernel(xbc_ref, z_ref, dt_ref, wc_ref, bc_ref, alog_ref, dtb_ref, dskip_ref, gn_ref, o_ref,
                win_ref, xc_ref, a_ref, dtv_ref, y_ref, st_ref, *, S):
    Q = D_CHUNK
    nchunk = S // Q
    N = D_STATE
    bm0 = D_INNER
    cm0 = D_INNER + D_GROUPS * N

    def conv_body(c, carry):
        r0 = pl.multiple_of(c * Q, Q)
        _fill_window(win_ref, lambda st, n: xbc_ref[pl.ds(pl.multiple_of(st, D_PAD), n), :], r0, c, nchunk, Q, D_PAD)
        for lb in range(XBC_W // LANES):
            ls = slice(lb * LANES, (lb + 1) * LANES)
            acc = _depthwise_taps(win_ref, wc_ref, bc_ref[:, ls], ls, D_PAD - D_CONV // 2, D_CONV, Q)
            xc_ref[pl.ds(r0, Q), ls] = _silu(acc)
        return carry

    lax.fori_loop(0, nchunk, conv_body, 0)

    lane1 = lax.broadcasted_iota(jnp.int32, (1, LANES), 1)
    a_row = jnp.where(lane1 < 2 * D_HEADS, -jnp.exp(alog_ref[...]), 0.0)
    xdt = dt_ref[...] + dtb_ref[...]
    dtv = jnp.maximum(xdt, 0.0) + jnp.log(1.0 + jnp.exp(-jnp.abs(xdt)))
    dtv_ref[...] = dtv
    a_ref[...] = dtv * a_row

    row = lax.broadcasted_iota(jnp.int32, (Q, Q), 0)
    col = lax.broadcasted_iota(jnp.int32, (Q, Q), 1)
    tril = row >= col
    triu = col >= row
    lane = col

    def scan_chunk(c, lower, off, finalize):
        r0 = pl.multiple_of(c * Q, Q)
        rows = pl.ds(r0, Q)
        mask = tril if lower else triu
        tri = mask.astype(BF16)
        a_hi, a_mid, a_lo = _split3(a_ref[rows, :])
        cs = _dot(tri, a_hi) + _dot(tri, a_mid) + _dot(tri, a_lo)
        cs_t = cs.T
        ecs = jnp.exp(cs)
        edge = Q - 1 if lower else 0
        edec = jnp.exp(cs[edge:edge + 1, :] - cs)
        dt_c = dtv_ref[rows, :]
        for g in range(D_GROUPS):
            bg = xc_ref[rows, bm0 + g * N:bm0 + (g + 1) * N]
            cg = xc_ref[rows, cm0 + g * N:cm0 + (g + 1) * N].astype(BF16)
            cb = _dot_nt(cg, bg.astype(BF16))
            bg_t = bg.T.astype(BF16)
            for pp in range(N_PAIR // D_GROUPS):
                p = g * (N_PAIR // D_GROUPS) + pp
                ps = slice(p * LANES, (p + 1) * LANES)
                xdt_p = xc_ref[rows, ps] * _pair_expand(dt_c, off + 2 * p)
                ms = []
                for hh in range(2):
                    k = off + 2 * p + hh
                    diff = jnp.broadcast_to(cs[:, k:k + 1], (Q, Q)) - cs_t[k:k + 1, :]
                    ms.append((jnp.exp(jnp.where(mask, diff, NEG_INF)) * cb).astype(BF16))
                x_lo = jnp.where(lane < D_HEAD_DIM, xdt_p, 0.0).astype(BF16)
                x_hi = jnp.where(lane >= D_HEAD_DIM, xdt_p, 0.0).astype(BF16)
                y_intra = _dot(jnp.concatenate(ms, axis=1), jnp.concatenate([x_lo, x_hi], axis=0))
                hp = st_ref[p]
                ecs_p = _pair_expand(ecs, off + 2 * p)
                y_new = y_intra + _dot(cg, hp.astype(BF16)) * ecs_p
                if lower:
                    y_ref[rows, ps] = y_new
                else:
                    y_ref[rows, ps] = y_ref[rows, ps] + y_new
                xs_p = (xdt_p * _pair_expand(edec, off + 2 * p)).astype(BF16)
                st_ref[p] = hp * ecs_p[edge:edge + 1, :] + _dot(bg_t, xs_p)
        if finalize:
            y = y_ref[rows, :] + xc_ref[rows, 0:D_INNER] * dskip_ref[...]
            gated = y * _silu(z_ref[rows, :])
            out = gated * lax.rsqrt(jnp.mean(gated * gated, axis=-1, keepdims=True) + EPS) * gn_ref[...]
            o_ref[rows, :] = out.astype(o_ref.dtype)

    st_ref[...] = jnp.zeros(st_ref.shape, F32)

    def fwd_body(c, carry):
        scan_chunk(c, True, 0, False)
        return carry

    lax.fori_loop(0, nchunk, fwd_body, 0)
    st_ref[...] = jnp.zeros(st_ref.shape, F32)

    def bwd_body(k, carry):
        scan_chunk(nchunk - 1 - k, False, D_HEADS, True)
        return carry

    lax.fori_loop(0, nchunk, bwd_body, 0)


def _mixer_d(rest, w_conv, b_conv, a_log_f, a_log_b, dt_bias_f, dt_bias_b, d_skip, g_norm, B, S):
    T = B * S
    pad16 = lambda f, b: jnp.concatenate([f, b, jnp.zeros((LANES - 2 * D_HEADS,), F32)]).reshape(1, LANES)
    const = lambda b: (0, 0)
    y = pl.pallas_call(
        functools.partial(_ssd_kernel, S=S),
        grid=(B,),
        in_specs=[pl.BlockSpec((None, S, XBC_W), lambda b: (b, 0, R_XBC // XBC_W)),
                  pl.BlockSpec((None, S, D_INNER), lambda b: (b, 0, R_Z // D_INNER)),
                  pl.BlockSpec((None, S, LANES), lambda b: (b, 0, R_DT // LANES)),
                  pl.BlockSpec((D_CONV, XBC_W), const),
                  pl.BlockSpec((1, XBC_W), const),
                  pl.BlockSpec((1, LANES), const),
                  pl.BlockSpec((1, LANES), const),
                  pl.BlockSpec((1, D_INNER), const),
                  pl.BlockSpec((1, D_INNER), const)],
        out_specs=pl.BlockSpec((None, S, D_INNER), lambda b: (b, 0, 0)),
        out_shape=jax.ShapeDtypeStruct((B, S, D_INNER), BF16),
        scratch_shapes=[pltpu.VMEM((D_CHUNK + 2 * D_PAD, XBC_W), F32),
                        pltpu.VMEM((S, XBC_W), F32),
                        pltpu.VMEM((S, LANES), F32),
                        pltpu.VMEM((S, LANES), F32),
                        pltpu.VMEM((S, D_INNER), F32),
                        pltpu.VMEM((N_PAIR, D_STATE, LANES), F32)],
        compiler_params=_cp("parallel"),
        name="ssd_mixer",
    )(rest.reshape(B, S, R_WIDTH), rest.reshape(B, S, R_WIDTH), rest.reshape(B, S, R_WIDTH),
      w_conv, b_conv.reshape(1, -1), pad16(a_log_f, a_log_b), pad16(dt_bias_f, dt_bias_b),
      jnp.repeat(d_skip, D_HEAD_DIM).reshape(1, -1), g_norm.reshape(1, -1))
    return y.reshape(T, D_INNER)


def _in_proj_weights(w_in_l):
    o = np.cumsum((0, A_WIDTH, A_WIDTH, A_WIDTH, B_Q_LORA, B_KV_LORA, B_ROPE, 2 * C_CH,
                   D_INNER, D_INNER, D_GROUPS * D_STATE, D_GROUPS * D_STATE, 2 * D_HEADS)).tolist()
    seg = lambda n: w_in_l[:, o[n]:o[n + 1]]
    w_a = w_in_l[:, :o[3]]
    cq, ckv, kr, glu, z, xs, bm, cm, dt = (seg(n) for n in range(3, 12))
    zeros = lambda n: jnp.zeros((w_in_l.shape[0], n), w_in_l.dtype)
    w_r = jnp.concatenate([glu, xs, bm, cm, cq, z, ckv,
                           kr, _swap_cols(kr), zeros(LANES - 2 * B_ROPE),
                           dt, zeros(LANES - 2 * D_HEADS)], axis=-1)
    assert w_r.shape[1] == R_WIDTH
    return w_a, w_r


def _merge_gate_kernel(h_ref, ya_ref, yb_ref, yc_ref, yd_ref, wg_ref, bg_ref, wbr_ref, o_ref):
    h = h_ref[...]
    acc = None
    for i, y_ref in enumerate((ya_ref, yb_ref, yc_ref, yd_ref)):
        gate = _sigmoid(_dot(h, wg_ref[i]) + bg_ref[i])
        term = gate * _dot(y_ref[...], wbr_ref[i])
        acc = term if acc is None else acc + term
    o_ref[...] = acc.astype(o_ref.dtype)


def _out_ln_kernel(m_ref, w_ref, h_ref, g_ref, b_ref, of_ref, ob_ref):
    y = _layernorm_rows(ALPHA * h_ref[...] + _dot(m_ref[...], w_ref[...]), g_ref[...], b_ref[...])
    of_ref[...] = y
    ob_ref[...] = y.astype(BF16)


def _merge(hb, hf, branches, w_gate, b_gate, w_br, w_out, ln_g, ln_b, tm=512, tn=512, tm2=256):
    T, D = hb.shape
    ybs = pl.BlockSpec((tm, BRANCH_W), lambda j, i: (i, 0))
    merged = pl.pallas_call(
        _merge_gate_kernel,
        grid=(D // tn, T // tm),
        in_specs=[pl.BlockSpec((tm, D), lambda j, i: (i, 0)), ybs, ybs, ybs, ybs,
                  pl.BlockSpec((N_BRANCH, D, tn), lambda j, i: (0, 0, j)),
                  pl.BlockSpec((N_BRANCH, 1, tn), lambda j, i: (0, 0, j)),
                  pl.BlockSpec((N_BRANCH, BRANCH_W, tn), lambda j, i: (0, 0, j))],
        out_specs=pl.BlockSpec((tm, tn), lambda j, i: (i, j)),
        out_shape=jax.ShapeDtypeStruct((T, D), BF16),
        compiler_params=_cp("parallel", "parallel"),
        name="merge_gate",
    )(hb, *branches, w_gate.astype(BF16), b_gate.reshape(N_BRANCH, 1, D), w_br.astype(BF16))
    const = lambda i: (0, 0)
    rows = lambda i: (i, 0)
    return pl.pallas_call(
        _out_ln_kernel,
        grid=(T // tm2,),
        in_specs=[pl.BlockSpec((tm2, D), rows), pl.BlockSpec((D, D), const), pl.BlockSpec((tm2, D), rows),
                  pl.BlockSpec((1, D), const), pl.BlockSpec((1, D), const)],
        out_specs=[pl.BlockSpec((tm2, D), rows), pl.BlockSpec((tm2, D), rows)],
        out_shape=[jax.ShapeDtypeStruct((T, D), F32), jax.ShapeDtypeStruct((T, D), BF16)],
        compiler_params=_cp("parallel"),
        name="out_proj_ln1",
    )(merged, w_out.astype(BF16), hf, ln_g.reshape(1, D), ln_b.reshape(1, D))


R_TM = 256
DISPATCH_TB = 256
COMBINE_TB = 128


def _router_kernel(h_ref, whi_ref, wlo_ref, b_ref, eid_ref, wts_ref, rank_ref, cnt_ref, carry_ref):
    i = pl.program_id(0)

    @pl.when(i == 0)
    def _():
        carry_ref[...] = jnp.zeros(carry_ref.shape, F32)

    x = h_ref[...]
    tm = x.shape[0]
    xh = x.astype(BF16)
    xl = (x - xh.astype(F32)).astype(BF16)
    whi = whi_ref[...]
    logits = _dot(xh, whi) + _dot(xh, wlo_ref[...]) + _dot(xl, whi) + b_ref[...]
    lane = lax.broadcasted_iota(jnp.int32, (tm, LANES), 1)
    big = jnp.int32(4 * LANES)
    is_g = (lane >= N_EXPERTS) & (lane < N_EXPERTS + N_GROUPS)
    lg = jnp.where(is_g, logits, NEG_INF)
    gmax = jnp.max(lg, axis=-1, keepdims=True)
    gidx = jnp.min(jnp.where(lg == gmax, lane - N_EXPERTS, big), axis=-1, keepdims=True)
    g_w = 1.0 / jnp.sum(jnp.where(is_g, jnp.exp(lg - gmax), 0.0), axis=-1, keepdims=True)
    in_grp = (lane < N_EXPERTS) & ((lane // EXP_PER_GROUP) == gidx)
    le = jnp.where(in_grp, logits, NEG_INF)
    e1 = jnp.max(le, axis=-1, keepdims=True)
    i1 = jnp.min(jnp.where(le == e1, lane, big), axis=-1, keepdims=True)
    le2 = jnp.where(lane == i1, NEG_INF, le)
    e2 = jnp.max(le2, axis=-1, keepdims=True)
    i2 = jnp.min(jnp.where(le2 == e2, lane, big), axis=-1, keepdims=True)
    zsum = jnp.sum(jnp.where(in_grp, jnp.exp(le - e1), 0.0), axis=-1, keepdims=True)
    p1 = 1.0 / zsum
    p2 = jnp.exp(e2 - e1) / zsum
    w1 = g_w * p1 / (p1 + p2)
    w2 = g_w * p2 / (p1 + p2)
    oh1 = lane == i1
    oh2 = lane == i2
    ohs = (oh1 | oh2).astype(BF16)
    row = lax.broadcasted_iota(jnp.int32, (tm, tm), 0)
    col = lax.broadcasted_iota(jnp.int32, (tm, tm), 1)
    before = _dot((row > col).astype(BF16), ohs) + carry_ref[0:1, :]
    r1 = jnp.sum(jnp.where(oh1, before, 0.0), axis=-1, keepdims=True)
    r2 = jnp.sum(jnp.where(oh2, before, 0.0), axis=-1, keepdims=True)
    total = carry_ref[0:1, :] + jnp.sum(ohs.astype(F32), axis=0, keepdims=True)
    carry_ref[...] = jnp.broadcast_to(total, carry_ref.shape)
    cnt_ref[...] = jnp.broadcast_to(total, cnt_ref.shape).astype(jnp.int32)
    eid_ref[...] = jnp.where(lane == 0, i1, jnp.where(lane == 1, i2, 0))
    wts_ref[...] = jnp.where(lane == 0, w1, jnp.where(lane == 1, w2, 0.0))
    rank_ref[...] = jnp.where(lane == 0, r1, jnp.where(lane == 1, r2, 0.0)).astype(jnp.int32)


ROW_SUB = 8
ROW_LANES = D_MODEL // ROW_SUB


def _dispatch_kernel(dest_ref, h_ref, rows_in_hbm, rows_hbm, stage_ref, sem):
    del rows_in_hbm
    for s in range(ROW_SUB):
        stage_ref[:, s, :] = h_ref[:, s * ROW_LANES:(s + 1) * ROW_LANES]

    def row_copy(t, d):
        return pltpu.make_async_copy(stage_ref.at[pl.ds(t, 1)], rows_hbm.at[pl.ds(d, 1)], sem)

    def start(t, carry):
        for k in range(TOP_K):
            row_copy(t, dest_ref[TOP_K * t + k]).start()
        return carry

    lax.fori_loop(0, DISPATCH_TB, start, 0)

    def wait(t, carry):
        for k in range(TOP_K):
            row_copy(0, 0).wait()
        return carry

    lax.fori_loop(0, DISPATCH_TB, wait, 0)


def _expert_kernel(be_ref, x_ref, wg_ref, wu_ref, wd_ref, o_ref):
    del be_ref
    g = None
    u = None
    for s in range(ROW_SUB):
        xs = x_ref[:, s, :].astype(BF16)
        ks = slice(s * ROW_LANES, (s + 1) * ROW_LANES)
        gs = _dot(xs, wg_ref[ks, :])
        us = _dot(xs, wu_ref[ks, :])
        g = gs if g is None else g + gs
        u = us if u is None else u + us
    hid = (_silu(g) * u).astype(BF16)
    for s in range(ROW_SUB):
        o_ref[:, s, :] = _dot(hid, wd_ref[:, s * ROW_LANES:(s + 1) * ROW_LANES])


def _combine_kernel(dest_ref, h_ref, w_ref, g_ref, b_ref, yrows_hbm, of_ref, ob_ref, ybuf_ref, acc_ref, sem):
    def row_copy(d, k, t):
        return pltpu.make_async_copy(yrows_hbm.at[pl.ds(d, 1)], ybuf_ref.at[k, pl.ds(t, 1)], sem)

    def start(t, carry):
        for k in range(TOP_K):
            row_copy(dest_ref[TOP_K * t + k], k, t).start()
        return carry

    lax.fori_loop(0, COMBINE_TB, start, 0)

    def wait(t, carry):
        for k in range(TOP_K):
            row_copy(0, k, t).wait()
        return carry

    lax.fori_loop(0, COMBINE_TB, wait, 0)
    w = w_ref[...]
    for s in range(ROW_SUB):
        acc_ref[:, s * ROW_LANES:(s + 1) * ROW_LANES] = (ybuf_ref[0, :, s, :] * w[:, 0:1]
                                                        + ybuf_ref[1, :, s, :] * w[:, 1:2])
    y = _layernorm_rows(ALPHA * h_ref[...] + acc_ref[...], g_ref[...], b_ref[...])
    of_ref[...] = y
    ob_ref[...] = y.astype(BF16)


def _moe_layer(hf, hb, w_rg, b_rg, w_re, b_re, w_e_gate, w_e_up, w_e_down, ln_g, ln_b):
    T, D = hf.shape
    n_rows = T * TOP_K + N_EXPERTS * MOE_BLOCK
    n_blocks = n_rows // MOE_BLOCK
    w_r = jnp.concatenate([w_re, w_rg, jnp.zeros((D, LANES - N_EXPERTS - N_GROUPS), F32)], axis=-1)
    b_r = jnp.concatenate([b_re, b_rg, jnp.zeros((LANES - N_EXPERTS - N_GROUPS,), F32)]).reshape(1, LANES)
    w_hi = w_r.astype(BF16)
    w_lo = (w_r - w_hi.astype(F32)).astype(BF16)
    const = lambda i: (0, 0)
    rows = lambda i: (i, 0)
    eid, wts, rank, cnt = pl.pallas_call(
        _router_kernel,
        grid=(T // R_TM,),
        in_specs=[pl.BlockSpec((R_TM, D), rows), pl.BlockSpec((D, LANES), const),
                  pl.BlockSpec((D, LANES), const), pl.BlockSpec((1, LANES), const)],
        out_specs=[pl.BlockSpec((R_TM, LANES), rows), pl.BlockSpec((R_TM, LANES), rows),
                   pl.BlockSpec((R_TM, LANES), rows), pl.BlockSpec((8, LANES), const)],
        out_shape=[jax.ShapeDtypeStruct((T, LANES), jnp.int32), jax.ShapeDtypeStruct((T, LANES), F32),
                   jax.ShapeDtypeStruct((T, LANES), jnp.int32), jax.ShapeDtypeStruct((8, LANES), jnp.int32)],
        scratch_shapes=[pltpu.VMEM((8, LANES), F32)],
        compiler_params=_cp("arbitrary"),
        name="moe_router",
    )(hf, w_hi, w_lo, b_r)
    counts = cnt[0, :N_EXPERTS]
    padded = (counts + MOE_BLOCK - 1) // MOE_BLOCK * MOE_BLOCK
    pends = jnp.cumsum(padded)
    pstarts = pends - padded
    blk_exp = jnp.minimum(jnp.searchsorted(pends, jnp.arange(n_blocks, dtype=jnp.int32) * MOE_BLOCK, side='right'),
                          N_EXPERTS - 1).astype(jnp.int32)
    dest = (pstarts[eid[:, :TOP_K]] + rank[:, :TOP_K]).astype(jnp.int32).reshape(T * TOP_K)
    any_spec = pl.BlockSpec(memory_space=pl.ANY)
    slab = lambda i, *_: (i, 0, 0)
    xrows = pl.pallas_call(
        _dispatch_kernel,
        grid=(T // DISPATCH_TB,),
        in_specs=[pl.BlockSpec((TOP_K * DISPATCH_TB,), lambda i: (i,), memory_space=pltpu.SMEM),
                  pl.BlockSpec((DISPATCH_TB, D), rows), any_spec],
        out_specs=any_spec,
        out_shape=jax.ShapeDtypeStruct((n_rows, ROW_SUB, ROW_LANES), F32),
        scratch_shapes=[pltpu.VMEM((DISPATCH_TB, ROW_SUB, ROW_LANES), F32), pltpu.SemaphoreType.DMA(())],
        input_output_aliases={2: 0},
        compiler_params=_cp("arbitrary"),
        name="moe_dispatch",
    )(dest, hf, jnp.zeros((n_rows, ROW_SUB, ROW_LANES), F32))
    yrows = pl.pallas_call(
        _expert_kernel,
        grid_spec=pltpu.PrefetchScalarGridSpec(
            num_scalar_prefetch=1,
            grid=(n_blocks,),
            in_specs=[pl.BlockSpec((MOE_BLOCK, ROW_SUB, ROW_LANES), slab),
                      pl.BlockSpec((None, D, D_FF), lambda i, be: (be[i], 0, 0)),
                      pl.BlockSpec((None, D, D_FF), lambda i, be: (be[i], 0, 0)),
                      pl.BlockSpec((None, D_FF, D), lambda i, be: (be[i], 0, 0))],
            out_specs=pl.BlockSpec((MOE_BLOCK, ROW_SUB, ROW_LANES), slab)),
        out_shape=jax.ShapeDtypeStruct((n_rows, ROW_SUB, ROW_LANES), F32),
        compiler_params=_cp("arbitrary"),
        name="moe_experts",
    )(blk_exp, xrows, w_e_gate.astype(BF16), w_e_up.astype(BF16), w_e_down.astype(BF16))
    return pl.pallas_call(
        _combine_kernel,
        grid=(T // COMBINE_TB,),
        in_specs=[pl.BlockSpec((TOP_K * COMBINE_TB,), lambda i: (i,), memory_space=pltpu.SMEM),
                  pl.BlockSpec((COMBINE_TB, D), rows), pl.BlockSpec((COMBINE_TB, LANES), rows),
                  pl.BlockSpec((1, D), const), pl.BlockSpec((1, D), const), any_spec],
        out_specs=[pl.BlockSpec((COMBINE_TB, D), rows), pl.BlockSpec((COMBINE_TB, D), rows)],
        out_shape=[jax.ShapeDtypeStruct((T, D), F32), jax.ShapeDtypeStruct((T, D), BF16)],
        scratch_shapes=[pltpu.VMEM((TOP_K, COMBINE_TB, ROW_SUB, ROW_LANES), F32),
                        pltpu.VMEM((COMBINE_TB, D), F32), pltpu.SemaphoreType.DMA(())],
        compiler_params=_cp("arbitrary"),
        name="moe_combine_ln2",
    )(dest, hf, wts, ln_g.reshape(1, D), ln_b.reshape(1, D), yrows)


def kernel(x, ln_in_g, ln_in_b, rel_bias, w_in, g_cq, w_uq, g_ckv, w_ukv, w_dw_c, b_dw_c, ln_c_g, ln_c_b,
           w_conv_d, b_conv_d, a_log_f, a_log_b, dt_bias_f, dt_bias_b, d_skip, g_norm_d, w_br, w_gate, b_gate,
           w_out, ln1_g, ln1_b, w_rg, b_rg, w_re, b_re, w_e_gate, w_e_up, w_e_down, ln2_g, ln2_b):
    B, S, D = x.shape
    T = B * S
    hf, hb = _layernorm(x.reshape(T, D), ln_in_g, ln_in_b)
    for l in range(DEPTH):
        w_a, w_r = _in_proj_weights(w_in[l])
        qkv = _matmul(hb, w_a.astype(BF16), BF16, 512, 3 * A_WIDTH, "in_proj_a")
        rest = _matmul(hb, w_r.astype(BF16), F32, 512, R_WIDTH // 2, "in_proj_rest")
        y_a = _mixer_a(qkv, rel_bias, B, S)
        y_b = _mixer_b(rest, g_cq[l], w_uq[l], g_ckv[l], w_ukv[l], B, S)
        y_c = _mixer_c(rest, w_dw_c[l], b_dw_c[l], ln_c_g[l], ln_c_b[l], B, S)
        y_d = _mixer_d(rest, w_conv_d[l], b_conv_d[l], a_log_f[l], a_log_b[l], dt_bias_f[l], dt_bias_b[l],
                       d_skip[l], g_norm_d[l], B, S)
        h1f, h1b = _merge(hb, hf, (y_a, y_b, y_c, y_d), w_gate[l], b_gate[l], w_br[l], w_out[l], ln1_g[l], ln1_b[l])
        hf, hb = _moe_layer(h1f, h1b, w_rg[l], b_rg[l], w_re[l], b_re[l], w_e_gate[l], w_e_up[l], w_e_down[l],
                            ln2_g[l], ln2_b[l])
    return hf.reshape(B, S, D)
```

```python
import functools

import numpy as np
import jax
import jax.numpy as jnp
from jax import lax
from jax.experimental import pallas as pl
from jax.experimental.pallas import tpu as pltpu

F32 = jnp.float32
BF16 = jnp.bfloat16

D_MODEL = 2048
DEPTH = 2
A_HEADS = 8
A_HEAD_DIM = 64
A_WIDTH = A_HEADS * A_HEAD_DIM
A_PATTERNS = ((128, 1), (512, 4), (2048, 16))
A_BAND = 64
REL_BUCKETS = 32
REL_MAX_DIST = 1024
B_HEADS = 8
B_NOPE = 64
B_ROPE = 32
B_V = 64
B_Q_LORA = 512
B_KV_LORA = 256
ROPE_THETA = 10000.0
C_CH = 512
C_KERNEL = 31
D_HEADS = 8
D_HEAD_DIM = 64
D_INNER = D_HEADS * D_HEAD_DIM
D_STATE = 128
D_GROUPS = 2
D_CONV = 5
D_CHUNK = 128
N_BRANCH = 4
BRANCH_W = 512
N_GROUPS = 4
EXP_PER_GROUP = 8
N_EXPERTS = N_GROUPS * EXP_PER_GROUP
TOP_K = 2
D_FF = 512
MOE_BLOCK = 128
ALPHA = (2 * DEPTH) ** 0.25
EPS = 1e-5
NEG_INF = -1e30

LANES = 128
R_GLU, R_XBC, R_CQ, R_Z, R_CKV, R_KR, R_DT = 0, 1024, 2048, 2560, 3072, 3328, 3456
R_WIDTH = 3584
VMEM_LIMIT = 56 * 1024 * 1024


def _cp(*sem):
    return pltpu.CompilerParams(dimension_semantics=sem, vmem_limit_bytes=VMEM_LIMIT)


def _dot(a, b):
    return jnp.dot(a, b, preferred_element_type=F32)


def _dot_nt(a, b):
    return lax.dot_general(a, b, (((1,), (1,)), ((), ())), preferred_element_type=F32)


def _split3(x):
    hi = x.astype(BF16)
    r1 = x - hi.astype(F32)
    mid = r1.astype(BF16)
    lo = (r1 - mid.astype(F32)).astype(BF16)
    return hi, mid, lo


def _layernorm_rows(x, g, b):
    mu = jnp.mean(x, axis=-1, keepdims=True)
    xc = x - mu
    var = jnp.mean(xc * xc, axis=-1, keepdims=True)
    return xc * lax.rsqrt(var + EPS) * g + b


def _sigmoid(x):
    return 1.0 / (1.0 + jnp.exp(-x))


def _silu(x):
    return x * _sigmoid(x)


def _ln_kernel(x_ref, g_ref, b_ref, of_ref, ob_ref):
    y = _layernorm_rows(x_ref[...], g_ref[...], b_ref[...])
    of_ref[...] = y
    ob_ref[...] = y.astype(BF16)


def _layernorm(x, g, b, tm=256):
    T, D = x.shape
    return pl.pallas_call(
        _ln_kernel,
        grid=(T // tm,),
        in_specs=[pl.BlockSpec((tm, D), lambda i: (i, 0)),
                  pl.BlockSpec((1, D), lambda i: (0, 0)),
                  pl.BlockSpec((1, D), lambda i: (0, 0))],
        out_specs=[pl.BlockSpec((tm, D), lambda i: (i, 0)),
                   pl.BlockSpec((tm, D), lambda i: (i, 0))],
        out_shape=[jax.ShapeDtypeStruct((T, D), F32), jax.ShapeDtypeStruct((T, D), BF16)],
        compiler_params=_cp("parallel"),
        name="ln_in",
    )(x, g.reshape(1, D), b.reshape(1, D))


def _mm_kernel(x_ref, w_ref, o_ref):
    o_ref[...] = _dot(x_ref[...], w_ref[...]).astype(o_ref.dtype)


def _matmul(x, w, out_dtype, tm, tn, name):
    M, K = x.shape
    N = w.shape[1]
    return pl.pallas_call(
        _mm_kernel,
        grid=(N // tn, M // tm),
        in_specs=[pl.BlockSpec((tm, K), lambda j, i: (i, 0)),
                  pl.BlockSpec((K, tn), lambda j, i: (0, j))],
        out_specs=pl.BlockSpec((tm, tn), lambda j, i: (i, j)),
        out_shape=jax.ShapeDtypeStruct((M, N), out_dtype),
        compiler_params=_cp("parallel", "parallel"),
        name=name,
    )(x, w)


def _t5_bucket(rel):
    half = REL_BUCKETS // 2
    max_exact = half // 2
    n = np.abs(rel)
    large = max_exact + (np.log(np.maximum(n, 1) / max_exact) / np.log(REL_MAX_DIST / max_exact)
                         * (half - max_exact)).astype(np.int32)
    large = np.minimum(large, half - 1)
    return (rel > 0).astype(np.int32) * half + np.where(n < max_exact, n, large)


def _a_window(L):
    return min(2 * LANES, L)


def _a_bias_tiles(rel_bias, d, L):
    W = _a_window(L)
    offs = (0,) if L == LANES else (0, -A_BAND, -2 * A_BAND)
    qi = np.arange(LANES)[:, None]
    kj = np.arange(W)[None, :]
    rel = np.stack([kj - qi + off for off in offs], axis=0)
    valid = np.abs(rel) <= A_BAND
    onehot = (jnp.asarray(_t5_bucket(rel * d), jnp.int32)[..., None] == jnp.arange(REL_BUCKETS)).astype(F32)
    b = jnp.einsum('vqkb,bh->vhqk', onehot, rel_bias.astype(F32), precision=lax.Precision.HIGHEST)
    return jnp.where(valid[:, None], b, NEG_INF)


def _attn_a_kernel(q_ref, k_ref, v_ref, bias_ref, *rest, L, nqb, final):
    W = _a_window(L)
    i = pl.program_id(2)
    if nqb == 1:
        ws = 0
    else:
        ws = pl.multiple_of(jnp.clip(i * LANES - A_BAND, 0, L - W), A_BAND)
    q = q_ref[...]
    kw = k_ref[pl.ds(ws, W), :]
    vw = v_ref[pl.ds(ws, W), :]
    lane = lax.broadcasted_iota(jnp.int32, (LANES, LANES), 1)
    lse_tile = jnp.zeros((LANES, LANES), F32)
    outs = []
    for h in range(A_HEADS):
        sl = slice(h * A_HEAD_DIM, (h + 1) * A_HEAD_DIM)
        s = _dot_nt(q[:, sl], kw[:, sl]) * (A_HEAD_DIM ** -0.5) + bias_ref[h]
        m = jnp.max(s, axis=-1, keepdims=True)
        p = jnp.exp(s - m)
        den = jnp.sum(p, axis=-1, keepdims=True)
        o = _dot(p.astype(BF16), vw[:, sl]) / den
        lse = m + jnp.log(den)
        outs.append(o)
        lse_tile = jnp.where(lane == h, lse, lse_tile)
    if not final:
        o_ref, lse_ref = rest
        for h in range(A_HEADS):
            o_ref[:, h * A_HEAD_DIM:(h + 1) * A_HEAD_DIM] = outs[h]
        lse_ref[...] = lse_tile
    else:
        o1_ref, l1_ref, o2_ref, l2_ref, y_ref = rest
        l1 = l1_ref[...]
        l2 = l2_ref[...]
        mx = jnp.maximum(jnp.maximum(l1, l2), lse_tile)
        e1 = jnp.exp(l1 - mx)
        e2 = jnp.exp(l2 - mx)
        e3 = jnp.exp(lse_tile - mx)
        tot = e1 + e2 + e3
        w1 = e1 / tot
        w2 = e2 / tot
        w3 = e3 / tot
        for h in range(A_HEADS):
            sl = slice(h * A_HEAD_DIM, (h + 1) * A_HEAD_DIM)
            y = (w1[:, h:h + 1] * o1_ref[:, sl] + w2[:, h:h + 1] * o2_ref[:, sl]
                 + w3[:, h:h + 1] * outs[h])
            y_ref[:, sl] = y.astype(y_ref.dtype)


def _attn_a_call(qkv, bias, B, S, d, prev=None):
    L = S // d
    nqb = L // LANES
    W = _a_window(L)
    nvar = bias.shape[0]
    qkv_v = qkv.reshape(B, L, d * 3 * A_WIDTH)

    def var_of(i):
        if nvar == 1:
            return 0
        return jnp.where(i == 0, 0, jnp.where(i == nqb - 1, 2, 1))

    in_specs = [
        pl.BlockSpec((None, LANES, A_WIDTH), lambda b, r, i: (b, i, 3 * r)),
        pl.BlockSpec((None, L, A_WIDTH), lambda b, r, i: (b, 0, 3 * r + 1)),
        pl.BlockSpec((None, L, A_WIDTH), lambda b, r, i: (b, 0, 3 * r + 2)),
        pl.BlockSpec((None, A_HEADS, LANES, W), lambda b, r, i: (var_of(i), 0, 0, 0)),
    ]
    args = [qkv_v, qkv_v, qkv_v, bias]
    final = prev is not None
    if final:
        assert d == 1
        for o_p, l_p in ((prev[0], prev[1]), (prev[2], prev[3])):
            in_specs.append(pl.BlockSpec((None, LANES, A_WIDTH), lambda b, r, i: (b, i, 0)))
            in_specs.append(pl.BlockSpec((None, LANES, LANES), lambda b, r, i: (b, i, 0)))
            args += [o_p.reshape(B, S, A_WIDTH), l_p.reshape(B, S, LANES)]
        out_specs = pl.BlockSpec((None, LANES, A_WIDTH), lambda b, r, i: (b, i, 0))
        out_shape = jax.ShapeDtypeStruct((B, S, A_WIDTH), BF16)
    else:
        out_specs = [pl.BlockSpec((None, LANES, A_WIDTH), lambda b, r, i: (b, i, r)),
                     pl.BlockSpec((None, LANES, LANES), lambda b, r, i: (b, i, r))]
        out_shape = [jax.ShapeDtypeStruct((B, L, d * A_WIDTH), F32),
                     jax.ShapeDtypeStruct((B, L, d * LANES), F32)]
    out = pl.pallas_call(
        functools.partial(_attn_a_kernel, L=L, nqb=nqb, final=final),
        grid=(B, d, nqb),
        in_specs=in_specs,
        out_specs=out_specs,
        out_shape=out_shape,
        compiler_params=_cp("parallel", "parallel", "arbitrary"),
        name=f"attn_a_d{d}",
    )(*args)
    if final:
        return out.reshape(B * S, A_WIDTH)
    return out[0].reshape(B * S, A_WIDTH), out[1].reshape(B * S, LANES)


def _mixer_a_bias(rel_bias, S):
    return tuple(_a_bias_tiles(rel_bias, d, S // d) for _, d in A_PATTERNS)


def _mixer_a(qkv, bias, B, S):
    (_, d1), (_, d4), (_, d16) = A_PATTERNS
    o16, l16 = _attn_a_call(qkv, bias[2], B, S, d16)
    o4, l4 = _attn_a_call(qkv, bias[1], B, S, d4)
    return _attn_a_call(qkv, bias[0], B, S, d1, prev=(o16, l16, o4, l4))


def _mla_proj_kernel(cq_ref, ckv_ref, kr_ref, gq_ref, gkv_ref, wqm_ref, wqs_ref, wk_ref, wv_ref,
                     ek_ref, cosq_ref, sinq_ref, csk_ref, q_ref, k_ref, v_ref):
    cq = cq_ref[...]
    xq = (cq * lax.rsqrt(jnp.mean(cq * cq, axis=-1, keepdims=True) + EPS) * gq_ref[...]).astype(BF16)
    ckv = ckv_ref[...]
    xkv = (ckv * lax.rsqrt(jnp.mean(ckv * ckv, axis=-1, keepdims=True) + EPS) * gkv_ref[...]).astype(BF16)
    qm = _dot(xq, wqm_ref[...])
    qs = _dot(xq, wqs_ref[...])
    cosq = cosq_ref[...]
    sinq = sinq_ref[...]
    t = kr_ref[...] * csk_ref[...]
    t_hi = t.astype(BF16)
    t_lo = (t - t_hi.astype(F32)).astype(BF16)
    kk = _dot(xkv, wk_ref[...]) + _dot(t_hi, ek_ref[...]) + _dot(t_lo, ek_ref[...])
    for h in range(B_HEADS):
        sl = slice(h * LANES, (h + 1) * LANES)
        q_ref[:, sl] = (qm[:, sl] * cosq + qs[:, sl] * sinq).astype(BF16)
    k_ref[...] = kk.astype(BF16)
    v_ref[...] = _dot(xkv, wv_ref[...]).astype(BF16)


def _mla_attn_kernel(q_ref, k_ref, v_ref, o_ref):
    scale = (B_NOPE + B_ROPE) ** -0.5
    v = v_ref[...]
    accs = []
    for hh in range(2):
        sl = slice(hh * LANES, (hh + 1) * LANES)
        s = _dot_nt(q_ref[:, sl], k_ref[:, sl]) * scale
        m = jnp.max(s, axis=-1, keepdims=True)
        p = jnp.exp(s - m)
        den = jnp.sum(p, axis=-1, keepdims=True)
        accs.append(_dot(p.astype(BF16), v) / den)
    lane = lax.broadcasted_iota(jnp.int32, accs[0].shape, 1)
    o_ref[...] = jnp.where(lane < B_V, accs[0], accs[1]).astype(o_ref.dtype)


def _mla_tables(S):
    inv_freq = ROPE_THETA ** (-jnp.arange(0, B_ROPE, 2, dtype=F32) / B_ROPE)
    ang = jnp.arange(S, dtype=F32)[:, None] * inv_freq[None]
    cos, sin = jnp.cos(ang), jnp.sin(ang)
    cos2 = jnp.concatenate([cos, cos], axis=-1)
    sin2 = jnp.concatenate([sin, sin], axis=-1)
    ones = jnp.ones((S, B_NOPE), F32)
    zn = jnp.zeros((S, B_NOPE), F32)
    zp = jnp.zeros((S, LANES - B_NOPE - B_ROPE), F32)
    cosq = jnp.concatenate([ones, cos2, zp], axis=-1)
    sinq = jnp.concatenate([zn, sin2, zp], axis=-1)
    csk = jnp.concatenate([cos2, sin2, jnp.zeros((S, LANES - 2 * B_ROPE), F32)], axis=-1)
    return cosq, sinq, csk


def _swap_cols(w):
    half = w.shape[-1] // 2
    return jnp.concatenate([-w[..., half:], w[..., :half]], axis=-1)


def _mla_weights(w_uq, w_ukv):
    dq = B_NOPE + B_ROPE
    wq = w_uq.reshape(B_Q_LORA, B_HEADS, dq)
    zpad = jnp.zeros((B_Q_LORA, B_HEADS, LANES - dq), F32)
    wqm = jnp.concatenate([wq, zpad], axis=-1).reshape(B_Q_LORA, B_HEADS * LANES)
    wqs = jnp.concatenate([jnp.zeros((B_Q_LORA, B_HEADS, B_NOPE), F32), _swap_cols(wq[..., B_NOPE:]), zpad],
                          axis=-1).reshape(B_Q_LORA, B_HEADS * LANES)
    wkv = w_ukv.reshape(B_KV_LORA, B_HEADS, B_NOPE + B_V)
    wk = jnp.concatenate([wkv[..., :B_NOPE], jnp.zeros((B_KV_LORA, B_HEADS, LANES - B_NOPE), F32)],
                         axis=-1).reshape(B_KV_LORA, B_HEADS * LANES)
    wv = wkv[..., B_NOPE:].reshape(B_KV_LORA, B_HEADS * B_V)
    ek = np.zeros((LANES, B_HEADS, LANES), np.float32)
    for j in range(B_ROPE):
        ek[j, :, B_NOPE + j] = 1.0
        ek[B_ROPE + j, :, B_NOPE + j] = 1.0
    ek = jnp.asarray(ek.reshape(LANES, B_HEADS * LANES))
    return wqm.astype(BF16), wqs.astype(BF16), wk.astype(BF16), wv.astype(BF16), ek.astype(BF16)


def _mixer_b(rest, g_cq, w_uq, g_ckv, w_ukv, B, S, tm=512, tq=256):
    T = B * S
    wqm, wqs, wk, wv, ek = _mla_weights(w_uq, w_ukv)
    cosq, sinq, csk = _mla_tables(S)
    nst = S // tm
    QW = B_HEADS * LANES
    const = lambda i: (0, 0)
    pos = lambda i: (i % nst, 0)
    q, k, v = pl.pallas_call(
        _mla_proj_kernel,
        grid=(T // tm,),
        in_specs=[pl.BlockSpec((tm, B_Q_LORA), lambda i: (i, R_CQ // B_Q_LORA)),
                  pl.BlockSpec((tm, B_KV_LORA), lambda i: (i, R_CKV // B_KV_LORA)),
                  pl.BlockSpec((tm, LANES), lambda i: (i, R_KR // LANES)),
                  pl.BlockSpec((1, B_Q_LORA), const),
                  pl.BlockSpec((1, B_KV_LORA), const),
                  pl.BlockSpec((B_Q_LORA, QW), const),
                  pl.BlockSpec((B_Q_LORA, QW), const),
                  pl.BlockSpec((B_KV_LORA, QW), const),
                  pl.BlockSpec((B_KV_LORA, B_HEADS * B_V), const),
                  pl.BlockSpec((LANES, QW), const),
                  pl.BlockSpec((tm, LANES), pos),
                  pl.BlockSpec((tm, LANES), pos),
                  pl.BlockSpec((tm, LANES), pos)],
        out_specs=[pl.BlockSpec((tm, QW), lambda i: (i, 0)),
                   pl.BlockSpec((tm, QW), lambda i: (i, 0)),
                   pl.BlockSpec((tm, B_HEADS * B_V), lambda i: (i, 0))],
        out_shape=[jax.ShapeDtypeStruct((T, QW), BF16), jax.ShapeDtypeStruct((T, QW), BF16),
                   jax.ShapeDtypeStruct((T, B_HEADS * B_V), BF16)],
        compiler_params=_cp("parallel"),
        name="mla_proj",
    )(rest, rest, rest, g_cq.reshape(1, -1), g_ckv.reshape(1, -1), wqm, wqs, wk, wv, ek, cosq, sinq, csk)
    y = pl.pallas_call(
        _mla_attn_kernel,
        grid=(B, B_HEADS // 2, S // tq),
        in_specs=[pl.BlockSpec((None, tq, 2 * LANES), lambda b, hp, i: (b, i, hp)),
                  pl.BlockSpec((None, S, 2 * LANES), lambda b, hp, i: (b, 0, hp)),
                  pl.BlockSpec((None, S, 2 * B_V), lambda b, hp, i: (b, 0, hp))],
        out_specs=pl.BlockSpec((None, tq, 2 * B_V), lambda b, hp, i: (b, i, hp)),
        out_shape=jax.ShapeDtypeStruct((B, S, B_HEADS * B_V), BF16),
        compiler_params=_cp("parallel", "parallel", "arbitrary"),
        name="mla_attn",
    )(q.reshape(B, S, QW), k.reshape(B, S, QW), v.reshape(B, S, B_HEADS * B_V))
    return y.reshape(T, B_HEADS * B_V)


C_PAD = 16
C_ROWS = 128


def _depthwise_taps(win_ref, w_ref, bias, ls, first, ntaps, rows):
    SUB = 8
    acc = jnp.broadcast_to(bias, (rows, LANES))
    span = rows + ((first + ntaps - 1) // SUB) * SUB
    for ph in range(SUB):
        taps = [j for j in range(ntaps) if (first + j) % SUB == ph]
        if not taps:
            continue
        xb = win_ref[ph:ph + span, ls]
        for j in taps:
            a = (first + j) // SUB * SUB
            acc = acc + w_ref[j:j + 1, ls] * xb[a:a + rows]
    return acc


def _fill_window(win_ref, load_rows, r0, i, nblk, rows, pad):
    width = win_ref.shape[1]
    win_ref[pad:pad + rows, :] = load_rows(r0, rows)

    @pl.when(i > 0)
    def _():
        win_ref[0:pad, :] = load_rows(r0 - pad, pad)

    @pl.when(i == 0)
    def _():
        win_ref[0:pad, :] = jnp.zeros((pad, width), F32)

    @pl.when(i < nblk - 1)
    def _():
        win_ref[pad + rows:pad + rows + pad, :] = load_rows(r0 + rows, pad)

    @pl.when(i == nblk - 1)
    def _():
        win_ref[pad + rows:pad + rows + pad, :] = jnp.zeros((pad, width), F32)


def _conv_c_kernel(glu_ref, w_ref, b_ref, g_ref, beta_ref, o_ref, win_ref, acc_ref, *, S):
    i = pl.program_id(1)
    r0 = pl.multiple_of(i * C_ROWS, C_ROWS)

    def glu_rows(start, n):
        rs = pl.ds(pl.multiple_of(start, C_PAD), n)
        return glu_ref[rs, 0:C_CH] * _sigmoid(glu_ref[rs, C_CH:2 * C_CH])

    _fill_window(win_ref, glu_rows, r0, i, S // C_ROWS, C_ROWS, C_PAD)
    first = C_PAD - C_KERNEL // 2
    for lb in range(C_CH // LANES):
        ls = slice(lb * LANES, (lb + 1) * LANES)
        acc_ref[:, ls] = _depthwise_taps(win_ref, w_ref, b_ref[:, ls], ls, first, C_KERNEL, C_ROWS)
    y = _layernorm_rows(acc_ref[...], g_ref[...], beta_ref[...])
    o_ref[...] = _silu(y).astype(o_ref.dtype)


def _mixer_c(rest, w_dw, b_dw, ln_g, ln_b, B, S):
    T = B * S
    const = lambda b, i: (0, 0)
    y = pl.pallas_call(
        functools.partial(_conv_c_kernel, S=S),
        grid=(B, S // C_ROWS),
        in_specs=[pl.BlockSpec((None, S, 2 * C_CH), lambda b, i: (b, 0, R_GLU // (2 * C_CH))),
                  pl.BlockSpec((C_KERNEL, C_CH), const),
                  pl.BlockSpec((1, C_CH), const),
                  pl.BlockSpec((1, C_CH), const),
                  pl.BlockSpec((1, C_CH), const)],
        out_specs=pl.BlockSpec((None, C_ROWS, C_CH), lambda b, i: (b, i, 0)),
        out_shape=jax.ShapeDtypeStruct((B, S, C_CH), BF16),
        scratch_shapes=[pltpu.VMEM((C_ROWS + 2 * C_PAD, C_CH), F32),
                        pltpu.VMEM((C_ROWS, C_CH), F32)],
        compiler_params=_cp("parallel", "parallel"),
        name="conformer_conv",
    )(rest.reshape(B, S, R_WIDTH), w_dw, b_dw.reshape(1, -1), ln_g.reshape(1, -1), ln_b.reshape(1, -1))
    return y.reshape(T, C_CH)


D_PAD = 8
XBC_W = D_INNER + 2 * D_GROUPS * D_STATE
N_PAIR = D_HEADS // 2


def _pair_expand(v, first):
    lane = lax.broadcasted_iota(jnp.int32, (v.shape[0], LANES), 1)
    lo = jnp.broadcast_to(v[:, first:first + 1], (v.shape[0], LANES))
    hi = jnp.broadcast_to(v[:, first + 1:first + 2], (v.shape[0], LANES))
    return jnp.where(lane < D_HEAD_DIM, lo, hi)


def _ssd_kernel(xbc_ref, z_ref, dt_ref, wc_ref, bc_ref, alog_ref, dtb_ref, dskip_ref, gn_ref, o_ref,
                win_ref, xc_ref, a_ref, dtv_ref, y_ref, st_ref, *, S):
    Q = D_CHUNK
    nchunk = S // Q
    N = D_STATE
    bm0 = D_INNER
    cm0 = D_INNER + D_GROUPS * N

    def conv_body(c, carry):
        r0 = pl.multiple_of(c * Q, Q)
        _fill_window(win_ref, lambda st, n: xbc_ref[pl.ds(pl.multiple_of(st, D_PAD), n), :], r0, c, nchunk, Q, D_PAD)
        for lb in range(XBC_W // LANES):
            ls = slice(lb * LANES, (lb + 1) * LANES)
            acc = _depthwise_taps(win_ref, wc_ref, bc_ref[:, ls], ls, D_PAD - D_CONV // 2, D_CONV, Q)
            xc_ref[pl.ds(r0, Q), ls] = _silu(acc)
        return carry

    lax.fori_loop(0, nchunk, conv_body, 0)

    lane1 = lax.broadcasted_iota(jnp.int32, (1, LANES), 1)
    a_row = jnp.where(lane1 < 2 * D_HEADS, -jnp.exp(alog_ref[...]), 0.0)
    xdt = dt_ref[...] + dtb_ref[...]
    dtv = jnp.maximum(xdt, 0.0) + jnp.log(1.0 + jnp.exp(-jnp.abs(xdt)))
    dtv_ref[...] = dtv
    a_ref[...] = dtv * a_row

    row = lax.broadcasted_iota(jnp.int32, (Q, Q), 0)
    col = lax.broadcasted_iota(jnp.int32, (Q, Q), 1)
    tril = row >= col
    triu = col >= row
    lane = col

    def scan_chunk(c, lower, off, finalize):
        r0 = pl.multiple_of(c * Q, Q)
        rows = pl.ds(r0, Q)
        mask = tril if lower else triu
        tri = mask.astype(BF16)
        a_hi, a_mid, a_lo = _split3(a_ref[rows, :])
        cs = _dot(tri, a_hi) + _dot(tri, a_mid) + _dot(tri, a_lo)
        cs_t = cs.T
        ecs = jnp.exp(cs)
        edge = Q - 1 if lower else 0
        edec = jnp.exp(cs[edge:edge + 1, :] - cs)
        dt_c = dtv_ref[rows, :]
        for g in range(D_GROUPS):
            bg = xc_ref[rows, bm0 + g * N:bm0 + (g + 1) * N]
            cg = xc_ref[rows, cm0 + g * N:cm0 + (g + 1) * N].astype(BF16)
            cb = _dot_nt(cg, bg.astype(BF16))
            bg_t = bg.T.astype(BF16)
            for pp in range(N_PAIR // D_GROUPS):
                p = g * (N_PAIR // D_GROUPS) + pp
                ps = slice(p * LANES, (p + 1) * LANES)
                xdt_p = xc_ref[rows, ps] * _pair_expand(dt_c, off + 2 * p)
                ms = []
                for hh in range(2):
                    k = off + 2 * p + hh
                    diff = jnp.broadcast_to(cs[:, k:k + 1], (Q, Q)) - cs_t[k:k + 1, :]
                    ms.append((jnp.exp(jnp.where(mask, diff, NEG_INF)) * cb).astype(BF16))
                x_lo = jnp.where(lane < D_HEAD_DIM, xdt_p, 0.0).astype(BF16)
                x_hi = jnp.where(lane >= D_HEAD_DIM, xdt_p, 0.0).astype(BF16)
                y_intra = _dot(jnp.concatenate(ms, axis=1), jnp.concatenate([x_lo, x_hi], axis=0))
                hp = st_ref[p]
                ecs_p = _pair_expand(ecs, off + 2 * p)
                y_new = y_intra + _dot(cg, hp.astype(BF16)) * ecs_p
                if lower:
                    y_ref[rows, ps] = y_new
                else:
                    y_ref[rows, ps] = y_ref[rows, ps] + y_new
                xs_p = (xdt_p * _pair_expand(edec, off + 2 * p)).astype(BF16)
                st_ref[p] = hp * ecs_p[edge:edge + 1, :] + _dot(bg_t, xs_p)
        if finalize:
            y = y_ref[rows, :] + xc_ref[rows, 0:D_INNER] * dskip_ref[...]
            gated = y * _silu(z_ref[rows, :])
            out = gated * lax.rsqrt(jnp.mean(gated * gated, axis=-1, keepdims=True) + EPS) * gn_ref[...]
            o_ref[rows, :] = out.astype(o_ref.dtype)

    st_ref[...] = jnp.zeros(st_ref.shape, F32)

    def fwd_body(c, carry):
        scan_chunk(c, True, 0, False)
        return carry

    lax.fori_loop(0, nchunk, fwd_body, 0)
    st_ref[...] = jnp.zeros(st_ref.shape, F32)

    def bwd_body(k, carry):
        scan_chunk(nchunk - 1 - k, False, D_HEADS, True)
        return carry

    lax.fori_loop(0, nchunk, bwd_body, 0)


def _mixer_d(rest, w_conv, b_conv, a_log_f, a_log_b, dt_bias_f, dt_bias_b, d_skip, g_norm, B, S):
    T = B * S
    pad16 = lambda f, b: jnp.concatenate([f, b, jnp.zeros((LANES - 2 * D_HEADS,), F32)]).reshape(1, LANES)
    const = lambda b: (0, 0)
    y = pl.pallas_call(
        functools.partial(_ssd_kernel, S=S),
        grid=(B,),
        in_specs=[pl.BlockSpec((None, S, XBC_W), lambda b: (b, 0, R_XBC // XBC_W)),
                  pl.BlockSpec((None, S, D_INNER), lambda b: (b, 0, R_Z // D_INNER)),
                  pl.BlockSpec((None, S, LANES), lambda b: (b, 0, R_DT // LANES)),
                  pl.BlockSpec((D_CONV, XBC_W), const),
                  pl.BlockSpec((1, XBC_W), const),
                  pl.BlockSpec((1, LANES), const),
                  pl.BlockSpec((1, LANES), const),
                  pl.BlockSpec((1, D_INNER), const),
                  pl.BlockSpec((1, D_INNER), const)],
        out_specs=pl.BlockSpec((None, S, D_INNER), lambda b: (b, 0, 0)),
        out_shape=jax.ShapeDtypeStruct((B, S, D_INNER), BF16),
        scratch_shapes=[pltpu.VMEM((D_CHUNK + 2 * D_PAD, XBC_W), F32),
                        pltpu.VMEM((S, XBC_W), F32),
                        pltpu.VMEM((S, LANES), F32),
                        pltpu.VMEM((S, LANES), F32),
                        pltpu.VMEM((S, D_INNER), F32),
                        pltpu.VMEM((N_PAIR, D_STATE, LANES), F32)],
        compiler_params=_cp("parallel"),
        name="ssd_mixer",
    )(rest.reshape(B, S, R_WIDTH), rest.reshape(B, S, R_WIDTH), rest.reshape(B, S, R_WIDTH),
      w_conv, b_conv.reshape(1, -1), pad16(a_log_f, a_log_b), pad16(dt_bias_f, dt_bias_b),
      jnp.repeat(d_skip, D_HEAD_DIM).reshape(1, -1), g_norm.reshape(1, -1))
    return y.reshape(T, D_INNER)


def _in_proj_weights(w_in_l):
    o = np.cumsum((0, A_WIDTH, A_WIDTH, A_WIDTH, B_Q_LORA, B_KV_LORA, B_ROPE, 2 * C_CH,
                   D_INNER, D_INNER, D_GROUPS * D_STATE, D_GROUPS * D_STATE, 2 * D_HEADS)).tolist()
    seg = lambda n: w_in_l[:, o[n]:o[n + 1]]
    w_a = w_in_l[:, :o[3]]
    cq, ckv, kr, glu, z, xs, bm, cm, dt = (seg(n) for n in range(3, 12))
    zeros = lambda n: jnp.zeros((w_in_l.shape[0], n), w_in_l.dtype)
    w_r = jnp.concatenate([glu, xs, bm, cm, cq, z, ckv,
                           kr, _swap_cols(kr), zeros(LANES - 2 * B_ROPE),
                           dt, zeros(LANES - 2 * D_HEADS)], axis=-1)
    assert w_r.shape[1] == R_WIDTH
    return w_a, w_r


def _merge_gate_kernel(h_ref, ya_ref, yb_ref, yc_ref, yd_ref, wg_ref, bg_ref, wbr_ref, o_ref):
    h = h_ref[...]
    acc = None
    for i, y_ref in enumerate((ya_ref, yb_ref, yc_ref, yd_ref)):
        gate = _sigmoid(_dot(h, wg_ref[i]) + bg_ref[i])
        term = gate * _dot(y_ref[...], wbr_ref[i])
        acc = term if acc is None else acc + term
    o_ref[...] = acc.astype(o_ref.dtype)


def _out_ln_kernel(m_ref, w_ref, h_ref, g_ref, b_ref, of_ref, ob_ref):
    y = _layernorm_rows(ALPHA * h_ref[...] + _dot(m_ref[...], w_ref[...]), g_ref[...], b_ref[...])
    of_ref[...] = y
    ob_ref[...] = y.astype(BF16)


def _merge(hb, hf, branches, w_gate, b_gate, w_br, w_out, ln_g, ln_b, tm=512, tn=512, tm2=256):
    T, D = hb.shape
    ybs = pl.BlockSpec((tm, BRANCH_W), lambda j, i: (i, 0))
    merged = pl.pallas_call(
        _merge_gate_kernel,
        grid=(D // tn, T // tm),
        in_specs=[pl.BlockSpec((tm, D), lambda j, i: (i, 0)), ybs, ybs, ybs, ybs,
                  pl.BlockSpec((N_BRANCH, D, tn), lambda j, i: (0, 0, j)),
                  pl.BlockSpec((N_BRANCH, 1, tn), lambda j, i: (0, 0, j)),
                  pl.BlockSpec((N_BRANCH, BRANCH_W, tn), lambda j, i: (0, 0, j))],
        out_specs=pl.BlockSpec((tm, tn), lambda j, i: (i, j)),
        out_shape=jax.ShapeDtypeStruct((T, D), BF16),
        compiler_params=_cp("parallel", "parallel"),
        name="merge_gate",
    )(hb, *branches, w_gate.astype(BF16), b_gate.reshape(N_BRANCH, 1, D), w_br.astype(BF16))
    const = lambda i: (0, 0)
    rows = lambda i: (i, 0)
    return pl.pallas_call(
        _out_ln_kernel,
        grid=(T // tm2,),
        in_specs=[pl.BlockSpec((tm2, D), rows), pl.BlockSpec((D, D), const), pl.BlockSpec((tm2, D), rows),
                  pl.BlockSpec((1, D), const), pl.BlockSpec((1, D), const)],
        out_specs=[pl.BlockSpec((tm2, D), rows), pl.BlockSpec((tm2, D), rows)],
        out_shape=[jax.ShapeDtypeStruct((T, D), F32), jax.ShapeDtypeStruct((T, D), BF16)],
        compiler_params=_cp("parallel"),
        name="out_proj_ln1",
    )(merged, w_out.astype(BF16), hf, ln_g.reshape(1, D), ln_b.reshape(1, D))


R_TM = 256
DISPATCH_TB = 256
COMBINE_TB = 128


def _router_kernel(h_ref, whi_ref, wlo_ref, b_ref, eid_ref, wts_ref, rank_ref, cnt_ref, carry_ref):
    i = pl.program_id(0)

    @pl.when(i == 0)
    def _():
        carry_ref[...] = jnp.zeros(carry_ref.shape, F32)

    x = h_ref[...]
    tm = x.shape[0]
    xh = x.astype(BF16)
    xl = (x - xh.astype(F32)).astype(BF16)
    whi = whi_ref[...]
    logits = _dot(xh, whi) + _dot(xh, wlo_ref[...]) + _dot(xl, whi) + b_ref[...]
    lane = lax.broadcasted_iota(jnp.int32, (tm, LANES), 1)
    big = jnp.int32(4 * LANES)
    is_g = (lane >= N_EXPERTS) & (lane < N_EXPERTS + N_GROUPS)
    lg = jnp.where(is_g, logits, NEG_INF)
    gmax = jnp.max(lg, axis=-1, keepdims=True)
    gidx = jnp.min(jnp.where(lg == gmax, lane - N_EXPERTS, big), axis=-1, keepdims=True)
    g_w = 1.0 / jnp.sum(jnp.where(is_g, jnp.exp(lg - gmax), 0.0), axis=-1, keepdims=True)
    in_grp = (lane < N_EXPERTS) & ((lane // EXP_PER_GROUP) == gidx)
    le = jnp.where(in_grp, logits, NEG_INF)
    e1 = jnp.max(le, axis=-1, keepdims=True)
    i1 = jnp.min(jnp.where(le == e1, lane, big), axis=-1, keepdims=True)
    le2 = jnp.where(lane == i1, NEG_INF, le)
    e2 = jnp.max(le2, axis=-1, keepdims=True)
    i2 = jnp.min(jnp.where(le2 == e2, lane, big), axis=-1, keepdims=True)
    zsum = jnp.sum(jnp.where(in_grp, jnp.exp(le - e1), 0.0), axis=-1, keepdims=True)
    p1 = 1.0 / zsum
    p2 = jnp.exp(e2 - e1) / zsum
    w1 = g_w * p1 / (p1 + p2)
    w2 = g_w * p2 / (p1 + p2)
    oh1 = lane == i1
    oh2 = lane == i2
    ohs = (oh1 | oh2).astype(BF16)
    row = lax.broadcasted_iota(jnp.int32, (tm, tm), 0)
    col = lax.broadcasted_iota(jnp.int32, (tm, tm), 1)
    before = _dot((row > col).astype(BF16), ohs) + carry_ref[0:1, :]
    r1 = jnp.sum(jnp.where(oh1, before, 0.0), axis=-1, keepdims=True)
    r2 = jnp.sum(jnp.where(oh2, before, 0.0), axis=-1, keepdims=True)
    total = carry_ref[0:1, :] + jnp.sum(ohs.astype(F32), axis=0, keepdims=True)
    carry_ref[...] = jnp.broadcast_to(total, carry_ref.shape)
    cnt_ref[...] = jnp.broadcast_to(total, cnt_ref.shape).astype(jnp.int32)
    eid_ref[...] = jnp.where(lane == 0, i1, jnp.where(lane == 1, i2, 0))
    wts_ref[...] = jnp.where(lane == 0, w1, jnp.where(lane == 1, w2, 0.0))
    rank_ref[...] = jnp.where(lane == 0, r1, jnp.where(lane == 1, r2, 0.0)).astype(jnp.int32)


ROW_SUB = D_MODEL // LANES


def _row_slab(ref, r):
    return ref.at[pl.ds(pl.multiple_of(r * ROW_SUB, ROW_SUB), ROW_SUB)]


def _col_block(n, c):
    return pl.ds(c, n, stride=ROW_SUB)


def _dispatch_kernel(padlo_ref, pend_ref, dest_ref, h_ref, rows_hbm, stage_ref, zrow_ref, zblk_ref, sem, zsem):
    i = pl.program_id(0)
    n_rows = rows_hbm.shape[0] // ROW_SUB
    for c in range(ROW_SUB):
        stage_ref[_col_block(DISPATCH_TB, c), :] = h_ref[:, c * LANES:(c + 1) * LANES]

    def start(t, carry):
        for k in range(TOP_K):
            pltpu.make_async_copy(_row_slab(stage_ref, t), _row_slab(rows_hbm, dest_ref[TOP_K * t + k]), sem).start()
        return carry

    lax.fori_loop(0, DISPATCH_TB, start, 0)

    @pl.when(i == 0)
    def _():
        zrow_ref[...] = jnp.zeros(zrow_ref.shape, F32)
        zblk_ref[...] = jnp.zeros(zblk_ref.shape, F32)

        def zero_row(r):
            return pltpu.make_async_copy(zrow_ref, _row_slab(rows_hbm, r), zsem)

        def zero_blk(b):
            dst = rows_hbm.at[pl.ds(pl.multiple_of(b * (MOE_BLOCK * ROW_SUB), MOE_BLOCK * ROW_SUB), MOE_BLOCK * ROW_SUB)]
            return pltpu.make_async_copy(zblk_ref, dst, zsem)

        def per_expert(e, carry):
            lax.fori_loop(padlo_ref[e], pend_ref[e], lambda r, c: (zero_row(r).start(), c)[1], 0)
            return carry

        lax.fori_loop(0, N_EXPERTS, per_expert, 0)
        first_free = pend_ref[N_EXPERTS - 1] // MOE_BLOCK
        lax.fori_loop(first_free, n_rows // MOE_BLOCK, lambda b, c: (zero_blk(b).start(), c)[1], 0)

        def per_expert_wait(e, carry):
            lax.fori_loop(padlo_ref[e], pend_ref[e], lambda r, c: (zero_row(0).wait(), c)[1], 0)
            return carry

        lax.fori_loop(0, N_EXPERTS, per_expert_wait, 0)
        lax.fori_loop(first_free, n_rows // MOE_BLOCK, lambda b, c: (zero_blk(0).wait(), c)[1], 0)

    def wait(t, carry):
        for k in range(TOP_K):
            pltpu.make_async_copy(_row_slab(stage_ref, 0), _row_slab(rows_hbm, 0), sem).wait()
        return carry

    lax.fori_loop(0, DISPATCH_TB, wait, 0)


def _expert_kernel(be_ref, x_ref, wg_ref, wu_ref, wd_ref, o_ref):
    del be_ref
    x = jnp.concatenate([x_ref[_col_block(MOE_BLOCK, c), :].astype(BF16) for c in range(ROW_SUB)], axis=1)
    hid = (_silu(_dot(x, wg_ref[...])) * _dot(x, wu_ref[...])).astype(BF16)
    y = _dot(hid, wd_ref[...])
    for c in range(ROW_SUB):
        o_ref[_col_block(MOE_BLOCK, c), :] = y[:, c * LANES:(c + 1) * LANES]


def _combine_kernel(dest_ref, dnext_ref, h_ref, w_ref, g_ref, b_ref, yrows_hbm, of_ref, ob_ref,
                    ybuf_ref, acc_ref, sem):
    i = pl.program_id(0)
    n = pl.num_programs(0)
    slot = i % 2

    def gather(d_ref, to_slot):
        def start(t, carry):
            for k in range(TOP_K):
                pltpu.make_async_copy(_row_slab(yrows_hbm, d_ref[TOP_K * t + k]),
                                      _row_slab(ybuf_ref.at[to_slot, k], t), sem.at[to_slot]).start()
            return carry

        lax.fori_loop(0, COMBINE_TB, start, 0)

    @pl.when(i == 0)
    def _():
        gather(dest_ref, 0)

    @pl.when(i + 1 < n)
    def _():
        gather(dnext_ref, 1 - slot)

    def wait(t, carry):
        for k in range(TOP_K):
            pltpu.make_async_copy(_row_slab(yrows_hbm, 0), _row_slab(ybuf_ref.at[slot, k], 0), sem.at[slot]).wait()
        return carry

    lax.fori_loop(0, COMBINE_TB, wait, 0)
    w = w_ref[...]
    y0_ref = ybuf_ref.at[slot, 0]
    y1_ref = ybuf_ref.at[slot, 1]
    for c in range(ROW_SUB):
        cb = _col_block(COMBINE_TB, c)
        acc_ref[:, c * LANES:(c + 1) * LANES] = y0_ref[cb, :] * w[:, 0:1] + y1_ref[cb, :] * w[:, 1:2]
    y = _layernorm_rows(ALPHA * h_ref[...] + acc_ref[...], g_ref[...], b_ref[...])
    of_ref[...] = y
    ob_ref[...] = y.astype(BF16)


def _moe_layer(hf, hb, w_rg, b_rg, w_re, b_re, w_e_gate, w_e_up, w_e_down, ln_g, ln_b):
    T, D = hf.shape
    n_rows = T * TOP_K + N_EXPERTS * MOE_BLOCK
    n_blocks = n_rows // MOE_BLOCK
    w_r = jnp.concatenate([w_re, w_rg, jnp.zeros((D, LANES - N_EXPERTS - N_GROUPS), F32)], axis=-1)
    b_r = jnp.concatenate([b_re, b_rg, jnp.zeros((LANES - N_EXPERTS - N_GROUPS,), F32)]).reshape(1, LANES)
    w_hi = w_r.astype(BF16)
    w_lo = (w_r - w_hi.astype(F32)).astype(BF16)
    const = lambda i: (0, 0)
    rows = lambda i: (i, 0)
    eid, wts, rank, cnt = pl.pallas_call(
        _router_kernel,
        grid=(T // R_TM,),
        in_specs=[pl.BlockSpec((R_TM, D), rows), pl.BlockSpec((D, LANES), const),
                  pl.BlockSpec((D, LANES), const), pl.BlockSpec((1, LANES), const)],
        out_specs=[pl.BlockSpec((R_TM, LANES), rows), pl.BlockSpec((R_TM, LANES), rows),
                   pl.BlockSpec((R_TM, LANES), rows), pl.BlockSpec((8, LANES), const)],
        out_shape=[jax.ShapeDtypeStruct((T, LANES), jnp.int32), jax.ShapeDtypeStruct((T, LANES), F32),
                   jax.ShapeDtypeStruct((T, LANES), jnp.int32), jax.ShapeDtypeStruct((8, LANES), jnp.int32)],
        scratch_shapes=[pltpu.VMEM((8, LANES), F32)],
        compiler_params=_cp("arbitrary"),
        name="moe_router",
    )(hf, w_hi, w_lo, b_r)
    counts = cnt[0, :N_EXPERTS]
    padded = (counts + MOE_BLOCK - 1) // MOE_BLOCK * MOE_BLOCK
    pends = jnp.cumsum(padded)
    pstarts = pends - padded
    blk_start = jnp.arange(n_blocks, dtype=jnp.int32) * MOE_BLOCK
    blk_exp = jnp.minimum(jnp.sum((pends[None, :] <= blk_start[:, None]).astype(jnp.int32), axis=1), N_EXPERTS - 1)
    sel = eid[:, :TOP_K, None] == jnp.arange(N_EXPERTS, dtype=jnp.int32)
    dest = (jnp.sum(jnp.where(sel, pstarts, 0), axis=-1) + rank[:, :TOP_K]).astype(jnp.int32).reshape(T * TOP_K)
    any_spec = pl.BlockSpec(memory_space=pl.ANY)
    xrows = pl.pallas_call(
        _dispatch_kernel,
        grid_spec=pltpu.PrefetchScalarGridSpec(
            num_scalar_prefetch=2,
            grid=(T // DISPATCH_TB,),
            in_specs=[pl.BlockSpec((TOP_K * DISPATCH_TB,), lambda i, *_: (i,), memory_space=pltpu.SMEM),
                      pl.BlockSpec((DISPATCH_TB, D), lambda i, *_: (i, 0))],
            out_specs=any_spec,
            scratch_shapes=[pltpu.VMEM((DISPATCH_TB * ROW_SUB, LANES), F32),
                            pltpu.VMEM((ROW_SUB, LANES), F32),
                            pltpu.VMEM((MOE_BLOCK * ROW_SUB, LANES), F32),
                            pltpu.SemaphoreType.DMA(()), pltpu.SemaphoreType.DMA(())]),
        out_shape=jax.ShapeDtypeStruct((n_rows * ROW_SUB, LANES), F32),
        compiler_params=_cp("arbitrary"),
        name="moe_dispatch",
    )((pstarts + counts).astype(jnp.int32), pends.astype(jnp.int32), dest, hf)
    slab = lambda i, *_: (i, 0)
    yrows = pl.pallas_call(
        _expert_kernel,
        grid_spec=pltpu.PrefetchScalarGridSpec(
            num_scalar_prefetch=1,
            grid=(n_blocks,),
            in_specs=[pl.BlockSpec((MOE_BLOCK * ROW_SUB, LANES), slab),
                      pl.BlockSpec((None, D, D_FF), lambda i, be: (be[i], 0, 0)),
                      pl.BlockSpec((None, D, D_FF), lambda i, be: (be[i], 0, 0)),
                      pl.BlockSpec((None, D_FF, D), lambda i, be: (be[i], 0, 0))],
            out_specs=pl.BlockSpec((MOE_BLOCK * ROW_SUB, LANES), slab)),
        out_shape=jax.ShapeDtypeStruct((n_rows * ROW_SUB, LANES), F32),
        compiler_params=_cp("arbitrary"),
        name="moe_experts",
    )(blk_exp, xrows, w_e_gate.astype(BF16), w_e_up.astype(BF16), w_e_down.astype(BF16))
    n_steps = T // COMBINE_TB
    dspec = lambda f: pl.BlockSpec((TOP_K * COMBINE_TB,), f, memory_space=pltpu.SMEM)
    return pl.pallas_call(
        _combine_kernel,
        grid=(n_steps,),
        in_specs=[dspec(lambda i: (i,)), dspec(lambda i: (jnp.minimum(i + 1, n_steps - 1),)),
                  pl.BlockSpec((COMBINE_TB, D), rows), pl.BlockSpec((COMBINE_TB, LANES), rows),
                  pl.BlockSpec((1, D), const), pl.BlockSpec((1, D), const), any_spec],
        out_specs=[pl.BlockSpec((COMBINE_TB, D), rows), pl.BlockSpec((COMBINE_TB, D), rows)],
        out_shape=[jax.ShapeDtypeStruct((T, D), F32), jax.ShapeDtypeStruct((T, D), BF16)],
        scratch_shapes=[pltpu.VMEM((2, TOP_K, COMBINE_TB * ROW_SUB, LANES), F32),
                        pltpu.VMEM((COMBINE_TB, D), F32), pltpu.SemaphoreType.DMA((2,))],
        compiler_params=_cp("arbitrary"),
        name="moe_combine_ln2",
    )(dest, dest, hf, wts, ln_g.reshape(1, D), ln_b.reshape(1, D), yrows)


def kernel(x, ln_in_g, ln_in_b, rel_bias, w_in, g_cq, w_uq, g_ckv, w_ukv, w_dw_c, b_dw_c, ln_c_g, ln_c_b,
           w_conv_d, b_conv_d, a_log_f, a_log_b, dt_bias_f, dt_bias_b, d_skip, g_norm_d, w_br, w_gate, b_gate,
           w_out, ln1_g, ln1_b, w_rg, b_rg, w_re, b_re, w_e_gate, w_e_up, w_e_down, ln2_g, ln2_b):
    B, S, D = x.shape
    T = B * S
    hf, hb = _layernorm(x.reshape(T, D), ln_in_g, ln_in_b)
    a_bias = _mixer_a_bias(rel_bias, S)
    for l in range(DEPTH):
        w_a, w_r = _in_proj_weights(w_in[l])
        qkv = _matmul(hb, w_a.astype(BF16), BF16, 512, 3 * A_WIDTH, "in_proj_a")
        rest = _matmul(hb, w_r.astype(BF16), F32, 512, R_WIDTH // 2, "in_proj_rest")
        y_a = _mixer_a(qkv, a_bias, B, S)
        y_b = _mixer_b(rest, g_cq[l], w_uq[l], g_ckv[l], w_ukv[l], B, S)
        y_c = _mixer_c(rest, w_dw_c[l], b_dw_c[l], ln_c_g[l], ln_c_b[l], B, S)
        y_d = _mixer_d(rest, w_conv_d[l], b_conv_d[l], a_log_f[l], a_log_b[l], dt_bias_f[l], dt_bias_b[l],
                       d_skip[l], g_norm_d[l], B, S)
        h1f, h1b = _merge(hb, hf, (y_a, y_b, y_c, y_d), w_gate[l], b_gate[l], w_br[l], w_out[l], ln1_g[l], ln1_b[l])
        hf, hb = _moe_layer(h1f, h1b, w_rg[l], b_rg[l], w_re[l], b_re[l], w_e_gate[l], w_e_up[l], w_e_down[l],
                            ln2_g[l], ln2_b[l])
    return hf.reshape(B, S, D)
```

```python
import functools

import numpy as np
import jax
import jax.numpy as jnp
from jax import lax
from jax.experimental import pallas as pl
from jax.experimental.pallas import tpu as pltpu

F32 = jnp.float32
BF16 = jnp.bfloat16

D_MODEL = 2048
DEPTH = 2
A_HEADS = 8
A_HEAD_DIM = 64
A_WIDTH = A_HEADS * A_HEAD_DIM
A_PATTERNS = ((128, 1), (512, 4), (2048, 16))
A_BAND = 64
REL_BUCKETS = 32
REL_MAX_DIST = 1024
B_HEADS = 8
B_NOPE = 64
B_ROPE = 32
B_V = 64
B_Q_LORA = 512
B_KV_LORA = 256
ROPE_THETA = 10000.0
C_CH = 512
C_KERNEL = 31
D_HEADS = 8
D_HEAD_DIM = 64
D_INNER = D_HEADS * D_HEAD_DIM
D_STATE = 128
D_GROUPS = 2
D_CONV = 5
D_CHUNK = 128
N_BRANCH = 4
BRANCH_W = 512
N_GROUPS = 4
EXP_PER_GROUP = 8
N_EXPERTS = N_GROUPS * EXP_PER_GROUP
TOP_K = 2
D_FF = 512
MOE_BLOCK = 128
ALPHA = (2 * DEPTH) ** 0.25
EPS = 1e-5
NEG_INF = -1e30

LANES = 128
R_GLU, R_XBC, R_CQ, R_Z, R_CKV, R_KR, R_DT = 0, 1024, 2048, 2560, 3072, 3328, 3456
R_WIDTH = 3584
VMEM_LIMIT = 56 * 1024 * 1024


def _cp(*sem):
    return pltpu.CompilerParams(dimension_semantics=sem, vmem_limit_bytes=VMEM_LIMIT)


def _dot(a, b):
    return jnp.dot(a, b, preferred_element_type=F32)


def _dot_nt(a, b):
    return lax.dot_general(a, b, (((1,), (1,)), ((), ())), preferred_element_type=F32)


def _split3(x):
    hi = x.astype(BF16)
    r1 = x - hi.astype(F32)
    mid = r1.astype(BF16)
    lo = (r1 - mid.astype(F32)).astype(BF16)
    return hi, mid, lo


def _layernorm_rows(x, g, b):
    mu = jnp.mean(x, axis=-1, keepdims=True)
    xc = x - mu
    var = jnp.mean(xc * xc, axis=-1, keepdims=True)
    return xc * lax.rsqrt(var + EPS) * g + b


def _sigmoid(x):
    return 1.0 / (1.0 + jnp.exp(-x))


def _silu(x):
    return x * _sigmoid(x)


def _ln_kernel(x_ref, g_ref, b_ref, of_ref, ob_ref):
    y = _layernorm_rows(x_ref[...], g_ref[...], b_ref[...])
    of_ref[...] = y
    ob_ref[...] = y.astype(BF16)


def _layernorm(x, g, b, tm=256):
    T, D = x.shape
    return pl.pallas_call(
        _ln_kernel,
        grid=(T // tm,),
        in_specs=[pl.BlockSpec((tm, D), lambda i: (i, 0)),
                  pl.BlockSpec((1, D), lambda i: (0, 0)),
                  pl.BlockSpec((1, D), lambda i: (0, 0))],
        out_specs=[pl.BlockSpec((tm, D), lambda i: (i, 0)),
                   pl.BlockSpec((tm, D), lambda i: (i, 0))],
        out_shape=[jax.ShapeDtypeStruct((T, D), F32), jax.ShapeDtypeStruct((T, D), BF16)],
        compiler_params=_cp("parallel"),
        name="ln_in",
    )(x, g.reshape(1, D), b.reshape(1, D))


def _mm_kernel(x_ref, w_ref, o_ref):
    o_ref[...] = _dot(x_ref[...], w_ref[...]).astype(o_ref.dtype)


def _matmul(x, w, out_dtype, tm, tn, name):
    M, K = x.shape
    N = w.shape[1]
    return pl.pallas_call(
        _mm_kernel,
        grid=(N // tn, M // tm),
        in_specs=[pl.BlockSpec((tm, K), lambda j, i: (i, 0)),
                  pl.BlockSpec((K, tn), lambda j, i: (0, j))],
        out_specs=pl.BlockSpec((tm, tn), lambda j, i: (i, j)),
        out_shape=jax.ShapeDtypeStruct((M, N), out_dtype),
        compiler_params=_cp("parallel", "parallel"),
        name=name,
    )(x, w)


def _t5_bucket(rel):
    half = REL_BUCKETS // 2
    max_exact = half // 2
    n = np.abs(rel)
    large = max_exact + (np.log(np.maximum(n, 1) / max_exact) / np.log(REL_MAX_DIST / max_exact)
                         * (half - max_exact)).astype(np.int32)
    large = np.minimum(large, half - 1)
    return (rel > 0).astype(np.int32) * half + np.where(n < max_exact, n, large)


def _a_window(L):
    return min(2 * LANES, L)


def _a_bias_tiles(rel_bias, d, L):
    W = _a_window(L)
    offs = (0,) if L == LANES else (0, -A_BAND, -2 * A_BAND)
    qi = np.arange(LANES)[:, None]
    kj = np.arange(W)[None, :]
    rel = np.stack([kj - qi + off for off in offs], axis=0)
    valid = np.abs(rel) <= A_BAND
    onehot = (jnp.asarray(_t5_bucket(rel * d), jnp.int32)[..., None] == jnp.arange(REL_BUCKETS)).astype(F32)
    b = jnp.einsum('vqkb,bh->vhqk', onehot, rel_bias.astype(F32), precision=lax.Precision.HIGHEST)
    return jnp.where(valid[:, None], b, NEG_INF)


def _attn_a_kernel(q_ref, k_ref, v_ref, bias_ref, *rest, L, nqb, final):
    W = _a_window(L)
    i = pl.program_id(2)
    if nqb == 1:
        ws = 0
    else:
        ws = pl.multiple_of(jnp.clip(i * LANES - A_BAND, 0, L - W), A_BAND)
    q = q_ref[...]
    kw = k_ref[pl.ds(ws, W), :]
    vw = v_ref[pl.ds(ws, W), :]
    lane = lax.broadcasted_iota(jnp.int32, (LANES, LANES), 1)
    lse_tile = jnp.zeros((LANES, LANES), F32)
    outs = []
    for h in range(A_HEADS):
        sl = slice(h * A_HEAD_DIM, (h + 1) * A_HEAD_DIM)
        s = _dot_nt(q[:, sl], kw[:, sl]) * (A_HEAD_DIM ** -0.5) + bias_ref[h]
        m = jnp.max(s, axis=-1, keepdims=True)
        p = jnp.exp(s - m)
        den = jnp.sum(p, axis=-1, keepdims=True)
        o = _dot(p.astype(BF16), vw[:, sl]) / den
        lse = m + jnp.log(den)
        outs.append(o)
        lse_tile = jnp.where(lane == h, lse, lse_tile)
    if not final:
        o_ref, lse_ref = rest
        for h in range(A_HEADS):
            o_ref[:, h * A_HEAD_DIM:(h + 1) * A_HEAD_DIM] = outs[h]
        lse_ref[...] = lse_tile
    else:
        o1_ref, l1_ref, o2_ref, l2_ref, y_ref = rest
        l1 = l1_ref[...]
        l2 = l2_ref[...]
        mx = jnp.maximum(jnp.maximum(l1, l2), lse_tile)
        e1 = jnp.exp(l1 - mx)
        e2 = jnp.exp(l2 - mx)
        e3 = jnp.exp(lse_tile - mx)
        tot = e1 + e2 + e3
        w1 = e1 / tot
        w2 = e2 / tot
        w3 = e3 / tot
        for h in range(A_HEADS):
            sl = slice(h * A_HEAD_DIM, (h + 1) * A_HEAD_DIM)
            y = (w1[:, h:h + 1] * o1_ref[:, sl] + w2[:, h:h + 1] * o2_ref[:, sl]
                 + w3[:, h:h + 1] * outs[h])
            y_ref[:, sl] = y.astype(y_ref.dtype)


def _attn_a_call(qkv, bias, B, S, d, prev=None):
    L = S // d
    nqb = L // LANES
    W = _a_window(L)
    nvar = bias.shape[0]
    qkv_v = qkv.reshape(B, L, d * 3 * A_WIDTH)

    def var_of(i):
        if nvar == 1:
            return 0
        return jnp.where(i == 0, 0, jnp.where(i == nqb - 1, 2, 1))

    in_specs = [
        pl.BlockSpec((None, LANES, A_WIDTH), lambda b, r, i: (b, i, 3 * r)),
        pl.BlockSpec((None, L, A_WIDTH), lambda b, r, i: (b, 0, 3 * r + 1)),
        pl.BlockSpec((None, L, A_WIDTH), lambda b, r, i: (b, 0, 3 * r + 2)),
        pl.BlockSpec((None, A_HEADS, LANES, W), lambda b, r, i: (var_of(i), 0, 0, 0)),
    ]
    args = [qkv_v, qkv_v, qkv_v, bias]
    final = prev is not None
    if final:
        assert d == 1
        for o_p, l_p in ((prev[0], prev[1]), (prev[2], prev[3])):
            in_specs.append(pl.BlockSpec((None, LANES, A_WIDTH), lambda b, r, i: (b, i, 0)))
            in_specs.append(pl.BlockSpec((None, LANES, LANES), lambda b, r, i: (b, i, 0)))
            args += [o_p.reshape(B, S, A_WIDTH), l_p.reshape(B, S, LANES)]
        out_specs = pl.BlockSpec((None, LANES, A_WIDTH), lambda b, r, i: (b, i, 0))
        out_shape = jax.ShapeDtypeStruct((B, S, A_WIDTH), BF16)
    else:
        out_specs = [pl.BlockSpec((None, LANES, A_WIDTH), lambda b, r, i: (b, i, r)),
                     pl.BlockSpec((None, LANES, LANES), lambda b, r, i: (b, i, r))]
        out_shape = [jax.ShapeDtypeStruct((B, L, d * A_WIDTH), F32),
                     jax.ShapeDtypeStruct((B, L, d * LANES), F32)]
    out = pl.pallas_call(
        functools.partial(_attn_a_kernel, L=L, nqb=nqb, final=final),
        grid=(B, d, nqb),
        in_specs=in_specs,
        out_specs=out_specs,
        out_shape=out_shape,
        compiler_params=_cp("parallel", "parallel", "arbitrary"),
        name=f"attn_a_d{d}",
    )(*args)
    if final:
        return out.reshape(B * S, A_WIDTH)
    return out[0].reshape(B * S, A_WIDTH), out[1].reshape(B * S, LANES)


def _mixer_a_bias(rel_bias, S):
    return tuple(_a_bias_tiles(rel_bias, d, S // d) for _, d in A_PATTERNS)


def _mixer_a(qkv, bias, B, S):
    (_, d1), (_, d4), (_, d16) = A_PATTERNS
    o16, l16 = _attn_a_call(qkv, bias[2], B, S, d16)
    o4, l4 = _attn_a_call(qkv, bias[1], B, S, d4)
    return _attn_a_call(qkv, bias[0], B, S, d1, prev=(o16, l16, o4, l4))


MLA_Q_SCALE = float((B_NOPE + B_ROPE) ** -0.5 * np.log2(np.e))


def _mla_proj_kernel(cq_ref, ckv_ref, kr_ref, gq_ref, gkv_ref, wqm_ref, wqs_ref, wk_ref, wv_ref, vone_ref,
                     ek_ref, cosq_ref, sinq_ref, csk_ref, q_ref, k_ref, v_ref):
    cq = cq_ref[...]
    xq = (cq * lax.rsqrt(jnp.mean(cq * cq, axis=-1, keepdims=True) + EPS) * gq_ref[...]).astype(BF16)
    ckv = ckv_ref[...]
    xkv = (ckv * lax.rsqrt(jnp.mean(ckv * ckv, axis=-1, keepdims=True) + EPS) * gkv_ref[...]).astype(BF16)
    qm = _dot(xq, wqm_ref[...])
    qs = _dot(xq, wqs_ref[...])
    cosq = cosq_ref[...] * MLA_Q_SCALE
    sinq = sinq_ref[...] * MLA_Q_SCALE
    t = kr_ref[...] * csk_ref[...]
    t_hi = t.astype(BF16)
    t_lo = (t - t_hi.astype(F32)).astype(BF16)
    kk = _dot(xkv, wk_ref[...]) + _dot(t_hi, ek_ref[...]) + _dot(t_lo, ek_ref[...])
    for h in range(B_HEADS):
        sl = slice(h * LANES, (h + 1) * LANES)
        q_ref[:, sl] = (qm[:, sl] * cosq + qs[:, sl] * sinq).astype(BF16)
    k_ref[...] = kk.astype(BF16)
    v_ref[...] = (_dot(xkv, wv_ref[...]) + vone_ref[...]).astype(BF16)


def _mla_attn_kernel(q_ref, k_ref, v_ref, o_ref):
    outs = []
    for hh in range(2):
        sl = slice(hh * LANES, (hh + 1) * LANES)
        s = _dot_nt(q_ref[:, sl], k_ref[:, sl])
        p = jnp.exp2(s - jnp.max(s, axis=-1, keepdims=True))
        outs.append(_dot(p.astype(BF16), v_ref[:, sl]))
    lane = lax.broadcasted_iota(jnp.int32, outs[0].shape, 1)
    acc = jnp.where(lane < B_V, outs[0], outs[1])
    den = pltpu.roll(jnp.where(lane < B_V, outs[1], outs[0]), B_V, axis=1)
    o_ref[...] = (acc / den).astype(o_ref.dtype)


def _mla_tables(S):
    inv_freq = ROPE_THETA ** (-jnp.arange(0, B_ROPE, 2, dtype=F32) / B_ROPE)
    ang = jnp.arange(S, dtype=F32)[:, None] * inv_freq[None]
    cos, sin = jnp.cos(ang), jnp.sin(ang)
    cos2 = jnp.concatenate([cos, cos], axis=-1)
    sin2 = jnp.concatenate([sin, sin], axis=-1)
    ones = jnp.ones((S, B_NOPE), F32)
    zn = jnp.zeros((S, B_NOPE), F32)
    zp = jnp.zeros((S, LANES - B_NOPE - B_ROPE), F32)
    cosq = jnp.concatenate([ones, cos2, zp], axis=-1)
    sinq = jnp.concatenate([zn, sin2, zp], axis=-1)
    csk = jnp.concatenate([cos2, sin2, jnp.zeros((S, LANES - 2 * B_ROPE), F32)], axis=-1)
    return cosq, sinq, csk


def _swap_cols(w):
    half = w.shape[-1] // 2
    return jnp.concatenate([-w[..., half:], w[..., :half]], axis=-1)


def _mla_weights(w_uq, w_ukv):
    dq = B_NOPE + B_ROPE
    wq = w_uq.reshape(B_Q_LORA, B_HEADS, dq)
    zpad = jnp.zeros((B_Q_LORA, B_HEADS, LANES - dq), F32)
    wqm = jnp.concatenate([wq, zpad], axis=-1).reshape(B_Q_LORA, B_HEADS * LANES)
    wqs = jnp.concatenate([jnp.zeros((B_Q_LORA, B_HEADS, B_NOPE), F32), _swap_cols(wq[..., B_NOPE:]), zpad],
                          axis=-1).reshape(B_Q_LORA, B_HEADS * LANES)
    wkv = w_ukv.reshape(B_KV_LORA, B_HEADS, B_NOPE + B_V)
    wk = jnp.concatenate([wkv[..., :B_NOPE], jnp.zeros((B_KV_LORA, B_HEADS, LANES - B_NOPE), F32)],
                         axis=-1).reshape(B_KV_LORA, B_HEADS * LANES)
    zv = jnp.zeros((B_KV_LORA, B_HEADS // 2, LANES - B_V), F32)
    wv_h = wkv[..., B_NOPE:]
    wv = jnp.stack([jnp.concatenate([wv_h[:, 0::2], zv], axis=-1),
                    jnp.concatenate([zv, wv_h[:, 1::2]], axis=-1)], axis=2).reshape(B_KV_LORA, B_HEADS * LANES)
    lane_in_pair = np.arange(B_HEADS * LANES) % (2 * LANES)
    vone = jnp.asarray(((lane_in_pair >= B_V) & (lane_in_pair < LANES + B_V)).astype(np.float32)).reshape(1, -1)
    ek = np.zeros((LANES, B_HEADS, LANES), np.float32)
    for j in range(B_ROPE):
        ek[j, :, B_NOPE + j] = 1.0
        ek[B_ROPE + j, :, B_NOPE + j] = 1.0
    ek = jnp.asarray(ek.reshape(LANES, B_HEADS * LANES))
    return wqm.astype(BF16), wqs.astype(BF16), wk.astype(BF16), wv.astype(BF16), vone, ek.astype(BF16)


def _mixer_b(rest, g_cq, w_uq, g_ckv, w_ukv, B, S, tm=512, tq=256):
    T = B * S
    wqm, wqs, wk, wv, vone, ek = _mla_weights(w_uq, w_ukv)
    cosq, sinq, csk = _mla_tables(S)
    nst = S // tm
    QW = B_HEADS * LANES
    const = lambda i: (0, 0)
    pos = lambda i: (i % nst, 0)
    q, k, v = pl.pallas_call(
        _mla_proj_kernel,
        grid=(T // tm,),
        in_specs=[pl.BlockSpec((tm, B_Q_LORA), lambda i: (i, R_CQ // B_Q_LORA)),
                  pl.BlockSpec((tm, B_KV_LORA), lambda i: (i, R_CKV // B_KV_LORA)),
                  pl.BlockSpec((tm, LANES), lambda i: (i, R_KR // LANES)),
                  pl.BlockSpec((1, B_Q_LORA), const),
                  pl.BlockSpec((1, B_KV_LORA), const),
                  pl.BlockSpec((B_Q_LORA, QW), const),
                  pl.BlockSpec((B_Q_LORA, QW), const),
                  pl.BlockSpec((B_KV_LORA, QW), const),
                  pl.BlockSpec((B_KV_LORA, QW), const),
                  pl.BlockSpec((1, QW), const),
                  pl.BlockSpec((LANES, QW), const),
                  pl.BlockSpec((tm, LANES), pos),
                  pl.BlockSpec((tm, LANES), pos),
                  pl.BlockSpec((tm, LANES), pos)],
        out_specs=[pl.BlockSpec((tm, QW), lambda i: (i, 0)),
                   pl.BlockSpec((tm, QW), lambda i: (i, 0)),
                   pl.BlockSpec((tm, QW), lambda i: (i, 0))],
        out_shape=[jax.ShapeDtypeStruct((T, QW), BF16), jax.ShapeDtypeStruct((T, QW), BF16),
                   jax.ShapeDtypeStruct((T, QW), BF16)],
        compiler_params=_cp("parallel"),
        name="mla_proj",
    )(rest, rest, rest, g_cq.reshape(1, -1), g_ckv.reshape(1, -1), wqm, wqs, wk, wv, vone, ek, cosq, sinq, csk)
    y = pl.pallas_call(
        _mla_attn_kernel,
        grid=(B, B_HEADS // 2, S // tq),
        in_specs=[pl.BlockSpec((None, tq, 2 * LANES), lambda b, hp, i: (b, i, hp)),
                  pl.BlockSpec((None, S, 2 * LANES), lambda b, hp, i: (b, 0, hp)),
                  pl.BlockSpec((None, S, 2 * LANES), lambda b, hp, i: (b, 0, hp))],
        out_specs=pl.BlockSpec((None, tq, 2 * B_V), lambda b, hp, i: (b, i, hp)),
        out_shape=jax.ShapeDtypeStruct((B, S, B_HEADS * B_V), BF16),
        compiler_params=_cp("parallel", "parallel", "arbitrary"),
        name="mla_attn",
    )(q.reshape(B, S, QW), k.reshape(B, S, QW), v.reshape(B, S, QW))
    return y.reshape(T, B_HEADS * B_V)


C_PAD = 16
C_ROWS = 128


SUBLANES = 8


def _tap_span(first, ntaps, rows):
    return rows + ((first + ntaps - 1) // SUBLANES) * SUBLANES


def _depthwise_taps(win_ref, sh_ref, w_ref, bias, ls, first, ntaps, rows):
    acc = jnp.broadcast_to(bias, (rows, LANES))
    span = _tap_span(first, ntaps, rows)
    for ph in range(SUBLANES):
        taps = [j for j in range(ntaps) if (first + j) % SUBLANES == ph]
        if not taps:
            continue
        sh_ref[0:span, :] = win_ref[ph:ph + span, ls]
        for j in taps:
            a = (first + j) // SUBLANES * SUBLANES
            acc = acc + w_ref[j:j + 1, ls] * sh_ref[a:a + rows, :]
    return acc


def _fill_window(win_ref, load_rows, r0, i, nblk, rows, pad):
    width = win_ref.shape[1]
    win_ref[pad:pad + rows, :] = load_rows(r0, rows)

    @pl.when(i > 0)
    def _():
        win_ref[0:pad, :] = load_rows(r0 - pad, pad)

    @pl.when(i == 0)
    def _():
        win_ref[0:pad, :] = jnp.zeros((pad, width), F32)

    @pl.when(i < nblk - 1)
    def _():
        win_ref[pad + rows:pad + rows + pad, :] = load_rows(r0 + rows, pad)

    @pl.when(i == nblk - 1)
    def _():
        win_ref[pad + rows:pad + rows + pad, :] = jnp.zeros((pad, width), F32)


def _conv_c_kernel(glu_ref, w_ref, b_ref, g_ref, beta_ref, o_ref, win_ref, sh_ref, acc_ref, *, S):
    i = pl.program_id(1)
    r0 = pl.multiple_of(i * C_ROWS, C_ROWS)

    def glu_rows(start, n):
        rs = pl.ds(pl.multiple_of(start, C_PAD), n)
        return glu_ref[rs, 0:C_CH] * _sigmoid(glu_ref[rs, C_CH:2 * C_CH])

    _fill_window(win_ref, glu_rows, r0, i, S // C_ROWS, C_ROWS, C_PAD)
    first = C_PAD - C_KERNEL // 2
    for lb in range(C_CH // LANES):
        ls = slice(lb * LANES, (lb + 1) * LANES)
        acc_ref[:, ls] = _depthwise_taps(win_ref, sh_ref, w_ref, b_ref[:, ls], ls, first, C_KERNEL, C_ROWS)
    y = _layernorm_rows(acc_ref[...], g_ref[...], beta_ref[...])
    o_ref[...] = _silu(y).astype(o_ref.dtype)


def _mixer_c(rest, w_dw, b_dw, ln_g, ln_b, B, S):
    T = B * S
    const = lambda b, i: (0, 0)
    y = pl.pallas_call(
        functools.partial(_conv_c_kernel, S=S),
        grid=(B, S // C_ROWS),
        in_specs=[pl.BlockSpec((None, S, 2 * C_CH), lambda b, i: (b, 0, R_GLU // (2 * C_CH))),
                  pl.BlockSpec((C_KERNEL, C_CH), const),
                  pl.BlockSpec((1, C_CH), const),
                  pl.BlockSpec((1, C_CH), const),
                  pl.BlockSpec((1, C_CH), const)],
        out_specs=pl.BlockSpec((None, C_ROWS, C_CH), lambda b, i: (b, i, 0)),
        out_shape=jax.ShapeDtypeStruct((B, S, C_CH), BF16),
        scratch_shapes=[pltpu.VMEM((C_ROWS + 2 * C_PAD, C_CH), F32),
                        pltpu.VMEM((_tap_span(C_PAD - C_KERNEL // 2, C_KERNEL, C_ROWS), LANES), F32),
                        pltpu.VMEM((C_ROWS, C_CH), F32)],
        compiler_params=_cp("parallel", "parallel"),
        name="conformer_conv",
    )(rest.reshape(B, S, R_WIDTH), w_dw, b_dw.reshape(1, -1), ln_g.reshape(1, -1), ln_b.reshape(1, -1))
    return y.reshape(T, C_CH)


D_PAD = 8
XBC_W = D_INNER + 2 * D_GROUPS * D_STATE
N_PAIR = D_HEADS // 2


def _pair_expand(v, first):
    lane = lax.broadcasted_iota(jnp.int32, (v.shape[0], LANES), 1)
    lo = jnp.broadcast_to(v[:, first:first + 1], (v.shape[0], LANES))
    hi = jnp.broadcast_to(v[:, first + 1:first + 2], (v.shape[0], LANES))
    return jnp.where(lane < D_HEAD_DIM, lo, hi)


def _ssd_kernel(xbc_ref, z_ref, dt_ref, wc_ref, bc_ref, alog_ref, dtb_ref, dskip_ref, gn_ref, o_ref,
                win_ref, sh_ref, xc_ref, a_ref, dtv_ref, y_ref, st_ref, *, S):
    Q = D_CHUNK
    nchunk = S // Q
    N = D_STATE
    bm0 = D_INNER
    cm0 = D_INNER + D_GROUPS * N

    def conv_body(c, carry):
        r0 = pl.multiple_of(c * Q, Q)
        _fill_window(win_ref, lambda st, n: xbc_ref[pl.ds(pl.multiple_of(st, D_PAD), n), :], r0, c, nchunk, Q, D_PAD)
        for lb in range(XBC_W // LANES):
            ls = slice(lb * LANES, (lb + 1) * LANES)
            acc = _depthwise_taps(win_ref, sh_ref, wc_ref, bc_ref[:, ls], ls, D_PAD - D_CONV // 2, D_CONV, Q)
            xc_ref[pl.ds(r0, Q), ls] = _silu(acc)
        return carry

    lax.fori_loop(0, nchunk, conv_body, 0)

    lane1 = lax.broadcasted_iota(jnp.int32, (1, LANES), 1)
    a_row = jnp.where(lane1 < 2 * D_HEADS, -jnp.exp(alog_ref[...]), 0.0)
    xdt = dt_ref[...] + dtb_ref[...]
    dtv = jnp.maximum(xdt, 0.0) + jnp.log(1.0 + jnp.exp(-jnp.abs(xdt)))
    dtv_ref[...] = dtv
    a_ref[...] = dtv * a_row

    row = lax.broadcasted_iota(jnp.int32, (Q, Q), 0)
    col = lax.broadcasted_iota(jnp.int32, (Q, Q), 1)
    tril = row >= col
    triu = col >= row
    lane = col

    def scan_chunk(c, lower, off, finalize):
        r0 = pl.multiple_of(c * Q, Q)
        rows = pl.ds(r0, Q)
        mask = tril if lower else triu
        tri = mask.astype(BF16)
        a_hi, a_mid, a_lo = _split3(a_ref[rows, :])
        cs = _dot(tri, a_hi) + _dot(tri, a_mid) + _dot(tri, a_lo)
        cs_t = cs.T
        ecs = jnp.exp(cs)
        edge = Q - 1 if lower else 0
        edec = jnp.exp(cs[edge:edge + 1, :] - cs)
        dt_c = dtv_ref[rows, :]
        for g in range(D_GROUPS):
            bg = xc_ref[rows, bm0 + g * N:bm0 + (g + 1) * N]
            cg = xc_ref[rows, cm0 + g * N:cm0 + (g + 1) * N].astype(BF16)
            cb = _dot_nt(cg, bg.astype(BF16))
            bg_t = bg.T.astype(BF16)
            for pp in range(N_PAIR // D_GROUPS):
                p = g * (N_PAIR // D_GROUPS) + pp
                ps = slice(p * LANES, (p + 1) * LANES)
                xdt_p = xc_ref[rows, ps] * _pair_expand(dt_c, off + 2 * p)
                ms = []
                for hh in range(2):
                    k = off + 2 * p + hh
                    diff = jnp.broadcast_to(cs[:, k:k + 1], (Q, Q)) - cs_t[k:k + 1, :]
                    ms.append((jnp.exp(jnp.where(mask, diff, NEG_INF)) * cb).astype(BF16))
                x_lo = jnp.where(lane < D_HEAD_DIM, xdt_p, 0.0).astype(BF16)
                x_hi = jnp.where(lane >= D_HEAD_DIM, xdt_p, 0.0).astype(BF16)
                y_intra = _dot(jnp.concatenate(ms, axis=1), jnp.concatenate([x_lo, x_hi], axis=0))
                hp = st_ref[p]
                ecs_p = _pair_expand(ecs, off + 2 * p)
                y_new = y_intra + _dot(cg, hp.astype(BF16)) * ecs_p
                if lower:
                    y_ref[rows, ps] = y_new
                else:
                    y_ref[rows, ps] = y_ref[rows, ps] + y_new
                xs_p = (xdt_p * _pair_expand(edec, off + 2 * p)).astype(BF16)
                st_ref[p] = hp * ecs_p[edge:edge + 1, :] + _dot(bg_t, xs_p)
        if finalize:
            y = y_ref[rows, :] + xc_ref[rows, 0:D_INNER] * dskip_ref[...]
            gated = y * _silu(z_ref[rows, :])
            out = gated * lax.rsqrt(jnp.mean(gated * gated, axis=-1, keepdims=True) + EPS) * gn_ref[...]
            o_ref[rows, :] = out.astype(o_ref.dtype)

    st_ref[...] = jnp.zeros(st_ref.shape, F32)

    def fwd_body(c, carry):
        scan_chunk(c, True, 0, False)
        return carry

    lax.fori_loop(0, nchunk, fwd_body, 0)
    st_ref[...] = jnp.zeros(st_ref.shape, F32)

    def bwd_body(k, carry):
        scan_chunk(nchunk - 1 - k, False, D_HEADS, True)
        return carry

    lax.fori_loop(0, nchunk, bwd_body, 0)


def _mixer_d(rest, w_conv, b_conv, a_log_f, a_log_b, dt_bias_f, dt_bias_b, d_skip, g_norm, B, S):
    T = B * S
    pad16 = lambda f, b: jnp.concatenate([f, b, jnp.zeros((LANES - 2 * D_HEADS,), F32)]).reshape(1, LANES)
    const = lambda b: (0, 0)
    y = pl.pallas_call(
        functools.partial(_ssd_kernel, S=S),
        grid=(B,),
        in_specs=[pl.BlockSpec((None, S, XBC_W), lambda b: (b, 0, R_XBC // XBC_W)),
                  pl.BlockSpec((None, S, D_INNER), lambda b: (b, 0, R_Z // D_INNER)),
                  pl.BlockSpec((None, S, LANES), lambda b: (b, 0, R_DT // LANES)),
                  pl.BlockSpec((D_CONV, XBC_W), const),
                  pl.BlockSpec((1, XBC_W), const),
                  pl.BlockSpec((1, LANES), const),
                  pl.BlockSpec((1, LANES), const),
                  pl.BlockSpec((1, D_INNER), const),
                  pl.BlockSpec((1, D_INNER), const)],
        out_specs=pl.BlockSpec((None, S, D_INNER), lambda b: (b, 0, 0)),
        out_shape=jax.ShapeDtypeStruct((B, S, D_INNER), BF16),
        scratch_shapes=[pltpu.VMEM((D_CHUNK + 2 * D_PAD, XBC_W), F32),
                        pltpu.VMEM((_tap_span(D_PAD - D_CONV // 2, D_CONV, D_CHUNK), LANES), F32),
                        pltpu.VMEM((S, XBC_W), F32),
                        pltpu.VMEM((S, LANES), F32),
                        pltpu.VMEM((S, LANES), F32),
                        pltpu.VMEM((S, D_INNER), F32),
                        pltpu.VMEM((N_PAIR, D_STATE, LANES), F32)],
        compiler_params=_cp("parallel"),
        name="ssd_mixer",
    )(rest.reshape(B, S, R_WIDTH), rest.reshape(B, S, R_WIDTH), rest.reshape(B, S, R_WIDTH),
      w_conv, b_conv.reshape(1, -1), pad16(a_log_f, a_log_b), pad16(dt_bias_f, dt_bias_b),
      jnp.repeat(d_skip, D_HEAD_DIM).reshape(1, -1), g_norm.reshape(1, -1))
    return y.reshape(T, D_INNER)


def _in_proj_weights(w_in_l):
    o = np.cumsum((0, A_WIDTH, A_WIDTH, A_WIDTH, B_Q_LORA, B_KV_LORA, B_ROPE, 2 * C_CH,
                   D_INNER, D_INNER, D_GROUPS * D_STATE, D_GROUPS * D_STATE, 2 * D_HEADS)).tolist()
    seg = lambda n: w_in_l[:, o[n]:o[n + 1]]
    w_a = w_in_l[:, :o[3]]
    cq, ckv, kr, glu, z, xs, bm, cm, dt = (seg(n) for n in range(3, 12))
    zeros = lambda n: jnp.zeros((w_in_l.shape[0], n), w_in_l.dtype)
    w_r = jnp.concatenate([glu, xs, bm, cm, cq, z, ckv,
                           kr, _swap_cols(kr), zeros(LANES - 2 * B_ROPE),
                           dt, zeros(LANES - 2 * D_HEADS)], axis=-1)
    assert w_r.shape[1] == R_WIDTH
    return w_a, w_r


def _merge_gate_kernel(h_ref, ya_ref, yb_ref, yc_ref, yd_ref, wg_ref, bg_ref, wbr_ref, o_ref):
    h = h_ref[...]
    acc = None
    for i, y_ref in enumerate((ya_ref, yb_ref, yc_ref, yd_ref)):
        gate = _sigmoid(_dot(h, wg_ref[i]) + bg_ref[i])
        term = gate * _dot(y_ref[...], wbr_ref[i])
        acc = term if acc is None else acc + term
    o_ref[...] = acc.astype(o_ref.dtype)


def _out_ln_kernel(m_ref, w_ref, h_ref, g_ref, b_ref, of_ref, ob_ref):
    y = _layernorm_rows(ALPHA * h_ref[...] + _dot(m_ref[...], w_ref[...]), g_ref[...], b_ref[...])
    of_ref[...] = y
    ob_ref[...] = y.astype(BF16)


def _merge(hb, hf, branches, w_gate, b_gate, w_br, w_out, ln_g, ln_b, tm=512, tn=512, tm2=256):
    T, D = hb.shape
    ybs = pl.BlockSpec((tm, BRANCH_W), lambda j, i: (i, 0))
    merged = pl.pallas_call(
        _merge_gate_kernel,
        grid=(D // tn, T // tm),
        in_specs=[pl.BlockSpec((tm, D), lambda j, i: (i, 0)), ybs, ybs, ybs, ybs,
                  pl.BlockSpec((N_BRANCH, D, tn), lambda j, i: (0, 0, j)),
                  pl.BlockSpec((N_BRANCH, 1, tn), lambda j, i: (0, 0, j)),
                  pl.BlockSpec((N_BRANCH, BRANCH_W, tn), lambda j, i: (0, 0, j))],
        out_specs=pl.BlockSpec((tm, tn), lambda j, i: (i, j)),
        out_shape=jax.ShapeDtypeStruct((T, D), BF16),
        compiler_params=_cp("parallel", "parallel"),
        name="merge_gate",
    )(hb, *branches, w_gate.astype(BF16), b_gate.reshape(N_BRANCH, 1, D), w_br.astype(BF16))
    const = lambda i: (0, 0)
    rows = lambda i: (i, 0)
    return pl.pallas_call(
        _out_ln_kernel,
        grid=(T // tm2,),
        in_specs=[pl.BlockSpec((tm2, D), rows), pl.BlockSpec((D, D), const), pl.BlockSpec((tm2, D), rows),
                  pl.BlockSpec((1, D), const), pl.BlockSpec((1, D), const)],
        out_specs=[pl.BlockSpec((tm2, D), rows), pl.BlockSpec((tm2, D), rows)],
        out_shape=[jax.ShapeDtypeStruct((T, D), F32), jax.ShapeDtypeStruct((T, D), BF16)],
        compiler_params=_cp("parallel"),
        name="out_proj_ln1",
    )(merged, w_out.astype(BF16), hf, ln_g.reshape(1, D), ln_b.reshape(1, D))


R_TM = 256
DISPATCH_TB = 256
COMBINE_TB = 128


def _router_kernel(h_ref, whi_ref, wlo_ref, b_ref, eid_ref, wts_ref, rank_ref, cnt_ref, carry_ref):
    i = pl.program_id(0)

    @pl.when(i == 0)
    def _():
        carry_ref[...] = jnp.zeros(carry_ref.shape, F32)

    x = h_ref[...]
    tm = x.shape[0]
    xh = x.astype(BF16)
    xl = (x - xh.astype(F32)).astype(BF16)
    whi = whi_ref[...]
    logits = _dot(xh, whi) + _dot(xh, wlo_ref[...]) + _dot(xl, whi) + b_ref[...]
    lane = lax.broadcasted_iota(jnp.int32, (tm, LANES), 1)
    big = jnp.int32(4 * LANES)
    is_g = (lane >= N_EXPERTS) & (lane < N_EXPERTS + N_GROUPS)
    lg = jnp.where(is_g, logits, NEG_INF)
    gmax = jnp.max(lg, axis=-1, keepdims=True)
    gidx = jnp.min(jnp.where(lg == gmax, lane - N_EXPERTS, big), axis=-1, keepdims=True)
    g_w = 1.0 / jnp.sum(jnp.where(is_g, jnp.exp(lg - gmax), 0.0), axis=-1, keepdims=True)
    in_grp = (lane < N_EXPERTS) & ((lane // EXP_PER_GROUP) == gidx)
    le = jnp.where(in_grp, logits, NEG_INF)
    e1 = jnp.max(le, axis=-1, keepdims=True)
    i1 = jnp.min(jnp.where(le == e1, lane, big), axis=-1, keepdims=True)
    le2 = jnp.where(lane == i1, NEG_INF, le)
    e2 = jnp.max(le2, axis=-1, keepdims=True)
    i2 = jnp.min(jnp.where(le2 == e2, lane, big), axis=-1, keepdims=True)
    zsum = jnp.sum(jnp.where(in_grp, jnp.exp(le - e1), 0.0), axis=-1, keepdims=True)
    p1 = 1.0 / zsum
    p2 = jnp.exp(e2 - e1) / zsum
    w1 = g_w * p1 / (p1 + p2)
    w2 = g_w * p2 / (p1 + p2)
    oh1 = lane == i1
    oh2 = lane == i2
    ohs = (oh1 | oh2).astype(BF16)
    row = lax.broadcasted_iota(jnp.int32, (tm, tm), 0)
    col = lax.broadcasted_iota(jnp.int32, (tm, tm), 1)
    before = _dot((row > col).astype(BF16), ohs) + carry_ref[0:1, :]
    r1 = jnp.sum(jnp.where(oh1, before, 0.0), axis=-1, keepdims=True)
    r2 = jnp.sum(jnp.where(oh2, before, 0.0), axis=-1, keepdims=True)
    total = carry_ref[0:1, :] + jnp.sum(ohs.astype(F32), axis=0, keepdims=True)
    carry_ref[...] = jnp.broadcast_to(total, carry_ref.shape)
    cnt_ref[...] = jnp.broadcast_to(total, cnt_ref.shape).astype(jnp.int32)
    eid_ref[...] = jnp.where(lane == 0, i1, jnp.where(lane == 1, i2, 0))
    wts_ref[...] = jnp.where(lane == 0, w1, jnp.where(lane == 1, w2, 0.0))
    rank_ref[...] = jnp.where(lane == 0, r1, jnp.where(lane == 1, r2, 0.0)).astype(jnp.int32)


ROW_SUB = D_MODEL // LANES // 2
ROW_DT = jnp.uint32


def _row_slab(ref, r):
    return ref.at[pl.ds(pl.multiple_of(r * ROW_SUB, ROW_SUB), ROW_SUB)]


def _col_block(n, c):
    return pl.ds(c, n, stride=ROW_SUB)


def _pack_cols(x, c):
    as_bits = lambda t: lax.bitcast_convert_type(t.astype(BF16).astype(F32), ROW_DT)
    lo = as_bits(x[:, c * LANES:(c + 1) * LANES])
    hi = as_bits(x[:, (c + ROW_SUB) * LANES:(c + ROW_SUB + 1) * LANES])
    return (lo >> 16) | hi


def _unpack_cols(w):
    return (lax.bitcast_convert_type(w << 16, F32),
            lax.bitcast_convert_type(w & jnp.uint32(0xFFFF0000), F32))


def _dispatch_kernel(padlo_ref, pend_ref, dest_ref, h_ref, rows_hbm, stage_ref, zrow_ref, zblk_ref, sem, zsem):
    i = pl.program_id(0)
    n_rows = rows_hbm.shape[0] // ROW_SUB
    h = h_ref[...]
    for c in range(ROW_SUB):
        stage_ref[_col_block(DISPATCH_TB, c), :] = _pack_cols(h, c)

    def start(t, carry):
        for k in range(TOP_K):
            pltpu.make_async_copy(_row_slab(stage_ref, t), _row_slab(rows_hbm, dest_ref[TOP_K * t + k]),
                                  sem).start(priority=k)
        return carry

    lax.fori_loop(0, DISPATCH_TB, start, 0)

    @pl.when(i == 0)
    def _():
        zrow_ref[...] = jnp.zeros(zrow_ref.shape, ROW_DT)
        zblk_ref[...] = jnp.zeros(zblk_ref.shape, ROW_DT)

        def zero_row(r):
            return pltpu.make_async_copy(zrow_ref, _row_slab(rows_hbm, r), zsem)

        def zero_blk(b):
            dst = rows_hbm.at[pl.ds(pl.multiple_of(b * (MOE_BLOCK * ROW_SUB), MOE_BLOCK * ROW_SUB), MOE_BLOCK * ROW_SUB)]
            return pltpu.make_async_copy(zblk_ref, dst, zsem)

        def per_expert(e, carry):
            lax.fori_loop(padlo_ref[e], pend_ref[e], lambda r, c: (zero_row(r).start(), c)[1], 0)
            return carry

        lax.fori_loop(0, N_EXPERTS, per_expert, 0)
        first_free = pend_ref[N_EXPERTS - 1] // MOE_BLOCK
        lax.fori_loop(first_free, n_rows // MOE_BLOCK, lambda b, c: (zero_blk(b).start(), c)[1], 0)

        def per_expert_wait(e, carry):
            lax.fori_loop(padlo_ref[e], pend_ref[e], lambda r, c: (zero_row(0).wait(), c)[1], 0)
            return carry

        lax.fori_loop(0, N_EXPERTS, per_expert_wait, 0)
        lax.fori_loop(first_free, n_rows // MOE_BLOCK, lambda b, c: (zero_blk(0).wait(), c)[1], 0)

    def wait(t, carry):
        for k in range(TOP_K):
            pltpu.make_async_copy(_row_slab(stage_ref, 0), _row_slab(rows_hbm, 0), sem).wait()
        return carry

    lax.fori_loop(0, DISPATCH_TB, wait, 0)


def _expert_kernel(be_ref, x_ref, wg_ref, wu_ref, wd_ref, o_ref):
    del be_ref
    halves = [_unpack_cols(x_ref[_col_block(MOE_BLOCK, c), :]) for c in range(ROW_SUB)]
    x = jnp.concatenate([lo for lo, _ in halves] + [hi for _, hi in halves], axis=1).astype(BF16)
    hid = (_silu(_dot(x, wg_ref[...])) * _dot(x, wu_ref[...])).astype(BF16)
    y = _dot(hid, wd_ref[...])
    for c in range(ROW_SUB):
        o_ref[_col_block(MOE_BLOCK, c), :] = _pack_cols(y, c)


def _combine_kernel(dest_ref, dnext_ref, h_ref, w_ref, g_ref, b_ref, yrows_hbm, of_ref, ob_ref,
                    ybuf_ref, acc_ref, sem):
    i = pl.program_id(0)
    n = pl.num_programs(0)
    slot = i % 2

    def gather(d_ref, to_slot):
        def start(t, carry):
            for k in range(TOP_K):
                pltpu.make_async_copy(_row_slab(yrows_hbm, d_ref[TOP_K * t + k]),
                                      _row_slab(ybuf_ref.at[to_slot, k], t), sem.at[to_slot]).start(priority=k)
            return carry

        lax.fori_loop(0, COMBINE_TB, start, 0)

    @pl.when(i == 0)
    def _():
        gather(dest_ref, 0)

    @pl.when(i + 1 < n)
    def _():
        gather(dnext_ref, 1 - slot)

    def wait(t, carry):
        for k in range(TOP_K):
            pltpu.make_async_copy(_row_slab(yrows_hbm, 0), _row_slab(ybuf_ref.at[slot, k], 0), sem.at[slot]).wait()
        return carry

    lax.fori_loop(0, COMBINE_TB, wait, 0)
    w = w_ref[...]
    y0_ref = ybuf_ref.at[slot, 0]
    y1_ref = ybuf_ref.at[slot, 1]
    for c in range(ROW_SUB):
        cb = _col_block(COMBINE_TB, c)
        lo0, hi0 = _unpack_cols(y0_ref[cb, :])
        lo1, hi1 = _unpack_cols(y1_ref[cb, :])
        acc_ref[:, c * LANES:(c + 1) * LANES] = lo0 * w[:, 0:1] + lo1 * w[:, 1:2]
        acc_ref[:, (c + ROW_SUB) * LANES:(c + ROW_SUB + 1) * LANES] = hi0 * w[:, 0:1] + hi1 * w[:, 1:2]
    y = _layernorm_rows(ALPHA * h_ref[...] + acc_ref[...], g_ref[...], b_ref[...])
    of_ref[...] = y
    ob_ref[...] = y.astype(BF16)


def _moe_layer(hf, hb, w_rg, b_rg, w_re, b_re, w_e_gate, w_e_up, w_e_down, ln_g, ln_b):
    T, D = hf.shape
    n_rows = T * TOP_K + N_EXPERTS * MOE_BLOCK
    n_blocks = n_rows // MOE_BLOCK
    w_r = jnp.concatenate([w_re, w_rg, jnp.zeros((D, LANES - N_EXPERTS - N_GROUPS), F32)], axis=-1)
    b_r = jnp.concatenate([b_re, b_rg, jnp.zeros((LANES - N_EXPERTS - N_GROUPS,), F32)]).reshape(1, LANES)
    w_hi = w_r.astype(BF16)
    w_lo = (w_r - w_hi.astype(F32)).astype(BF16)
    const = lambda i: (0, 0)
    rows = lambda i: (i, 0)
    eid, wts, rank, cnt = pl.pallas_call(
        _router_kernel,
        grid=(T // R_TM,),
        in_specs=[pl.BlockSpec((R_TM, D), rows), pl.BlockSpec((D, LANES), const),
                  pl.BlockSpec((D, LANES), const), pl.BlockSpec((1, LANES), const)],
        out_specs=[pl.BlockSpec((R_TM, LANES), rows), pl.BlockSpec((R_TM, LANES), rows),
                   pl.BlockSpec((R_TM, LANES), rows), pl.BlockSpec((8, LANES), const)],
        out_shape=[jax.ShapeDtypeStruct((T, LANES), jnp.int32), jax.ShapeDtypeStruct((T, LANES), F32),
                   jax.ShapeDtypeStruct((T, LANES), jnp.int32), jax.ShapeDtypeStruct((8, LANES), jnp.int32)],
        scratch_shapes=[pltpu.VMEM((8, LANES), F32)],
        compiler_params=_cp("arbitrary"),
        name="moe_router",
    )(hf, w_hi, w_lo, b_r)
    counts = cnt[0, :N_EXPERTS]
    padded = (counts + MOE_BLOCK - 1) // MOE_BLOCK * MOE_BLOCK
    pends = jnp.cumsum(padded)
    pstarts = pends - padded
    blk_start = jnp.arange(n_blocks, dtype=jnp.int32) * MOE_BLOCK
    blk_exp = jnp.minimum(jnp.sum((pends[None, :] <= blk_start[:, None]).astype(jnp.int32), axis=1), N_EXPERTS - 1)
    sel = eid[:, :TOP_K, None] == jnp.arange(N_EXPERTS, dtype=jnp.int32)
    dest = (jnp.sum(jnp.where(sel, pstarts, 0), axis=-1) + rank[:, :TOP_K]).astype(jnp.int32).reshape(T * TOP_K)
    any_spec = pl.BlockSpec(memory_space=pl.ANY)
    xrows = pl.pallas_call(
        _dispatch_kernel,
        grid_spec=pltpu.PrefetchScalarGridSpec(
            num_scalar_prefetch=2,
            grid=(T // DISPATCH_TB,),
            in_specs=[pl.BlockSpec((TOP_K * DISPATCH_TB,), lambda i, *_: (i,), memory_space=pltpu.SMEM),
                      pl.BlockSpec((DISPATCH_TB, D), lambda i, *_: (i, 0))],
            out_specs=any_spec,
            scratch_shapes=[pltpu.VMEM((DISPATCH_TB * ROW_SUB, LANES), ROW_DT),
                            pltpu.VMEM((ROW_SUB, LANES), ROW_DT),
                            pltpu.VMEM((MOE_BLOCK * ROW_SUB, LANES), ROW_DT),
                            pltpu.SemaphoreType.DMA(()), pltpu.SemaphoreType.DMA(())]),
        out_shape=jax.ShapeDtypeStruct((n_rows * ROW_SUB, LANES), ROW_DT),
        compiler_params=_cp("arbitrary"),
        name="moe_dispatch",
    )((pstarts + counts).astype(jnp.int32), pends.astype(jnp.int32), dest, hf)
    slab = lambda i, *_: (i, 0)
    yrows = pl.pallas_call(
        _expert_kernel,
        grid_spec=pltpu.PrefetchScalarGridSpec(
            num_scalar_prefetch=1,
            grid=(n_blocks,),
            in_specs=[pl.BlockSpec((MOE_BLOCK * ROW_SUB, LANES), slab),
                      pl.BlockSpec((None, D, D_FF), lambda i, be: (be[i], 0, 0)),
                      pl.BlockSpec((None, D, D_FF), lambda i, be: (be[i], 0, 0)),
                      pl.BlockSpec((None, D_FF, D), lambda i, be: (be[i], 0, 0))],
            out_specs=pl.BlockSpec((MOE_BLOCK * ROW_SUB, LANES), slab)),
        out_shape=jax.ShapeDtypeStruct((n_rows * ROW_SUB, LANES), ROW_DT),
        compiler_params=_cp("arbitrary"),
        name="moe_experts",
    )(blk_exp, xrows, w_e_gate.astype(BF16), w_e_up.astype(BF16), w_e_down.astype(BF16))
    n_steps = T // COMBINE_TB
    dspec = lambda f: pl.BlockSpec((TOP_K * COMBINE_TB,), f, memory_space=pltpu.SMEM)
    return pl.pallas_call(
        _combine_kernel,
        grid=(n_steps,),
        in_specs=[dspec(lambda i: (i,)), dspec(lambda i: (jnp.minimum(i + 1, n_steps - 1),)),
                  pl.BlockSpec((COMBINE_TB, D), rows), pl.BlockSpec((COMBINE_TB, LANES), rows),
                  pl.BlockSpec((1, D), const), pl.BlockSpec((1, D), const), any_spec],
        out_specs=[pl.BlockSpec((COMBINE_TB, D), rows), pl.BlockSpec((COMBINE_TB, D), rows)],
        out_shape=[jax.ShapeDtypeStruct((T, D), F32), jax.ShapeDtypeStruct((T, D), BF16)],
        scratch_shapes=[pltpu.VMEM((2, TOP_K, COMBINE_TB * ROW_SUB, LANES), ROW_DT),
                        pltpu.VMEM((COMBINE_TB, D), F32), pltpu.SemaphoreType.DMA((2,))],
        compiler_params=_cp("arbitrary"),
        name="moe_combine_ln2",
    )(dest, dest, hf, wts, ln_g.reshape(1, D), ln_b.reshape(1, D), yrows)


def kernel(x, ln_in_g, ln_in_b, rel_bias, w_in, g_cq, w_uq, g_ckv, w_ukv, w_dw_c, b_dw_c, ln_c_g, ln_c_b,
           w_conv_d, b_conv_d, a_log_f, a_log_b, dt_bias_f, dt_bias_b, d_skip, g_norm_d, w_br, w_gate, b_gate,
           w_out, ln1_g, ln1_b, w_rg, b_rg, w_re, b_re, w_e_gate, w_e_up, w_e_down, ln2_g, ln2_b):
    B, S, D = x.shape
    T = B * S
    hf, hb = _layernorm(x.reshape(T, D), ln_in_g, ln_in_b)
    a_bias = _mixer_a_bias(rel_bias, S)
    for l in range(DEPTH):
        w_a, w_r = _in_proj_weights(w_in[l])
        qkv = _matmul(hb, w_a.astype(BF16), BF16, 512, 3 * A_WIDTH, "in_proj_a")
        rest = _matmul(hb, w_r.astype(BF16), F32, 512, R_WIDTH // 2, "in_proj_rest")
        y_a = _mixer_a(qkv, a_bias, B, S)
        y_b = _mixer_b(rest, g_cq[l], w_uq[l], g_ckv[l], w_ukv[l], B, S)
        y_c = _mixer_c(rest, w_dw_c[l], b_dw_c[l], ln_c_g[l], ln_c_b[l], B, S)
        y_d = _mixer_d(rest, w_conv_d[l], b_conv_d[l], a_log_f[l], a_log_b[l], dt_bias_f[l], dt_bias_b[l],
                       d_skip[l], g_norm_d[l], B, S)
        h1f, h1b = _merge(hb, hf, (y_a, y_b, y_c, y_d), w_gate[l], b_gate[l], w_br[l], w_out[l], ln1_g[l], ln1_b[l])
        hf, hb = _moe_layer(h1f, h1b, w_rg[l], b_rg[l], w_re[l], b_re[l], w_e_gate[l], w_e_up[l], w_e_down[l],
                            ln2_g[l], ln2_b[l])
    return hf.reshape(B, S, D)
```

```python
import functools

import numpy as np
import jax
import jax.numpy as jnp
from jax import lax
from jax.experimental import pallas as pl
from jax.experimental.pallas import tpu as pltpu

F32 = jnp.float32
BF16 = jnp.bfloat16

D_MODEL = 2048
DEPTH = 2
A_HEADS = 8
A_HEAD_DIM = 64
A_WIDTH = A_HEADS * A_HEAD_DIM
A_PATTERNS = ((128, 1), (512, 4), (2048, 16))
A_BAND = 64
REL_BUCKETS = 32
REL_MAX_DIST = 1024
B_HEADS = 8
B_NOPE = 64
B_ROPE = 32
B_V = 64
B_Q_LORA = 512
B_KV_LORA = 256
ROPE_THETA = 10000.0
C_CH = 512
C_KERNEL = 31
D_HEADS = 8
D_HEAD_DIM = 64
D_INNER = D_HEADS * D_HEAD_DIM
D_STATE = 128
D_GROUPS = 2
D_CONV = 5
D_CHUNK = 128
N_BRANCH = 4
BRANCH_W = 512
N_GROUPS = 4
EXP_PER_GROUP = 8
N_EXPERTS = N_GROUPS * EXP_PER_GROUP
TOP_K = 2
D_FF = 512
MOE_BLOCK = 128
ALPHA = (2 * DEPTH) ** 0.25
EPS = 1e-5
NEG_INF = -1e30

LANES = 128
R_GLU, R_XBC, R_CQ, R_Z, R_CKV, R_KR, R_DT = 0, 1024, 2048, 2560, 3072, 3328, 3456
R_WIDTH = 3584
VMEM_LIMIT = 56 * 1024 * 1024


def _cp(*sem):
    return pltpu.CompilerParams(dimension_semantics=sem, vmem_limit_bytes=VMEM_LIMIT)


def _dot(a, b):
    return jnp.dot(a, b, preferred_element_type=F32)


def _dot_nt(a, b):
    return lax.dot_general(a, b, (((1,), (1,)), ((), ())), preferred_element_type=F32)


def _split3(x):
    hi = x.astype(BF16)
    r1 = x - hi.astype(F32)
    mid = r1.astype(BF16)
    lo = (r1 - mid.astype(F32)).astype(BF16)
    return hi, mid, lo


def _layernorm_rows(x, g, b):
    mu = jnp.mean(x, axis=-1, keepdims=True)
    xc = x - mu
    var = jnp.mean(xc * xc, axis=-1, keepdims=True)
    return xc * lax.rsqrt(var + EPS) * g + b


def _sigmoid(x):
    return 1.0 / (1.0 + jnp.exp(-x))


def _silu(x):
    return x * _sigmoid(x)


def _ln_kernel(x_ref, g_ref, b_ref, of_ref, ob_ref):
    y = _layernorm_rows(x_ref[...], g_ref[...], b_ref[...])
    of_ref[...] = y
    ob_ref[...] = y.astype(BF16)


def _layernorm(x, g, b, tm=256):
    T, D = x.shape
    return pl.pallas_call(
        _ln_kernel,
        grid=(T // tm,),
        in_specs=[pl.BlockSpec((tm, D), lambda i: (i, 0)),
                  pl.BlockSpec((1, D), lambda i: (0, 0)),
                  pl.BlockSpec((1, D), lambda i: (0, 0))],
        out_specs=[pl.BlockSpec((tm, D), lambda i: (i, 0)),
                   pl.BlockSpec((tm, D), lambda i: (i, 0))],
        out_shape=[jax.ShapeDtypeStruct((T, D), F32), jax.ShapeDtypeStruct((T, D), BF16)],
        compiler_params=_cp("parallel"),
        name="ln_in",
    )(x, g.reshape(1, D), b.reshape(1, D))


def _mm_kernel(x_ref, w_ref, o_ref):
    o_ref[...] = _dot(x_ref[...], w_ref[...]).astype(o_ref.dtype)


def _matmul(x, w, out_dtype, tm, tn, name):
    M, K = x.shape
    N = w.shape[1]
    return pl.pallas_call(
        _mm_kernel,
        grid=(N // tn, M // tm),
        in_specs=[pl.BlockSpec((tm, K), lambda j, i: (i, 0)),
                  pl.BlockSpec((K, tn), lambda j, i: (0, j))],
        out_specs=pl.BlockSpec((tm, tn), lambda j, i: (i, j)),
        out_shape=jax.ShapeDtypeStruct((M, N), out_dtype),
        compiler_params=_cp("parallel", "parallel"),
        name=name,
    )(x, w)


def _t5_bucket(rel):
    half = REL_BUCKETS // 2
    max_exact = half // 2
    n = np.abs(rel)
    large = max_exact + (np.log(np.maximum(n, 1) / max_exact) / np.log(REL_MAX_DIST / max_exact)
                         * (half - max_exact)).astype(np.int32)
    large = np.minimum(large, half - 1)
    return (rel > 0).astype(np.int32) * half + np.where(n < max_exact, n, large)


def _a_window(L):
    return min(2 * LANES, L)


def _a_bias_tiles(rel_bias, d, L):
    W = _a_window(L)
    offs = (0,) if L == LANES else (0, -A_BAND, -2 * A_BAND)
    qi = np.arange(LANES)[:, None]
    kj = np.arange(W)[None, :]
    rel = np.stack([kj - qi + off for off in offs], axis=0)
    valid = np.abs(rel) <= A_BAND
    onehot = (jnp.asarray(_t5_bucket(rel * d), jnp.int32)[..., None] == jnp.arange(REL_BUCKETS)).astype(F32)
    b = jnp.einsum('vqkb,bh->vhqk', onehot, rel_bias.astype(F32), precision=lax.Precision.HIGHEST)
    b = jnp.where(valid[:, None], b, NEG_INF)
    return b.reshape(len(offs), A_HEADS // 2, 2 * LANES, W)


A_GROUP = 2


def _attn_a_kernel(q_ref, k_ref, v_ref, b16_ref, b4_ref, b1_ref, y_ref,
                   q4_ref, k4_ref, v4_ref, m_ref, l_ref, acc_ref, tmp_ref, *, S):
    (_, d1), (_, d4), (_, d16) = A_PATTERNS
    lane = lax.broadcasted_iota(jnp.int32, (LANES, LANES), 1)
    head0 = lane < A_HEAD_DIM
    scale = A_HEAD_DIM ** -0.5

    def partial_softmax(qs, ks, vs, bias_ref):
        q2 = jnp.concatenate([jnp.where(head0, qs, 0.0), jnp.where(head0, 0.0, qs)], axis=0).astype(BF16)
        s = _dot_nt(q2, ks.astype(BF16)) * scale + bias_ref[...]
        m = jnp.max(s, axis=-1, keepdims=True)
        p = jnp.exp(s - m).astype(BF16)
        num = _dot(p, vs.astype(BF16))
        den = _dot(p, jnp.ones((vs.shape[0], LANES), BF16))
        both = lambda t: jnp.where(head0, t[:LANES], t[LANES:])
        return both(jnp.broadcast_to(m, (2 * LANES, LANES))), both(den), both(num)

    def fold(old, new):
        (m_old, l_old, a_old), (m_new, l_new, a_new) = old, new
        m = jnp.maximum(m_old, m_new)
        c_old = jnp.exp(m_old - m)
        c_new = jnp.exp(m_new - m)
        return m, c_old * l_old + c_new * l_new, c_old * a_old + c_new * a_new

    stat_refs = (m_ref, l_ref, acc_ref)

    def get(c, rows):
        return tuple(ref.at[c][rows, :] for ref in stat_refs)

    def put(c, rows, stats):
        for ref, val in zip(stat_refs, stats):
            ref.at[c][rows, :] = val

    def grouped(n, group, unit):
        def trip(g, carry):
            pending = [unit(g * group + u) for u in range(group)]
            for finish in pending:
                finish()
            return carry

        lax.fori_loop(0, n // group, trip, 0)

    L4 = S // d4
    sub = d16 // d4
    assert S // d16 == LANES and sub == d4
    for c in range(d4):
        cls = pl.ds(c, L4, stride=d4)
        q4_ref[c] = q_ref[cls, :]
        k4_ref[c] = k_ref[cls, :]
        v4_ref[c] = v_ref[cls, :]

    def unit16(t):
        c = t % d4
        rows = pl.ds(t // d4, LANES, stride=sub)
        stats = partial_softmax(q4_ref.at[c][rows, :], k4_ref.at[c][rows, :], v4_ref.at[c][rows, :], b16_ref.at[0])
        return lambda: put(c, rows, stats)

    grouped(d16, A_GROUP, unit16)

    def window(i, L):
        nqb = L // LANES
        ws = pl.multiple_of(jnp.clip(i * LANES - A_BAND, 0, L - 2 * LANES), A_BAND)
        return pl.ds(ws, 2 * LANES), jnp.where(i == 0, 0, jnp.where(i == nqb - 1, 2, 1))

    def unit4(t):
        c = t % d4
        i = t // d4
        keys, var = window(i, L4)
        rows = pl.ds(pl.multiple_of(i * LANES, LANES), LANES)
        stats = partial_softmax(q4_ref.at[c][rows, :], k4_ref.at[c][keys, :], v4_ref.at[c][keys, :], b4_ref.at[var])
        return lambda: put(c, rows, fold(get(c, rows), stats))

    grouped(d4 * (L4 // LANES), A_GROUP, unit4)

    def unit1(i):
        keys, var = window(i, S)
        rows = pl.ds(pl.multiple_of(i * LANES, LANES), LANES)
        stats = partial_softmax(q_ref[rows, :], k_ref[keys, :], v_ref[keys, :], b1_ref.at[var])

        def finish():
            part = pl.ds(pl.multiple_of(i * (LANES // d4), LANES // d4), LANES // d4)
            for n, ref in enumerate(stat_refs):
                for c in range(d4):
                    tmp_ref.at[n][pl.ds(c, LANES // d4, stride=d4), :] = ref.at[c][part, :]
            _, l, a = fold(tuple(tmp_ref[n] for n in range(len(stat_refs))), stats)
            y_ref[rows, :] = (a / l).astype(y_ref.dtype)

        return finish

    grouped(S // LANES, A_GROUP, unit1)


def _mixer_a_bias(rel_bias, S):
    return tuple(_a_bias_tiles(rel_bias, d, S // d) for _, d in A_PATTERNS)


def _mixer_a(qkv, bias, B, S):
    b1, b4, b16 = bias
    npair = A_HEADS // 2
    pair_bias = lambda t: pl.BlockSpec((t.shape[0], None) + t.shape[2:], lambda b, hp: (0, hp, 0, 0))
    slab = lambda first: pl.BlockSpec((S, LANES), lambda b, hp: (b, first + hp))
    return pl.pallas_call(
        functools.partial(_attn_a_kernel, S=S),
        grid=(B, npair),
        in_specs=[slab(0), slab(npair), slab(2 * npair), pair_bias(b16), pair_bias(b4), pair_bias(b1)],
        out_specs=pl.BlockSpec((S, LANES), lambda b, hp: (b, hp)),
        out_shape=jax.ShapeDtypeStruct((B * S, A_WIDTH), BF16),
        scratch_shapes=[pltpu.VMEM((A_PATTERNS[1][1], S // A_PATTERNS[1][1], LANES), F32)] * 6
        + [pltpu.VMEM((3, LANES, LANES), F32)],
        compiler_params=_cp("parallel", "parallel"),
        name="attn_a",
    )(qkv, qkv, qkv, b16, b4, b1)


MLA_Q_SCALE = float((B_NOPE + B_ROPE) ** -0.5 * np.log2(np.e))


def _mla_proj_kernel(cq_ref, ckv_ref, kr_ref, gq_ref, gkv_ref, wqm_ref, wqs_ref, wk_ref, wv_ref, vone_ref,
                     ek_ref, cosq_ref, sinq_ref, csk_ref, q_ref, k_ref, v_ref):
    cq = cq_ref[...]
    xq = (cq * lax.rsqrt(jnp.mean(cq * cq, axis=-1, keepdims=True) + EPS) * gq_ref[...]).astype(BF16)
    ckv = ckv_ref[...]
    xkv = (ckv * lax.rsqrt(jnp.mean(ckv * ckv, axis=-1, keepdims=True) + EPS) * gkv_ref[...]).astype(BF16)
    qm = _dot(xq, wqm_ref[...])
    qs = _dot(xq, wqs_ref[...])
    cosq = cosq_ref[...] * MLA_Q_SCALE
    sinq = sinq_ref[...] * MLA_Q_SCALE
    t = kr_ref[...] * csk_ref[...]
    t_hi = t.astype(BF16)
    t_lo = (t - t_hi.astype(F32)).astype(BF16)
    kk = _dot(xkv, wk_ref[...]) + _dot(t_hi, ek_ref[...]) + _dot(t_lo, ek_ref[...])
    for h in range(B_HEADS):
        sl = slice(h * LANES, (h + 1) * LANES)
        q_ref[:, sl] = (qm[:, sl] * cosq + qs[:, sl] * sinq).astype(BF16)
    k_ref[...] = kk.astype(BF16)
    v_ref[...] = (_dot(xkv, wv_ref[...]) + vone_ref[...]).astype(BF16)


def _mla_attn_kernel(q_ref, k_ref, v_ref, o_ref):
    outs = []
    for hh in range(2):
        sl = slice(hh * LANES, (hh + 1) * LANES)
        s = _dot_nt(q_ref[:, sl], k_ref[:, sl])
        p = jnp.exp2(s - jnp.max(s, axis=-1, keepdims=True))
        outs.append(_dot(p.astype(BF16), v_ref[:, sl]))
    lane = lax.broadcasted_iota(jnp.int32, outs[0].shape, 1)
    acc = jnp.where(lane < B_V, outs[0], outs[1])
    den = pltpu.roll(jnp.where(lane < B_V, outs[1], outs[0]), B_V, axis=1)
    o_ref[...] = (acc / den).astype(o_ref.dtype)


def _mla_tables(S):
    inv_freq = ROPE_THETA ** (-jnp.arange(0, B_ROPE, 2, dtype=F32) / B_ROPE)
    ang = jnp.arange(S, dtype=F32)[:, None] * inv_freq[None]
    cos, sin = jnp.cos(ang), jnp.sin(ang)
    cos2 = jnp.concatenate([cos, cos], axis=-1)
    sin2 = jnp.concatenate([sin, sin], axis=-1)
    ones = jnp.ones((S, B_NOPE), F32)
    zn = jnp.zeros((S, B_NOPE), F32)
    zp = jnp.zeros((S, LANES - B_NOPE - B_ROPE), F32)
    cosq = jnp.concatenate([ones, cos2, zp], axis=-1)
    sinq = jnp.concatenate([zn, sin2, zp], axis=-1)
    csk = jnp.concatenate([cos2, sin2, jnp.zeros((S, LANES - 2 * B_ROPE), F32)], axis=-1)
    return cosq, sinq, csk


def _swap_cols(w):
    half = w.shape[-1] // 2
    return jnp.concatenate([-w[..., half:], w[..., :half]], axis=-1)


def _mla_weights(w_uq, w_ukv):
    dq = B_NOPE + B_ROPE
    wq = w_uq.reshape(B_Q_LORA, B_HEADS, dq)
    zpad = jnp.zeros((B_Q_LORA, B_HEADS, LANES - dq), F32)
    wqm = jnp.concatenate([wq, zpad], axis=-1).reshape(B_Q_LORA, B_HEADS * LANES)
    wqs = jnp.concatenate([jnp.zeros((B_Q_LORA, B_HEADS, B_NOPE), F32), _swap_cols(wq[..., B_NOPE:]), zpad],
                          axis=-1).reshape(B_Q_LORA, B_HEADS * LANES)
    wkv = w_ukv.reshape(B_KV_LORA, B_HEADS, B_NOPE + B_V)
    wk = jnp.concatenate([wkv[..., :B_NOPE], jnp.zeros((B_KV_LORA, B_HEADS, LANES - B_NOPE), F32)],
                         axis=-1).reshape(B_KV_LORA, B_HEADS * LANES)
    zv = jnp.zeros((B_KV_LORA, B_HEADS // 2, LANES - B_V), F32)
    wv_h = wkv[..., B_NOPE:]
    wv = jnp.stack([jnp.concatenate([wv_h[:, 0::2], zv], axis=-1),
                    jnp.concatenate([zv, wv_h[:, 1::2]], axis=-1)], axis=2).reshape(B_KV_LORA, B_HEADS * LANES)
    lane_in_pair = np.arange(B_HEADS * LANES) % (2 * LANES)
    vone = jnp.asarray(((lane_in_pair >= B_V) & (lane_in_pair < LANES + B_V)).astype(np.float32)).reshape(1, -1)
    ek = np.zeros((LANES, B_HEADS, LANES), np.float32)
    for j in range(B_ROPE):
        ek[j, :, B_NOPE + j] = 1.0
        ek[B_ROPE + j, :, B_NOPE + j] = 1.0
    ek = jnp.asarray(ek.reshape(LANES, B_HEADS * LANES))
    return wqm.astype(BF16), wqs.astype(BF16), wk.astype(BF16), wv.astype(BF16), vone, ek.astype(BF16)


def _mixer_b(rest, g_cq, w_uq, g_ckv, w_ukv, B, S, tm=512, tq=256):
    T = B * S
    wqm, wqs, wk, wv, vone, ek = _mla_weights(w_uq, w_ukv)
    cosq, sinq, csk = _mla_tables(S)
    nst = S // tm
    QW = B_HEADS * LANES
    const = lambda i: (0, 0)
    pos = lambda i: (i % nst, 0)
    q, k, v = pl.pallas_call(
        _mla_proj_kernel,
        grid=(T // tm,),
        in_specs=[pl.BlockSpec((tm, B_Q_LORA), lambda i: (i, R_CQ // B_Q_LORA)),
                  pl.BlockSpec((tm, B_KV_LORA), lambda i: (i, R_CKV // B_KV_LORA)),
                  pl.BlockSpec((tm, LANES), lambda i: (i, R_KR // LANES)),
                  pl.BlockSpec((1, B_Q_LORA), const),
                  pl.BlockSpec((1, B_KV_LORA), const),
                  pl.BlockSpec((B_Q_LORA, QW), const),
                  pl.BlockSpec((B_Q_LORA, QW), const),
                  pl.BlockSpec((B_KV_LORA, QW), const),
                  pl.BlockSpec((B_KV_LORA, QW), const),
                  pl.BlockSpec((1, QW), const),
                  pl.BlockSpec((LANES, QW), const),
                  pl.BlockSpec((tm, LANES), pos),
                  pl.BlockSpec((tm, LANES), pos),
                  pl.BlockSpec((tm, LANES), pos)],
        out_specs=[pl.BlockSpec((tm, QW), lambda i: (i, 0)),
                   pl.BlockSpec((tm, QW), lambda i: (i, 0)),
                   pl.BlockSpec((tm, QW), lambda i: (i, 0))],
        out_shape=[jax.ShapeDtypeStruct((T, QW), BF16), jax.ShapeDtypeStruct((T, QW), BF16),
                   jax.ShapeDtypeStruct((T, QW), BF16)],
        compiler_params=_cp("parallel"),
        name="mla_proj",
    )(rest, rest, rest, g_cq.reshape(1, -1), g_ckv.reshape(1, -1), wqm, wqs, wk, wv, vone, ek, cosq, sinq, csk)
    y = pl.pallas_call(
        _mla_attn_kernel,
        grid=(B, B_HEADS // 2, S // tq),
        in_specs=[pl.BlockSpec((None, tq, 2 * LANES), lambda b, hp, i: (b, i, hp)),
                  pl.BlockSpec((None, S, 2 * LANES), lambda b, hp, i: (b, 0, hp)),
                  pl.BlockSpec((None, S, 2 * LANES), lambda b, hp, i: (b, 0, hp))],
        out_specs=pl.BlockSpec((None, tq, 2 * B_V), lambda b, hp, i: (b, i, hp)),
        out_shape=jax.ShapeDtypeStruct((B, S, B_HEADS * B_V), BF16),
        compiler_params=_cp("parallel", "parallel", "arbitrary"),
        name="mla_attn",
    )(q.reshape(B, S, QW), k.reshape(B, S, QW), v.reshape(B, S, QW))
    return y.reshape(T, B_HEADS * B_V)


C_PAD = 16
C_ROWS = 128


SUBLANES = 8


def _tap_span(first, ntaps, rows):
    return rows + ((first + ntaps - 1) // SUBLANES) * SUBLANES


def _depthwise_taps(win_ref, sh_ref, w_ref, bias, ls, first, ntaps, rows):
    acc = jnp.broadcast_to(bias, (rows, LANES))
    span = _tap_span(first, ntaps, rows)
    for ph in range(SUBLANES):
        taps = [j for j in range(ntaps) if (first + j) % SUBLANES == ph]
        if not taps:
            continue
        if len(taps) == 1:
            j = taps[0]
            acc = acc + w_ref[j:j + 1, ls] * win_ref[first + j:first + j + rows, ls]
            continue
        sh_ref[0:span, :] = win_ref[ph:ph + span, ls]
        for j in taps:
            a = (first + j) // SUBLANES * SUBLANES
            acc = acc + w_ref[j:j + 1, ls] * sh_ref[a:a + rows, :]
    return acc


def _fill_window(win_ref, load_rows, r0, i, nblk, rows, pad):
    width = win_ref.shape[1]
    win_ref[pad:pad + rows, :] = load_rows(r0, rows)

    @pl.when(i > 0)
    def _():
        win_ref[0:pad, :] = load_rows(r0 - pad, pad)

    @pl.when(i == 0)
    def _():
        win_ref[0:pad, :] = jnp.zeros((pad, width), F32)

    @pl.when(i < nblk - 1)
    def _():
        win_ref[pad + rows:pad + rows + pad, :] = load_rows(r0 + rows, pad)

    @pl.when(i == nblk - 1)
    def _():
        win_ref[pad + rows:pad + rows + pad, :] = jnp.zeros((pad, width), F32)


def _conv_c_kernel(glu_ref, w_ref, b_ref, g_ref, beta_ref, o_ref, win_ref, sh_ref, acc_ref, *, S):
    i = pl.program_id(1)
    r0 = pl.multiple_of(i * C_ROWS, C_ROWS)

    def glu_rows(start, n):
        rs = pl.ds(pl.multiple_of(start, C_PAD), n)
        return glu_ref[rs, 0:C_CH] * _sigmoid(glu_ref[rs, C_CH:2 * C_CH])

    _fill_window(win_ref, glu_rows, r0, i, S // C_ROWS, C_ROWS, C_PAD)
    first = C_PAD - C_KERNEL // 2
    for lb in range(C_CH // LANES):
        ls = slice(lb * LANES, (lb + 1) * LANES)
        acc_ref[:, ls] = _depthwise_taps(win_ref, sh_ref, w_ref, b_ref[:, ls], ls, first, C_KERNEL, C_ROWS)
    y = _layernorm_rows(acc_ref[...], g_ref[...], beta_ref[...])
    o_ref[...] = _silu(y).astype(o_ref.dtype)


def _mixer_c(rest, w_dw, b_dw, ln_g, ln_b, B, S):
    T = B * S
    const = lambda b, i: (0, 0)
    y = pl.pallas_call(
        functools.partial(_conv_c_kernel, S=S),
        grid=(B, S // C_ROWS),
        in_specs=[pl.BlockSpec((None, S, 2 * C_CH), lambda b, i: (b, 0, R_GLU // (2 * C_CH))),
                  pl.BlockSpec((C_KERNEL, C_CH), const),
                  pl.BlockSpec((1, C_CH), const),
                  pl.BlockSpec((1, C_CH), const),
                  pl.BlockSpec((1, C_CH), const)],
        out_specs=pl.BlockSpec((None, C_ROWS, C_CH), lambda b, i: (b, i, 0)),
        out_shape=jax.ShapeDtypeStruct((B, S, C_CH), BF16),
        scratch_shapes=[pltpu.VMEM((C_ROWS + 2 * C_PAD, C_CH), F32),
                        pltpu.VMEM((_tap_span(C_PAD - C_KERNEL // 2, C_KERNEL, C_ROWS), LANES), F32),
                        pltpu.VMEM((C_ROWS, C_CH), F32)],
        compiler_params=_cp("parallel", "parallel"),
        name="conformer_conv",
    )(rest.reshape(B, S, R_WIDTH), w_dw, b_dw.reshape(1, -1), ln_g.reshape(1, -1), ln_b.reshape(1, -1))
    return y.reshape(T, C_CH)


D_PAD = 8
XBC_W = D_INNER + 2 * D_GROUPS * D_STATE
N_PAIR = D_HEADS // 2


def _pair_expand(v, first):
    lane = lax.broadcasted_iota(jnp.int32, (v.shape[0], LANES), 1)
    lo = jnp.broadcast_to(v[:, first:first + 1], (v.shape[0], LANES))
    hi = jnp.broadcast_to(v[:, first + 1:first + 2], (v.shape[0], LANES))
    return jnp.where(lane < D_HEAD_DIM, lo, hi)


def _ssd_kernel(xbc_ref, z_ref, dt_ref, wc_ref, bc_ref, alog_ref, dtb_ref, dskip_ref, gn_ref, o_ref,
                win_ref, sh_ref, xc_ref, a_ref, dtv_ref, y_ref, st_ref, *, S):
    Q = D_CHUNK
    nchunk = S // Q
    N = D_STATE
    bm0 = D_INNER
    cm0 = D_INNER + D_GROUPS * N

    def conv_body(c, carry):
        r0 = pl.multiple_of(c * Q, Q)
        _fill_window(win_ref, lambda st, n: xbc_ref[pl.ds(pl.multiple_of(st, D_PAD), n), :], r0, c, nchunk, Q, D_PAD)
        for lb in range(XBC_W // LANES):
            ls = slice(lb * LANES, (lb + 1) * LANES)
            acc = _depthwise_taps(win_ref, sh_ref, wc_ref, bc_ref[:, ls], ls, D_PAD - D_CONV // 2, D_CONV, Q)
            xc_ref[pl.ds(r0, Q), ls] = _silu(acc)
        return carry

    lax.fori_loop(0, nchunk, conv_body, 0)

    lane1 = lax.broadcasted_iota(jnp.int32, (1, LANES), 1)
    a_row = jnp.where(lane1 < 2 * D_HEADS, -jnp.exp(alog_ref[...]), 0.0)
    xdt = dt_ref[...] + dtb_ref[...]
    dtv = jnp.maximum(xdt, 0.0) + jnp.log(1.0 + jnp.exp(-jnp.abs(xdt)))
    dtv_ref[...] = dtv
    a_ref[...] = dtv * a_row

    row = lax.broadcasted_iota(jnp.int32, (Q, Q), 0)
    col = lax.broadcasted_iota(jnp.int32, (Q, Q), 1)
    tril = row >= col
    triu = col >= row
    lane = col

    def scan_chunk(c, lower, off, finalize):
        r0 = pl.multiple_of(c * Q, Q)
        rows = pl.ds(r0, Q)
        mask = tril if lower else triu
        tri = mask.astype(BF16)
        a_hi, a_mid, a_lo = _split3(a_ref[rows, :])
        cs = _dot(tri, a_hi) + _dot(tri, a_mid) + _dot(tri, a_lo)
        cs_t = cs.T
        ecs = jnp.exp(cs)
        edge = Q - 1 if lower else 0
        edec = jnp.exp(cs[edge:edge + 1, :] - cs)
        dt_c = dtv_ref[rows, :]
        for g in range(D_GROUPS):
            bg = xc_ref[rows, bm0 + g * N:bm0 + (g + 1) * N]
            cg = xc_ref[rows, cm0 + g * N:cm0 + (g + 1) * N].astype(BF16)
            cb = _dot_nt(cg, bg.astype(BF16))
            bg_t = bg.T.astype(BF16)
            for pp in range(N_PAIR // D_GROUPS):
                p = g * (N_PAIR // D_GROUPS) + pp
                ps = slice(p * LANES, (p + 1) * LANES)
                xdt_p = xc_ref[rows, ps] * _pair_expand(dt_c, off + 2 * p)
                ms = []
                for hh in range(2):
                    k = off + 2 * p + hh
                    diff = jnp.broadcast_to(cs[:, k:k + 1], (Q, Q)) - cs_t[k:k + 1, :]
                    ms.append((jnp.exp(jnp.where(mask, diff, NEG_INF)) * cb).astype(BF16))
                x_lo = jnp.where(lane < D_HEAD_DIM, xdt_p, 0.0).astype(BF16)
                x_hi = jnp.where(lane >= D_HEAD_DIM, xdt_p, 0.0).astype(BF16)
                y_intra = _dot(jnp.concatenate(ms, axis=1), jnp.concatenate([x_lo, x_hi], axis=0))
                hp = st_ref[p]
                ecs_p = _pair_expand(ecs, off + 2 * p)
                y_new = y_intra + _dot(cg, hp.astype(BF16)) * ecs_p
                if lower:
                    y_ref[rows, ps] = y_new
                else:
                    y_ref[rows, ps] = y_ref[rows, ps] + y_new
                xs_p = (xdt_p * _pair_expand(edec, off + 2 * p)).astype(BF16)
                st_ref[p] = hp * ecs_p[edge:edge + 1, :] + _dot(bg_t, xs_p)
        if finalize:
            y = y_ref[rows, :] + xc_ref[rows, 0:D_INNER] * dskip_ref[...]
            gated = y * _silu(z_ref[rows, :])
            out = gated * lax.rsqrt(jnp.mean(gated * gated, axis=-1, keepdims=True) + EPS) * gn_ref[...]
            o_ref[rows, :] = out.astype(o_ref.dtype)

    st_ref[...] = jnp.zeros(st_ref.shape, F32)

    def fwd_body(c, carry):
        scan_chunk(c, True, 0, False)
        return carry

    lax.fori_loop(0, nchunk, fwd_body, 0)
    st_ref[...] = jnp.zeros(st_ref.shape, F32)

    def bwd_body(k, carry):
        scan_chunk(nchunk - 1 - k, False, D_HEADS, True)
        return carry

    lax.fori_loop(0, nchunk, bwd_body, 0)


def _mixer_d(rest, w_conv, b_conv, a_log_f, a_log_b, dt_bias_f, dt_bias_b, d_skip, g_norm, B, S):
    T = B * S
    pad16 = lambda f, b: jnp.concatenate([f, b, jnp.zeros((LANES - 2 * D_HEADS,), F32)]).reshape(1, LANES)
    const = lambda b: (0, 0)
    y = pl.pallas_call(
        functools.partial(_ssd_kernel, S=S),
        grid=(B,),
        in_specs=[pl.BlockSpec((None, S, XBC_W), lambda b: (b, 0, R_XBC // XBC_W)),
                  pl.BlockSpec((None, S, D_INNER), lambda b: (b, 0, R_Z // D_INNER)),
                  pl.BlockSpec((None, S, LANES), lambda b: (b, 0, R_DT // LANES)),
                  pl.BlockSpec((D_CONV, XBC_W), const),
                  pl.BlockSpec((1, XBC_W), const),
                  pl.BlockSpec((1, LANES), const),
                  pl.BlockSpec((1, LANES), const),
                  pl.BlockSpec((1, D_INNER), const),
                  pl.BlockSpec((1, D_INNER), const)],
        out_specs=pl.BlockSpec((None, S, D_INNER), lambda b: (b, 0, 0)),
        out_shape=jax.ShapeDtypeStruct((B, S, D_INNER), BF16),
        scratch_shapes=[pltpu.VMEM((D_CHUNK + 2 * D_PAD, XBC_W), F32),
                        pltpu.VMEM((_tap_span(D_PAD - D_CONV // 2, D_CONV, D_CHUNK), LANES), F32),
                        pltpu.VMEM((S, XBC_W), F32),
                        pltpu.VMEM((S, LANES), F32),
                        pltpu.VMEM((S, LANES), F32),
                        pltpu.VMEM((S, D_INNER), F32),
                        pltpu.VMEM((N_PAIR, D_STATE, LANES), F32)],
        compiler_params=_cp("parallel"),
        name="ssd_mixer",
    )(rest.reshape(B, S, R_WIDTH), rest.reshape(B, S, R_WIDTH), rest.reshape(B, S, R_WIDTH),
      w_conv, b_conv.reshape(1, -1), pad16(a_log_f, a_log_b), pad16(dt_bias_f, dt_bias_b),
      jnp.repeat(d_skip, D_HEAD_DIM).reshape(1, -1), g_norm.reshape(1, -1))
    return y.reshape(T, D_INNER)


def _in_proj_weights(w_in_l):
    o = np.cumsum((0, A_WIDTH, A_WIDTH, A_WIDTH, B_Q_LORA, B_KV_LORA, B_ROPE, 2 * C_CH,
                   D_INNER, D_INNER, D_GROUPS * D_STATE, D_GROUPS * D_STATE, 2 * D_HEADS)).tolist()
    seg = lambda n: w_in_l[:, o[n]:o[n + 1]]
    w_a = w_in_l[:, :o[3]]
    cq, ckv, kr, glu, z, xs, bm, cm, dt = (seg(n) for n in range(3, 12))
    zeros = lambda n: jnp.zeros((w_in_l.shape[0], n), w_in_l.dtype)
    w_r = jnp.concatenate([glu, xs, bm, cm, cq, z, ckv,
                           kr, _swap_cols(kr), zeros(LANES - 2 * B_ROPE),
                           dt, zeros(LANES - 2 * D_HEADS)], axis=-1)
    assert w_r.shape[1] == R_WIDTH
    return w_a, w_r


def _merge_gate_kernel(h_ref, ya_ref, yb_ref, yc_ref, yd_ref, wg_ref, bg_ref, wbr_ref, o_ref):
    h = h_ref[...]
    acc = None
    for i, y_ref in enumerate((ya_ref, yb_ref, yc_ref, yd_ref)):
        gate = _sigmoid(_dot(h, wg_ref[i]) + bg_ref[i])
        term = gate * _dot(y_ref[...], wbr_ref[i])
        acc = term if acc is None else acc + term
    o_ref[...] = acc.astype(o_ref.dtype)


def _out_ln_kernel(m_ref, w_ref, h_ref, g_ref, b_ref, of_ref, ob_ref):
    y = _layernorm_rows(ALPHA * h_ref[...] + _dot(m_ref[...], w_ref[...]), g_ref[...], b_ref[...])
    of_ref[...] = y
    ob_ref[...] = y.astype(BF16)


def _merge(hb, hf, branches, w_gate, b_gate, w_br, w_out, ln_g, ln_b, tm=512, tn=512, tm2=256):
    T, D = hb.shape
    ybs = pl.BlockSpec((tm, BRANCH_W), lambda j, i: (i, 0))
    merged = pl.pallas_call(
        _merge_gate_kernel,
        grid=(D // tn, T // tm),
        in_specs=[pl.BlockSpec((tm, D), lambda j, i: (i, 0)), ybs, ybs, ybs, ybs,
                  pl.BlockSpec((N_BRANCH, D, tn), lambda j, i: (0, 0, j)),
                  pl.BlockSpec((N_BRANCH, 1, tn), lambda j, i: (0, 0, j)),
                  pl.BlockSpec((N_BRANCH, BRANCH_W, tn), lambda j, i: (0, 0, j))],
        out_specs=pl.BlockSpec((tm, tn), lambda j, i: (i, j)),
        out_shape=jax.ShapeDtypeStruct((T, D), BF16),
        compiler_params=_cp("parallel", "parallel"),
        name="merge_gate",
    )(hb, *branches, w_gate.astype(BF16), b_gate.reshape(N_BRANCH, 1, D), w_br.astype(BF16))
    const = lambda i: (0, 0)
    rows = lambda i: (i, 0)
    return pl.pallas_call(
        _out_ln_kernel,
        grid=(T // tm2,),
        in_specs=[pl.BlockSpec((tm2, D), rows), pl.BlockSpec((D, D), const), pl.BlockSpec((tm2, D), rows),
                  pl.BlockSpec((1, D), const), pl.BlockSpec((1, D), const)],
        out_specs=[pl.BlockSpec((tm2, D), rows), pl.BlockSpec((tm2, D), rows)],
        out_shape=[jax.ShapeDtypeStruct((T, D), F32), jax.ShapeDtypeStruct((T, D), BF16)],
        compiler_params=_cp("parallel"),
        name="out_proj_ln1",
    )(merged, w_out.astype(BF16), hf, ln_g.reshape(1, D), ln_b.reshape(1, D))


R_TM = 256
DISPATCH_TB = 256
COMBINE_TB = 128


def _router_kernel(h_ref, whi_ref, wlo_ref, b_ref, eid_ref, wts_ref, rank_ref, cnt_ref, carry_ref):
    i = pl.program_id(0)

    @pl.when(i == 0)
    def _():
        carry_ref[...] = jnp.zeros(carry_ref.shape, F32)

    x = h_ref[...]
    tm = x.shape[0]
    xh = x.astype(BF16)
    xl = (x - xh.astype(F32)).astype(BF16)
    whi = whi_ref[...]
    logits = _dot(xh, whi) + _dot(xh, wlo_ref[...]) + _dot(xl, whi) + b_ref[...]
    lane = lax.broadcasted_iota(jnp.int32, (tm, LANES), 1)
    big = jnp.int32(4 * LANES)
    is_g = (lane >= N_EXPERTS) & (lane < N_EXPERTS + N_GROUPS)
    lg = jnp.where(is_g, logits, NEG_INF)
    gmax = jnp.max(lg, axis=-1, keepdims=True)
    gidx = jnp.min(jnp.where(lg == gmax, lane - N_EXPERTS, big), axis=-1, keepdims=True)
    g_w = 1.0 / jnp.sum(jnp.where(is_g, jnp.exp(lg - gmax), 0.0), axis=-1, keepdims=True)
    in_grp = (lane < N_EXPERTS) & ((lane // EXP_PER_GROUP) == gidx)
    le = jnp.where(in_grp, logits, NEG_INF)
    e1 = jnp.max(le, axis=-1, keepdims=True)
    i1 = jnp.min(jnp.where(le == e1, lane, big), axis=-1, keepdims=True)
    le2 = jnp.where(lane == i1, NEG_INF, le)
    e2 = jnp.max(le2, axis=-1, keepdims=True)
    i2 = jnp.min(jnp.where(le2 == e2, lane, big), axis=-1, keepdims=True)
    zsum = jnp.sum(jnp.where(in_grp, jnp.exp(le - e1), 0.0), axis=-1, keepdims=True)
    p1 = 1.0 / zsum
    p2 = jnp.exp(e2 - e1) / zsum
    w1 = g_w * p1 / (p1 + p2)
    w2 = g_w * p2 / (p1 + p2)
    oh1 = lane == i1
    oh2 = lane == i2
    ohs = (oh1 | oh2).astype(BF16)
    row = lax.broadcasted_iota(jnp.int32, (tm, tm), 0)
    col = lax.broadcasted_iota(jnp.int32, (tm, tm), 1)
    before = _dot((row > col).astype(BF16), ohs) + carry_ref[0:1, :]
    r1 = jnp.sum(jnp.where(oh1, before, 0.0), axis=-1, keepdims=True)
    r2 = jnp.sum(jnp.where(oh2, before, 0.0), axis=-1, keepdims=True)
    total = carry_ref[0:1, :] + jnp.sum(ohs.astype(F32), axis=0, keepdims=True)
    carry_ref[...] = jnp.broadcast_to(total, carry_ref.shape)
    cnt_ref[...] = jnp.broadcast_to(total, cnt_ref.shape).astype(jnp.int32)
    eid_ref[...] = jnp.where(lane == 0, i1, jnp.where(lane == 1, i2, 0))
    wts_ref[...] = jnp.where(lane == 0, w1, jnp.where(lane == 1, w2, 0.0))
    rank_ref[...] = jnp.where(lane == 0, r1, jnp.where(lane == 1, r2, 0.0)).astype(jnp.int32)


ROW_SUB = D_MODEL // LANES // 2
ROW_DT = jnp.uint32


def _row_slab(ref, r):
    return ref.at[pl.ds(pl.multiple_of(r * ROW_SUB, ROW_SUB), ROW_SUB)]


def _col_block(n, c):
    return pl.ds(c, n, stride=ROW_SUB)


def _pack_cols(x, c):
    as_bits = lambda t: lax.bitcast_convert_type(t.astype(BF16).astype(F32), ROW_DT)
    lo = as_bits(x[:, c * LANES:(c + 1) * LANES])
    hi = as_bits(x[:, (c + ROW_SUB) * LANES:(c + ROW_SUB + 1) * LANES])
    return (lo >> 16) | hi


def _unpack_cols(w):
    return (lax.bitcast_convert_type(w << 16, F32),
            lax.bitcast_convert_type(w & jnp.uint32(0xFFFF0000), F32))


def _dispatch_kernel(padlo_ref, pend_ref, dest_ref, h_ref, rows_hbm, stage_ref, zrow_ref, zblk_ref, sem, zsem):
    i = pl.program_id(0)
    n_rows = rows_hbm.shape[0] // ROW_SUB
    h = h_ref[...]
    for c in range(ROW_SUB):
        stage_ref[_col_block(DISPATCH_TB, c), :] = _pack_cols(h, c)

    def start(t, carry):
        for k in range(TOP_K):
            pltpu.make_async_copy(_row_slab(stage_ref, t), _row_slab(rows_hbm, dest_ref[TOP_K * t + k]),
                                  sem).start(priority=k)
        return carry

    lax.fori_loop(0, DISPATCH_TB, start, 0)

    @pl.when(i == 0)
    def _():
        zrow_ref[...] = jnp.zeros(zrow_ref.shape, ROW_DT)
        zblk_ref[...] = jnp.zeros(zblk_ref.shape, ROW_DT)

        def zero_row(r):
            return pltpu.make_async_copy(zrow_ref, _row_slab(rows_hbm, r), zsem)

        def zero_blk(b):
            dst = rows_hbm.at[pl.ds(pl.multiple_of(b * (MOE_BLOCK * ROW_SUB), MOE_BLOCK * ROW_SUB), MOE_BLOCK * ROW_SUB)]
            return pltpu.make_async_copy(zblk_ref, dst, zsem)

        def per_expert(e, carry):
            lax.fori_loop(padlo_ref[e], pend_ref[e], lambda r, c: (zero_row(r).start(), c)[1], 0)
            return carry

        lax.fori_loop(0, N_EXPERTS, per_expert, 0)
        first_free = pend_ref[N_EXPERTS - 1] // MOE_BLOCK
        lax.fori_loop(first_free, n_rows // MOE_BLOCK, lambda b, c: (zero_blk(b).start(), c)[1], 0)

        def per_expert_wait(e, carry):
            lax.fori_loop(padlo_ref[e], pend_ref[e], lambda r, c: (zero_row(0).wait(), c)[1], 0)
            return carry

        lax.fori_loop(0, N_EXPERTS, per_expert_wait, 0)
        lax.fori_loop(first_free, n_rows // MOE_BLOCK, lambda b, c: (zero_blk(0).wait(), c)[1], 0)

    def wait(t, carry):
        for k in range(TOP_K):
            pltpu.make_async_copy(_row_slab(stage_ref, 0), _row_slab(rows_hbm, 0), sem).wait()
        return carry

    lax.fori_loop(0, DISPATCH_TB, wait, 0)


def _expert_kernel(be_ref, x_ref, wg_ref, wu_ref, wd_ref, o_ref):
    del be_ref
    halves = [_unpack_cols(x_ref[_col_block(MOE_BLOCK, c), :]) for c in range(ROW_SUB)]
    x = jnp.concatenate([lo for lo, _ in halves] + [hi for _, hi in halves], axis=1).astype(BF16)
    hid = (_silu(_dot(x, wg_ref[...])) * _dot(x, wu_ref[...])).astype(BF16)
    y = _dot(hid, wd_ref[...])
    for c in range(ROW_SUB):
        o_ref[_col_block(MOE_BLOCK, c), :] = _pack_cols(y, c)


def _combine_kernel(dest_ref, dnext_ref, h_ref, w_ref, g_ref, b_ref, yrows_hbm, of_ref, ob_ref,
                    ybuf_ref, acc_ref, sem):
    i = pl.program_id(0)
    n = pl.num_programs(0)
    slot = i % 2

    def gather(d_ref, to_slot):
        def start(t, carry):
            for k in range(TOP_K):
                pltpu.make_async_copy(_row_slab(yrows_hbm, d_ref[TOP_K * t + k]),
                                      _row_slab(ybuf_ref.at[to_slot, k], t), sem.at[to_slot]).start(priority=k)
            return carry

        lax.fori_loop(0, COMBINE_TB, start, 0)

    @pl.when(i == 0)
    def _():
        gather(dest_ref, 0)

    @pl.when(i + 1 < n)
    def _():
        gather(dnext_ref, 1 - slot)

    def wait(t, carry):
        for k in range(TOP_K):
            pltpu.make_async_copy(_row_slab(yrows_hbm, 0), _row_slab(ybuf_ref.at[slot, k], 0), sem.at[slot]).wait()
        return carry

    lax.fori_loop(0, COMBINE_TB, wait, 0)
    w = w_ref[...]
    y0_ref = ybuf_ref.at[slot, 0]
    y1_ref = ybuf_ref.at[slot, 1]
    for c in range(ROW_SUB):
        cb = _col_block(COMBINE_TB, c)
        lo0, hi0 = _unpack_cols(y0_ref[cb, :])
        lo1, hi1 = _unpack_cols(y1_ref[cb, :])
        acc_ref[:, c * LANES:(c + 1) * LANES] = lo0 * w[:, 0:1] + lo1 * w[:, 1:2]
        acc_ref[:, (c + ROW_SUB) * LANES:(c + ROW_SUB + 1) * LANES] = hi0 * w[:, 0:1] + hi1 * w[:, 1:2]
    y = _layernorm_rows(ALPHA * h_ref[...] + acc_ref[...], g_ref[...], b_ref[...])
    of_ref[...] = y
    ob_ref[...] = y.astype(BF16)


def _moe_layer(hf, hb, w_rg, b_rg, w_re, b_re, w_e_gate, w_e_up, w_e_down, ln_g, ln_b):
    T, D = hf.shape
    n_rows = T * TOP_K + N_EXPERTS * MOE_BLOCK
    n_blocks = n_rows // MOE_BLOCK
    w_r = jnp.concatenate([w_re, w_rg, jnp.zeros((D, LANES - N_EXPERTS - N_GROUPS), F32)], axis=-1)
    b_r = jnp.concatenate([b_re, b_rg, jnp.zeros((LANES - N_EXPERTS - N_GROUPS,), F32)]).reshape(1, LANES)
    w_hi = w_r.astype(BF16)
    w_lo = (w_r - w_hi.astype(F32)).astype(BF16)
    const = lambda i: (0, 0)
    rows = lambda i: (i, 0)
    eid, wts, rank, cnt = pl.pallas_call(
        _router_kernel,
        grid=(T // R_TM,),
        in_specs=[pl.BlockSpec((R_TM, D), rows), pl.BlockSpec((D, LANES), const),
                  pl.BlockSpec((D, LANES), const), pl.BlockSpec((1, LANES), const)],
        out_specs=[pl.BlockSpec((R_TM, LANES), rows), pl.BlockSpec((R_TM, LANES), rows),
                   pl.BlockSpec((R_TM, LANES), rows), pl.BlockSpec((8, LANES), const)],
        out_shape=[jax.ShapeDtypeStruct((T, LANES), jnp.int32), jax.ShapeDtypeStruct((T, LANES), F32),
                   jax.ShapeDtypeStruct((T, LANES), jnp.int32), jax.ShapeDtypeStruct((8, LANES), jnp.int32)],
        scratch_shapes=[pltpu.VMEM((8, LANES), F32)],
        compiler_params=_cp("arbitrary"),
        name="moe_router",
    )(hf, w_hi, w_lo, b_r)
    counts = cnt[0, :N_EXPERTS]
    padded = (counts + MOE_BLOCK - 1) // MOE_BLOCK * MOE_BLOCK
    pends = jnp.cumsum(padded)
    pstarts = pends - padded
    blk_start = jnp.arange(n_blocks, dtype=jnp.int32) * MOE_BLOCK
    blk_exp = jnp.minimum(jnp.sum((pends[None, :] <= blk_start[:, None]).astype(jnp.int32), axis=1), N_EXPERTS - 1)
    sel = eid[:, :TOP_K, None] == jnp.arange(N_EXPERTS, dtype=jnp.int32)
    dest = (jnp.sum(jnp.where(sel, pstarts, 0), axis=-1) + rank[:, :TOP_K]).astype(jnp.int32).reshape(T * TOP_K)
    any_spec = pl.BlockSpec(memory_space=pl.ANY)
    xrows = pl.pallas_call(
        _dispatch_kernel,
        grid_spec=pltpu.PrefetchScalarGridSpec(
            num_scalar_prefetch=2,
            grid=(T // DISPATCH_TB,),
            in_specs=[pl.BlockSpec((TOP_K * DISPATCH_TB,), lambda i, *_: (i,), memory_space=pltpu.SMEM),
                      pl.BlockSpec((DISPATCH_TB, D), lambda i, *_: (i, 0))],
            out_specs=any_spec,
            scratch_shapes=[pltpu.VMEM((DISPATCH_TB * ROW_SUB, LANES), ROW_DT),
                            pltpu.VMEM((ROW_SUB, LANES), ROW_DT),
                            pltpu.VMEM((MOE_BLOCK * ROW_SUB, LANES), ROW_DT),
                            pltpu.SemaphoreType.DMA(()), pltpu.SemaphoreType.DMA(())]),
        out_shape=jax.ShapeDtypeStruct((n_rows * ROW_SUB, LANES), ROW_DT),
        compiler_params=_cp("arbitrary"),
        name="moe_dispatch",
    )((pstarts + counts).astype(jnp.int32), pends.astype(jnp.int32), dest, hf)
    slab = lambda i, *_: (i, 0)
    yrows = pl.pallas_call(
        _expert_kernel,
        grid_spec=pltpu.PrefetchScalarGridSpec(
            num_scalar_prefetch=1,
            grid=(n_blocks,),
            in_specs=[pl.BlockSpec((MOE_BLOCK * ROW_SUB, LANES), slab),
                      pl.BlockSpec((None, D, D_FF), lambda i, be: (be[i], 0, 0)),
                      pl.BlockSpec((None, D, D_FF), lambda i, be: (be[i], 0, 0)),
                      pl.BlockSpec((None, D_FF, D), lambda i, be: (be[i], 0, 0))],
            out_specs=pl.BlockSpec((MOE_BLOCK * ROW_SUB, LANES), slab)),
        out_shape=jax.ShapeDtypeStruct((n_rows * ROW_SUB, LANES), ROW_DT),
        compiler_params=_cp("arbitrary"),
        name="moe_experts",
    )(blk_exp, xrows, w_e_gate.astype(BF16), w_e_up.astype(BF16), w_e_down.astype(BF16))
    n_steps = T // COMBINE_TB
    dspec = lambda f: pl.BlockSpec((TOP_K * COMBINE_TB,), f, memory_space=pltpu.SMEM)
    return pl.pallas_call(
        _combine_kernel,
        grid=(n_steps,),
        in_specs=[dspec(lambda i: (i,)), dspec(lambda i: (jnp.minimum(i + 1, n_steps - 1),)),
                  pl.BlockSpec((COMBINE_TB, D), rows), pl.BlockSpec((COMBINE_TB, LANES), rows),
                  pl.BlockSpec((1, D), const), pl.BlockSpec((1, D), const), any_spec],
        out_specs=[pl.BlockSpec((COMBINE_TB, D), rows), pl.BlockSpec((COMBINE_TB, D), rows)],
        out_shape=[jax.ShapeDtypeStruct((T, D), F32), jax.ShapeDtypeStruct((T, D), BF16)],
        scratch_shapes=[pltpu.VMEM((2, TOP_K, COMBINE_TB * ROW_SUB, LANES), ROW_DT),
                        pltpu.VMEM((COMBINE_TB, D), F32), pltpu.SemaphoreType.DMA((2,))],
        compiler_params=_cp("arbitrary"),
        name="moe_combine_ln2",
    )(dest, dest, hf, wts, ln_g.reshape(1, D), ln_b.reshape(1, D), yrows)


def kernel(x, ln_in_g, ln_in_b, rel_bias, w_in, g_cq, w_uq, g_ckv, w_ukv, w_dw_c, b_dw_c, ln_c_g, ln_c_b,
           w_conv_d, b_conv_d, a_log_f, a_log_b, dt_bias_f, dt_bias_b, d_skip, g_norm_d, w_br, w_gate, b_gate,
           w_out, ln1_g, ln1_b, w_rg, b_rg, w_re, b_re, w_e_gate, w_e_up, w_e_down, ln2_g, ln2_b):
    B, S, D = x.shape
    T = B * S
    hf, hb = _layernorm(x.reshape(T, D), ln_in_g, ln_in_b)
    a_bias = _mixer_a_bias(rel_bias, S)
    for l in range(DEPTH):
        w_a, w_r = _in_proj_weights(w_in[l])
        qkv = _matmul(hb, w_a.astype(BF16), F32, 512, 3 * A_WIDTH, "in_proj_a")
        rest = _matmul(hb, w_r.astype(BF16), F32, 512, R_WIDTH // 2, "in_proj_rest")
        y_a = _mixer_a(qkv, a_bias, B, S)
        y_b = _mixer_b(rest, g_cq[l], w_uq[l], g_ckv[l], w_ukv[l], B, S)
        y_c = _mixer_c(rest, w_dw_c[l], b_dw_c[l], ln_c_g[l], ln_c_b[l], B, S)
        y_d = _mixer_d(rest, w_conv_d[l], b_conv_d[l], a_log_f[l], a_log_b[l], dt_bias_f[l], dt_bias_b[l],
                       d_skip[l], g_norm_d[l], B, S)
        h1f, h1b = _merge(hb, hf, (y_a, y_b, y_c, y_d), w_gate[l], b_gate[l], w_br[l], w_out[l], ln1_g[l], ln1_b[l])
        hf, hb = _moe_layer(h1f, h1b, w_rg[l], b_rg[l], w_re[l], b_re[l], w_e_gate[l], w_e_up[l], w_e_down[l],
                            ln2_g[l], ln2_b[l])
    return hf.reshape(B, S, D)
```

```python
import functools

import numpy as np
import jax
import jax.numpy as jnp
from jax import lax
from jax.experimental import pallas as pl
from jax.experimental.pallas import tpu as pltpu

F32 = jnp.float32
BF16 = jnp.bfloat16

D_MODEL = 2048
DEPTH = 2
A_HEADS = 8
A_HEAD_DIM = 64
A_WIDTH = A_HEADS * A_HEAD_DIM
A_PATTERNS = ((128, 1), (512, 4), (2048, 16))
A_BAND = 64
REL_BUCKETS = 32
REL_MAX_DIST = 1024
B_HEADS = 8
B_NOPE = 64
B_ROPE = 32
B_V = 64
B_Q_LORA = 512
B_KV_LORA = 256
ROPE_THETA = 10000.0
C_CH = 512
C_KERNEL = 31
D_HEADS = 8
D_HEAD_DIM = 64
D_INNER = D_HEADS * D_HEAD_DIM
D_STATE = 128
D_GROUPS = 2
D_CONV = 5
D_CHUNK = 128
N_BRANCH = 4
BRANCH_W = 512
N_GROUPS = 4
EXP_PER_GROUP = 8
N_EXPERTS = N_GROUPS * EXP_PER_GROUP
TOP_K = 2
D_FF = 512
MOE_BLOCK = 128
ALPHA = (2 * DEPTH) ** 0.25
EPS = 1e-5
NEG_INF = -1e30

LANES = 128
R_GLU, R_XBC, R_CQ, R_Z, R_CKV, R_KR, R_DT = 0, 1024, 2048, 2560, 3072, 3328, 3456
R_WIDTH = 3584
VMEM_LIMIT = 56 * 1024 * 1024


def _cp(*sem):
    return pltpu.CompilerParams(dimension_semantics=sem, vmem_limit_bytes=VMEM_LIMIT)


def _dot(a, b):
    return jnp.dot(a, b, preferred_element_type=F32)


def _dot_nt(a, b):
    return lax.dot_general(a, b, (((1,), (1,)), ((), ())), preferred_element_type=F32)


def _split3(x):
    hi = x.astype(BF16)
    r1 = x - hi.astype(F32)
    mid = r1.astype(BF16)
    lo = (r1 - mid.astype(F32)).astype(BF16)
    return hi, mid, lo


def _layernorm_rows(x, g, b):
    mu = jnp.mean(x, axis=-1, keepdims=True)
    xc = x - mu
    var = jnp.mean(xc * xc, axis=-1, keepdims=True)
    return xc * lax.rsqrt(var + EPS) * g + b


def _sigmoid(x):
    return 1.0 / (1.0 + jnp.exp(-x))


def _silu(x):
    return x * _sigmoid(x)


def _ln_kernel(x_ref, g_ref, b_ref, of_ref, ob_ref):
    y = _layernorm_rows(x_ref[...], g_ref[...], b_ref[...])
    of_ref[...] = y
    ob_ref[...] = y.astype(BF16)


def _layernorm(x, g, b, tm=256):
    T, D = x.shape
    return pl.pallas_call(
        _ln_kernel,
        grid=(T // tm,),
        in_specs=[pl.BlockSpec((tm, D), lambda i: (i, 0)),
                  pl.BlockSpec((1, D), lambda i: (0, 0)),
                  pl.BlockSpec((1, D), lambda i: (0, 0))],
        out_specs=[pl.BlockSpec((tm, D), lambda i: (i, 0)),
                   pl.BlockSpec((tm, D), lambda i: (i, 0))],
        out_shape=[jax.ShapeDtypeStruct((T, D), F32), jax.ShapeDtypeStruct((T, D), BF16)],
        compiler_params=_cp("parallel"),
        name="ln_in",
    )(x, g.reshape(1, D), b.reshape(1, D))


def _mm_kernel(x_ref, w_ref, o_ref):
    o_ref[...] = _dot(x_ref[...], w_ref[...]).astype(o_ref.dtype)


def _matmul(x, w, out_dtype, tm, tn, name):
    M, K = x.shape
    N = w.shape[1]
    return pl.pallas_call(
        _mm_kernel,
        grid=(N // tn, M // tm),
        in_specs=[pl.BlockSpec((tm, K), lambda j, i: (i, 0)),
                  pl.BlockSpec((K, tn), lambda j, i: (0, j))],
        out_specs=pl.BlockSpec((tm, tn), lambda j, i: (i, j)),
        out_shape=jax.ShapeDtypeStruct((M, N), out_dtype),
        compiler_params=_cp("parallel", "parallel"),
        name=name,
    )(x, w)


def _t5_bucket(rel):
    half = REL_BUCKETS // 2
    max_exact = half // 2
    n = np.abs(rel)
    large = max_exact + (np.log(np.maximum(n, 1) / max_exact) / np.log(REL_MAX_DIST / max_exact)
                         * (half - max_exact)).astype(np.int32)
    large = np.minimum(large, half - 1)
    return (rel > 0).astype(np.int32) * half + np.where(n < max_exact, n, large)


def _a_window(L):
    return min(2 * LANES, L)


def _a_bias_tiles(rel_bias, d, L):
    W = _a_window(L)
    offs = (0,) if L == LANES else (0, -A_BAND, -2 * A_BAND)
    qi = np.arange(LANES)[:, None]
    kj = np.arange(W)[None, :]
    rel = np.stack([kj - qi + off for off in offs], axis=0)
    valid = np.abs(rel) <= A_BAND
    onehot = (jnp.asarray(_t5_bucket(rel * d), jnp.int32)[..., None] == jnp.arange(REL_BUCKETS)).astype(F32)
    b = jnp.einsum('vqkb,bh->vhqk', onehot, rel_bias.astype(F32), precision=lax.Precision.HIGHEST)
    b = jnp.where(valid[:, None], b, NEG_INF)
    return b.reshape(len(offs), A_HEADS // 2, 2 * LANES, W)


A_GROUP = 2


def _attn_a_kernel(q_ref, k_ref, v_ref, b16_ref, b4_ref, b1_ref, y_ref,
                   q4_ref, k4_ref, v4_ref, m_ref, l_ref, acc_ref, tmp_ref, *, S):
    (_, d1), (_, d4), (_, d16) = A_PATTERNS
    lane = lax.broadcasted_iota(jnp.int32, (LANES, LANES), 1)
    head0 = lane < A_HEAD_DIM
    scale = A_HEAD_DIM ** -0.5

    def partial_softmax(qs, ks, vs, bias_ref):
        q2 = jnp.concatenate([jnp.where(head0, qs, 0.0), jnp.where(head0, 0.0, qs)], axis=0).astype(BF16)
        s = _dot_nt(q2, ks.astype(BF16)) * scale + bias_ref[...]
        m = jnp.max(s, axis=-1, keepdims=True)
        p = jnp.exp(s - m).astype(BF16)
        num = _dot(p, vs.astype(BF16))
        den = _dot(p, jnp.ones((vs.shape[0], LANES), BF16))
        both = lambda t: jnp.where(head0, t[:LANES], t[LANES:])
        return both(jnp.broadcast_to(m, (2 * LANES, LANES))), both(den), both(num)

    def fold(old, new):
        (m_old, l_old, a_old), (m_new, l_new, a_new) = old, new
        m = jnp.maximum(m_old, m_new)
        c_old = jnp.exp(m_old - m)
        c_new = jnp.exp(m_new - m)
        return m, c_old * l_old + c_new * l_new, c_old * a_old + c_new * a_new

    stat_refs = (m_ref, l_ref, acc_ref)

    def get(c, rows):
        return tuple(ref.at[c][rows, :] for ref in stat_refs)

    def put(c, rows, stats):
        for ref, val in zip(stat_refs, stats):
            ref.at[c][rows, :] = val

    def grouped(n, group, unit):
        def trip(g, carry):
            pending = [unit(g * group + u) for u in range(group)]
            for finish in pending:
                finish()
            return carry

        lax.fori_loop(0, n // group, trip, 0)

    L4 = S // d4
    sub = d16 // d4
    assert S // d16 == LANES and sub == d4
    for c in range(d4):
        cls = pl.ds(c, L4, stride=d4)
        q4_ref[c] = q_ref[cls, :]
        k4_ref[c] = k_ref[cls, :]
        v4_ref[c] = v_ref[cls, :]

    def unit16(t):
        c = t % d4
        rows = pl.ds(t // d4, LANES, stride=sub)
        stats = partial_softmax(q4_ref.at[c][rows, :], k4_ref.at[c][rows, :], v4_ref.at[c][rows, :], b16_ref.at[0])
        return lambda: put(c, rows, stats)

    grouped(d16, A_GROUP, unit16)

    def window(i, L):
        nqb = L // LANES
        ws = pl.multiple_of(jnp.clip(i * LANES - A_BAND, 0, L - 2 * LANES), A_BAND)
        return pl.ds(ws, 2 * LANES), jnp.where(i == 0, 0, jnp.where(i == nqb - 1, 2, 1))

    def unit4(t):
        c = t % d4
        i = t // d4
        keys, var = window(i, L4)
        rows = pl.ds(pl.multiple_of(i * LANES, LANES), LANES)
        stats = partial_softmax(q4_ref.at[c][rows, :], k4_ref.at[c][keys, :], v4_ref.at[c][keys, :], b4_ref.at[var])
        return lambda: put(c, rows, fold(get(c, rows), stats))

    grouped(d4 * (L4 // LANES), A_GROUP, unit4)

    def unit1(i):
        keys, var = window(i, S)
        rows = pl.ds(pl.multiple_of(i * LANES, LANES), LANES)
        stats = partial_softmax(q_ref[rows, :], k_ref[keys, :], v_ref[keys, :], b1_ref.at[var])

        def finish():
            part = pl.ds(pl.multiple_of(i * (LANES // d4), LANES // d4), LANES // d4)
            for n, ref in enumerate(stat_refs):
                for c in range(d4):
                    tmp_ref.at[n][pl.ds(c, LANES // d4, stride=d4), :] = ref.at[c][part, :]
            _, l, a = fold(tuple(tmp_ref[n] for n in range(len(stat_refs))), stats)
            y_ref[rows, :] = (a / l).astype(y_ref.dtype)

        return finish

    grouped(S // LANES, A_GROUP, unit1)


def _mixer_a_bias(rel_bias, S):
    return tuple(_a_bias_tiles(rel_bias, d, S // d) for _, d in A_PATTERNS)


def _mixer_a(qkv, bias, B, S):
    b1, b4, b16 = bias
    npair = A_HEADS // 2
    pair_bias = lambda t: pl.BlockSpec((t.shape[0], None) + t.shape[2:], lambda b, hp: (0, hp, 0, 0))
    slab = lambda first: pl.BlockSpec((S, LANES), lambda b, hp: (b, first + hp))
    return pl.pallas_call(
        functools.partial(_attn_a_kernel, S=S),
        grid=(B, npair),
        in_specs=[slab(0), slab(npair), slab(2 * npair), pair_bias(b16), pair_bias(b4), pair_bias(b1)],
        out_specs=pl.BlockSpec((S, LANES), lambda b, hp: (b, hp)),
        out_shape=jax.ShapeDtypeStruct((B * S, A_WIDTH), BF16),
        scratch_shapes=[pltpu.VMEM((A_PATTERNS[1][1], S // A_PATTERNS[1][1], LANES), F32)] * 6
        + [pltpu.VMEM((3, LANES, LANES), F32)],
        compiler_params=_cp("parallel", "parallel"),
        name="attn_a",
    )(qkv, qkv, qkv, b16, b4, b1)


MLA_Q_SCALE = float((B_NOPE + B_ROPE) ** -0.5 * np.log2(np.e))


def _mla_proj_kernel(cq_ref, ckv_ref, kr_ref, gq_ref, gkv_ref, wqm_ref, wqs_ref, wk_ref, wv_ref, vone_ref,
                     ek_ref, cosq_ref, sinq_ref, csk_ref, q_ref, k_ref, v_ref):
    cq = cq_ref[...]
    xq = (cq * lax.rsqrt(jnp.mean(cq * cq, axis=-1, keepdims=True) + EPS) * gq_ref[...]).astype(BF16)
    ckv = ckv_ref[...]
    xkv = (ckv * lax.rsqrt(jnp.mean(ckv * ckv, axis=-1, keepdims=True) + EPS) * gkv_ref[...]).astype(BF16)
    qm = _dot(xq, wqm_ref[...])
    qs = _dot(xq, wqs_ref[...])
    cosq = cosq_ref[...] * MLA_Q_SCALE
    sinq = sinq_ref[...] * MLA_Q_SCALE
    t = kr_ref[...] * csk_ref[...]
    t_hi = t.astype(BF16)
    t_lo = (t - t_hi.astype(F32)).astype(BF16)
    kk = _dot(xkv, wk_ref[...]) + _dot(t_hi, ek_ref[...]) + _dot(t_lo, ek_ref[...])
    for h in range(B_HEADS):
        sl = slice(h * LANES, (h + 1) * LANES)
        q_ref[:, sl] = (qm[:, sl] * cosq + qs[:, sl] * sinq).astype(BF16)
    k_ref[...] = kk.astype(BF16)
    v_ref[...] = (_dot(xkv, wv_ref[...]) + vone_ref[...]).astype(BF16)


def _mla_attn_kernel(q_ref, k_ref, v_ref, o_ref):
    outs = []
    for hh in range(2):
        sl = slice(hh * LANES, (hh + 1) * LANES)
        s = _dot_nt(q_ref[:, sl], k_ref[:, sl])
        p = jnp.exp2(s - jnp.max(s, axis=-1, keepdims=True))
        outs.append(_dot(p.astype(BF16), v_ref[:, sl]))
    lane = lax.broadcasted_iota(jnp.int32, outs[0].shape, 1)
    acc = jnp.where(lane < B_V, outs[0], outs[1])
    den = pltpu.roll(jnp.where(lane < B_V, outs[1], outs[0]), B_V, axis=1)
    o_ref[...] = (acc / den).astype(o_ref.dtype)


def _mla_tables(S):
    inv_freq = ROPE_THETA ** (-jnp.arange(0, B_ROPE, 2, dtype=F32) / B_ROPE)
    ang = jnp.arange(S, dtype=F32)[:, None] * inv_freq[None]
    cos, sin = jnp.cos(ang), jnp.sin(ang)
    cos2 = jnp.concatenate([cos, cos], axis=-1)
    sin2 = jnp.concatenate([sin, sin], axis=-1)
    ones = jnp.ones((S, B_NOPE), F32)
    zn = jnp.zeros((S, B_NOPE), F32)
    zp = jnp.zeros((S, LANES - B_NOPE - B_ROPE), F32)
    cosq = jnp.concatenate([ones, cos2, zp], axis=-1)
    sinq = jnp.concatenate([zn, sin2, zp], axis=-1)
    csk = jnp.concatenate([cos2, sin2, jnp.zeros((S, LANES - 2 * B_ROPE), F32)], axis=-1)
    return cosq, sinq, csk


def _swap_cols(w):
    half = w.shape[-1] // 2
    return jnp.concatenate([-w[..., half:], w[..., :half]], axis=-1)


def _mla_weights(w_uq, w_ukv):
    dq = B_NOPE + B_ROPE
    wq = w_uq.reshape(B_Q_LORA, B_HEADS, dq)
    zpad = jnp.zeros((B_Q_LORA, B_HEADS, LANES - dq), F32)
    wqm = jnp.concatenate([wq, zpad], axis=-1).reshape(B_Q_LORA, B_HEADS * LANES)
    wqs = jnp.concatenate([jnp.zeros((B_Q_LORA, B_HEADS, B_NOPE), F32), _swap_cols(wq[..., B_NOPE:]), zpad],
                          axis=-1).reshape(B_Q_LORA, B_HEADS * LANES)
    wkv = w_ukv.reshape(B_KV_LORA, B_HEADS, B_NOPE + B_V)
    wk = jnp.concatenate([wkv[..., :B_NOPE], jnp.zeros((B_KV_LORA, B_HEADS, LANES - B_NOPE), F32)],
                         axis=-1).reshape(B_KV_LORA, B_HEADS * LANES)
    zv = jnp.zeros((B_KV_LORA, B_HEADS // 2, LANES - B_V), F32)
    wv_h = wkv[..., B_NOPE:]
    wv = jnp.stack([jnp.concatenate([wv_h[:, 0::2], zv], axis=-1),
                    jnp.concatenate([zv, wv_h[:, 1::2]], axis=-1)], axis=2).reshape(B_KV_LORA, B_HEADS * LANES)
    lane_in_pair = np.arange(B_HEADS * LANES) % (2 * LANES)
    vone = jnp.asarray(((lane_in_pair >= B_V) & (lane_in_pair < LANES + B_V)).astype(np.float32)).reshape(1, -1)
    ek = np.zeros((LANES, B_HEADS, LANES), np.float32)
    for j in range(B_ROPE):
        ek[j, :, B_NOPE + j] = 1.0
        ek[B_ROPE + j, :, B_NOPE + j] = 1.0
    ek = jnp.asarray(ek.reshape(LANES, B_HEADS * LANES))
    return wqm.astype(BF16), wqs.astype(BF16), wk.astype(BF16), wv.astype(BF16), vone, ek.astype(BF16)


def _mixer_b(rest, g_cq, w_uq, g_ckv, w_ukv, B, S, tm=512, tq=256):
    T = B * S
    wqm, wqs, wk, wv, vone, ek = _mla_weights(w_uq, w_ukv)
    cosq, sinq, csk = _mla_tables(S)
    nst = S // tm
    QW = B_HEADS * LANES
    const = lambda i: (0, 0)
    pos = lambda i: (i % nst, 0)
    q, k, v = pl.pallas_call(
        _mla_proj_kernel,
        grid=(T // tm,),
        in_specs=[pl.BlockSpec((tm, B_Q_LORA), lambda i: (i, R_CQ // B_Q_LORA)),
                  pl.BlockSpec((tm, B_KV_LORA), lambda i: (i, R_CKV // B_KV_LORA)),
                  pl.BlockSpec((tm, LANES), lambda i: (i, R_KR // LANES)),
                  pl.BlockSpec((1, B_Q_LORA), const),
                  pl.BlockSpec((1, B_KV_LORA), const),
                  pl.BlockSpec((B_Q_LORA, QW), const),
                  pl.BlockSpec((B_Q_LORA, QW), const),
                  pl.BlockSpec((B_KV_LORA, QW), const),
                  pl.BlockSpec((B_KV_LORA, QW), const),
                  pl.BlockSpec((1, QW), const),
                  pl.BlockSpec((LANES, QW), const),
                  pl.BlockSpec((tm, LANES), pos),
                  pl.BlockSpec((tm, LANES), pos),
                  pl.BlockSpec((tm, LANES), pos)],
        out_specs=[pl.BlockSpec((tm, QW), lambda i: (i, 0)),
                   pl.BlockSpec((tm, QW), lambda i: (i, 0)),
                   pl.BlockSpec((tm, QW), lambda i: (i, 0))],
        out_shape=[jax.ShapeDtypeStruct((T, QW), BF16), jax.ShapeDtypeStruct((T, QW), BF16),
                   jax.ShapeDtypeStruct((T, QW), BF16)],
        compiler_params=_cp("parallel"),
        name="mla_proj",
    )(rest, rest, rest, g_cq.reshape(1, -1), g_ckv.reshape(1, -1), wqm, wqs, wk, wv, vone, ek, cosq, sinq, csk)
    y = pl.pallas_call(
        _mla_attn_kernel,
        grid=(B, B_HEADS // 2, S // tq),
        in_specs=[pl.BlockSpec((None, tq, 2 * LANES), lambda b, hp, i: (b, i, hp)),
                  pl.BlockSpec((None, S, 2 * LANES), lambda b, hp, i: (b, 0, hp)),
                  pl.BlockSpec((None, S, 2 * LANES), lambda b, hp, i: (b, 0, hp))],
        out_specs=pl.BlockSpec((None, tq, 2 * B_V), lambda b, hp, i: (b, i, hp)),
        out_shape=jax.ShapeDtypeStruct((B, S, B_HEADS * B_V), BF16),
        compiler_params=_cp("parallel", "parallel", "arbitrary"),
        name="mla_attn",
    )(q.reshape(B, S, QW), k.reshape(B, S, QW), v.reshape(B, S, QW))
    return y.reshape(T, B_HEADS * B_V)


C_PAD = 16
C_ROWS = 128


SUBLANES = 8


def _tap_span(first, ntaps, rows):
    return rows + ((first + ntaps - 1) // SUBLANES) * SUBLANES


def _depthwise_taps(win_ref, sh_ref, w_ref, bias, ls, first, ntaps, rows):
    acc = jnp.broadcast_to(bias, (rows, LANES))
    span = _tap_span(first, ntaps, rows)
    for ph in range(SUBLANES):
        taps = [j for j in range(ntaps) if (first + j) % SUBLANES == ph]
        if not taps:
            continue
        if len(taps) == 1:
            j = taps[0]
            acc = acc + w_ref[j:j + 1, ls] * win_ref[first + j:first + j + rows, ls]
            continue
        sh_ref[0:span, :] = win_ref[ph:ph + span, ls]
        for j in taps:
            a = (first + j) // SUBLANES * SUBLANES
            acc = acc + w_ref[j:j + 1, ls] * sh_ref[a:a + rows, :]
    return acc


def _fill_window(win_ref, load_rows, r0, i, nblk, rows, pad):
    width = win_ref.shape[1]
    win_ref[pad:pad + rows, :] = load_rows(r0, rows)

    @pl.when(i > 0)
    def _():
        win_ref[0:pad, :] = load_rows(r0 - pad, pad)

    @pl.when(i == 0)
    def _():
        win_ref[0:pad, :] = jnp.zeros((pad, width), F32)

    @pl.when(i < nblk - 1)
    def _():
        win_ref[pad + rows:pad + rows + pad, :] = load_rows(r0 + rows, pad)

    @pl.when(i == nblk - 1)
    def _():
        win_ref[pad + rows:pad + rows + pad, :] = jnp.zeros((pad, width), F32)


def _conv_c_kernel(glu_ref, w_ref, b_ref, g_ref, beta_ref, o_ref, win_ref, sh_ref, acc_ref, *, S):
    i = pl.program_id(1)
    r0 = pl.multiple_of(i * C_ROWS, C_ROWS)

    def glu_rows(start, n):
        rs = pl.ds(pl.multiple_of(start, C_PAD), n)
        return glu_ref[rs, 0:C_CH] * _sigmoid(glu_ref[rs, C_CH:2 * C_CH])

    _fill_window(win_ref, glu_rows, r0, i, S // C_ROWS, C_ROWS, C_PAD)
    first = C_PAD - C_KERNEL // 2
    for lb in range(C_CH // LANES):
        ls = slice(lb * LANES, (lb + 1) * LANES)
        acc_ref[:, ls] = _depthwise_taps(win_ref, sh_ref, w_ref, b_ref[:, ls], ls, first, C_KERNEL, C_ROWS)
    y = _layernorm_rows(acc_ref[...], g_ref[...], beta_ref[...])
    o_ref[...] = _silu(y).astype(o_ref.dtype)


def _mixer_c(rest, w_dw, b_dw, ln_g, ln_b, B, S):
    T = B * S
    const = lambda b, i: (0, 0)
    y = pl.pallas_call(
        functools.partial(_conv_c_kernel, S=S),
        grid=(B, S // C_ROWS),
        in_specs=[pl.BlockSpec((None, S, 2 * C_CH), lambda b, i: (b, 0, R_GLU // (2 * C_CH))),
                  pl.BlockSpec((C_KERNEL, C_CH), const),
                  pl.BlockSpec((1, C_CH), const),
                  pl.BlockSpec((1, C_CH), const),
                  pl.BlockSpec((1, C_CH), const)],
        out_specs=pl.BlockSpec((None, C_ROWS, C_CH), lambda b, i: (b, i, 0)),
        out_shape=jax.ShapeDtypeStruct((B, S, C_CH), BF16),
        scratch_shapes=[pltpu.VMEM((C_ROWS + 2 * C_PAD, C_CH), F32),
                        pltpu.VMEM((_tap_span(C_PAD - C_KERNEL // 2, C_KERNEL, C_ROWS), LANES), F32),
                        pltpu.VMEM((C_ROWS, C_CH), F32)],
        compiler_params=_cp("parallel", "parallel"),
        name="conformer_conv",
    )(rest.reshape(B, S, R_WIDTH), w_dw, b_dw.reshape(1, -1), ln_g.reshape(1, -1), ln_b.reshape(1, -1))
    return y.reshape(T, C_CH)


D_PAD = 8
XBC_W = D_INNER + 2 * D_GROUPS * D_STATE
N_PAIR = D_HEADS // 2


def _pair_expand(v, first):
    lane = lax.broadcasted_iota(jnp.int32, (v.shape[0], LANES), 1)
    lo = jnp.broadcast_to(v[:, first:first + 1], (v.shape[0], LANES))
    hi = jnp.broadcast_to(v[:, first + 1:first + 2], (v.shape[0], LANES))
    return jnp.where(lane < D_HEAD_DIM, lo, hi)


def _ssd_kernel(xbc_ref, z_ref, dt_ref, wc_ref, bc_ref, alog_ref, dtb_ref, dskip_ref, gn_ref, o_ref,
                win_ref, sh_ref, xc_ref, a_ref, dtv_ref, y_ref, st_ref, *, S):
    Q = D_CHUNK
    nchunk = S // Q
    N = D_STATE
    bm0 = D_INNER
    cm0 = D_INNER + D_GROUPS * N

    def conv_body(c, carry):
        r0 = pl.multiple_of(c * Q, Q)
        _fill_window(win_ref, lambda st, n: xbc_ref[pl.ds(pl.multiple_of(st, D_PAD), n), :], r0, c, nchunk, Q, D_PAD)
        for lb in range(XBC_W // LANES):
            ls = slice(lb * LANES, (lb + 1) * LANES)
            acc = _depthwise_taps(win_ref, sh_ref, wc_ref, bc_ref[:, ls], ls, D_PAD - D_CONV // 2, D_CONV, Q)
            xc_ref[pl.ds(r0, Q), ls] = _silu(acc)
        return carry

    lax.fori_loop(0, nchunk, conv_body, 0)

    lane1 = lax.broadcasted_iota(jnp.int32, (1, LANES), 1)
    a_row = jnp.where(lane1 < 2 * D_HEADS, -jnp.exp(alog_ref[...]), 0.0)
    xdt = dt_ref[...] + dtb_ref[...]
    dtv = jnp.maximum(xdt, 0.0) + jnp.log(1.0 + jnp.exp(-jnp.abs(xdt)))
    dtv_ref[...] = dtv
    a_ref[...] = dtv * a_row

    row = lax.broadcasted_iota(jnp.int32, (Q, Q), 0)
    col = lax.broadcasted_iota(jnp.int32, (Q, Q), 1)
    tril = row >= col
    triu = col >= row
    lane = col

    def scan_chunk(c, lower, off, finalize):
        r0 = pl.multiple_of(c * Q, Q)
        rows = pl.ds(r0, Q)
        mask = tril if lower else triu
        tri = mask.astype(BF16)
        a_hi, a_mid, a_lo = _split3(a_ref[rows, :])
        cs = _dot(tri, a_hi) + _dot(tri, a_mid) + _dot(tri, a_lo)
        cs_t = cs.T
        ecs = jnp.exp(cs)
        edge = Q - 1 if lower else 0
        edec = jnp.exp(cs[edge:edge + 1, :] - cs)
        dt_c = dtv_ref[rows, :]
        for g in range(D_GROUPS):
            bg = xc_ref[rows, bm0 + g * N:bm0 + (g + 1) * N]
            cg = xc_ref[rows, cm0 + g * N:cm0 + (g + 1) * N].astype(BF16)
            cb = _dot_nt(cg, bg.astype(BF16))
            bg_t = bg.T.astype(BF16)
            for pp in range(N_PAIR // D_GROUPS):
                p = g * (N_PAIR // D_GROUPS) + pp
                ps = slice(p * LANES, (p + 1) * LANES)
                xdt_p = xc_ref[rows, ps] * _pair_expand(dt_c, off + 2 * p)
                ms = []
                for hh in range(2):
                    k = off + 2 * p + hh
                    diff = jnp.broadcast_to(cs[:, k:k + 1], (Q, Q)) - cs_t[k:k + 1, :]
                    ms.append((jnp.exp(jnp.where(mask, diff, NEG_INF)) * cb).astype(BF16))
                x_lo = jnp.where(lane < D_HEAD_DIM, xdt_p, 0.0).astype(BF16)
                x_hi = jnp.where(lane >= D_HEAD_DIM, xdt_p, 0.0).astype(BF16)
                y_intra = _dot(jnp.concatenate(ms, axis=1), jnp.concatenate([x_lo, x_hi], axis=0))
                hp = st_ref[p]
                ecs_p = _pair_expand(ecs, off + 2 * p)
                y_new = y_intra + _dot(cg, hp.astype(BF16)) * ecs_p
                if lower:
                    y_ref[rows, ps] = y_new
                else:
                    y_ref[rows, ps] = y_ref[rows, ps] + y_new
                xs_p = (xdt_p * _pair_expand(edec, off + 2 * p)).astype(BF16)
                st_ref[p] = hp * ecs_p[edge:edge + 1, :] + _dot(bg_t, xs_p)
        if finalize:
            y = y_ref[rows, :] + xc_ref[rows, 0:D_INNER] * dskip_ref[...]
            gated = y * _silu(z_ref[rows, :])
            out = gated * lax.rsqrt(jnp.mean(gated * gated, axis=-1, keepdims=True) + EPS) * gn_ref[...]
            o_ref[rows, :] = out.astype(o_ref.dtype)

    st_ref[...] = jnp.zeros(st_ref.shape, F32)

    def fwd_body(c, carry):
        scan_chunk(c, True, 0, False)
        return carry

    lax.fori_loop(0, nchunk, fwd_body, 0)
    st_ref[...] = jnp.zeros(st_ref.shape, F32)

    def bwd_body(k, carry):
        scan_chunk(nchunk - 1 - k, False, D_HEADS, True)
        return carry

    lax.fori_loop(0, nchunk, bwd_body, 0)


def _mixer_d(rest, w_conv, b_conv, a_log_f, a_log_b, dt_bias_f, dt_bias_b, d_skip, g_norm, B, S):
    T = B * S
    pad16 = lambda f, b: jnp.concatenate([f, b, jnp.zeros((LANES - 2 * D_HEADS,), F32)]).reshape(1, LANES)
    const = lambda b: (0, 0)
    y = pl.pallas_call(
        functools.partial(_ssd_kernel, S=S),
        grid=(B,),
        in_specs=[pl.BlockSpec((None, S, XBC_W), lambda b: (b, 0, R_XBC // XBC_W)),
                  pl.BlockSpec((None, S, D_INNER), lambda b: (b, 0, R_Z // D_INNER)),
                  pl.BlockSpec((None, S, LANES), lambda b: (b, 0, R_DT // LANES)),
                  pl.BlockSpec((D_CONV, XBC_W), const),
                  pl.BlockSpec((1, XBC_W), const),
                  pl.BlockSpec((1, LANES), const),
                  pl.BlockSpec((1, LANES), const),
                  pl.BlockSpec((1, D_INNER), const),
                  pl.BlockSpec((1, D_INNER), const)],
        out_specs=pl.BlockSpec((None, S, D_INNER), lambda b: (b, 0, 0)),
        out_shape=jax.ShapeDtypeStruct((B, S, D_INNER), BF16),
        scratch_shapes=[pltpu.VMEM((D_CHUNK + 2 * D_PAD, XBC_W), F32),
                        pltpu.VMEM((_tap_span(D_PAD - D_CONV // 2, D_CONV, D_CHUNK), LANES), F32),
                        pltpu.VMEM((S, XBC_W), F32),
                        pltpu.VMEM((S, LANES), F32),
                        pltpu.VMEM((S, LANES), F32),
                        pltpu.VMEM((S, D_INNER), F32),
                        pltpu.VMEM((N_PAIR, D_STATE, LANES), F32)],
        compiler_params=_cp("parallel"),
        name="ssd_mixer",
    )(rest.reshape(B, S, R_WIDTH), rest.reshape(B, S, R_WIDTH), rest.reshape(B, S, R_WIDTH),
      w_conv, b_conv.reshape(1, -1), pad16(a_log_f, a_log_b), pad16(dt_bias_f, dt_bias_b),
      jnp.repeat(d_skip, D_HEAD_DIM).reshape(1, -1), g_norm.reshape(1, -1))
    return y.reshape(T, D_INNER)


def _in_proj_weights(w_in_l):
    o = np.cumsum((0, A_WIDTH, A_WIDTH, A_WIDTH, B_Q_LORA, B_KV_LORA, B_ROPE, 2 * C_CH,
                   D_INNER, D_INNER, D_GROUPS * D_STATE, D_GROUPS * D_STATE, 2 * D_HEADS)).tolist()
    seg = lambda n: w_in_l[:, o[n]:o[n + 1]]
    w_a = w_in_l[:, :o[3]]
    cq, ckv, kr, glu, z, xs, bm, cm, dt = (seg(n) for n in range(3, 12))
    zeros = lambda n: jnp.zeros((w_in_l.shape[0], n), w_in_l.dtype)
    w_r = jnp.concatenate([glu, xs, bm, cm, cq, z, ckv,
                           kr, _swap_cols(kr), zeros(LANES - 2 * B_ROPE),
                           dt, zeros(LANES - 2 * D_HEADS)], axis=-1)
    assert w_r.shape[1] == R_WIDTH
    return w_a, w_r


def _merge_gate_kernel(h_ref, ya_ref, yb_ref, yc_ref, yd_ref, wg_ref, bg_ref, wbr_ref, o_ref):
    h = h_ref[...]
    acc = None
    for i, y_ref in enumerate((ya_ref, yb_ref, yc_ref, yd_ref)):
        gate = _sigmoid(_dot(h, wg_ref[i]) + bg_ref[i])
        term = gate * _dot(y_ref[...], wbr_ref[i])
        acc = term if acc is None else acc + term
    o_ref[...] = acc.astype(o_ref.dtype)


def _out_ln_kernel(m_ref, w_ref, h_ref, g_ref, b_ref, of_ref, op_ref):
    y = _layernorm_rows(ALPHA * h_ref[...] + _dot(m_ref[...], w_ref[...]), g_ref[...], b_ref[...])
    of_ref[...] = y
    for c in range(ROW_SUB):
        op_ref[_col_block(y.shape[0], c), :] = _pack_cols(y, c)


def _merge(hb, hf, branches, w_gate, b_gate, w_br, w_out, ln_g, ln_b, tm=512, tn=512, tm2=256):
    T, D = hb.shape
    ybs = pl.BlockSpec((tm, BRANCH_W), lambda j, i: (i, 0))
    merged = pl.pallas_call(
        _merge_gate_kernel,
        grid=(D // tn, T // tm),
        in_specs=[pl.BlockSpec((tm, D), lambda j, i: (i, 0)), ybs, ybs, ybs, ybs,
                  pl.BlockSpec((N_BRANCH, D, tn), lambda j, i: (0, 0, j)),
                  pl.BlockSpec((N_BRANCH, 1, tn), lambda j, i: (0, 0, j)),
                  pl.BlockSpec((N_BRANCH, BRANCH_W, tn), lambda j, i: (0, 0, j))],
        out_specs=pl.BlockSpec((tm, tn), lambda j, i: (i, j)),
        out_shape=jax.ShapeDtypeStruct((T, D), BF16),
        compiler_params=_cp("parallel", "parallel"),
        name="merge_gate",
    )(hb, *branches, w_gate.astype(BF16), b_gate.reshape(N_BRANCH, 1, D), w_br.astype(BF16))
    const = lambda i: (0, 0)
    rows = lambda i: (i, 0)
    return pl.pallas_call(
        _out_ln_kernel,
        grid=(T // tm2,),
        in_specs=[pl.BlockSpec((tm2, D), rows), pl.BlockSpec((D, D), const), pl.BlockSpec((tm2, D), rows),
                  pl.BlockSpec((1, D), const), pl.BlockSpec((1, D), const)],
        out_specs=[pl.BlockSpec((tm2, D), rows), pl.BlockSpec((tm2 * ROW_SUB, LANES), rows)],
        out_shape=[jax.ShapeDtypeStruct((T, D), F32), jax.ShapeDtypeStruct((T * ROW_SUB, LANES), ROW_DT)],
        compiler_params=_cp("parallel"),
        name="out_proj_ln1",
    )(merged, w_out.astype(BF16), hf, ln_g.reshape(1, D), ln_b.reshape(1, D))


R_TM = 256
COMBINE_TB = 128


def _router_kernel(h_ref, whi_ref, wlo_ref, b_ref, eid_ref, wts_ref, rank_ref, cnt_ref, carry_ref):
    i = pl.program_id(0)

    @pl.when(i == 0)
    def _():
        carry_ref[...] = jnp.zeros(carry_ref.shape, F32)

    x = h_ref[...]
    tm = x.shape[0]
    xh = x.astype(BF16)
    xl = (x - xh.astype(F32)).astype(BF16)
    whi = whi_ref[...]
    logits = _dot(xh, whi) + _dot(xh, wlo_ref[...]) + _dot(xl, whi) + b_ref[...]
    lane = lax.broadcasted_iota(jnp.int32, (tm, LANES), 1)
    big = jnp.int32(4 * LANES)
    is_g = (lane >= N_EXPERTS) & (lane < N_EXPERTS + N_GROUPS)
    lg = jnp.where(is_g, logits, NEG_INF)
    gmax = jnp.max(lg, axis=-1, keepdims=True)
    gidx = jnp.min(jnp.where(lg == gmax, lane - N_EXPERTS, big), axis=-1, keepdims=True)
    g_w = 1.0 / jnp.sum(jnp.where(is_g, jnp.exp(lg - gmax), 0.0), axis=-1, keepdims=True)
    in_grp = (lane < N_EXPERTS) & ((lane // EXP_PER_GROUP) == gidx)
    le = jnp.where(in_grp, logits, NEG_INF)
    e1 = jnp.max(le, axis=-1, keepdims=True)
    i1 = jnp.min(jnp.where(le == e1, lane, big), axis=-1, keepdims=True)
    le2 = jnp.where(lane == i1, NEG_INF, le)
    e2 = jnp.max(le2, axis=-1, keepdims=True)
    i2 = jnp.min(jnp.where(le2 == e2, lane, big), axis=-1, keepdims=True)
    zsum = jnp.sum(jnp.where(in_grp, jnp.exp(le - e1), 0.0), axis=-1, keepdims=True)
    p1 = 1.0 / zsum
    p2 = jnp.exp(e2 - e1) / zsum
    w1 = g_w * p1 / (p1 + p2)
    w2 = g_w * p2 / (p1 + p2)
    oh1 = lane == i1
    oh2 = lane == i2
    ohs = (oh1 | oh2).astype(BF16)
    row = lax.broadcasted_iota(jnp.int32, (tm, tm), 0)
    col = lax.broadcasted_iota(jnp.int32, (tm, tm), 1)
    before = _dot((row > col).astype(BF16), ohs) + carry_ref[0:1, :]
    r1 = jnp.sum(jnp.where(oh1, before, 0.0), axis=-1, keepdims=True)
    r2 = jnp.sum(jnp.where(oh2, before, 0.0), axis=-1, keepdims=True)
    total = carry_ref[0:1, :] + jnp.sum(ohs.astype(F32), axis=0, keepdims=True)
    carry_ref[...] = jnp.broadcast_to(total, carry_ref.shape)
    cnt_ref[...] = jnp.broadcast_to(total, cnt_ref.shape).astype(jnp.int32)
    eid_ref[...] = jnp.where(lane == 0, i1, jnp.where(lane == 1, i2, 0))
    wts_ref[...] = jnp.where(lane == 0, w1, jnp.where(lane == 1, w2, 0.0))
    rank_ref[...] = jnp.where(lane == 0, r1, jnp.where(lane == 1, r2, 0.0)).astype(jnp.int32)


ROW_SUB = D_MODEL // LANES // 2
ROW_DT = jnp.uint32


def _row_slab(ref, r):
    return ref.at[pl.ds(pl.multiple_of(r * ROW_SUB, ROW_SUB), ROW_SUB)]


def _col_block(n, c):
    return pl.ds(c, n, stride=ROW_SUB)


def _pack_cols(x, c):
    as_bits = lambda t: lax.bitcast_convert_type(t.astype(BF16).astype(F32), ROW_DT)
    lo = as_bits(x[:, c * LANES:(c + 1) * LANES])
    hi = as_bits(x[:, (c + ROW_SUB) * LANES:(c + ROW_SUB + 1) * LANES])
    return (lo >> 16) | hi


def _unpack_cols(w):
    return (lax.bitcast_convert_type(w << 16, F32),
            lax.bitcast_convert_type(w & jnp.uint32(0xFFFF0000), F32))


def _expert_kernel(be_ref, src_ref, src_next_ref, dst_ref, h_hbm, wg_ref, wu_ref, wd_ref, out_hbm,
                   xbuf_ref, ybuf_ref, wgb_ref, wub_ref, wdb_ref, gsem, ssem):
    i = pl.program_id(0)
    n = pl.num_programs(0)
    slot = i % 2

    @pl.when((i == 0) | (be_ref[i] != be_ref[jnp.maximum(i - 1, 0)]))
    def _():
        wgb_ref[...] = wg_ref[...].astype(BF16)
        wub_ref[...] = wu_ref[...].astype(BF16)
        wdb_ref[...] = wd_ref[...].astype(BF16)

    def gather_copy(tok, to_slot, t):
        return pltpu.make_async_copy(_row_slab(h_hbm, tok), _row_slab(xbuf_ref.at[to_slot], t), gsem.at[to_slot])

    def scatter_copy(from_slot, t, row):
        return pltpu.make_async_copy(_row_slab(ybuf_ref.at[from_slot], t), _row_slab(out_hbm, row), ssem.at[from_slot])

    def gather(idx_ref, to_slot):
        for t in range(MOE_BLOCK):
            gather_copy(idx_ref[t], to_slot, t).start(priority=t % 2)

    def drain_scatter(which):
        for t in range(MOE_BLOCK):
            scatter_copy(which, 0, 0).wait()

    @pl.when(i == 0)
    def _():
        gather(src_ref, 0)

    @pl.when(i + 1 < n)
    def _():
        gather(src_next_ref, 1 - slot)

    for t in range(MOE_BLOCK):
        gather_copy(0, slot, 0).wait()
    x_ref = xbuf_ref.at[slot]
    halves = [_unpack_cols(x_ref[_col_block(MOE_BLOCK, c), :]) for c in range(ROW_SUB)]
    x = jnp.concatenate([lo for lo, _ in halves] + [hi for _, hi in halves], axis=1).astype(BF16)
    hid = (_silu(_dot(x, wgb_ref[...])) * _dot(x, wub_ref[...])).astype(BF16)
    y = _dot(hid, wdb_ref[...])

    @pl.when(i >= 2)
    def _():
        drain_scatter(slot)

    y_ref = ybuf_ref.at[slot]
    for c in range(ROW_SUB):
        y_ref[_col_block(MOE_BLOCK, c), :] = _pack_cols(y, c)
    for t in range(MOE_BLOCK):
        scatter_copy(slot, t, dst_ref[t]).start(priority=t % 2)

    @pl.when(i == n - 1)
    def _():
        drain_scatter(slot)
        drain_scatter(1 - slot)


def _combine_kernel(y0_ref, y1_ref, h_ref, w_ref, g_ref, b_ref, of_ref, ob_ref, acc_ref):
    w = w_ref[...]
    for c in range(ROW_SUB):
        cb = _col_block(COMBINE_TB, c)
        lo0, hi0 = _unpack_cols(y0_ref[cb, :])
        lo1, hi1 = _unpack_cols(y1_ref[cb, :])
        acc_ref[:, c * LANES:(c + 1) * LANES] = lo0 * w[:, 0:1] + lo1 * w[:, 1:2]
        acc_ref[:, (c + ROW_SUB) * LANES:(c + ROW_SUB + 1) * LANES] = hi0 * w[:, 0:1] + hi1 * w[:, 1:2]
    y = _layernorm_rows(ALPHA * h_ref[...] + acc_ref[...], g_ref[...], b_ref[...])
    of_ref[...] = y
    ob_ref[...] = y.astype(BF16)


def _moe_layer(hf, hp, w_rg, b_rg, w_re, b_re, w_e_gate, w_e_up, w_e_down, ln_g, ln_b):
    T, D = hf.shape
    n_rows = T * TOP_K + N_EXPERTS * MOE_BLOCK
    n_blocks = n_rows // MOE_BLOCK
    w_r = jnp.concatenate([w_re, w_rg, jnp.zeros((D, LANES - N_EXPERTS - N_GROUPS), F32)], axis=-1)
    b_r = jnp.concatenate([b_re, b_rg, jnp.zeros((LANES - N_EXPERTS - N_GROUPS,), F32)]).reshape(1, LANES)
    w_hi = w_r.astype(BF16)
    w_lo = (w_r - w_hi.astype(F32)).astype(BF16)
    const = lambda i: (0, 0)
    rows = lambda i: (i, 0)
    eid, wts, rank, cnt = pl.pallas_call(
        _router_kernel,
        grid=(T // R_TM,),
        in_specs=[pl.BlockSpec((R_TM, D), rows), pl.BlockSpec((D, LANES), const),
                  pl.BlockSpec((D, LANES), const), pl.BlockSpec((1, LANES), const)],
        out_specs=[pl.BlockSpec((R_TM, LANES), rows), pl.BlockSpec((R_TM, LANES), rows),
                   pl.BlockSpec((R_TM, LANES), rows), pl.BlockSpec((8, LANES), const)],
        out_shape=[jax.ShapeDtypeStruct((T, LANES), jnp.int32), jax.ShapeDtypeStruct((T, LANES), F32),
                   jax.ShapeDtypeStruct((T, LANES), jnp.int32), jax.ShapeDtypeStruct((8, LANES), jnp.int32)],
        scratch_shapes=[pltpu.VMEM((8, LANES), F32)],
        compiler_params=_cp("arbitrary"),
        name="moe_router",
    )(hf, w_hi, w_lo, b_r)
    counts = cnt[0, :N_EXPERTS]
    padded = (counts + MOE_BLOCK - 1) // MOE_BLOCK * MOE_BLOCK
    pends = jnp.cumsum(padded)
    pstarts = pends - padded
    blk_start = jnp.arange(n_blocks, dtype=jnp.int32) * MOE_BLOCK
    blk_exp = jnp.minimum(jnp.sum((pends[None, :] <= blk_start[:, None]).astype(jnp.int32), axis=1), N_EXPERTS - 1)
    sel = eid[:, :TOP_K, None] == jnp.arange(N_EXPERTS, dtype=jnp.int32)
    dest = (jnp.sum(jnp.where(sel, pstarts, 0), axis=-1) + rank[:, :TOP_K]).astype(jnp.int32).reshape(T * TOP_K)
    flat = jnp.full((n_rows,), -1, jnp.int32).at[dest].set(jnp.arange(T * TOP_K, dtype=jnp.int32),
                                                           unique_indices=True)
    row_tok = jnp.where(flat < 0, 0, flat // TOP_K)
    is_pad = flat < 0
    row_out = jnp.where(is_pad, T * TOP_K - 1 + jnp.cumsum(is_pad.astype(jnp.int32)),
                        (flat % TOP_K) * T + flat // TOP_K)
    assert n_blocks >= 2
    any_spec = pl.BlockSpec(memory_space=pl.ANY)
    idx_spec = lambda f: pl.BlockSpec((MOE_BLOCK,), f, memory_space=pltpu.SMEM)
    yrows = pl.pallas_call(
        _expert_kernel,
        grid_spec=pltpu.PrefetchScalarGridSpec(
            num_scalar_prefetch=1,
            grid=(n_blocks,),
            in_specs=[idx_spec(lambda i, be: (i,)), idx_spec(lambda i, be: (jnp.minimum(i + 1, n_blocks - 1),)),
                      idx_spec(lambda i, be: (i,)), any_spec,
                      pl.BlockSpec((None, D, D_FF), lambda i, be: (be[i], 0, 0)),
                      pl.BlockSpec((None, D, D_FF), lambda i, be: (be[i], 0, 0)),
                      pl.BlockSpec((None, D_FF, D), lambda i, be: (be[i], 0, 0))],
            out_specs=any_spec,
            scratch_shapes=[pltpu.VMEM((2, MOE_BLOCK * ROW_SUB, LANES), ROW_DT),
                            pltpu.VMEM((2, MOE_BLOCK * ROW_SUB, LANES), ROW_DT),
                            pltpu.VMEM((D, D_FF), BF16), pltpu.VMEM((D, D_FF), BF16), pltpu.VMEM((D_FF, D), BF16),
                            pltpu.SemaphoreType.DMA((2,)), pltpu.SemaphoreType.DMA((2,))]),
        out_shape=jax.ShapeDtypeStruct((n_rows * ROW_SUB, LANES), ROW_DT),
        compiler_params=_cp("arbitrary"),
        name="moe_experts",
    )(blk_exp, row_tok, row_tok, row_out, hp, w_e_gate, w_e_up, w_e_down)
    n_steps = T // COMBINE_TB
    return pl.pallas_call(
        _combine_kernel,
        grid=(n_steps,),
        in_specs=[pl.BlockSpec((COMBINE_TB * ROW_SUB, LANES), rows),
                  pl.BlockSpec((COMBINE_TB * ROW_SUB, LANES), lambda i: (n_steps + i, 0)),
                  pl.BlockSpec((COMBINE_TB, D), rows), pl.BlockSpec((COMBINE_TB, LANES), rows),
                  pl.BlockSpec((1, D), const), pl.BlockSpec((1, D), const)],
        out_specs=[pl.BlockSpec((COMBINE_TB, D), rows), pl.BlockSpec((COMBINE_TB, D), rows)],
        out_shape=[jax.ShapeDtypeStruct((T, D), F32), jax.ShapeDtypeStruct((T, D), BF16)],
        scratch_shapes=[pltpu.VMEM((COMBINE_TB, D), F32)],
        compiler_params=_cp("parallel"),
        name="moe_combine_ln2",
    )(yrows, yrows, hf, wts, ln_g.reshape(1, D), ln_b.reshape(1, D))


def kernel(x, ln_in_g, ln_in_b, rel_bias, w_in, g_cq, w_uq, g_ckv, w_ukv, w_dw_c, b_dw_c, ln_c_g, ln_c_b,
           w_conv_d, b_conv_d, a_log_f, a_log_b, dt_bias_f, dt_bias_b, d_skip, g_norm_d, w_br, w_gate, b_gate,
           w_out, ln1_g, ln1_b, w_rg, b_rg, w_re, b_re, w_e_gate, w_e_up, w_e_down, ln2_g, ln2_b):
    B, S, D = x.shape
    T = B * S
    hf, hb = _layernorm(x.reshape(T, D), ln_in_g, ln_in_b)
    a_bias = _mixer_a_bias(rel_bias, S)
    for l in range(DEPTH):
        w_a, w_r = _in_proj_weights(w_in[l])
        qkv = _matmul(hb, w_a.astype(BF16), F32, 512, 3 * A_WIDTH, "in_proj_a")
        rest = _matmul(hb, w_r.astype(BF16), F32, 512, R_WIDTH // 2, "in_proj_rest")
        y_a = _mixer_a(qkv, a_bias, B, S)
        y_b = _mixer_b(rest, g_cq[l], w_uq[l], g_ckv[l], w_ukv[l], B, S)
        y_c = _mixer_c(rest, w_dw_c[l], b_dw_c[l], ln_c_g[l], ln_c_b[l], B, S)
        y_d = _mixer_d(rest, w_conv_d[l], b_conv_d[l], a_log_f[l], a_log_b[l], dt_bias_f[l], dt_bias_b[l],
                       d_skip[l], g_norm_d[l], B, S)
        h1f, h1p = _merge(hb, hf, (y_a, y_b, y_c, y_d), w_gate[l], b_gate[l], w_br[l], w_out[l], ln1_g[l], ln1_b[l])
        hf, hb = _moe_layer(h1f, h1p, w_rg[l], b_rg[l], w_re[l], b_re[l], w_e_gate[l], w_e_up[l], w_e_down[l],
                            ln2_g[l], ln2_b[l])
    return hf.reshape(B, S, D)
```

```python
import functools

import numpy as np
import jax
import jax.numpy as jnp
from jax import lax
from jax.experimental import pallas as pl
from jax.experimental.pallas import tpu as pltpu

F32 = jnp.float32
BF16 = jnp.bfloat16

D_MODEL = 2048
DEPTH = 2
A_HEADS = 8
A_HEAD_DIM = 64
A_WIDTH = A_HEADS * A_HEAD_DIM
A_PATTERNS = ((128, 1), (512, 4), (2048, 16))
A_BAND = 64
REL_BUCKETS = 32
REL_MAX_DIST = 1024
B_HEADS = 8
B_NOPE = 64
B_ROPE = 32
B_V = 64
B_Q_LORA = 512
B_KV_LORA = 256
ROPE_THETA = 10000.0
C_CH = 512
C_KERNEL = 31
D_HEADS = 8
D_HEAD_DIM = 64
D_INNER = D_HEADS * D_HEAD_DIM
D_STATE = 128
D_GROUPS = 2
D_CONV = 5
D_CHUNK = 128
N_BRANCH = 4
BRANCH_W = 512
N_GROUPS = 4
EXP_PER_GROUP = 8
N_EXPERTS = N_GROUPS * EXP_PER_GROUP
TOP_K = 2
D_FF = 512
MOE_BLOCK = 128
ALPHA = (2 * DEPTH) ** 0.25
EPS = 1e-5
NEG_INF = -1e30

LANES = 128
R_GLU, R_XBC, R_CQ, R_Z, R_CKV, R_KR, R_DT = 0, 1024, 2048, 2560, 3072, 3328, 3456
R_WIDTH = 3584
VMEM_LIMIT = 56 * 1024 * 1024


def _cp(*sem):
    return pltpu.CompilerParams(dimension_semantics=sem, vmem_limit_bytes=VMEM_LIMIT)


def _dot(a, b):
    return jnp.dot(a, b, preferred_element_type=F32)


def _dot_nt(a, b):
    return lax.dot_general(a, b, (((1,), (1,)), ((), ())), preferred_element_type=F32)


def _split3(x):
    hi = x.astype(BF16)
    r1 = x - hi.astype(F32)
    mid = r1.astype(BF16)
    lo = (r1 - mid.astype(F32)).astype(BF16)
    return hi, mid, lo


def _layernorm_rows(x, g, b):
    mu = jnp.mean(x, axis=-1, keepdims=True)
    xc = x - mu
    var = jnp.mean(xc * xc, axis=-1, keepdims=True)
    return xc * lax.rsqrt(var + EPS) * g + b


def _sigmoid(x):
    return 1.0 / (1.0 + jnp.exp(-x))


def _silu(x):
    return x * _sigmoid(x)


def _ln_kernel(x_ref, g_ref, b_ref, of_ref, ob_ref):
    y = _layernorm_rows(x_ref[...], g_ref[...], b_ref[...])
    of_ref[...] = y
    ob_ref[...] = y.astype(BF16)


def _layernorm(x, g, b, tm=256):
    T, D = x.shape
    return pl.pallas_call(
        _ln_kernel,
        grid=(T // tm,),
        in_specs=[pl.BlockSpec((tm, D), lambda i: (i, 0)),
                  pl.BlockSpec((1, D), lambda i: (0, 0)),
                  pl.BlockSpec((1, D), lambda i: (0, 0))],
        out_specs=[pl.BlockSpec((tm, D), lambda i: (i, 0)),
                   pl.BlockSpec((tm, D), lambda i: (i, 0))],
        out_shape=[jax.ShapeDtypeStruct((T, D), F32), jax.ShapeDtypeStruct((T, D), BF16)],
        compiler_params=_cp("parallel"),
        name="ln_in",
    )(x, g.reshape(1, D), b.reshape(1, D))


def _mm_kernel(x_ref, w_ref, o_ref):
    o_ref[...] = _dot(x_ref[...], w_ref[...]).astype(o_ref.dtype)


def _matmul(x, w, out_dtype, tm, tn, name):
    M, K = x.shape
    N = w.shape[1]
    return pl.pallas_call(
        _mm_kernel,
        grid=(N // tn, M // tm),
        in_specs=[pl.BlockSpec((tm, K), lambda j, i: (i, 0)),
                  pl.BlockSpec((K, tn), lambda j, i: (0, j))],
        out_specs=pl.BlockSpec((tm, tn), lambda j, i: (i, j)),
        out_shape=jax.ShapeDtypeStruct((M, N), out_dtype),
        compiler_params=_cp("parallel", "parallel"),
        name=name,
    )(x, w)


def _t5_bucket(rel):
    half = REL_BUCKETS // 2
    max_exact = half // 2
    n = np.abs(rel)
    large = max_exact + (np.log(np.maximum(n, 1) / max_exact) / np.log(REL_MAX_DIST / max_exact)
                         * (half - max_exact)).astype(np.int32)
    large = np.minimum(large, half - 1)
    return (rel > 0).astype(np.int32) * half + np.where(n < max_exact, n, large)


def _a_window(L):
    return min(2 * LANES, L)


def _a_bias_tiles(rel_bias, d, L):
    W = _a_window(L)
    offs = (0,) if L == LANES else (0, -A_BAND, -2 * A_BAND)
    qi = np.arange(LANES)[:, None]
    kj = np.arange(W)[None, :]
    rel = np.stack([kj - qi + off for off in offs], axis=0)
    valid = np.abs(rel) <= A_BAND
    onehot = (jnp.asarray(_t5_bucket(rel * d), jnp.int32)[..., None] == jnp.arange(REL_BUCKETS)).astype(F32)
    b = jnp.einsum('vqkb,bh->vhqk', onehot, rel_bias.astype(F32), precision=lax.Precision.HIGHEST)
    b = jnp.where(valid[:, None], b, NEG_INF)
    return b.reshape(len(offs), A_HEADS // 2, 2 * LANES, W)


A_GROUP = 2


def _attn_a_kernel(q_ref, k_ref, v_ref, b16_ref, b4_ref, b1_ref, y_ref,
                   q4_ref, k4_ref, v4_ref, m_ref, l_ref, acc_ref, tmp_ref, *, S):
    (_, d1), (_, d4), (_, d16) = A_PATTERNS
    lane = lax.broadcasted_iota(jnp.int32, (LANES, LANES), 1)
    head0 = lane < A_HEAD_DIM
    scale = A_HEAD_DIM ** -0.5

    def partial_softmax(qs, ks, vs, bias_ref):
        q2 = jnp.concatenate([jnp.where(head0, qs, 0.0), jnp.where(head0, 0.0, qs)], axis=0).astype(BF16)
        s = _dot_nt(q2, ks.astype(BF16)) * scale + bias_ref[...]
        m = jnp.max(s, axis=-1, keepdims=True)
        p = jnp.exp(s - m).astype(BF16)
        num = _dot(p, vs.astype(BF16))
        den = _dot(p, jnp.ones((vs.shape[0], LANES), BF16))
        both = lambda t: jnp.where(head0, t[:LANES], t[LANES:])
        return both(jnp.broadcast_to(m, (2 * LANES, LANES))), both(den), both(num)

    def fold(old, new):
        (m_old, l_old, a_old), (m_new, l_new, a_new) = old, new
        m = jnp.maximum(m_old, m_new)
        c_old = jnp.exp(m_old - m)
        c_new = jnp.exp(m_new - m)
        return m, c_old * l_old + c_new * l_new, c_old * a_old + c_new * a_new

    stat_refs = (m_ref, l_ref, acc_ref)

    def get(c, rows):
        return tuple(ref.at[c][rows, :] for ref in stat_refs)

    def put(c, rows, stats):
        for ref, val in zip(stat_refs, stats):
            ref.at[c][rows, :] = val

    def grouped(n, group, unit):
        def trip(g, carry):
            pending = [unit(g * group + u) for u in range(group)]
            for finish in pending:
                finish()
            return carry

        lax.fori_loop(0, n // group, trip, 0)

    L4 = S // d4
    sub = d16 // d4
    assert S // d16 == LANES and sub == d4
    for c in range(d4):
        cls = pl.ds(c, L4, stride=d4)
        q4_ref[c] = q_ref[cls, :]
        k4_ref[c] = k_ref[cls, :]
        v4_ref[c] = v_ref[cls, :]

    def unit16(t):
        c = t % d4
        rows = pl.ds(t // d4, LANES, stride=sub)
        stats = partial_softmax(q4_ref.at[c][rows, :], k4_ref.at[c][rows, :], v4_ref.at[c][rows, :], b16_ref.at[0])
        return lambda: put(c, rows, stats)

    grouped(d16, A_GROUP, unit16)

    def window(i, L):
        nqb = L // LANES
        ws = pl.multiple_of(jnp.clip(i * LANES - A_BAND, 0, L - 2 * LANES), A_BAND)
        return pl.ds(ws, 2 * LANES), jnp.where(i == 0, 0, jnp.where(i == nqb - 1, 2, 1))

    def unit4(t):
        c = t % d4
        i = t // d4
        keys, var = window(i, L4)
        rows = pl.ds(pl.multiple_of(i * LANES, LANES), LANES)
        stats = partial_softmax(q4_ref.at[c][rows, :], k4_ref.at[c][keys, :], v4_ref.at[c][keys, :], b4_ref.at[var])
        return lambda: put(c, rows, fold(get(c, rows), stats))

    grouped(d4 * (L4 // LANES), A_GROUP, unit4)

    def unit1(i):
        keys, var = window(i, S)
        rows = pl.ds(pl.multiple_of(i * LANES, LANES), LANES)
        stats = partial_softmax(q_ref[rows, :], k_ref[keys, :], v_ref[keys, :], b1_ref.at[var])

        def finish():
            part = pl.ds(pl.multiple_of(i * (LANES // d4), LANES // d4), LANES // d4)
            for n, ref in enumerate(stat_refs):
                for c in range(d4):
                    tmp_ref.at[n][pl.ds(c, LANES // d4, stride=d4), :] = ref.at[c][part, :]
            _, l, a = fold(tuple(tmp_ref[n] for n in range(len(stat_refs))), stats)
            y_ref[rows, :] = (a / l).astype(y_ref.dtype)

        return finish

    grouped(S // LANES, A_GROUP, unit1)


def _mixer_a_bias(rel_bias, S):
    return tuple(_a_bias_tiles(rel_bias, d, S // d) for _, d in A_PATTERNS)


def _mixer_a(qkv, bias, B, S):
    b1, b4, b16 = bias
    npair = A_HEADS // 2
    pair_bias = lambda t: pl.BlockSpec((t.shape[0], None) + t.shape[2:], lambda b, hp: (0, hp, 0, 0))
    slab = lambda first: pl.BlockSpec((S, LANES), lambda b, hp: (b, first + hp))
    return pl.pallas_call(
        functools.partial(_attn_a_kernel, S=S),
        grid=(B, npair),
        in_specs=[slab(0), slab(npair), slab(2 * npair), pair_bias(b16), pair_bias(b4), pair_bias(b1)],
        out_specs=pl.BlockSpec((S, LANES), lambda b, hp: (b, hp)),
        out_shape=jax.ShapeDtypeStruct((B * S, A_WIDTH), BF16),
        scratch_shapes=[pltpu.VMEM((A_PATTERNS[1][1], S // A_PATTERNS[1][1], LANES), F32)] * 6
        + [pltpu.VMEM((3, LANES, LANES), F32)],
        compiler_params=_cp("parallel", "parallel"),
        name="attn_a",
    )(qkv, qkv, qkv, b16, b4, b1)


MLA_Q_SCALE = float((B_NOPE + B_ROPE) ** -0.5 * np.log2(np.e))


def _mla_proj_kernel(cq_ref, ckv_ref, kr_ref, gq_ref, gkv_ref, wqm_ref, wqs_ref, wk_ref, wv_ref, vone_ref,
                     ek_ref, cosq_ref, sinq_ref, csk_ref, q_ref, k_ref, v_ref):
    cq = cq_ref[...]
    xq = (cq * lax.rsqrt(jnp.mean(cq * cq, axis=-1, keepdims=True) + EPS) * gq_ref[...]).astype(BF16)
    ckv = ckv_ref[...]
    xkv = (ckv * lax.rsqrt(jnp.mean(ckv * ckv, axis=-1, keepdims=True) + EPS) * gkv_ref[...]).astype(BF16)
    qm = _dot(xq, wqm_ref[...])
    qs = _dot(xq, wqs_ref[...])
    cosq = cosq_ref[...] * MLA_Q_SCALE
    sinq = sinq_ref[...] * MLA_Q_SCALE
    t = kr_ref[...] * csk_ref[...]
    t_hi = t.astype(BF16)
    t_lo = (t - t_hi.astype(F32)).astype(BF16)
    kk = _dot(xkv, wk_ref[...]) + _dot(t_hi, ek_ref[...]) + _dot(t_lo, ek_ref[...])
    for h in range(B_HEADS):
        sl = slice(h * LANES, (h + 1) * LANES)
        q_ref[:, sl] = (qm[:, sl] * cosq + qs[:, sl] * sinq).astype(BF16)
    k_ref[...] = kk.astype(BF16)
    v_ref[...] = (_dot(xkv, wv_ref[...]) + vone_ref[...]).astype(BF16)


def _mla_attn_kernel(q_ref, k_ref, v_ref, o_ref):
    outs = []
    for hh in range(2):
        sl = slice(hh * LANES, (hh + 1) * LANES)
        s = _dot_nt(q_ref[:, sl], k_ref[:, sl])
        p = jnp.exp2(s - jnp.max(s, axis=-1, keepdims=True))
        outs.append(_dot(p.astype(BF16), v_ref[:, sl]))
    lane = lax.broadcasted_iota(jnp.int32, outs[0].shape, 1)
    acc = jnp.where(lane < B_V, outs[0], outs[1])
    den = pltpu.roll(jnp.where(lane < B_V, outs[1], outs[0]), B_V, axis=1)
    o_ref[...] = (acc / den).astype(o_ref.dtype)


def _mla_tables(S):
    inv_freq = ROPE_THETA ** (-jnp.arange(0, B_ROPE, 2, dtype=F32) / B_ROPE)
    ang = jnp.arange(S, dtype=F32)[:, None] * inv_freq[None]
    cos, sin = jnp.cos(ang), jnp.sin(ang)
    cos2 = jnp.concatenate([cos, cos], axis=-1)
    sin2 = jnp.concatenate([sin, sin], axis=-1)
    ones = jnp.ones((S, B_NOPE), F32)
    zn = jnp.zeros((S, B_NOPE), F32)
    zp = jnp.zeros((S, LANES - B_NOPE - B_ROPE), F32)
    cosq = jnp.concatenate([ones, cos2, zp], axis=-1)
    sinq = jnp.concatenate([zn, sin2, zp], axis=-1)
    csk = jnp.concatenate([cos2, sin2, jnp.zeros((S, LANES - 2 * B_ROPE), F32)], axis=-1)
    return cosq, sinq, csk


def _swap_cols(w):
    half = w.shape[-1] // 2
    return jnp.concatenate([-w[..., half:], w[..., :half]], axis=-1)


def _mla_weights(w_uq, w_ukv):
    dq = B_NOPE + B_ROPE
    wq = w_uq.reshape(B_Q_LORA, B_HEADS, dq)
    zpad = jnp.zeros((B_Q_LORA, B_HEADS, LANES - dq), F32)
    wqm = jnp.concatenate([wq, zpad], axis=-1).reshape(B_Q_LORA, B_HEADS * LANES)
    wqs = jnp.concatenate([jnp.zeros((B_Q_LORA, B_HEADS, B_NOPE), F32), _swap_cols(wq[..., B_NOPE:]), zpad],
                          axis=-1).reshape(B_Q_LORA, B_HEADS * LANES)
    wkv = w_ukv.reshape(B_KV_LORA, B_HEADS, B_NOPE + B_V)
    wk = jnp.concatenate([wkv[..., :B_NOPE], jnp.zeros((B_KV_LORA, B_HEADS, LANES - B_NOPE), F32)],
                         axis=-1).reshape(B_KV_LORA, B_HEADS * LANES)
    zv = jnp.zeros((B_KV_LORA, B_HEADS // 2, LANES - B_V), F32)
    wv_h = wkv[..., B_NOPE:]
    wv = jnp.stack([jnp.concatenate([wv_h[:, 0::2], zv], axis=-1),
                    jnp.concatenate([zv, wv_h[:, 1::2]], axis=-1)], axis=2).reshape(B_KV_LORA, B_HEADS * LANES)
    lane_in_pair = np.arange(B_HEADS * LANES) % (2 * LANES)
    vone = jnp.asarray(((lane_in_pair >= B_V) & (lane_in_pair < LANES + B_V)).astype(np.float32)).reshape(1, -1)
    ek = np.zeros((LANES, B_HEADS, LANES), np.float32)
    for j in range(B_ROPE):
        ek[j, :, B_NOPE + j] = 1.0
        ek[B_ROPE + j, :, B_NOPE + j] = 1.0
    ek = jnp.asarray(ek.reshape(LANES, B_HEADS * LANES))
    return wqm.astype(BF16), wqs.astype(BF16), wk.astype(BF16), wv.astype(BF16), vone, ek.astype(BF16)


def _mixer_b(rest, g_cq, w_uq, g_ckv, w_ukv, B, S, tm=512, tq=256):
    T = B * S
    wqm, wqs, wk, wv, vone, ek = _mla_weights(w_uq, w_ukv)
    cosq, sinq, csk = _mla_tables(S)
    nst = S // tm
    QW = B_HEADS * LANES
    const = lambda i: (0, 0)
    pos = lambda i: (i % nst, 0)
    q, k, v = pl.pallas_call(
        _mla_proj_kernel,
        grid=(T // tm,),
        in_specs=[pl.BlockSpec((tm, B_Q_LORA), lambda i: (i, R_CQ // B_Q_LORA)),
                  pl.BlockSpec((tm, B_KV_LORA), lambda i: (i, R_CKV // B_KV_LORA)),
                  pl.BlockSpec((tm, LANES), lambda i: (i, R_KR // LANES)),
                  pl.BlockSpec((1, B_Q_LORA), const),
                  pl.BlockSpec((1, B_KV_LORA), const),
                  pl.BlockSpec((B_Q_LORA, QW), const),
                  pl.BlockSpec((B_Q_LORA, QW), const),
                  pl.BlockSpec((B_KV_LORA, QW), const),
                  pl.BlockSpec((B_KV_LORA, QW), const),
                  pl.BlockSpec((1, QW), const),
                  pl.BlockSpec((LANES, QW), const),
                  pl.BlockSpec((tm, LANES), pos),
                  pl.BlockSpec((tm, LANES), pos),
                  pl.BlockSpec((tm, LANES), pos)],
        out_specs=[pl.BlockSpec((tm, QW), lambda i: (i, 0)),
                   pl.BlockSpec((tm, QW), lambda i: (i, 0)),
                   pl.BlockSpec((tm, QW), lambda i: (i, 0))],
        out_shape=[jax.ShapeDtypeStruct((T, QW), BF16), jax.ShapeDtypeStruct((T, QW), BF16),
                   jax.ShapeDtypeStruct((T, QW), BF16)],
        compiler_params=_cp("parallel"),
        name="mla_proj",
    )(rest, rest, rest, g_cq.reshape(1, -1), g_ckv.reshape(1, -1), wqm, wqs, wk, wv, vone, ek, cosq, sinq, csk)
    y = pl.pallas_call(
        _mla_attn_kernel,
        grid=(B, B_HEADS // 2, S // tq),
        in_specs=[pl.BlockSpec((None, tq, 2 * LANES), lambda b, hp, i: (b, i, hp)),
                  pl.BlockSpec((None, S, 2 * LANES), lambda b, hp, i: (b, 0, hp)),
                  pl.BlockSpec((None, S, 2 * LANES), lambda b, hp, i: (b, 0, hp))],
        out_specs=pl.BlockSpec((None, tq, 2 * B_V), lambda b, hp, i: (b, i, hp)),
        out_shape=jax.ShapeDtypeStruct((B, S, B_HEADS * B_V), BF16),
        compiler_params=_cp("parallel", "parallel", "arbitrary"),
        name="mla_attn",
    )(q.reshape(B, S, QW), k.reshape(B, S, QW), v.reshape(B, S, QW))
    return y.reshape(T, B_HEADS * B_V)


C_PAD = 16
C_ROWS = 128


SUBLANES = 8


def _tap_span(first, ntaps, rows):
    return rows + ((first + ntaps - 1) // SUBLANES) * SUBLANES


def _depthwise_taps(win_ref, sh_ref, w_ref, bias, ls, first, ntaps, rows):
    acc = jnp.broadcast_to(bias, (rows, LANES))
    span = _tap_span(first, ntaps, rows)
    for ph in range(SUBLANES):
        taps = [j for j in range(ntaps) if (first + j) % SUBLANES == ph]
        if not taps:
            continue
        if len(taps) == 1:
            j = taps[0]
            acc = acc + w_ref[j:j + 1, ls] * win_ref[first + j:first + j + rows, ls]
            continue
        sh_ref[0:span, :] = win_ref[ph:ph + span, ls]
        for j in taps:
            a = (first + j) // SUBLANES * SUBLANES
            acc = acc + w_ref[j:j + 1, ls] * sh_ref[a:a + rows, :]
    return acc


def _fill_window(win_ref, load_rows, r0, i, nblk, rows, pad):
    width = win_ref.shape[1]
    win_ref[pad:pad + rows, :] = load_rows(r0, rows)

    @pl.when(i > 0)
    def _():
        win_ref[0:pad, :] = load_rows(r0 - pad, pad)

    @pl.when(i == 0)
    def _():
        win_ref[0:pad, :] = jnp.zeros((pad, width), F32)

    @pl.when(i < nblk - 1)
    def _():
        win_ref[pad + rows:pad + rows + pad, :] = load_rows(r0 + rows, pad)

    @pl.when(i == nblk - 1)
    def _():
        win_ref[pad + rows:pad + rows + pad, :] = jnp.zeros((pad, width), F32)


def _conv_c_kernel(glu_ref, w_ref, b_ref, g_ref, beta_ref, o_ref, win_ref, sh_ref, acc_ref, *, S):
    i = pl.program_id(1)
    r0 = pl.multiple_of(i * C_ROWS, C_ROWS)

    def glu_rows(start, n):
        rs = pl.ds(pl.multiple_of(start, C_PAD), n)
        return glu_ref[rs, 0:C_CH] * _sigmoid(glu_ref[rs, C_CH:2 * C_CH])

    _fill_window(win_ref, glu_rows, r0, i, S // C_ROWS, C_ROWS, C_PAD)
    first = C_PAD - C_KERNEL // 2
    for lb in range(C_CH // LANES):
        ls = slice(lb * LANES, (lb + 1) * LANES)
        acc_ref[:, ls] = _depthwise_taps(win_ref, sh_ref, w_ref, b_ref[:, ls], ls, first, C_KERNEL, C_ROWS)
    y = _layernorm_rows(acc_ref[...], g_ref[...], beta_ref[...])
    o_ref[...] = _silu(y).astype(o_ref.dtype)


def _mixer_c(rest, w_dw, b_dw, ln_g, ln_b, B, S):
    T = B * S
    const = lambda b, i: (0, 0)
    y = pl.pallas_call(
        functools.partial(_conv_c_kernel, S=S),
        grid=(B, S // C_ROWS),
        in_specs=[pl.BlockSpec((None, S, 2 * C_CH), lambda b, i: (b, 0, R_GLU // (2 * C_CH))),
                  pl.BlockSpec((C_KERNEL, C_CH), const),
                  pl.BlockSpec((1, C_CH), const),
                  pl.BlockSpec((1, C_CH), const),
                  pl.BlockSpec((1, C_CH), const)],
        out_specs=pl.BlockSpec((None, C_ROWS, C_CH), lambda b, i: (b, i, 0)),
        out_shape=jax.ShapeDtypeStruct((B, S, C_CH), BF16),
        scratch_shapes=[pltpu.VMEM((C_ROWS + 2 * C_PAD, C_CH), F32),
                        pltpu.VMEM((_tap_span(C_PAD - C_KERNEL // 2, C_KERNEL, C_ROWS), LANES), F32),
                        pltpu.VMEM((C_ROWS, C_CH), F32)],
        compiler_params=_cp("parallel", "parallel"),
        name="conformer_conv",
    )(rest.reshape(B, S, R_WIDTH), w_dw, b_dw.reshape(1, -1), ln_g.reshape(1, -1), ln_b.reshape(1, -1))
    return y.reshape(T, C_CH)


D_PAD = 8
XBC_W = D_INNER + 2 * D_GROUPS * D_STATE
N_PAIR = D_HEADS // 2


def _pair_expand(v, first):
    lane = lax.broadcasted_iota(jnp.int32, (v.shape[0], LANES), 1)
    lo = jnp.broadcast_to(v[:, first:first + 1], (v.shape[0], LANES))
    hi = jnp.broadcast_to(v[:, first + 1:first + 2], (v.shape[0], LANES))
    return jnp.where(lane < D_HEAD_DIM, lo, hi)


def _ssd_kernel(xbc_ref, z_ref, dt_ref, wc_ref, bc_ref, alog_ref, dtb_ref, dskip_ref, gn_ref, o_ref,
                win_ref, sh_ref, xc_ref, a_ref, dtv_ref, y_ref, st_ref, *, S):
    Q = D_CHUNK
    nchunk = S // Q
    N = D_STATE
    bm0 = D_INNER
    cm0 = D_INNER + D_GROUPS * N

    def conv_body(c, carry):
        r0 = pl.multiple_of(c * Q, Q)
        _fill_window(win_ref, lambda st, n: xbc_ref[pl.ds(pl.multiple_of(st, D_PAD), n), :], r0, c, nchunk, Q, D_PAD)
        for lb in range(XBC_W // LANES):
            ls = slice(lb * LANES, (lb + 1) * LANES)
            acc = _depthwise_taps(win_ref, sh_ref, wc_ref, bc_ref[:, ls], ls, D_PAD - D_CONV // 2, D_CONV, Q)
            xc_ref[pl.ds(r0, Q), ls] = _silu(acc)
        return carry

    lax.fori_loop(0, nchunk, conv_body, 0)

    lane1 = lax.broadcasted_iota(jnp.int32, (1, LANES), 1)
    a_row = jnp.where(lane1 < 2 * D_HEADS, -jnp.exp(alog_ref[...]), 0.0)
    xdt = dt_ref[...] + dtb_ref[...]
    dtv = jnp.maximum(xdt, 0.0) + jnp.log(1.0 + jnp.exp(-jnp.abs(xdt)))
    dtv_ref[...] = dtv
    a_ref[...] = dtv * a_row

    row = lax.broadcasted_iota(jnp.int32, (Q, Q), 0)
    col = lax.broadcasted_iota(jnp.int32, (Q, Q), 1)
    tril = row >= col
    triu = col >= row
    lane = col

    def scan_chunk(c, lower, off, finalize):
        r0 = pl.multiple_of(c * Q, Q)
        rows = pl.ds(r0, Q)
        mask = tril if lower else triu
        tri = mask.astype(BF16)
        a_hi, a_mid, a_lo = _split3(a_ref[rows, :])
        cs = _dot(tri, a_hi) + _dot(tri, a_mid) + _dot(tri, a_lo)
        cs_t = cs.T
        ecs = jnp.exp(cs)
        edge = Q - 1 if lower else 0
        edec = jnp.exp(cs[edge:edge + 1, :] - cs)
        dt_c = dtv_ref[rows, :]
        for g in range(D_GROUPS):
            bg = xc_ref[rows, bm0 + g * N:bm0 + (g + 1) * N]
            cg = xc_ref[rows, cm0 + g * N:cm0 + (g + 1) * N].astype(BF16)
            cb = _dot_nt(cg, bg.astype(BF16))
            bg_t = bg.T.astype(BF16)
            for pp in range(N_PAIR // D_GROUPS):
                p = g * (N_PAIR // D_GROUPS) + pp
                ps = slice(p * LANES, (p + 1) * LANES)
                xdt_p = xc_ref[rows, ps] * _pair_expand(dt_c, off + 2 * p)
                ms = []
                for hh in range(2):
                    k = off + 2 * p + hh
                    diff = jnp.broadcast_to(cs[:, k:k + 1], (Q, Q)) - cs_t[k:k + 1, :]
                    ms.append((jnp.exp(jnp.where(mask, diff, NEG_INF)) * cb).astype(BF16))
                x_lo = jnp.where(lane < D_HEAD_DIM, xdt_p, 0.0).astype(BF16)
                x_hi = jnp.where(lane >= D_HEAD_DIM, xdt_p, 0.0).astype(BF16)
                y_intra = _dot(jnp.concatenate(ms, axis=1), jnp.concatenate([x_lo, x_hi], axis=0))
                hp = st_ref[p]
                ecs_p = _pair_expand(ecs, off + 2 * p)
                y_new = y_intra + _dot(cg, hp.astype(BF16)) * ecs_p
                if lower:
                    y_ref[rows, ps] = y_new
                else:
                    y_ref[rows, ps] = y_ref[rows, ps] + y_new
                xs_p = (xdt_p * _pair_expand(edec, off + 2 * p)).astype(BF16)
                st_ref[p] = hp * ecs_p[edge:edge + 1, :] + _dot(bg_t, xs_p)
        if finalize:
            y = y_ref[rows, :] + xc_ref[rows, 0:D_INNER] * dskip_ref[...]
            gated = y * _silu(z_ref[rows, :])
            out = gated * lax.rsqrt(jnp.mean(gated * gated, axis=-1, keepdims=True) + EPS) * gn_ref[...]
            o_ref[rows, :] = out.astype(o_ref.dtype)

    st_ref[...] = jnp.zeros(st_ref.shape, F32)

    def fwd_body(c, carry):
        scan_chunk(c, True, 0, False)
        return carry

    lax.fori_loop(0, nchunk, fwd_body, 0)
    st_ref[...] = jnp.zeros(st_ref.shape, F32)

    def bwd_body(k, carry):
        scan_chunk(nchunk - 1 - k, False, D_HEADS, True)
        return carry

    lax.fori_loop(0, nchunk, bwd_body, 0)


def _mixer_d(rest, w_conv, b_conv, a_log_f, a_log_b, dt_bias_f, dt_bias_b, d_skip, g_norm, B, S):
    T = B * S
    pad16 = lambda f, b: jnp.concatenate([f, b, jnp.zeros((LANES - 2 * D_HEADS,), F32)]).reshape(1, LANES)
    const = lambda b: (0, 0)
    y = pl.pallas_call(
        functools.partial(_ssd_kernel, S=S),
        grid=(B,),
        in_specs=[pl.BlockSpec((None, S, XBC_W), lambda b: (b, 0, R_XBC // XBC_W)),
                  pl.BlockSpec((None, S, D_INNER), lambda b: (b, 0, R_Z // D_INNER)),
                  pl.BlockSpec((None, S, LANES), lambda b: (b, 0, R_DT // LANES)),
                  pl.BlockSpec((D_CONV, XBC_W), const),
                  pl.BlockSpec((1, XBC_W), const),
                  pl.BlockSpec((1, LANES), const),
                  pl.BlockSpec((1, LANES), const),
                  pl.BlockSpec((1, D_INNER), const),
                  pl.BlockSpec((1, D_INNER), const)],
        out_specs=pl.BlockSpec((None, S, D_INNER), lambda b: (b, 0, 0)),
        out_shape=jax.ShapeDtypeStruct((B, S, D_INNER), BF16),
        scratch_shapes=[pltpu.VMEM((D_CHUNK + 2 * D_PAD, XBC_W), F32),
                        pltpu.VMEM((_tap_span(D_PAD - D_CONV // 2, D_CONV, D_CHUNK), LANES), F32),
                        pltpu.VMEM((S, XBC_W), F32),
                        pltpu.VMEM((S, LANES), F32),
                        pltpu.VMEM((S, LANES), F32),
                        pltpu.VMEM((S, D_INNER), F32),
                        pltpu.VMEM((N_PAIR, D_STATE, LANES), F32)],
        compiler_params=_cp("parallel"),
        name="ssd_mixer",
    )(rest.reshape(B, S, R_WIDTH), rest.reshape(B, S, R_WIDTH), rest.reshape(B, S, R_WIDTH),
      w_conv, b_conv.reshape(1, -1), pad16(a_log_f, a_log_b), pad16(dt_bias_f, dt_bias_b),
      jnp.repeat(d_skip, D_HEAD_DIM).reshape(1, -1), g_norm.reshape(1, -1))
    return y.reshape(T, D_INNER)


def _in_proj_weights(w_in_l):
    o = np.cumsum((0, A_WIDTH, A_WIDTH, A_WIDTH, B_Q_LORA, B_KV_LORA, B_ROPE, 2 * C_CH,
                   D_INNER, D_INNER, D_GROUPS * D_STATE, D_GROUPS * D_STATE, 2 * D_HEADS)).tolist()
    seg = lambda n: w_in_l[:, o[n]:o[n + 1]]
    w_a = w_in_l[:, :o[3]]
    cq, ckv, kr, glu, z, xs, bm, cm, dt = (seg(n) for n in range(3, 12))
    zeros = lambda n: jnp.zeros((w_in_l.shape[0], n), w_in_l.dtype)
    w_r = jnp.concatenate([glu, xs, bm, cm, cq, z, ckv,
                           kr, _swap_cols(kr), zeros(LANES - 2 * B_ROPE),
                           dt, zeros(LANES - 2 * D_HEADS)], axis=-1)
    assert w_r.shape[1] == R_WIDTH
    return w_a, w_r


def _merge_gate_kernel(h_ref, ya_ref, yb_ref, yc_ref, yd_ref, wg_ref, bg_ref, wbr_ref, o_ref):
    h = h_ref[...]
    acc = None
    for i, y_ref in enumerate((ya_ref, yb_ref, yc_ref, yd_ref)):
        gate = _sigmoid(_dot(h, wg_ref[i]) + bg_ref[i])
        term = gate * _dot(y_ref[...], wbr_ref[i])
        acc = term if acc is None else acc + term
    o_ref[...] = acc.astype(o_ref.dtype)


def _out_ln_kernel(m_ref, w_ref, h_ref, g_ref, b_ref, of_ref, op_ref):
    y = _layernorm_rows(ALPHA * h_ref[...] + _dot(m_ref[...], w_ref[...]), g_ref[...], b_ref[...])
    of_ref[...] = y
    for c in range(ROW_SUB):
        op_ref[_col_block(y.shape[0], c), :] = _pack_cols(y, c)


def _merge(hb, hf, branches, w_gate, b_gate, w_br, w_out, ln_g, ln_b, tm=512, tn=512, tm2=256):
    T, D = hb.shape
    ybs = pl.BlockSpec((tm, BRANCH_W), lambda j, i: (i, 0))
    merged = pl.pallas_call(
        _merge_gate_kernel,
        grid=(D // tn, T // tm),
        in_specs=[pl.BlockSpec((tm, D), lambda j, i: (i, 0)), ybs, ybs, ybs, ybs,
                  pl.BlockSpec((N_BRANCH, D, tn), lambda j, i: (0, 0, j)),
                  pl.BlockSpec((N_BRANCH, 1, tn), lambda j, i: (0, 0, j)),
                  pl.BlockSpec((N_BRANCH, BRANCH_W, tn), lambda j, i: (0, 0, j))],
        out_specs=pl.BlockSpec((tm, tn), lambda j, i: (i, j)),
        out_shape=jax.ShapeDtypeStruct((T, D), BF16),
        compiler_params=_cp("parallel", "parallel"),
        name="merge_gate",
    )(hb, *branches, w_gate.astype(BF16), b_gate.reshape(N_BRANCH, 1, D), w_br.astype(BF16))
    const = lambda i: (0, 0)
    rows = lambda i: (i, 0)
    return pl.pallas_call(
        _out_ln_kernel,
        grid=(T // tm2,),
        in_specs=[pl.BlockSpec((tm2, D), rows), pl.BlockSpec((D, D), const), pl.BlockSpec((tm2, D), rows),
                  pl.BlockSpec((1, D), const), pl.BlockSpec((1, D), const)],
        out_specs=[pl.BlockSpec((tm2, D), rows), pl.BlockSpec((tm2 * ROW_SUB, LANES), rows)],
        out_shape=[jax.ShapeDtypeStruct((T, D), F32), jax.ShapeDtypeStruct((T * ROW_SUB, LANES), ROW_DT)],
        compiler_params=_cp("parallel"),
        name="out_proj_ln1",
    )(merged, w_out.astype(BF16), hf, ln_g.reshape(1, D), ln_b.reshape(1, D))


R_TM = 256
COMBINE_TB = 128


def _router_kernel(h_ref, whi_ref, wlo_ref, b_ref, eid_ref, wts_ref, rank_ref, cnt_ref, carry_ref):
    i = pl.program_id(0)

    @pl.when(i == 0)
    def _():
        carry_ref[...] = jnp.zeros(carry_ref.shape, F32)

    x = h_ref[...]
    tm = x.shape[0]
    xh = x.astype(BF16)
    xl = (x - xh.astype(F32)).astype(BF16)
    whi = whi_ref[...]
    logits = _dot(xh, whi) + _dot(xh, wlo_ref[...]) + _dot(xl, whi) + b_ref[...]
    lane = lax.broadcasted_iota(jnp.int32, (tm, LANES), 1)
    big = jnp.int32(4 * LANES)
    is_g = (lane >= N_EXPERTS) & (lane < N_EXPERTS + N_GROUPS)
    lg = jnp.where(is_g, logits, NEG_INF)
    gmax = jnp.max(lg, axis=-1, keepdims=True)
    gidx = jnp.min(jnp.where(lg == gmax, lane - N_EXPERTS, big), axis=-1, keepdims=True)
    g_w = 1.0 / jnp.sum(jnp.where(is_g, jnp.exp(lg - gmax), 0.0), axis=-1, keepdims=True)
    in_grp = (lane < N_EXPERTS) & ((lane // EXP_PER_GROUP) == gidx)
    le = jnp.where(in_grp, logits, NEG_INF)
    e1 = jnp.max(le, axis=-1, keepdims=True)
    i1 = jnp.min(jnp.where(le == e1, lane, big), axis=-1, keepdims=True)
    le2 = jnp.where(lane == i1, NEG_INF, le)
    e2 = jnp.max(le2, axis=-1, keepdims=True)
    i2 = jnp.min(jnp.where(le2 == e2, lane, big), axis=-1, keepdims=True)
    zsum = jnp.sum(jnp.where(in_grp, jnp.exp(le - e1), 0.0), axis=-1, keepdims=True)
    p1 = 1.0 / zsum
    p2 = jnp.exp(e2 - e1) / zsum
    w1 = g_w * p1 / (p1 + p2)
    w2 = g_w * p2 / (p1 + p2)
    oh1 = lane == i1
    oh2 = lane == i2
    ohs = (oh1 | oh2).astype(BF16)
    row = lax.broadcasted_iota(jnp.int32, (tm, tm), 0)
    col = lax.broadcasted_iota(jnp.int32, (tm, tm), 1)
    before = _dot((row > col).astype(BF16), ohs) + carry_ref[0:1, :]
    r1 = jnp.sum(jnp.where(oh1, before, 0.0), axis=-1, keepdims=True)
    r2 = jnp.sum(jnp.where(oh2, before, 0.0), axis=-1, keepdims=True)
    total = carry_ref[0:1, :] + jnp.sum(ohs.astype(F32), axis=0, keepdims=True)
    carry_ref[...] = jnp.broadcast_to(total, carry_ref.shape)
    cnt_ref[...] = jnp.broadcast_to(total, cnt_ref.shape).astype(jnp.int32)
    eid_ref[...] = jnp.where(lane == 0, i1, jnp.where(lane == 1, i2, 0))
    wts_ref[...] = jnp.where(lane == 0, w1, jnp.where(lane == 1, w2, 0.0))
    rank_ref[...] = jnp.where(lane == 0, r1, jnp.where(lane == 1, r2, 0.0)).astype(jnp.int32)


ROW_SUB = D_MODEL // LANES // 2
ROW_DT = jnp.uint32


def _row_slab(ref, r):
    return ref.at[pl.ds(pl.multiple_of(r * ROW_SUB, ROW_SUB), ROW_SUB)]


def _col_block(n, c):
    return pl.ds(c, n, stride=ROW_SUB)


def _pack_cols(x, c):
    as_bits = lambda t: lax.bitcast_convert_type(t.astype(BF16).astype(F32), ROW_DT)
    lo = as_bits(x[:, c * LANES:(c + 1) * LANES])
    hi = as_bits(x[:, (c + ROW_SUB) * LANES:(c + ROW_SUB + 1) * LANES])
    return (lo >> 16) | hi


def _unpack_cols(w):
    return (lax.bitcast_convert_type(w << 16, F32),
            lax.bitcast_convert_type(w & jnp.uint32(0xFFFF0000), F32))


def _expert_kernel(be_ref, src_ref, src_next_ref, h_hbm, wg_ref, wu_ref, wd_ref, o_ref,
                   xbuf_ref, wgb_ref, wub_ref, wdb_ref, gsem):
    i = pl.program_id(0)
    n = pl.num_programs(0)
    slot = i % 2

    @pl.when((i == 0) | (be_ref[i] != be_ref[jnp.maximum(i - 1, 0)]))
    def _():
        wgb_ref[...] = wg_ref[...].astype(BF16)
        wub_ref[...] = wu_ref[...].astype(BF16)
        wdb_ref[...] = wd_ref[...].astype(BF16)

    def gather_copy(tok, to_slot, t):
        return pltpu.make_async_copy(_row_slab(h_hbm, tok), _row_slab(xbuf_ref.at[to_slot], t), gsem.at[to_slot])

    def gather(idx_ref, to_slot):
        for t in range(MOE_BLOCK):
            gather_copy(idx_ref[t], to_slot, t).start(priority=t % 2)

    @pl.when(i == 0)
    def _():
        gather(src_ref, 0)

    @pl.when(i + 1 < n)
    def _():
        gather(src_next_ref, 1 - slot)

    for t in range(MOE_BLOCK):
        gather_copy(0, slot, 0).wait()
    x_ref = xbuf_ref.at[slot]
    halves = [_unpack_cols(x_ref[_col_block(MOE_BLOCK, c), :]) for c in range(ROW_SUB)]
    x = jnp.concatenate([lo for lo, _ in halves] + [hi for _, hi in halves], axis=1).astype(BF16)
    hid = (_silu(_dot(x, wgb_ref[...])) * _dot(x, wub_ref[...])).astype(BF16)
    y = _dot(hid, wdb_ref[...])
    for c in range(ROW_SUB):
        o_ref[_col_block(MOE_BLOCK, c), :] = _pack_cols(y, c)


def _combine_kernel(dest_ref, dnext_ref, h_ref, w_ref, g_ref, b_ref, yrows_hbm, of_ref, ob_ref,
                    ybuf_ref, acc_ref, sem):
    i = pl.program_id(0)
    n = pl.num_programs(0)
    slot = i % 2

    def row_copy(d, to_slot, k, t):
        return pltpu.make_async_copy(_row_slab(yrows_hbm, d), _row_slab(ybuf_ref.at[to_slot, k], t), sem.at[to_slot])

    def gather(d_ref, to_slot):
        def start(t, carry):
            for k in range(TOP_K):
                row_copy(d_ref[TOP_K * t + k], to_slot, k, t).start(priority=k)
            return carry

        lax.fori_loop(0, COMBINE_TB, start, 0, unroll=8)

    @pl.when(i == 0)
    def _():
        gather(dest_ref, 0)

    @pl.when(i + 1 < n)
    def _():
        gather(dnext_ref, 1 - slot)

    for t in range(COMBINE_TB):
        for k in range(TOP_K):
            row_copy(0, slot, k, 0).wait()
    w = w_ref[...]
    y0_ref = ybuf_ref.at[slot, 0]
    y1_ref = ybuf_ref.at[slot, 1]
    for c in range(ROW_SUB):
        cb = _col_block(COMBINE_TB, c)
        lo0, hi0 = _unpack_cols(y0_ref[cb, :])
        lo1, hi1 = _unpack_cols(y1_ref[cb, :])
        acc_ref[:, c * LANES:(c + 1) * LANES] = lo0 * w[:, 0:1] + lo1 * w[:, 1:2]
        acc_ref[:, (c + ROW_SUB) * LANES:(c + ROW_SUB + 1) * LANES] = hi0 * w[:, 0:1] + hi1 * w[:, 1:2]
    y = _layernorm_rows(ALPHA * h_ref[...] + acc_ref[...], g_ref[...], b_ref[...])
    of_ref[...] = y
    ob_ref[...] = y.astype(BF16)


def _moe_layer(hf, hp, w_rg, b_rg, w_re, b_re, w_e_gate, w_e_up, w_e_down, layer, ln_g, ln_b):
    T, D = hf.shape
    n_rows = T * TOP_K + N_EXPERTS * MOE_BLOCK
    n_blocks = n_rows // MOE_BLOCK
    w_r = jnp.concatenate([w_re, w_rg, jnp.zeros((D, LANES - N_EXPERTS - N_GROUPS), F32)], axis=-1)
    b_r = jnp.concatenate([b_re, b_rg, jnp.zeros((LANES - N_EXPERTS - N_GROUPS,), F32)]).reshape(1, LANES)
    w_hi = w_r.astype(BF16)
    w_lo = (w_r - w_hi.astype(F32)).astype(BF16)
    const = lambda i: (0, 0)
    rows = lambda i: (i, 0)
    eid, wts, rank, cnt = pl.pallas_call(
        _router_kernel,
        grid=(T // R_TM,),
        in_specs=[pl.BlockSpec((R_TM, D), rows), pl.BlockSpec((D, LANES), const),
                  pl.BlockSpec((D, LANES), const), pl.BlockSpec((1, LANES), const)],
        out_specs=[pl.BlockSpec((R_TM, LANES), rows), pl.BlockSpec((R_TM, LANES), rows),
                   pl.BlockSpec((R_TM, LANES), rows), pl.BlockSpec((8, LANES), const)],
        out_shape=[jax.ShapeDtypeStruct((T, LANES), jnp.int32), jax.ShapeDtypeStruct((T, LANES), F32),
                   jax.ShapeDtypeStruct((T, LANES), jnp.int32), jax.ShapeDtypeStruct((8, LANES), jnp.int32)],
        scratch_shapes=[pltpu.VMEM((8, LANES), F32)],
        compiler_params=_cp("arbitrary"),
        name="moe_router",
    )(hf, w_hi, w_lo, b_r)
    counts = cnt[0, :N_EXPERTS]
    padded = (counts + MOE_BLOCK - 1) // MOE_BLOCK * MOE_BLOCK
    pends = jnp.cumsum(padded)
    pstarts = pends - padded
    blk_start = jnp.arange(n_blocks, dtype=jnp.int32) * MOE_BLOCK
    blk_exp = jnp.minimum(jnp.sum((pends[None, :] <= blk_start[:, None]).astype(jnp.int32), axis=1), N_EXPERTS - 1)
    sel = eid[:, :TOP_K, None] == jnp.arange(N_EXPERTS, dtype=jnp.int32)
    dest = (jnp.sum(jnp.where(sel, pstarts, 0), axis=-1) + rank[:, :TOP_K]).astype(jnp.int32).reshape(T * TOP_K)
    flat = jnp.full((n_rows,), -1, jnp.int32).at[dest].set(jnp.arange(T * TOP_K, dtype=jnp.int32),
                                                           unique_indices=True)
    row_tok = jnp.where(flat < 0, 0, flat // TOP_K)
    assert n_blocks >= 2
    any_spec = pl.BlockSpec(memory_space=pl.ANY)
    idx_spec = lambda f: pl.BlockSpec((MOE_BLOCK,), f, memory_space=pltpu.SMEM)
    w_spec = lambda shape: pl.BlockSpec((None, None) + shape, lambda i, be: (layer, be[i], 0, 0))
    yrows = pl.pallas_call(
        _expert_kernel,
        grid_spec=pltpu.PrefetchScalarGridSpec(
            num_scalar_prefetch=1,
            grid=(n_blocks,),
            in_specs=[idx_spec(lambda i, be: (i,)), idx_spec(lambda i, be: (jnp.minimum(i + 1, n_blocks - 1),)),
                      any_spec, w_spec((D, D_FF)), w_spec((D, D_FF)), w_spec((D_FF, D))],
            out_specs=pl.BlockSpec((MOE_BLOCK * ROW_SUB, LANES), lambda i, be: (i, 0)),
            scratch_shapes=[pltpu.VMEM((2, MOE_BLOCK * ROW_SUB, LANES), ROW_DT),
                            pltpu.VMEM((D, D_FF), BF16), pltpu.VMEM((D, D_FF), BF16), pltpu.VMEM((D_FF, D), BF16),
                            pltpu.SemaphoreType.DMA((2,))]),
        out_shape=jax.ShapeDtypeStruct((n_rows * ROW_SUB, LANES), ROW_DT),
        compiler_params=_cp("arbitrary"),
        name="moe_experts",
    )(blk_exp, row_tok, row_tok, hp, w_e_gate, w_e_up, w_e_down)
    n_steps = T // COMBINE_TB
    dspec = lambda f: pl.BlockSpec((TOP_K * COMBINE_TB,), f, memory_space=pltpu.SMEM)
    return pl.pallas_call(
        _combine_kernel,
        grid=(n_steps,),
        in_specs=[dspec(lambda i: (i,)), dspec(lambda i: (jnp.minimum(i + 1, n_steps - 1),)),
                  pl.BlockSpec((COMBINE_TB, D), rows), pl.BlockSpec((COMBINE_TB, LANES), rows),
                  pl.BlockSpec((1, D), const), pl.BlockSpec((1, D), const), any_spec],
        out_specs=[pl.BlockSpec((COMBINE_TB, D), rows), pl.BlockSpec((COMBINE_TB, D), rows)],
        out_shape=[jax.ShapeDtypeStruct((T, D), F32), jax.ShapeDtypeStruct((T, D), BF16)],
        scratch_shapes=[pltpu.VMEM((2, TOP_K, COMBINE_TB * ROW_SUB, LANES), ROW_DT),
                        pltpu.VMEM((COMBINE_TB, D), F32), pltpu.SemaphoreType.DMA((2,))],
        compiler_params=_cp("arbitrary"),
        name="moe_combine_ln2",
    )(dest, dest, hf, wts, ln_g.reshape(1, D), ln_b.reshape(1, D), yrows)


def kernel(x, ln_in_g, ln_in_b, rel_bias, w_in, g_cq, w_uq, g_ckv, w_ukv, w_dw_c, b_dw_c, ln_c_g, ln_c_b,
           w_conv_d, b_conv_d, a_log_f, a_log_b, dt_bias_f, dt_bias_b, d_skip, g_norm_d, w_br, w_gate, b_gate,
           w_out, ln1_g, ln1_b, w_rg, b_rg, w_re, b_re, w_e_gate, w_e_up, w_e_down, ln2_g, ln2_b):
    B, S, D = x.shape
    T = B * S
    hf, hb = _layernorm(x.reshape(T, D), ln_in_g, ln_in_b)
    a_bias = _mixer_a_bias(rel_bias, S)
    for l in range(DEPTH):
        w_a, w_r = _in_proj_weights(w_in[l])
        qkv = _matmul(hb, w_a.astype(BF16), F32, 512, 3 * A_WIDTH, "in_proj_a")
        rest = _matmul(hb, w_r.astype(BF16), F32, 512, R_WIDTH // 2, "in_proj_rest")
        y_a = _mixer_a(qkv, a_bias, B, S)
        y_b = _mixer_b(rest, g_cq[l], w_uq[l], g_ckv[l], w_ukv[l], B, S)
        y_c = _mixer_c(rest, w_dw_c[l], b_dw_c[l], ln_c_g[l], ln_c_b[l], B, S)
        y_d = _mixer_d(rest, w_conv_d[l], b_conv_d[l], a_log_f[l], a_log_b[l], dt_bias_f[l], dt_bias_b[l],
                       d_skip[l], g_norm_d[l], B, S)
        h1f, h1p = _merge(hb, hf, (y_a, y_b, y_c, y_d), w_gate[l], b_gate[l], w_br[l], w_out[l], ln1_g[l], ln1_b[l])
        hf, hb = _moe_layer(h1f, h1p, w_rg[l], b_rg[l], w_re[l], b_re[l], w_e_gate, w_e_up, w_e_down, l,
                            ln2_g[l], ln2_b[l])
    return hf.reshape(B, S, D)
```

```python
import functools

import numpy as np
import jax
import jax.numpy as jnp
from jax import lax
from jax.experimental import pallas as pl
from jax.experimental.pallas import tpu as pltpu

F32 = jnp.float32
BF16 = jnp.bfloat16

D_MODEL = 2048
DEPTH = 2
A_HEADS = 8
A_HEAD_DIM = 64
A_WIDTH = A_HEADS * A_HEAD_DIM
A_PATTERNS = ((128, 1), (512, 4), (2048, 16))
A_BAND = 64
REL_BUCKETS = 32
REL_MAX_DIST = 1024
B_HEADS = 8
B_NOPE = 64
B_ROPE = 32
B_V = 64
B_Q_LORA = 512
B_KV_LORA = 256
ROPE_THETA = 10000.0
C_CH = 512
C_KERNEL = 31
D_HEADS = 8
D_HEAD_DIM = 64
D_INNER = D_HEADS * D_HEAD_DIM
D_STATE = 128
D_GROUPS = 2
D_CONV = 5
D_CHUNK = 128
N_BRANCH = 4
BRANCH_W = 512
N_GROUPS = 4
EXP_PER_GROUP = 8
N_EXPERTS = N_GROUPS * EXP_PER_GROUP
TOP_K = 2
D_FF = 512
MOE_BLOCK = 128
ALPHA = (2 * DEPTH) ** 0.25
EPS = 1e-5
NEG_INF = -1e30

LANES = 128
R_GLU, R_XBC, R_CQ, R_Z, R_CKV, R_KR, R_DT = 0, 1024, 2048, 2560, 3072, 3328, 3456
R_WIDTH = 3584
VMEM_LIMIT = 56 * 1024 * 1024


def _cp(*sem):
    return pltpu.CompilerParams(dimension_semantics=sem, vmem_limit_bytes=VMEM_LIMIT)


def _dot(a, b):
    return jnp.dot(a, b, preferred_element_type=F32)


def _dot_nt(a, b):
    return lax.dot_general(a, b, (((1,), (1,)), ((), ())), preferred_element_type=F32)


def _split3(x):
    hi = x.astype(BF16)
    r1 = x - hi.astype(F32)
    mid = r1.astype(BF16)
    lo = (r1 - mid.astype(F32)).astype(BF16)
    return hi, mid, lo


def _layernorm_rows(x, g, b):
    mu = jnp.mean(x, axis=-1, keepdims=True)
    xc = x - mu
    var = jnp.mean(xc * xc, axis=-1, keepdims=True)
    return xc * lax.rsqrt(var + EPS) * g + b


def _sigmoid(x):
    return 1.0 / (1.0 + jnp.exp(-x))


def _silu(x):
    return x * _sigmoid(x)


def _ln_kernel(x_ref, g_ref, b_ref, of_ref, ob_ref):
    y = _layernorm_rows(x_ref[...], g_ref[...], b_ref[...])
    of_ref[...] = y
    ob_ref[...] = y.astype(BF16)


def _layernorm(x, g, b, tm=256):
    T, D = x.shape
    return pl.pallas_call(
        _ln_kernel,
        grid=(T // tm,),
        in_specs=[pl.BlockSpec((tm, D), lambda i: (i, 0)),
                  pl.BlockSpec((1, D), lambda i: (0, 0)),
                  pl.BlockSpec((1, D), lambda i: (0, 0))],
        out_specs=[pl.BlockSpec((tm, D), lambda i: (i, 0)),
                   pl.BlockSpec((tm, D), lambda i: (i, 0))],
        out_shape=[jax.ShapeDtypeStruct((T, D), F32), jax.ShapeDtypeStruct((T, D), BF16)],
        compiler_params=_cp("parallel"),
        name="ln_in",
    )(x, g.reshape(1, D), b.reshape(1, D))


def _mm_kernel(x_ref, w_ref, o_ref):
    o_ref[...] = _dot(x_ref[...], w_ref[...]).astype(o_ref.dtype)


def _matmul(x, w, out_dtype, tm, tn, name):
    M, K = x.shape
    N = w.shape[1]
    return pl.pallas_call(
        _mm_kernel,
        grid=(N // tn, M // tm),
        in_specs=[pl.BlockSpec((tm, K), lambda j, i: (i, 0)),
                  pl.BlockSpec((K, tn), lambda j, i: (0, j))],
        out_specs=pl.BlockSpec((tm, tn), lambda j, i: (i, j)),
        out_shape=jax.ShapeDtypeStruct((M, N), out_dtype),
        compiler_params=_cp("parallel", "parallel"),
        name=name,
    )(x, w)


def _t5_bucket(rel):
    half = REL_BUCKETS // 2
    max_exact = half // 2
    n = np.abs(rel)
    large = max_exact + (np.log(np.maximum(n, 1) / max_exact) / np.log(REL_MAX_DIST / max_exact)
                         * (half - max_exact)).astype(np.int32)
    large = np.minimum(large, half - 1)
    return (rel > 0).astype(np.int32) * half + np.where(n < max_exact, n, large)


def _a_window(L):
    return min(2 * LANES, L)


def _a_bias_tiles(rel_bias, d, L):
    W = _a_window(L)
    offs = (0,) if L == LANES else (0, -A_BAND, -2 * A_BAND)
    qi = np.arange(LANES)[:, None]
    kj = np.arange(W)[None, :]
    rel = np.stack([kj - qi + off for off in offs], axis=0)
    valid = np.abs(rel) <= A_BAND
    onehot = (jnp.asarray(_t5_bucket(rel * d), jnp.int32)[..., None] == jnp.arange(REL_BUCKETS)).astype(F32)
    b = jnp.einsum('vqkb,bh->vhqk', onehot, rel_bias.astype(F32), precision=lax.Precision.HIGHEST)
    b = jnp.where(valid[:, None], b, NEG_INF)
    return b.reshape(len(offs), A_HEADS // 2, 2 * LANES, W)


A_GROUP = 2


def _attn_a_kernel(q_ref, k_ref, v_ref, b16_ref, b4_ref, b1_ref, y_ref,
                   q4_ref, k4_ref, v4_ref, m_ref, l_ref, acc_ref, tmp_ref, *, S):
    (_, d1), (_, d4), (_, d16) = A_PATTERNS
    lane = lax.broadcasted_iota(jnp.int32, (LANES, LANES), 1)
    head0 = lane < A_HEAD_DIM
    scale = A_HEAD_DIM ** -0.5

    def partial_softmax(qs, ks, vs, bias_ref):
        q2 = jnp.concatenate([jnp.where(head0, qs, 0.0), jnp.where(head0, 0.0, qs)], axis=0).astype(BF16)
        s = _dot_nt(q2, ks.astype(BF16)) * scale + bias_ref[...]
        m = jnp.max(s, axis=-1, keepdims=True)
        p = jnp.exp(s - m).astype(BF16)
        num = _dot(p, vs.astype(BF16))
        den = _dot(p, jnp.ones((vs.shape[0], LANES), BF16))
        both = lambda t: jnp.where(head0, t[:LANES], t[LANES:])
        return both(jnp.broadcast_to(m, (2 * LANES, LANES))), both(den), both(num)

    def fold(old, new):
        (m_old, l_old, a_old), (m_new, l_new, a_new) = old, new
        m = jnp.maximum(m_old, m_new)
        c_old = jnp.exp(m_old - m)
        c_new = jnp.exp(m_new - m)
        return m, c_old * l_old + c_new * l_new, c_old * a_old + c_new * a_new

    stat_refs = (m_ref, l_ref, acc_ref)

    def get(c, rows):
        return tuple(ref.at[c][rows, :] for ref in stat_refs)

    def put(c, rows, stats):
        for ref, val in zip(stat_refs, stats):
            ref.at[c][rows, :] = val

    def grouped(n, group, unit):
        def trip(g, carry):
            pending = [unit(g * group + u) for u in range(group)]
            for finish in pending:
                finish()
            return carry

        lax.fori_loop(0, n // group, trip, 0)

    L4 = S // d4
    sub = d16 // d4
    assert S // d16 == LANES and sub == d4
    for c in range(d4):
        cls = pl.ds(c, L4, stride=d4)
        q4_ref[c] = q_ref[cls, :]
        k4_ref[c] = k_ref[cls, :]
        v4_ref[c] = v_ref[cls, :]

    def unit16(t):
        c = t % d4
        rows = pl.ds(t // d4, LANES, stride=sub)
        stats = partial_softmax(q4_ref.at[c][rows, :], k4_ref.at[c][rows, :], v4_ref.at[c][rows, :], b16_ref.at[0])
        return lambda: put(c, rows, stats)

    grouped(d16, A_GROUP, unit16)

    def window(i, L):
        nqb = L // LANES
        ws = pl.multiple_of(jnp.clip(i * LANES - A_BAND, 0, L - 2 * LANES), A_BAND)
        return pl.ds(ws, 2 * LANES), jnp.where(i == 0, 0, jnp.where(i == nqb - 1, 2, 1))

    def unit4(t):
        c = t % d4
        i = t // d4
        keys, var = window(i, L4)
        rows = pl.ds(pl.multiple_of(i * LANES, LANES), LANES)
        stats = partial_softmax(q4_ref.at[c][rows, :], k4_ref.at[c][keys, :], v4_ref.at[c][keys, :], b4_ref.at[var])
        return lambda: put(c, rows, fold(get(c, rows), stats))

    grouped(d4 * (L4 // LANES), A_GROUP, unit4)

    def unit1(i):
        keys, var = window(i, S)
        rows = pl.ds(pl.multiple_of(i * LANES, LANES), LANES)
        stats = partial_softmax(q_ref[rows, :], k_ref[keys, :], v_ref[keys, :], b1_ref.at[var])

        def finish():
            part = pl.ds(pl.multiple_of(i * (LANES // d4), LANES // d4), LANES // d4)
            for n, ref in enumerate(stat_refs):
                for c in range(d4):
                    tmp_ref.at[n][pl.ds(c, LANES // d4, stride=d4), :] = ref.at[c][part, :]
            _, l, a = fold(tuple(tmp_ref[n] for n in range(len(stat_refs))), stats)
            y_ref[rows, :] = (a / l).astype(y_ref.dtype)

        return finish

    grouped(S // LANES, A_GROUP, unit1)


def _mixer_a_bias(rel_bias, S):
    return tuple(_a_bias_tiles(rel_bias, d, S // d) for _, d in A_PATTERNS)


def _mixer_a(qkv, bias, B, S):
    b1, b4, b16 = bias
    npair = A_HEADS // 2
    pair_bias = lambda t: pl.BlockSpec((t.shape[0], None) + t.shape[2:], lambda b, hp: (0, hp, 0, 0))
    slab = lambda first: pl.BlockSpec((S, LANES), lambda b, hp: (b, first + hp))
    return pl.pallas_call(
        functools.partial(_attn_a_kernel, S=S),
        grid=(B, npair),
        in_specs=[slab(0), slab(npair), slab(2 * npair), pair_bias(b16), pair_bias(b4), pair_bias(b1)],
        out_specs=pl.BlockSpec((S, LANES), lambda b, hp: (b, hp)),
        out_shape=jax.ShapeDtypeStruct((B * S, A_WIDTH), BF16),
        scratch_shapes=[pltpu.VMEM((A_PATTERNS[1][1], S // A_PATTERNS[1][1], LANES), F32)] * 6
        + [pltpu.VMEM((3, LANES, LANES), F32)],
        compiler_params=_cp("parallel", "parallel"),
        name="attn_a",
    )(qkv, qkv, qkv, b16, b4, b1)


MLA_Q_SCALE = float((B_NOPE + B_ROPE) ** -0.5 * np.log2(np.e))


def _mla_proj_kernel(cq_ref, ckv_ref, kr_ref, gq_ref, gkv_ref, wqm_ref, wqs_ref, wk_ref, wv_ref, vone_ref,
                     ek_ref, cosq_ref, sinq_ref, csk_ref, q_ref, k_ref, v_ref):
    cq = cq_ref[...]
    xq = (cq * lax.rsqrt(jnp.mean(cq * cq, axis=-1, keepdims=True) + EPS) * gq_ref[...]).astype(BF16)
    ckv = ckv_ref[...]
    xkv = (ckv * lax.rsqrt(jnp.mean(ckv * ckv, axis=-1, keepdims=True) + EPS) * gkv_ref[...]).astype(BF16)
    qm = _dot(xq, wqm_ref[...])
    qs = _dot(xq, wqs_ref[...])
    cosq = cosq_ref[...] * MLA_Q_SCALE
    sinq = sinq_ref[...] * MLA_Q_SCALE
    t = kr_ref[...] * csk_ref[...]
    t_hi = t.astype(BF16)
    t_lo = (t - t_hi.astype(F32)).astype(BF16)
    kk = _dot(xkv, wk_ref[...]) + _dot(t_hi, ek_ref[...]) + _dot(t_lo, ek_ref[...])
    for h in range(B_HEADS):
        sl = slice(h * LANES, (h + 1) * LANES)
        q_ref[:, sl] = (qm[:, sl] * cosq + qs[:, sl] * sinq).astype(BF16)
    k_ref[...] = kk.astype(BF16)
    v_ref[...] = (_dot(xkv, wv_ref[...]) + vone_ref[...]).astype(BF16)


def _mla_attn_kernel(q_ref, k_ref, v_ref, o_ref):
    outs = []
    for hh in range(2):
        sl = slice(hh * LANES, (hh + 1) * LANES)
        s = _dot_nt(q_ref[:, sl], k_ref[:, sl])
        p = jnp.exp2(s - jnp.max(s, axis=-1, keepdims=True))
        outs.append(_dot(p.astype(BF16), v_ref[:, sl]))
    lane = lax.broadcasted_iota(jnp.int32, outs[0].shape, 1)
    acc = jnp.where(lane < B_V, outs[0], outs[1])
    den = pltpu.roll(jnp.where(lane < B_V, outs[1], outs[0]), B_V, axis=1)
    o_ref[...] = (acc / den).astype(o_ref.dtype)


def _mla_tables(S):
    inv_freq = ROPE_THETA ** (-jnp.arange(0, B_ROPE, 2, dtype=F32) / B_ROPE)
    ang = jnp.arange(S, dtype=F32)[:, None] * inv_freq[None]
    cos, sin = jnp.cos(ang), jnp.sin(ang)
    cos2 = jnp.concatenate([cos, cos], axis=-1)
    sin2 = jnp.concatenate([sin, sin], axis=-1)
    ones = jnp.ones((S, B_NOPE), F32)
    zn = jnp.zeros((S, B_NOPE), F32)
    zp = jnp.zeros((S, LANES - B_NOPE - B_ROPE), F32)
    cosq = jnp.concatenate([ones, cos2, zp], axis=-1)
    sinq = jnp.concatenate([zn, sin2, zp], axis=-1)
    csk = jnp.concatenate([cos2, sin2, jnp.zeros((S, LANES - 2 * B_ROPE), F32)], axis=-1)
    return cosq, sinq, csk


def _swap_cols(w):
    half = w.shape[-1] // 2
    return jnp.concatenate([-w[..., half:], w[..., :half]], axis=-1)


def _mla_weights(w_uq, w_ukv):
    dq = B_NOPE + B_ROPE
    wq = w_uq.reshape(B_Q_LORA, B_HEADS, dq)
    zpad = jnp.zeros((B_Q_LORA, B_HEADS, LANES - dq), F32)
    wqm = jnp.concatenate([wq, zpad], axis=-1).reshape(B_Q_LORA, B_HEADS * LANES)
    wqs = jnp.concatenate([jnp.zeros((B_Q_LORA, B_HEADS, B_NOPE), F32), _swap_cols(wq[..., B_NOPE:]), zpad],
                          axis=-1).reshape(B_Q_LORA, B_HEADS * LANES)
    wkv = w_ukv.reshape(B_KV_LORA, B_HEADS, B_NOPE + B_V)
    wk = jnp.concatenate([wkv[..., :B_NOPE], jnp.zeros((B_KV_LORA, B_HEADS, LANES - B_NOPE), F32)],
                         axis=-1).reshape(B_KV_LORA, B_HEADS * LANES)
    zv = jnp.zeros((B_KV_LORA, B_HEADS // 2, LANES - B_V), F32)
    wv_h = wkv[..., B_NOPE:]
    wv = jnp.stack([jnp.concatenate([wv_h[:, 0::2], zv], axis=-1),
                    jnp.concatenate([zv, wv_h[:, 1::2]], axis=-1)], axis=2).reshape(B_KV_LORA, B_HEADS * LANES)
    lane_in_pair = np.arange(B_HEADS * LANES) % (2 * LANES)
    vone = jnp.asarray(((lane_in_pair >= B_V) & (lane_in_pair < LANES + B_V)).astype(np.float32)).reshape(1, -1)
    ek = np.zeros((LANES, B_HEADS, LANES), np.float32)
    for j in range(B_ROPE):
        ek[j, :, B_NOPE + j] = 1.0
        ek[B_ROPE + j, :, B_NOPE + j] = 1.0
    ek = jnp.asarray(ek.reshape(LANES, B_HEADS * LANES))
    return wqm.astype(BF16), wqs.astype(BF16), wk.astype(BF16), wv.astype(BF16), vone, ek.astype(BF16)


def _mixer_b(rest, g_cq, w_uq, g_ckv, w_ukv, B, S, tm=512, tq=256):
    T = B * S
    wqm, wqs, wk, wv, vone, ek = _mla_weights(w_uq, w_ukv)
    cosq, sinq, csk = _mla_tables(S)
    nst = S // tm
    QW = B_HEADS * LANES
    const = lambda i: (0, 0)
    pos = lambda i: (i % nst, 0)
    q, k, v = pl.pallas_call(
        _mla_proj_kernel,
        grid=(T // tm,),
        in_specs=[pl.BlockSpec((tm, B_Q_LORA), lambda i: (i, R_CQ // B_Q_LORA)),
                  pl.BlockSpec((tm, B_KV_LORA), lambda i: (i, R_CKV // B_KV_LORA)),
                  pl.BlockSpec((tm, LANES), lambda i: (i, R_KR // LANES)),
                  pl.BlockSpec((1, B_Q_LORA), const),
                  pl.BlockSpec((1, B_KV_LORA), const),
                  pl.BlockSpec((B_Q_LORA, QW), const),
                  pl.BlockSpec((B_Q_LORA, QW), const),
                  pl.BlockSpec((B_KV_LORA, QW), const),
                  pl.BlockSpec((B_KV_LORA, QW), const),
                  pl.BlockSpec((1, QW), const),
                  pl.BlockSpec((LANES, QW), const),
                  pl.BlockSpec((tm, LANES), pos),
                  pl.BlockSpec((tm, LANES), pos),
                  pl.BlockSpec((tm, LANES), pos)],
        out_specs=[pl.BlockSpec((tm, QW), lambda i: (i, 0)),
                   pl.BlockSpec((tm, QW), lambda i: (i, 0)),
                   pl.BlockSpec((tm, QW), lambda i: (i, 0))],
        out_shape=[jax.ShapeDtypeStruct((T, QW), BF16), jax.ShapeDtypeStruct((T, QW), BF16),
                   jax.ShapeDtypeStruct((T, QW), BF16)],
        compiler_params=_cp("parallel"),
        name="mla_proj",
    )(rest, rest, rest, g_cq.reshape(1, -1), g_ckv.reshape(1, -1), wqm, wqs, wk, wv, vone, ek, cosq, sinq, csk)
    y = pl.pallas_call(
        _mla_attn_kernel,
        grid=(B, B_HEADS // 2, S // tq),
        in_specs=[pl.BlockSpec((None, tq, 2 * LANES), lambda b, hp, i: (b, i, hp)),
                  pl.BlockSpec((None, S, 2 * LANES), lambda b, hp, i: (b, 0, hp)),
                  pl.BlockSpec((None, S, 2 * LANES), lambda b, hp, i: (b, 0, hp))],
        out_specs=pl.BlockSpec((None, tq, 2 * B_V), lambda b, hp, i: (b, i, hp)),
        out_shape=jax.ShapeDtypeStruct((B, S, B_HEADS * B_V), BF16),
        compiler_params=_cp("parallel", "parallel", "arbitrary"),
        name="mla_attn",
    )(q.reshape(B, S, QW), k.reshape(B, S, QW), v.reshape(B, S, QW))
    return y.reshape(T, B_HEADS * B_V)


C_PAD = 16
C_ROWS = 128


SUBLANES = 8


def _tap_span(first, ntaps, rows):
    return rows + ((first + ntaps - 1) // SUBLANES) * SUBLANES


def _depthwise_taps(win_ref, sh_ref, w_ref, bias, ls, first, ntaps, rows):
    acc = jnp.broadcast_to(bias, (rows, LANES))
    span = _tap_span(first, ntaps, rows)
    for ph in range(SUBLANES):
        taps = [j for j in range(ntaps) if (first + j) % SUBLANES == ph]
        if not taps:
            continue
        if len(taps) == 1:
            j = taps[0]
            acc = acc + w_ref[j:j + 1, ls] * win_ref[first + j:first + j + rows, ls]
            continue
        sh_ref[0:span, :] = win_ref[ph:ph + span, ls]
        for j in taps:
            a = (first + j) // SUBLANES * SUBLANES
            acc = acc + w_ref[j:j + 1, ls] * sh_ref[a:a + rows, :]
    return acc


def _fill_window(win_ref, load_rows, r0, i, nblk, rows, pad):
    width = win_ref.shape[1]
    win_ref[pad:pad + rows, :] = load_rows(r0, rows)

    @pl.when(i > 0)
    def _():
        win_ref[0:pad, :] = load_rows(r0 - pad, pad)

    @pl.when(i == 0)
    def _():
        win_ref[0:pad, :] = jnp.zeros((pad, width), F32)

    @pl.when(i < nblk - 1)
    def _():
        win_ref[pad + rows:pad + rows + pad, :] = load_rows(r0 + rows, pad)

    @pl.when(i == nblk - 1)
    def _():
        win_ref[pad + rows:pad + rows + pad, :] = jnp.zeros((pad, width), F32)


def _conv_c_kernel(glu_ref, w_ref, b_ref, g_ref, beta_ref, o_ref, win_ref, sh_ref, acc_ref, *, S):
    i = pl.program_id(1)
    r0 = pl.multiple_of(i * C_ROWS, C_ROWS)

    def glu_rows(start, n):
        rs = pl.ds(pl.multiple_of(start, C_PAD), n)
        return glu_ref[rs, 0:C_CH] * _sigmoid(glu_ref[rs, C_CH:2 * C_CH])

    _fill_window(win_ref, glu_rows, r0, i, S // C_ROWS, C_ROWS, C_PAD)
    first = C_PAD - C_KERNEL // 2
    for lb in range(C_CH // LANES):
        ls = slice(lb * LANES, (lb + 1) * LANES)
        acc_ref[:, ls] = _depthwise_taps(win_ref, sh_ref, w_ref, b_ref[:, ls], ls, first, C_KERNEL, C_ROWS)
    y = _layernorm_rows(acc_ref[...], g_ref[...], beta_ref[...])
    o_ref[...] = _silu(y).astype(o_ref.dtype)


def _mixer_c(rest, w_dw, b_dw, ln_g, ln_b, B, S):
    T = B * S
    const = lambda b, i: (0, 0)
    y = pl.pallas_call(
        functools.partial(_conv_c_kernel, S=S),
        grid=(B, S // C_ROWS),
        in_specs=[pl.BlockSpec((None, S, 2 * C_CH), lambda b, i: (b, 0, R_GLU // (2 * C_CH))),
                  pl.BlockSpec((C_KERNEL, C_CH), const),
                  pl.BlockSpec((1, C_CH), const),
                  pl.BlockSpec((1, C_CH), const),
                  pl.BlockSpec((1, C_CH), const)],
        out_specs=pl.BlockSpec((None, C_ROWS, C_CH), lambda b, i: (b, i, 0)),
        out_shape=jax.ShapeDtypeStruct((B, S, C_CH), BF16),
        scratch_shapes=[pltpu.VMEM((C_ROWS + 2 * C_PAD, C_CH), F32),
                        pltpu.VMEM((_tap_span(C_PAD - C_KERNEL // 2, C_KERNEL, C_ROWS), LANES), F32),
                        pltpu.VMEM((C_ROWS, C_CH), F32)],
        compiler_params=_cp("parallel", "parallel"),
        name="conformer_conv",
    )(rest.reshape(B, S, R_WIDTH), w_dw, b_dw.reshape(1, -1), ln_g.reshape(1, -1), ln_b.reshape(1, -1))
    return y.reshape(T, C_CH)


D_PAD = 8
XBC_W = D_INNER + 2 * D_GROUPS * D_STATE
N_PAIR = D_HEADS // 2


def _pair_expand(v, first):
    lane = lax.broadcasted_iota(jnp.int32, (v.shape[0], LANES), 1)
    lo = jnp.broadcast_to(v[:, first:first + 1], (v.shape[0], LANES))
    hi = jnp.broadcast_to(v[:, first + 1:first + 2], (v.shape[0], LANES))
    return jnp.where(lane < D_HEAD_DIM, lo, hi)


def _ssd_kernel(xbc_ref, z_ref, dt_ref, wc_ref, bc_ref, alog_ref, dtb_ref, dskip_ref, gn_ref, o_ref,
                win_ref, sh_ref, xc_ref, a_ref, dtv_ref, y_ref, st_ref, *, S):
    Q = D_CHUNK
    nchunk = S // Q
    N = D_STATE
    bm0 = D_INNER
    cm0 = D_INNER + D_GROUPS * N

    def conv_body(c, carry):
        r0 = pl.multiple_of(c * Q, Q)
        _fill_window(win_ref, lambda st, n: xbc_ref[pl.ds(pl.multiple_of(st, D_PAD), n), :], r0, c, nchunk, Q, D_PAD)
        for lb in range(XBC_W // LANES):
            ls = slice(lb * LANES, (lb + 1) * LANES)
            acc = _depthwise_taps(win_ref, sh_ref, wc_ref, bc_ref[:, ls], ls, D_PAD - D_CONV // 2, D_CONV, Q)
            xc_ref[pl.ds(r0, Q), ls] = _silu(acc)
        return carry

    lax.fori_loop(0, nchunk, conv_body, 0)

    lane1 = lax.broadcasted_iota(jnp.int32, (1, LANES), 1)
    a_row = jnp.where(lane1 < 2 * D_HEADS, -jnp.exp(alog_ref[...]), 0.0)
    xdt = dt_ref[...] + dtb_ref[...]
    dtv = jnp.maximum(xdt, 0.0) + jnp.log(1.0 + jnp.exp(-jnp.abs(xdt)))
    dtv_ref[...] = dtv
    a_ref[...] = dtv * a_row

    row = lax.broadcasted_iota(jnp.int32, (Q, Q), 0)
    col = lax.broadcasted_iota(jnp.int32, (Q, Q), 1)
    tril = row >= col
    triu = col >= row
    lane = col

    def scan_chunk(c, lower, off, finalize):
        r0 = pl.multiple_of(c * Q, Q)
        rows = pl.ds(r0, Q)
        mask = tril if lower else triu
        tri = mask.astype(BF16)
        a_hi, a_mid, a_lo = _split3(a_ref[rows, :])
        cs = _dot(tri, a_hi) + _dot(tri, a_mid) + _dot(tri, a_lo)
        cs_t = cs.T
        ecs = jnp.exp(cs)
        edge = Q - 1 if lower else 0
        edec = jnp.exp(cs[edge:edge + 1, :] - cs)
        dt_c = dtv_ref[rows, :]
        for g in range(D_GROUPS):
            bg = xc_ref[rows, bm0 + g * N:bm0 + (g + 1) * N]
            cg = xc_ref[rows, cm0 + g * N:cm0 + (g + 1) * N].astype(BF16)
            cb = _dot_nt(cg, bg.astype(BF16))
            bg_t = bg.T.astype(BF16)
            for pp in range(N_PAIR // D_GROUPS):
                p = g * (N_PAIR // D_GROUPS) + pp
                ps = slice(p * LANES, (p + 1) * LANES)
                xdt_p = xc_ref[rows, ps] * _pair_expand(dt_c, off + 2 * p)
                ms = []
                for hh in range(2):
                    k = off + 2 * p + hh
                    diff = jnp.broadcast_to(cs[:, k:k + 1], (Q, Q)) - cs_t[k:k + 1, :]
                    ms.append((jnp.exp(jnp.where(mask, diff, NEG_INF)) * cb).astype(BF16))
                x_lo = jnp.where(lane < D_HEAD_DIM, xdt_p, 0.0).astype(BF16)
                x_hi = jnp.where(lane >= D_HEAD_DIM, xdt_p, 0.0).astype(BF16)
                y_intra = _dot(jnp.concatenate(ms, axis=1), jnp.concatenate([x_lo, x_hi], axis=0))
                hp = st_ref[p]
                ecs_p = _pair_expand(ecs, off + 2 * p)
                y_new = y_intra + _dot(cg, hp.astype(BF16)) * ecs_p
                if lower:
                    y_ref[rows, ps] = y_new
                else:
                    y_ref[rows, ps] = y_ref[rows, ps] + y_new
                xs_p = (xdt_p * _pair_expand(edec, off + 2 * p)).astype(BF16)
                st_ref[p] = hp * ecs_p[edge:edge + 1, :] + _dot(bg_t, xs_p)
        if finalize:
            y = y_ref[rows, :] + xc_ref[rows, 0:D_INNER] * dskip_ref[...]
            gated = y * _silu(z_ref[rows, :])
            out = gated * lax.rsqrt(jnp.mean(gated * gated, axis=-1, keepdims=True) + EPS) * gn_ref[...]
            o_ref[rows, :] = out.astype(o_ref.dtype)

    st_ref[...] = jnp.zeros(st_ref.shape, F32)

    def fwd_body(c, carry):
        scan_chunk(c, True, 0, False)
        return carry

    lax.fori_loop(0, nchunk, fwd_body, 0)
    st_ref[...] = jnp.zeros(st_ref.shape, F32)

    def bwd_body(k, carry):
        scan_chunk(nchunk - 1 - k, False, D_HEADS, True)
        return carry

    lax.fori_loop(0, nchunk, bwd_body, 0)


def _mixer_d(rest, w_conv, b_conv, a_log_f, a_log_b, dt_bias_f, dt_bias_b, d_skip, g_norm, B, S):
    T = B * S
    pad16 = lambda f, b: jnp.concatenate([f, b, jnp.zeros((LANES - 2 * D_HEADS,), F32)]).reshape(1, LANES)
    const = lambda b: (0, 0)
    y = pl.pallas_call(
        functools.partial(_ssd_kernel, S=S),
        grid=(B,),
        in_specs=[pl.BlockSpec((None, S, XBC_W), lambda b: (b, 0, R_XBC // XBC_W)),
                  pl.BlockSpec((None, S, D_INNER), lambda b: (b, 0, R_Z // D_INNER)),
                  pl.BlockSpec((None, S, LANES), lambda b: (b, 0, R_DT // LANES)),
                  pl.BlockSpec((D_CONV, XBC_W), const),
                  pl.BlockSpec((1, XBC_W), const),
                  pl.BlockSpec((1, LANES), const),
                  pl.BlockSpec((1, LANES), const),
                  pl.BlockSpec((1, D_INNER), const),
                  pl.BlockSpec((1, D_INNER), const)],
        out_specs=pl.BlockSpec((None, S, D_INNER), lambda b: (b, 0, 0)),
        out_shape=jax.ShapeDtypeStruct((B, S, D_INNER), BF16),
        scratch_shapes=[pltpu.VMEM((D_CHUNK + 2 * D_PAD, XBC_W), F32),
                        pltpu.VMEM((_tap_span(D_PAD - D_CONV // 2, D_CONV, D_CHUNK), LANES), F32),
                        pltpu.VMEM((S, XBC_W), F32),
                        pltpu.VMEM((S, LANES), F32),
                        pltpu.VMEM((S, LANES), F32),
                        pltpu.VMEM((S, D_INNER), F32),
                        pltpu.VMEM((N_PAIR, D_STATE, LANES), F32)],
        compiler_params=_cp("parallel"),
        name="ssd_mixer",
    )(rest.reshape(B, S, R_WIDTH), rest.reshape(B, S, R_WIDTH), rest.reshape(B, S, R_WIDTH),
      w_conv, b_conv.reshape(1, -1), pad16(a_log_f, a_log_b), pad16(dt_bias_f, dt_bias_b),
      jnp.repeat(d_skip, D_HEAD_DIM).reshape(1, -1), g_norm.reshape(1, -1))
    return y.reshape(T, D_INNER)


def _in_proj_weights(w_in_l):
    o = np.cumsum((0, A_WIDTH, A_WIDTH, A_WIDTH, B_Q_LORA, B_KV_LORA, B_ROPE, 2 * C_CH,
                   D_INNER, D_INNER, D_GROUPS * D_STATE, D_GROUPS * D_STATE, 2 * D_HEADS)).tolist()
    seg = lambda n: w_in_l[:, o[n]:o[n + 1]]
    w_a = w_in_l[:, :o[3]]
    cq, ckv, kr, glu, z, xs, bm, cm, dt = (seg(n) for n in range(3, 12))
    zeros = lambda n: jnp.zeros((w_in_l.shape[0], n), w_in_l.dtype)
    w_r = jnp.concatenate([glu, xs, bm, cm, cq, z, ckv,
                           kr, _swap_cols(kr), zeros(LANES - 2 * B_ROPE),
                           dt, zeros(LANES - 2 * D_HEADS)], axis=-1)
    assert w_r.shape[1] == R_WIDTH
    return w_a, w_r


def _merge_gate_kernel(h_ref, ya_ref, yb_ref, yc_ref, yd_ref, wg_ref, bg_ref, wbr_ref, o_ref):
    h = h_ref[...]
    acc = None
    for i, y_ref in enumerate((ya_ref, yb_ref, yc_ref, yd_ref)):
        gate = _sigmoid(_dot(h, wg_ref[i]) + bg_ref[i])
        term = gate * _dot(y_ref[...], wbr_ref[i])
        acc = term if acc is None else acc + term
    o_ref[...] = acc.astype(o_ref.dtype)


def _out_ln_kernel(m_ref, w_ref, h_ref, g_ref, b_ref, of_ref, op_ref):
    y = _layernorm_rows(ALPHA * h_ref[...] + _dot(m_ref[...], w_ref[...]), g_ref[...], b_ref[...])
    of_ref[...] = y
    for c in range(ROW_SUB):
        op_ref[_col_block(y.shape[0], c), :] = _pack_cols(y, c)


def _merge(hb, hf, branches, w_gate, b_gate, w_br, w_out, ln_g, ln_b, tm=512, tn=512, tm2=256):
    T, D = hb.shape
    ybs = pl.BlockSpec((tm, BRANCH_W), lambda j, i: (i, 0))
    merged = pl.pallas_call(
        _merge_gate_kernel,
        grid=(D // tn, T // tm),
        in_specs=[pl.BlockSpec((tm, D), lambda j, i: (i, 0)), ybs, ybs, ybs, ybs,
                  pl.BlockSpec((N_BRANCH, D, tn), lambda j, i: (0, 0, j)),
                  pl.BlockSpec((N_BRANCH, 1, tn), lambda j, i: (0, 0, j)),
                  pl.BlockSpec((N_BRANCH, BRANCH_W, tn), lambda j, i: (0, 0, j))],
        out_specs=pl.BlockSpec((tm, tn), lambda j, i: (i, j)),
        out_shape=jax.ShapeDtypeStruct((T, D), BF16),
        compiler_params=_cp("parallel", "parallel"),
        name="merge_gate",
    )(hb, *branches, w_gate.astype(BF16), b_gate.reshape(N_BRANCH, 1, D), w_br.astype(BF16))
    const = lambda i: (0, 0)
    rows = lambda i: (i, 0)
    return pl.pallas_call(
        _out_ln_kernel,
        grid=(T // tm2,),
        in_specs=[pl.BlockSpec((tm2, D), rows), pl.BlockSpec((D, D), const), pl.BlockSpec((tm2, D), rows),
                  pl.BlockSpec((1, D), const), pl.BlockSpec((1, D), const)],
        out_specs=[pl.BlockSpec((tm2, D), rows), pl.BlockSpec((tm2 * ROW_SUB, LANES), rows)],
        out_shape=[jax.ShapeDtypeStruct((T, D), F32), jax.ShapeDtypeStruct((T * ROW_SUB, LANES), ROW_DT)],
        compiler_params=_cp("parallel"),
        name="out_proj_ln1",
    )(merged, w_out.astype(BF16), hf, ln_g.reshape(1, D), ln_b.reshape(1, D))


R_TM = 256
COMBINE_TB = 128


def _router_kernel(h_ref, whi_ref, wlo_ref, b_ref, eid_ref, wts_ref, rank_ref, cnt_ref, carry_ref):
    i = pl.program_id(0)

    @pl.when(i == 0)
    def _():
        carry_ref[...] = jnp.zeros(carry_ref.shape, F32)

    x = h_ref[...]
    tm = x.shape[0]
    xh = x.astype(BF16)
    xl = (x - xh.astype(F32)).astype(BF16)
    whi = whi_ref[...]
    logits = _dot(xh, whi) + _dot(xh, wlo_ref[...]) + _dot(xl, whi) + b_ref[...]
    lane = lax.broadcasted_iota(jnp.int32, (tm, LANES), 1)
    big = jnp.int32(4 * LANES)
    is_g = (lane >= N_EXPERTS) & (lane < N_EXPERTS + N_GROUPS)
    lg = jnp.where(is_g, logits, NEG_INF)
    gmax = jnp.max(lg, axis=-1, keepdims=True)
    gidx = jnp.min(jnp.where(lg == gmax, lane - N_EXPERTS, big), axis=-1, keepdims=True)
    g_w = 1.0 / jnp.sum(jnp.where(is_g, jnp.exp(lg - gmax), 0.0), axis=-1, keepdims=True)
    in_grp = (lane < N_EXPERTS) & ((lane // EXP_PER_GROUP) == gidx)
    le = jnp.where(in_grp, logits, NEG_INF)
    e1 = jnp.max(le, axis=-1, keepdims=True)
    i1 = jnp.min(jnp.where(le == e1, lane, big), axis=-1, keepdims=True)
    le2 = jnp.where(lane == i1, NEG_INF, le)
    e2 = jnp.max(le2, axis=-1, keepdims=True)
    i2 = jnp.min(jnp.where(le2 == e2, lane, big), axis=-1, keepdims=True)
    zsum = jnp.sum(jnp.where(in_grp, jnp.exp(le - e1), 0.0), axis=-1, keepdims=True)
    p1 = 1.0 / zsum
    p2 = jnp.exp(e2 - e1) / zsum
    w1 = g_w * p1 / (p1 + p2)
    w2 = g_w * p2 / (p1 + p2)
    oh1 = lane == i1
    oh2 = lane == i2
    ohs = (oh1 | oh2).astype(BF16)
    row = lax.broadcasted_iota(jnp.int32, (tm, tm), 0)
    col = lax.broadcasted_iota(jnp.int32, (tm, tm), 1)
    before = _dot((row > col).astype(BF16), ohs) + carry_ref[0:1, :]
    r1 = jnp.sum(jnp.where(oh1, before, 0.0), axis=-1, keepdims=True)
    r2 = jnp.sum(jnp.where(oh2, before, 0.0), axis=-1, keepdims=True)
    total = carry_ref[0:1, :] + jnp.sum(ohs.astype(F32), axis=0, keepdims=True)
    carry_ref[...] = jnp.broadcast_to(total, carry_ref.shape)
    cnt_ref[...] = jnp.broadcast_to(total, cnt_ref.shape).astype(jnp.int32)
    eid_ref[...] = jnp.where(lane == 0, i1, jnp.where(lane == 1, i2, 0))
    wts_ref[...] = jnp.where(lane == 0, w1, jnp.where(lane == 1, w2, 0.0))
    rank_ref[...] = jnp.where(lane == 0, r1, jnp.where(lane == 1, r2, 0.0)).astype(jnp.int32)


ROW_SUB = D_MODEL // LANES // 2
ROW_DT = jnp.uint32


def _row_slab(ref, r):
    return ref.at[pl.ds(pl.multiple_of(r * ROW_SUB, ROW_SUB), ROW_SUB)]


def _col_block(n, c):
    return pl.ds(c, n, stride=ROW_SUB)


def _pack_cols(x, c):
    as_bits = lambda t: lax.bitcast_convert_type(t.astype(BF16).astype(F32), ROW_DT)
    lo = as_bits(x[:, c * LANES:(c + 1) * LANES])
    hi = as_bits(x[:, (c + ROW_SUB) * LANES:(c + ROW_SUB + 1) * LANES])
    return (lo >> 16) | hi


def _unpack_cols(w):
    return (lax.bitcast_convert_type(w << 16, F32),
            lax.bitcast_convert_type(w & jnp.uint32(0xFFFF0000), F32))


EXPERT_AHEAD = 2


def _expert_kernel(be_ref, *refs):
    src_refs = refs[:EXPERT_AHEAD + 1]
    h_hbm, wg_ref, wu_ref, wd_ref, o_ref, xbuf_ref, wgb_ref, wub_ref, wdb_ref, gsem = refs[EXPERT_AHEAD + 1:]
    i = pl.program_id(0)
    n = pl.num_programs(0)
    nbuf = EXPERT_AHEAD + 1
    slot = i % nbuf

    @pl.when((i == 0) | (be_ref[i] != be_ref[jnp.maximum(i - 1, 0)]))
    def _():
        wgb_ref[...] = wg_ref[...].astype(BF16)
        wub_ref[...] = wu_ref[...].astype(BF16)
        wdb_ref[...] = wd_ref[...].astype(BF16)

    def gather_copy(tok, to_slot, t):
        return pltpu.make_async_copy(_row_slab(h_hbm, tok), _row_slab(xbuf_ref.at[to_slot], t), gsem.at[to_slot])

    def gather(idx_ref, to_slot):
        for t in range(MOE_BLOCK):
            gather_copy(idx_ref[t], to_slot, t).start(priority=t % 2)

    @pl.when(i == 0)
    def _():
        for a in range(EXPERT_AHEAD):
            gather(src_refs[a], a)

    @pl.when(i + EXPERT_AHEAD < n)
    def _():
        gather(src_refs[EXPERT_AHEAD], (i + EXPERT_AHEAD) % nbuf)

    for t in range(MOE_BLOCK):
        gather_copy(0, slot, 0).wait()
    x_ref = xbuf_ref.at[slot]
    halves = [_unpack_cols(x_ref[_col_block(MOE_BLOCK, c), :]) for c in range(ROW_SUB)]
    x = jnp.concatenate([lo for lo, _ in halves] + [hi for _, hi in halves], axis=1).astype(BF16)
    hid = (_silu(_dot(x, wgb_ref[...])) * _dot(x, wub_ref[...])).astype(BF16)
    y = _dot(hid, wdb_ref[...])
    for c in range(ROW_SUB):
        o_ref[_col_block(MOE_BLOCK, c), :] = _pack_cols(y, c)


def _combine_kernel(dest_ref, dnext_ref, h_ref, w_ref, g_ref, b_ref, yrows_hbm, of_ref, ob_ref,
                    ybuf_ref, acc_ref, sem):
    i = pl.program_id(0)
    n = pl.num_programs(0)
    slot = i % 2

    def row_copy(d, to_slot, k, t):
        return pltpu.make_async_copy(_row_slab(yrows_hbm, d), _row_slab(ybuf_ref.at[to_slot, k], t), sem.at[to_slot])

    def gather(d_ref, to_slot):
        def start(t, carry):
            for k in range(TOP_K):
                row_copy(d_ref[TOP_K * t + k], to_slot, k, t).start(priority=k)
            return carry

        lax.fori_loop(0, COMBINE_TB, start, 0, unroll=8)

    @pl.when(i == 0)
    def _():
        gather(dest_ref, 0)

    @pl.when(i + 1 < n)
    def _():
        gather(dnext_ref, 1 - slot)

    for t in range(COMBINE_TB):
        for k in range(TOP_K):
            row_copy(0, slot, k, 0).wait()
    w = w_ref[...]
    y0_ref = ybuf_ref.at[slot, 0]
    y1_ref = ybuf_ref.at[slot, 1]
    for c in range(ROW_SUB):
        cb = _col_block(COMBINE_TB, c)
        lo0, hi0 = _unpack_cols(y0_ref[cb, :])
        lo1, hi1 = _unpack_cols(y1_ref[cb, :])
        acc_ref[:, c * LANES:(c + 1) * LANES] = lo0 * w[:, 0:1] + lo1 * w[:, 1:2]
        acc_ref[:, (c + ROW_SUB) * LANES:(c + ROW_SUB + 1) * LANES] = hi0 * w[:, 0:1] + hi1 * w[:, 1:2]
    y = _layernorm_rows(ALPHA * h_ref[...] + acc_ref[...], g_ref[...], b_ref[...])
    of_ref[...] = y
    ob_ref[...] = y.astype(BF16)


def _moe_layer(hf, hp, w_rg, b_rg, w_re, b_re, w_e_gate, w_e_up, w_e_down, layer, ln_g, ln_b):
    T, D = hf.shape
    n_rows = T * TOP_K + N_EXPERTS * MOE_BLOCK
    n_blocks = n_rows // MOE_BLOCK
    w_r = jnp.concatenate([w_re, w_rg, jnp.zeros((D, LANES - N_EXPERTS - N_GROUPS), F32)], axis=-1)
    b_r = jnp.concatenate([b_re, b_rg, jnp.zeros((LANES - N_EXPERTS - N_GROUPS,), F32)]).reshape(1, LANES)
    w_hi = w_r.astype(BF16)
    w_lo = (w_r - w_hi.astype(F32)).astype(BF16)
    const = lambda i: (0, 0)
    rows = lambda i: (i, 0)
    eid, wts, rank, cnt = pl.pallas_call(
        _router_kernel,
        grid=(T // R_TM,),
        in_specs=[pl.BlockSpec((R_TM, D), rows), pl.BlockSpec((D, LANES), const),
                  pl.BlockSpec((D, LANES), const), pl.BlockSpec((1, LANES), const)],
        out_specs=[pl.BlockSpec((R_TM, LANES), rows), pl.BlockSpec((R_TM, LANES), rows),
                   pl.BlockSpec((R_TM, LANES), rows), pl.BlockSpec((8, LANES), const)],
        out_shape=[jax.ShapeDtypeStruct((T, LANES), jnp.int32), jax.ShapeDtypeStruct((T, LANES), F32),
                   jax.ShapeDtypeStruct((T, LANES), jnp.int32), jax.ShapeDtypeStruct((8, LANES), jnp.int32)],
        scratch_shapes=[pltpu.VMEM((8, LANES), F32)],
        compiler_params=_cp("arbitrary"),
        name="moe_router",
    )(hf, w_hi, w_lo, b_r)
    counts = cnt[0, :N_EXPERTS]
    padded = (counts + MOE_BLOCK - 1) // MOE_BLOCK * MOE_BLOCK
    pends = jnp.cumsum(padded)
    pstarts = pends - padded
    blk_start = jnp.arange(n_blocks, dtype=jnp.int32) * MOE_BLOCK
    blk_exp = jnp.minimum(jnp.sum((pends[None, :] <= blk_start[:, None]).astype(jnp.int32), axis=1), N_EXPERTS - 1)
    sel = eid[:, :TOP_K, None] == jnp.arange(N_EXPERTS, dtype=jnp.int32)
    dest = (jnp.sum(jnp.where(sel, pstarts, 0), axis=-1) + rank[:, :TOP_K]).astype(jnp.int32).reshape(T * TOP_K)
    flat = jnp.full((n_rows,), -1, jnp.int32).at[dest].set(jnp.arange(T * TOP_K, dtype=jnp.int32),
                                                           unique_indices=True)
    row_tok = jnp.where(flat < 0, 0, flat // TOP_K)
    assert n_blocks > EXPERT_AHEAD
    any_spec = pl.BlockSpec(memory_space=pl.ANY)
    idx_spec = lambda f: pl.BlockSpec((MOE_BLOCK,), f, memory_space=pltpu.SMEM)
    w_spec = lambda shape: pl.BlockSpec((None, None) + shape, lambda i, be: (layer, be[i], 0, 0))
    yrows = pl.pallas_call(
        _expert_kernel,
        grid_spec=pltpu.PrefetchScalarGridSpec(
            num_scalar_prefetch=1,
            grid=(n_blocks,),
            in_specs=[idx_spec(lambda i, be, a=a: (jnp.minimum(i + a, n_blocks - 1),)) for a in range(EXPERT_AHEAD + 1)]
            + [any_spec, w_spec((D, D_FF)), w_spec((D, D_FF)), w_spec((D_FF, D))],
            out_specs=pl.BlockSpec((MOE_BLOCK * ROW_SUB, LANES), lambda i, be: (i, 0)),
            scratch_shapes=[pltpu.VMEM((EXPERT_AHEAD + 1, MOE_BLOCK * ROW_SUB, LANES), ROW_DT),
                            pltpu.VMEM((D, D_FF), BF16), pltpu.VMEM((D, D_FF), BF16), pltpu.VMEM((D_FF, D), BF16),
                            pltpu.SemaphoreType.DMA((EXPERT_AHEAD + 1,))]),
        out_shape=jax.ShapeDtypeStruct((n_rows * ROW_SUB, LANES), ROW_DT),
        compiler_params=_cp("arbitrary"),
        name="moe_experts",
    )(blk_exp, *([row_tok] * (EXPERT_AHEAD + 1)), hp, w_e_gate, w_e_up, w_e_down)
    n_steps = T // COMBINE_TB
    dspec = lambda f: pl.BlockSpec((TOP_K * COMBINE_TB,), f, memory_space=pltpu.SMEM)
    return pl.pallas_call(
        _combine_kernel,
        grid=(n_steps,),
        in_specs=[dspec(lambda i: (i,)), dspec(lambda i: (jnp.minimum(i + 1, n_steps - 1),)),
                  pl.BlockSpec((COMBINE_TB, D), rows), pl.BlockSpec((COMBINE_TB, LANES), rows),
                  pl.BlockSpec((1, D), const), pl.BlockSpec((1, D), const), any_spec],
        out_specs=[pl.BlockSpec((COMBINE_TB, D), rows), pl.BlockSpec((COMBINE_TB, D), rows)],
        out_shape=[jax.ShapeDtypeStruct((T, D), F32), jax.ShapeDtypeStruct((T, D), BF16)],
        scratch_shapes=[pltpu.VMEM((2, TOP_K, COMBINE_TB * ROW_SUB, LANES), ROW_DT),
                        pltpu.VMEM((COMBINE_TB, D), F32), pltpu.SemaphoreType.DMA((2,))],
        compiler_params=_cp("arbitrary"),
        name="moe_combine_ln2",
    )(dest, dest, hf, wts, ln_g.reshape(1, D), ln_b.reshape(1, D), yrows)


def kernel(x, ln_in_g, ln_in_b, rel_bias, w_in, g_cq, w_uq, g_ckv, w_ukv, w_dw_c, b_dw_c, ln_c_g, ln_c_b,
           w_conv_d, b_conv_d, a_log_f, a_log_b, dt_bias_f, dt_bias_b, d_skip, g_norm_d, w_br, w_gate, b_gate,
           w_out, ln1_g, ln1_b, w_rg, b_rg, w_re, b_re, w_e_gate, w_e_up, w_e_down, ln2_g, ln2_b):
    B, S, D = x.shape
    T = B * S
    hf, hb = _layernorm(x.reshape(T, D), ln_in_g, ln_in_b)
    a_bias = _mixer_a_bias(rel_bias, S)
    for l in range(DEPTH):
        w_a, w_r = _in_proj_weights(w_in[l])
        qkv = _matmul(hb, w_a.astype(BF16), F32, 512, 3 * A_WIDTH, "in_proj_a")
        rest = _matmul(hb, w_r.astype(BF16), F32, 512, R_WIDTH // 2, "in_proj_rest")
        y_a = _mixer_a(qkv, a_bias, B, S)
        y_b = _mixer_b(rest, g_cq[l], w_uq[l], g_ckv[l], w_ukv[l], B, S)
        y_c = _mixer_c(rest, w_dw_c[l], b_dw_c[l], ln_c_g[l], ln_c_b[l], B, S)
        y_d = _mixer_d(rest, w_conv_d[l], b_conv_d[l], a_log_f[l], a_log_b[l], dt_bias_f[l], dt_bias_b[l],
                       d_skip[l], g_norm_d[l], B, S)
        h1f, h1p = _merge(hb, hf, (y_a, y_b, y_c, y_d), w_gate[l], b_gate[l], w_br[l], w_out[l], ln1_g[l], ln1_b[l])
        hf, hb = _moe_layer(h1f, h1p, w_rg[l], b_rg[l], w_re[l], b_re[l], w_e_gate, w_e_up, w_e_down, l,
                            ln2_g[l], ln2_b[l])
    return hf.reshape(B, S, D)
```

```python
import functools

import numpy as np
import jax
import jax.numpy as jnp
from jax import lax
from jax.experimental import pallas as pl
from jax.experimental.pallas import tpu as pltpu

F32 = jnp.float32
BF16 = jnp.bfloat16

D_MODEL = 2048
DEPTH = 2
A_HEADS = 8
A_HEAD_DIM = 64
A_WIDTH = A_HEADS * A_HEAD_DIM
A_PATTERNS = ((128, 1), (512, 4), (2048, 16))
A_BAND = 64
REL_BUCKETS = 32
REL_MAX_DIST = 1024
B_HEADS = 8
B_NOPE = 64
B_ROPE = 32
B_V = 64
B_Q_LORA = 512
B_KV_LORA = 256
ROPE_THETA = 10000.0
C_CH = 512
C_KERNEL = 31
D_HEADS = 8
D_HEAD_DIM = 64
D_INNER = D_HEADS * D_HEAD_DIM
D_STATE = 128
D_GROUPS = 2
D_CONV = 5
D_CHUNK = 128
N_BRANCH = 4
BRANCH_W = 512
N_GROUPS = 4
EXP_PER_GROUP = 8
N_EXPERTS = N_GROUPS * EXP_PER_GROUP
TOP_K = 2
D_FF = 512
MOE_BLOCK = 128
ALPHA = (2 * DEPTH) ** 0.25
EPS = 1e-5
NEG_INF = -1e30

LANES = 128
R_GLU, R_XBC, R_CQ, R_Z, R_CKV, R_KR, R_DT = 0, 1024, 2048, 2560, 3072, 3328, 3456
R_WIDTH = 3584
VMEM_LIMIT = 56 * 1024 * 1024


def _cp(*sem):
    return pltpu.CompilerParams(dimension_semantics=sem, vmem_limit_bytes=VMEM_LIMIT)


def _dot(a, b):
    return jnp.dot(a, b, preferred_element_type=F32)


def _dot_nt(a, b):
    return lax.dot_general(a, b, (((1,), (1,)), ((), ())), preferred_element_type=F32)


def _split3(x):
    hi = x.astype(BF16)
    r1 = x - hi.astype(F32)
    mid = r1.astype(BF16)
    lo = (r1 - mid.astype(F32)).astype(BF16)
    return hi, mid, lo


def _layernorm_rows(x, g, b):
    mu = jnp.mean(x, axis=-1, keepdims=True)
    xc = x - mu
    var = jnp.mean(xc * xc, axis=-1, keepdims=True)
    return xc * lax.rsqrt(var + EPS) * g + b


def _sigmoid(x):
    return 1.0 / (1.0 + jnp.exp(-x))


def _silu(x):
    return x * _sigmoid(x)


def _ln_kernel(x_ref, g_ref, b_ref, of_ref, ob_ref):
    y = _layernorm_rows(x_ref[...], g_ref[...], b_ref[...])
    of_ref[...] = y
    ob_ref[...] = y.astype(BF16)


def _layernorm(x, g, b, tm=256):
    T, D = x.shape
    return pl.pallas_call(
        _ln_kernel,
        grid=(T // tm,),
        in_specs=[pl.BlockSpec((tm, D), lambda i: (i, 0)),
                  pl.BlockSpec((1, D), lambda i: (0, 0)),
                  pl.BlockSpec((1, D), lambda i: (0, 0))],
        out_specs=[pl.BlockSpec((tm, D), lambda i: (i, 0)),
                   pl.BlockSpec((tm, D), lambda i: (i, 0))],
        out_shape=[jax.ShapeDtypeStruct((T, D), F32), jax.ShapeDtypeStruct((T, D), BF16)],
        compiler_params=_cp("parallel"),
        name="ln_in",
    )(x, g.reshape(1, D), b.reshape(1, D))


def _mm_kernel(x_ref, w_ref, o_ref):
    o_ref[...] = _dot(x_ref[...], w_ref[...]).astype(o_ref.dtype)


def _matmul(x, w, out_dtype, tm, tn, name):
    M, K = x.shape
    N = w.shape[1]
    return pl.pallas_call(
        _mm_kernel,
        grid=(N // tn, M // tm),
        in_specs=[pl.BlockSpec((tm, K), lambda j, i: (i, 0)),
                  pl.BlockSpec((K, tn), lambda j, i: (0, j))],
        out_specs=pl.BlockSpec((tm, tn), lambda j, i: (i, j)),
        out_shape=jax.ShapeDtypeStruct((M, N), out_dtype),
        compiler_params=_cp("parallel", "parallel"),
        name=name,
    )(x, w)


def _t5_bucket(rel):
    half = REL_BUCKETS // 2
    max_exact = half // 2
    n = np.abs(rel)
    large = max_exact + (np.log(np.maximum(n, 1) / max_exact) / np.log(REL_MAX_DIST / max_exact)
                         * (half - max_exact)).astype(np.int32)
    large = np.minimum(large, half - 1)
    return (rel > 0).astype(np.int32) * half + np.where(n < max_exact, n, large)


def _a_window(L):
    return min(2 * LANES, L)


def _a_bias_tiles(rel_bias, d, L):
    W = _a_window(L)
    offs = (0,) if L == LANES else (0, -A_BAND, -2 * A_BAND)
    qi = np.arange(LANES)[:, None]
    kj = np.arange(W)[None, :]
    rel = np.stack([kj - qi + off for off in offs], axis=0)
    valid = np.abs(rel) <= A_BAND
    onehot = (jnp.asarray(_t5_bucket(rel * d), jnp.int32)[..., None] == jnp.arange(REL_BUCKETS)).astype(F32)
    b = jnp.einsum('vqkb,bh->vhqk', onehot, rel_bias.astype(F32), precision=lax.Precision.HIGHEST)
    b = jnp.where(valid[:, None], b, NEG_INF)
    return b.reshape(len(offs), A_HEADS // 2, 2 * LANES, W)


A_GROUP = 2


def _attn_a_kernel(q_ref, k_ref, v_ref, b16_ref, b4_ref, b1_ref, y_ref,
                   q4_ref, k4_ref, v4_ref, m_ref, l_ref, acc_ref, tmp_ref, *, S):
    (_, d1), (_, d4), (_, d16) = A_PATTERNS
    lane = lax.broadcasted_iota(jnp.int32, (LANES, LANES), 1)
    head0 = lane < A_HEAD_DIM
    scale = A_HEAD_DIM ** -0.5

    def partial_softmax(qs, ks, vs, bias_ref):
        q2 = jnp.concatenate([jnp.where(head0, qs, 0.0), jnp.where(head0, 0.0, qs)], axis=0).astype(BF16)
        s = _dot_nt(q2, ks.astype(BF16)) * scale + bias_ref[...]
        m = jnp.max(s, axis=-1, keepdims=True)
        p = jnp.exp(s - m).astype(BF16)
        num = _dot(p, vs.astype(BF16))
        den = _dot(p, jnp.ones((vs.shape[0], LANES), BF16))
        both = lambda t: jnp.where(head0, t[:LANES], t[LANES:])
        return both(jnp.broadcast_to(m, (2 * LANES, LANES))), both(den), both(num)

    def fold(old, new):
        (m_old, l_old, a_old), (m_new, l_new, a_new) = old, new
        m = jnp.maximum(m_old, m_new)
        c_old = jnp.exp(m_old - m)
        c_new = jnp.exp(m_new - m)
        return m, c_old * l_old + c_new * l_new, c_old * a_old + c_new * a_new

    stat_refs = (m_ref, l_ref, acc_ref)

    def get(c, rows):
        return tuple(ref.at[c][rows, :] for ref in stat_refs)

    def put(c, rows, stats):
        for ref, val in zip(stat_refs, stats):
            ref.at[c][rows, :] = val

    def grouped(n, group, unit):
        def trip(g, carry):
            pending = [unit(g * group + u) for u in range(group)]
            for finish in pending:
                finish()
            return carry

        lax.fori_loop(0, n // group, trip, 0)

    L4 = S // d4
    sub = d16 // d4
    assert S // d16 == LANES and sub == d4
    for c in range(d4):
        cls = pl.ds(c, L4, stride=d4)
        q4_ref[c] = q_ref[cls, :]
        k4_ref[c] = k_ref[cls, :]
        v4_ref[c] = v_ref[cls, :]

    def unit16(t):
        c = t % d4
        rows = pl.ds(t // d4, LANES, stride=sub)
        stats = partial_softmax(q4_ref.at[c][rows, :], k4_ref.at[c][rows, :], v4_ref.at[c][rows, :], b16_ref.at[0])
        return lambda: put(c, rows, stats)

    grouped(d16, A_GROUP, unit16)

    def window(i, L):
        nqb = L // LANES
        ws = pl.multiple_of(jnp.clip(i * LANES - A_BAND, 0, L - 2 * LANES), A_BAND)
        return pl.ds(ws, 2 * LANES), jnp.where(i == 0, 0, jnp.where(i == nqb - 1, 2, 1))

    def unit4(t):
        c = t % d4
        i = t // d4
        keys, var = window(i, L4)
        rows = pl.ds(pl.multiple_of(i * LANES, LANES), LANES)
        stats = partial_softmax(q4_ref.at[c][rows, :], k4_ref.at[c][keys, :], v4_ref.at[c][keys, :], b4_ref.at[var])
        return lambda: put(c, rows, fold(get(c, rows), stats))

    grouped(d4 * (L4 // LANES), A_GROUP, unit4)

    def unit1(i):
        keys, var = window(i, S)
        rows = pl.ds(pl.multiple_of(i * LANES, LANES), LANES)
        stats = partial_softmax(q_ref[rows, :], k_ref[keys, :], v_ref[keys, :], b1_ref.at[var])

        def finish():
            part = pl.ds(pl.multiple_of(i * (LANES // d4), LANES // d4), LANES // d4)
            for n, ref in enumerate(stat_refs):
                for c in range(d4):
                    tmp_ref.at[n][pl.ds(c, LANES // d4, stride=d4), :] = ref.at[c][part, :]
            _, l, a = fold(tuple(tmp_ref[n] for n in range(len(stat_refs))), stats)
            y_ref[rows, :] = (a / l).astype(y_ref.dtype)

        return finish

    grouped(S // LANES, A_GROUP, unit1)


def _mixer_a_bias(rel_bias, S):
    return tuple(_a_bias_tiles(rel_bias, d, S // d) for _, d in A_PATTERNS)


def _mixer_a(qkv, bias, B, S):
    b1, b4, b16 = bias
    npair = A_HEADS // 2
    pair_bias = lambda t: pl.BlockSpec((t.shape[0], None) + t.shape[2:], lambda b, hp: (0, hp, 0, 0))
    slab = lambda first: pl.BlockSpec((S, LANES), lambda b, hp: (b, first + hp))
    return pl.pallas_call(
        functools.partial(_attn_a_kernel, S=S),
        grid=(B, npair),
        in_specs=[slab(0), slab(npair), slab(2 * npair), pair_bias(b16), pair_bias(b4), pair_bias(b1)],
        out_specs=pl.BlockSpec((S, LANES), lambda b, hp: (b, hp)),
        out_shape=jax.ShapeDtypeStruct((B * S, A_WIDTH), BF16),
        scratch_shapes=[pltpu.VMEM((A_PATTERNS[1][1], S // A_PATTERNS[1][1], LANES), F32)] * 6
        + [pltpu.VMEM((3, LANES, LANES), F32)],
        compiler_params=_cp("parallel", "parallel"),
        name="attn_a",
    )(qkv, qkv, qkv, b16, b4, b1)


MLA_Q_SCALE = float((B_NOPE + B_ROPE) ** -0.5 * np.log2(np.e))


def _mla_proj_kernel(cq_ref, ckv_ref, kr_ref, gq_ref, gkv_ref, wqm_ref, wqs_ref, wk_ref, wv_ref, vone_ref,
                     ek_ref, cosq_ref, sinq_ref, csk_ref, q_ref, k_ref, v_ref):
    cq = cq_ref[...]
    xq = (cq * lax.rsqrt(jnp.mean(cq * cq, axis=-1, keepdims=True) + EPS) * gq_ref[...]).astype(BF16)
    ckv = ckv_ref[...]
    xkv = (ckv * lax.rsqrt(jnp.mean(ckv * ckv, axis=-1, keepdims=True) + EPS) * gkv_ref[...]).astype(BF16)
    qm = _dot(xq, wqm_ref[...])
    qs = _dot(xq, wqs_ref[...])
    cosq = cosq_ref[...] * MLA_Q_SCALE
    sinq = sinq_ref[...] * MLA_Q_SCALE
    t = kr_ref[...] * csk_ref[...]
    t_hi = t.astype(BF16)
    t_lo = (t - t_hi.astype(F32)).astype(BF16)
    kk = _dot(xkv, wk_ref[...]) + _dot(t_hi, ek_ref[...]) + _dot(t_lo, ek_ref[...])
    for h in range(B_HEADS):
        sl = slice(h * LANES, (h + 1) * LANES)
        q_ref[:, sl] = (qm[:, sl] * cosq + qs[:, sl] * sinq).astype(BF16)
    k_ref[...] = kk.astype(BF16)
    v_ref[...] = (_dot(xkv, wv_ref[...]) + vone_ref[...]).astype(BF16)


def _mla_attn_kernel(q_ref, k_ref, v_ref, o_ref):
    outs = []
    for hh in range(2):
        sl = slice(hh * LANES, (hh + 1) * LANES)
        s = _dot_nt(q_ref[:, sl], k_ref[:, sl])
        p = jnp.exp2(s - jnp.max(s, axis=-1, keepdims=True))
        outs.append(_dot(p.astype(BF16), v_ref[:, sl]))
    lane = lax.broadcasted_iota(jnp.int32, outs[0].shape, 1)
    acc = jnp.where(lane < B_V, outs[0], outs[1])
    den = pltpu.roll(jnp.where(lane < B_V, outs[1], outs[0]), B_V, axis=1)
    o_ref[...] = (acc / den).astype(o_ref.dtype)


def _mla_tables(S):
    inv_freq = ROPE_THETA ** (-jnp.arange(0, B_ROPE, 2, dtype=F32) / B_ROPE)
    ang = jnp.arange(S, dtype=F32)[:, None] * inv_freq[None]
    cos, sin = jnp.cos(ang), jnp.sin(ang)
    cos2 = jnp.concatenate([cos, cos], axis=-1)
    sin2 = jnp.concatenate([sin, sin], axis=-1)
    ones = jnp.ones((S, B_NOPE), F32)
    zn = jnp.zeros((S, B_NOPE), F32)
    zp = jnp.zeros((S, LANES - B_NOPE - B_ROPE), F32)
    cosq = jnp.concatenate([ones, cos2, zp], axis=-1)
    sinq = jnp.concatenate([zn, sin2, zp], axis=-1)
    csk = jnp.concatenate([cos2, sin2, jnp.zeros((S, LANES - 2 * B_ROPE), F32)], axis=-1)
    return cosq, sinq, csk


def _swap_cols(w):
    half = w.shape[-1] // 2
    return jnp.concatenate([-w[..., half:], w[..., :half]], axis=-1)


def _mla_weights(w_uq, w_ukv):
    dq = B_NOPE + B_ROPE
    wq = w_uq.reshape(B_Q_LORA, B_HEADS, dq)
    zpad = jnp.zeros((B_Q_LORA, B_HEADS, LANES - dq), F32)
    wqm = jnp.concatenate([wq, zpad], axis=-1).reshape(B_Q_LORA, B_HEADS * LANES)
    wqs = jnp.concatenate([jnp.zeros((B_Q_LORA, B_HEADS, B_NOPE), F32), _swap_cols(wq[..., B_NOPE:]), zpad],
                          axis=-1).reshape(B_Q_LORA, B_HEADS * LANES)
    wkv = w_ukv.reshape(B_KV_LORA, B_HEADS, B_NOPE + B_V)
    wk = jnp.concatenate([wkv[..., :B_NOPE], jnp.zeros((B_KV_LORA, B_HEADS, LANES - B_NOPE), F32)],
                         axis=-1).reshape(B_KV_LORA, B_HEADS * LANES)
    zv = jnp.zeros((B_KV_LORA, B_HEADS // 2, LANES - B_V), F32)
    wv_h = wkv[..., B_NOPE:]
    wv = jnp.stack([jnp.concatenate([wv_h[:, 0::2], zv], axis=-1),
                    jnp.concatenate([zv, wv_h[:, 1::2]], axis=-1)], axis=2).reshape(B_KV_LORA, B_HEADS * LANES)
    lane_in_pair = np.arange(B_HEADS * LANES) % (2 * LANES)
    vone = jnp.asarray(((lane_in_pair >= B_V) & (lane_in_pair < LANES + B_V)).astype(np.float32)).reshape(1, -1)
    ek = np.zeros((LANES, B_HEADS, LANES), np.float32)
    for j in range(B_ROPE):
        ek[j, :, B_NOPE + j] = 1.0
        ek[B_ROPE + j, :, B_NOPE + j] = 1.0
    ek = jnp.asarray(ek.reshape(LANES, B_HEADS * LANES))
    return wqm.astype(BF16), wqs.astype(BF16), wk.astype(BF16), wv.astype(BF16), vone, ek.astype(BF16)


def _mixer_b(rest, g_cq, w_uq, g_ckv, w_ukv, B, S, tm=512, tq=256):
    T = B * S
    wqm, wqs, wk, wv, vone, ek = _mla_weights(w_uq, w_ukv)
    cosq, sinq, csk = _mla_tables(S)
    nst = S // tm
    QW = B_HEADS * LANES
    const = lambda i: (0, 0)
    pos = lambda i: (i % nst, 0)
    q, k, v = pl.pallas_call(
        _mla_proj_kernel,
        grid=(T // tm,),
        in_specs=[pl.BlockSpec((tm, B_Q_LORA), lambda i: (i, R_CQ // B_Q_LORA)),
                  pl.BlockSpec((tm, B_KV_LORA), lambda i: (i, R_CKV // B_KV_LORA)),
                  pl.BlockSpec((tm, LANES), lambda i: (i, R_KR // LANES)),
                  pl.BlockSpec((1, B_Q_LORA), const),
                  pl.BlockSpec((1, B_KV_LORA), const),
                  pl.BlockSpec((B_Q_LORA, QW), const),
                  pl.BlockSpec((B_Q_LORA, QW), const),
                  pl.BlockSpec((B_KV_LORA, QW), const),
                  pl.BlockSpec((B_KV_LORA, QW), const),
                  pl.BlockSpec((1, QW), const),
                  pl.BlockSpec((LANES, QW), const),
                  pl.BlockSpec((tm, LANES), pos),
                  pl.BlockSpec((tm, LANES), pos),
                  pl.BlockSpec((tm, LANES), pos)],
        out_specs=[pl.BlockSpec((tm, QW), lambda i: (i, 0)),
                   pl.BlockSpec((tm, QW), lambda i: (i, 0)),
                   pl.BlockSpec((tm, QW), lambda i: (i, 0))],
        out_shape=[jax.ShapeDtypeStruct((T, QW), BF16), jax.ShapeDtypeStruct((T, QW), BF16),
                   jax.ShapeDtypeStruct((T, QW), BF16)],
        compiler_params=_cp("parallel"),
        name="mla_proj",
    )(rest, rest, rest, g_cq.reshape(1, -1), g_ckv.reshape(1, -1), wqm, wqs, wk, wv, vone, ek, cosq, sinq, csk)
    y = pl.pallas_call(
        _mla_attn_kernel,
        grid=(B, B_HEADS // 2, S // tq),
        in_specs=[pl.BlockSpec((None, tq, 2 * LANES), lambda b, hp, i: (b, i, hp)),
                  pl.BlockSpec((None, S, 2 * LANES), lambda b, hp, i: (b, 0, hp)),
                  pl.BlockSpec((None, S, 2 * LANES), lambda b, hp, i: (b, 0, hp))],
        out_specs=pl.BlockSpec((None, tq, 2 * B_V), lambda b, hp, i: (b, i, hp)),
        out_shape=jax.ShapeDtypeStruct((B, S, B_HEADS * B_V), BF16),
        compiler_params=_cp("parallel", "parallel", "arbitrary"),
        name="mla_attn",
    )(q.reshape(B, S, QW), k.reshape(B, S, QW), v.reshape(B, S, QW))
    return y.reshape(T, B_HEADS * B_V)


C_PAD = 16
C_ROWS = 128


SUBLANES = 8


def _tap_span(first, ntaps, rows):
    return rows + ((first + ntaps - 1) // SUBLANES) * SUBLANES


def _depthwise_taps(win_ref, sh_ref, w_ref, bias, ls, first, ntaps, rows):
    acc = jnp.broadcast_to(bias, (rows, LANES))
    span = _tap_span(first, ntaps, rows)
    for ph in range(SUBLANES):
        taps = [j for j in range(ntaps) if (first + j) % SUBLANES == ph]
        if not taps:
            continue
        if len(taps) == 1:
            j = taps[0]
            acc = acc + w_ref[j:j + 1, ls] * win_ref[first + j:first + j + rows, ls]
            continue
        sh_ref[0:span, :] = win_ref[ph:ph + span, ls]
        for j in taps:
            a = (first + j) // SUBLANES * SUBLANES
            acc = acc + w_ref[j:j + 1, ls] * sh_ref[a:a + rows, :]
    return acc


def _fill_window(win_ref, load_rows, r0, i, nblk, rows, pad):
    width = win_ref.shape[1]
    win_ref[pad:pad + rows, :] = load_rows(r0, rows)

    @pl.when(i > 0)
    def _():
        win_ref[0:pad, :] = load_rows(r0 - pad, pad)

    @pl.when(i == 0)
    def _():
        win_ref[0:pad, :] = jnp.zeros((pad, width), F32)

    @pl.when(i < nblk - 1)
    def _():
        win_ref[pad + rows:pad + rows + pad, :] = load_rows(r0 + rows, pad)

    @pl.when(i == nblk - 1)
    def _():
        win_ref[pad + rows:pad + rows + pad, :] = jnp.zeros((pad, width), F32)


def _conv_c_kernel(glu_ref, w_ref, b_ref, g_ref, beta_ref, o_ref, win_ref, sh_ref, acc_ref, *, S):
    i = pl.program_id(1)
    r0 = pl.multiple_of(i * C_ROWS, C_ROWS)

    def glu_rows(start, n):
        rs = pl.ds(pl.multiple_of(start, C_PAD), n)
        return glu_ref[rs, 0:C_CH] * _sigmoid(glu_ref[rs, C_CH:2 * C_CH])

    _fill_window(win_ref, glu_rows, r0, i, S // C_ROWS, C_ROWS, C_PAD)
    first = C_PAD - C_KERNEL // 2
    for lb in range(C_CH // LANES):
        ls = slice(lb * LANES, (lb + 1) * LANES)
        acc_ref[:, ls] = _depthwise_taps(win_ref, sh_ref, w_ref, b_ref[:, ls], ls, first, C_KERNEL, C_ROWS)
    y = _layernorm_rows(acc_ref[...], g_ref[...], beta_ref[...])
    o_ref[...] = _silu(y).astype(o_ref.dtype)


def _mixer_c(rest, w_dw, b_dw, ln_g, ln_b, B, S):
    T = B * S
    const = lambda b, i: (0, 0)
    y = pl.pallas_call(
        functools.partial(_conv_c_kernel, S=S),
        grid=(B, S // C_ROWS),
        in_specs=[pl.BlockSpec((None, S, 2 * C_CH), lambda b, i: (b, 0, R_GLU // (2 * C_CH))),
                  pl.BlockSpec((C_KERNEL, C_CH), const),
                  pl.BlockSpec((1, C_CH), const),
                  pl.BlockSpec((1, C_CH), const),
                  pl.BlockSpec((1, C_CH), const)],
        out_specs=pl.BlockSpec((None, C_ROWS, C_CH), lambda b, i: (b, i, 0)),
        out_shape=jax.ShapeDtypeStruct((B, S, C_CH), BF16),
        scratch_shapes=[pltpu.VMEM((C_ROWS + 2 * C_PAD, C_CH), F32),
                        pltpu.VMEM((_tap_span(C_PAD - C_KERNEL // 2, C_KERNEL, C_ROWS), LANES), F32),
                        pltpu.VMEM((C_ROWS, C_CH), F32)],
        compiler_params=_cp("parallel", "parallel"),
        name="conformer_conv",
    )(rest.reshape(B, S, R_WIDTH), w_dw, b_dw.reshape(1, -1), ln_g.reshape(1, -1), ln_b.reshape(1, -1))
    return y.reshape(T, C_CH)


D_PAD = 8
XBC_W = D_INNER + 2 * D_GROUPS * D_STATE
N_PAIR = D_HEADS // 2


def _pair_expand(v, first):
    lane = lax.broadcasted_iota(jnp.int32, (v.shape[0], LANES), 1)
    lo = jnp.broadcast_to(v[:, first:first + 1], (v.shape[0], LANES))
    hi = jnp.broadcast_to(v[:, first + 1:first + 2], (v.shape[0], LANES))
    return jnp.where(lane < D_HEAD_DIM, lo, hi)


def _ssd_kernel(xbc_ref, z_ref, dt_ref, wc_ref, bc_ref, alog_ref, dtb_ref, dskip_ref, gn_ref, o_ref,
                win_ref, sh_ref, xc_ref, a_ref, dtv_ref, y_ref, st_ref, *, S):
    Q = D_CHUNK
    nchunk = S // Q
    N = D_STATE
    bm0 = D_INNER
    cm0 = D_INNER + D_GROUPS * N

    def conv_body(c, carry):
        r0 = pl.multiple_of(c * Q, Q)
        _fill_window(win_ref, lambda st, n: xbc_ref[pl.ds(pl.multiple_of(st, D_PAD), n), :], r0, c, nchunk, Q, D_PAD)
        for lb in range(XBC_W // LANES):
            ls = slice(lb * LANES, (lb + 1) * LANES)
            acc = _depthwise_taps(win_ref, sh_ref, wc_ref, bc_ref[:, ls], ls, D_PAD - D_CONV // 2, D_CONV, Q)
            xc_ref[pl.ds(r0, Q), ls] = _silu(acc)
        return carry

    lax.fori_loop(0, nchunk, conv_body, 0)

    lane1 = lax.broadcasted_iota(jnp.int32, (1, LANES), 1)
    a_row = jnp.where(lane1 < 2 * D_HEADS, -jnp.exp(alog_ref[...]), 0.0)
    xdt = dt_ref[...] + dtb_ref[...]
    dtv = jnp.maximum(xdt, 0.0) + jnp.log(1.0 + jnp.exp(-jnp.abs(xdt)))
    dtv_ref[...] = dtv
    a_ref[...] = dtv * a_row

    row = lax.broadcasted_iota(jnp.int32, (Q, Q), 0)
    col = lax.broadcasted_iota(jnp.int32, (Q, Q), 1)
    tril = row >= col
    triu = col >= row
    lane = col

    def scan_chunk(c, lower, off, finalize):
        r0 = pl.multiple_of(c * Q, Q)
        rows = pl.ds(r0, Q)
        mask = tril if lower else triu
        tri = mask.astype(BF16)
        a_hi, a_mid, a_lo = _split3(a_ref[rows, :])
        cs = _dot(tri, a_hi) + _dot(tri, a_mid) + _dot(tri, a_lo)
        cs_t = cs.T
        ecs = jnp.exp(cs)
        edge = Q - 1 if lower else 0
        edec = jnp.exp(cs[edge:edge + 1, :] - cs)
        dt_c = dtv_ref[rows, :]
        for g in range(D_GROUPS):
            bg = xc_ref[rows, bm0 + g * N:bm0 + (g + 1) * N]
            cg = xc_ref[rows, cm0 + g * N:cm0 + (g + 1) * N].astype(BF16)
            cb = _dot_nt(cg, bg.astype(BF16))
            bg_t = bg.T.astype(BF16)
            for pp in range(N_PAIR // D_GROUPS):
                p = g * (N_PAIR // D_GROUPS) + pp
                ps = slice(p * LANES, (p + 1) * LANES)
                xdt_p = xc_ref[rows, ps] * _pair_expand(dt_c, off + 2 * p)
                ms = []
                for hh in range(2):
                    k = off + 2 * p + hh
                    diff = jnp.broadcast_to(cs[:, k:k + 1], (Q, Q)) - cs_t[k:k + 1, :]
                    ms.append((jnp.exp(jnp.where(mask, diff, NEG_INF)) * cb).astype(BF16))
                x_lo = jnp.where(lane < D_HEAD_DIM, xdt_p, 0.0).astype(BF16)
                x_hi = jnp.where(lane >= D_HEAD_DIM, xdt_p, 0.0).astype(BF16)
                y_intra = _dot(jnp.concatenate(ms, axis=1), jnp.concatenate([x_lo, x_hi], axis=0))
                hp = st_ref[p]
                ecs_p = _pair_expand(ecs, off + 2 * p)
                y_new = y_intra + _dot(cg, hp.astype(BF16)) * ecs_p
                if lower:
                    y_ref[rows, ps] = y_new
                else:
                    y_ref[rows, ps] = y_ref[rows, ps] + y_new
                xs_p = (xdt_p * _pair_expand(edec, off + 2 * p)).astype(BF16)
                st_ref[p] = hp * ecs_p[edge:edge + 1, :] + _dot(bg_t, xs_p)
        if finalize:
            y = y_ref[rows, :] + xc_ref[rows, 0:D_INNER] * dskip_ref[...]
            gated = y * _silu(z_ref[rows, :])
            out = gated * lax.rsqrt(jnp.mean(gated * gated, axis=-1, keepdims=True) + EPS) * gn_ref[...]
            o_ref[rows, :] = out.astype(o_ref.dtype)

    st_ref[...] = jnp.zeros(st_ref.shape, F32)

    def fwd_body(c, carry):
        scan_chunk(c, True, 0, False)
        return carry

    lax.fori_loop(0, nchunk, fwd_body, 0)
    st_ref[...] = jnp.zeros(st_ref.shape, F32)

    def bwd_body(k, carry):
        scan_chunk(nchunk - 1 - k, False, D_HEADS, True)
        return carry

    lax.fori_loop(0, nchunk, bwd_body, 0)


def _mixer_d(rest, w_conv, b_conv, a_log_f, a_log_b, dt_bias_f, dt_bias_b, d_skip, g_norm, B, S):
    T = B * S
    pad16 = lambda f, b: jnp.concatenate([f, b, jnp.zeros((LANES - 2 * D_HEADS,), F32)]).reshape(1, LANES)
    const = lambda b: (0, 0)
    y = pl.pallas_call(
        functools.partial(_ssd_kernel, S=S),
        grid=(B,),
        in_specs=[pl.BlockSpec((None, S, XBC_W), lambda b: (b, 0, R_XBC // XBC_W)),
                  pl.BlockSpec((None, S, D_INNER), lambda b: (b, 0, R_Z // D_INNER)),
                  pl.BlockSpec((None, S, LANES), lambda b: (b, 0, R_DT // LANES)),
                  pl.BlockSpec((D_CONV, XBC_W), const),
                  pl.BlockSpec((1, XBC_W), const),
                  pl.BlockSpec((1, LANES), const),
                  pl.BlockSpec((1, LANES), const),
                  pl.BlockSpec((1, D_INNER), const),
                  pl.BlockSpec((1, D_INNER), const)],
        out_specs=pl.BlockSpec((None, S, D_INNER), lambda b: (b, 0, 0)),
        out_shape=jax.ShapeDtypeStruct((B, S, D_INNER), BF16),
        scratch_shapes=[pltpu.VMEM((D_CHUNK + 2 * D_PAD, XBC_W), F32),
                        pltpu.VMEM((_tap_span(D_PAD - D_CONV // 2, D_CONV, D_CHUNK), LANES), F32),
                        pltpu.VMEM((S, XBC_W), F32),
                        pltpu.VMEM((S, LANES), F32),
                        pltpu.VMEM((S, LANES), F32),
                        pltpu.VMEM((S, D_INNER), F32),
                        pltpu.VMEM((N_PAIR, D_STATE, LANES), F32)],
        compiler_params=_cp("parallel"),
        name="ssd_mixer",
    )(rest.reshape(B, S, R_WIDTH), rest.reshape(B, S, R_WIDTH), rest.reshape(B, S, R_WIDTH),
      w_conv, b_conv.reshape(1, -1), pad16(a_log_f, a_log_b), pad16(dt_bias_f, dt_bias_b),
      jnp.repeat(d_skip, D_HEAD_DIM).reshape(1, -1), g_norm.reshape(1, -1))
    return y.reshape(T, D_INNER)


def _in_proj_weights(w_in_l):
    o = np.cumsum((0, A_WIDTH, A_WIDTH, A_WIDTH, B_Q_LORA, B_KV_LORA, B_ROPE, 2 * C_CH,
                   D_INNER, D_INNER, D_GROUPS * D_STATE, D_GROUPS * D_STATE, 2 * D_HEADS)).tolist()
    seg = lambda n: w_in_l[:, o[n]:o[n + 1]]
    w_a = w_in_l[:, :o[3]]
    cq, ckv, kr, glu, z, xs, bm, cm, dt = (seg(n) for n in range(3, 12))
    zeros = lambda n: jnp.zeros((w_in_l.shape[0], n), w_in_l.dtype)
    w_r = jnp.concatenate([glu, xs, bm, cm, cq, z, ckv,
                           kr, _swap_cols(kr), zeros(LANES - 2 * B_ROPE),
                           dt, zeros(LANES - 2 * D_HEADS)], axis=-1)
    assert w_r.shape[1] == R_WIDTH
    return w_a, w_r


def _merge_gate_kernel(h_ref, ya_ref, yb_ref, yc_ref, yd_ref, wg_ref, bg_ref, wbr_ref, o_ref):
    h = h_ref[...]
    acc = None
    for i, y_ref in enumerate((ya_ref, yb_ref, yc_ref, yd_ref)):
        gate = _sigmoid(_dot(h, wg_ref[i]) + bg_ref[i])
        term = gate * _dot(y_ref[...], wbr_ref[i])
        acc = term if acc is None else acc + term
    o_ref[...] = acc.astype(o_ref.dtype)


def _out_ln_kernel(m_ref, w_ref, h_ref, g_ref, b_ref, of_ref, op_ref):
    y = _layernorm_rows(ALPHA * h_ref[...] + _dot(m_ref[...], w_ref[...]), g_ref[...], b_ref[...])
    of_ref[...] = y
    for c in range(ROW_SUB):
        op_ref[_col_block(y.shape[0], c), :] = _pack_cols(y, c)


def _merge(hb, hf, branches, w_gate, b_gate, w_br, w_out, ln_g, ln_b, tm=512, tn=512, tm2=256):
    T, D = hb.shape
    ybs = pl.BlockSpec((tm, BRANCH_W), lambda j, i: (i, 0))
    merged = pl.pallas_call(
        _merge_gate_kernel,
        grid=(D // tn, T // tm),
        in_specs=[pl.BlockSpec((tm, D), lambda j, i: (i, 0)), ybs, ybs, ybs, ybs,
                  pl.BlockSpec((N_BRANCH, D, tn), lambda j, i: (0, 0, j)),
                  pl.BlockSpec((N_BRANCH, 1, tn), lambda j, i: (0, 0, j)),
                  pl.BlockSpec((N_BRANCH, BRANCH_W, tn), lambda j, i: (0, 0, j))],
        out_specs=pl.BlockSpec((tm, tn), lambda j, i: (i, j)),
        out_shape=jax.ShapeDtypeStruct((T, D), BF16),
        compiler_params=_cp("parallel", "parallel"),
        name="merge_gate",
    )(hb, *branches, w_gate.astype(BF16), b_gate.reshape(N_BRANCH, 1, D), w_br.astype(BF16))
    const = lambda i: (0, 0)
    rows = lambda i: (i, 0)
    return pl.pallas_call(
        _out_ln_kernel,
        grid=(T // tm2,),
        in_specs=[pl.BlockSpec((tm2, D), rows), pl.BlockSpec((D, D), const), pl.BlockSpec((tm2, D), rows),
                  pl.BlockSpec((1, D), const), pl.BlockSpec((1, D), const)],
        out_specs=[pl.BlockSpec((tm2, D), rows), pl.BlockSpec((tm2 * ROW_SUB, LANES), rows)],
        out_shape=[jax.ShapeDtypeStruct((T, D), F32), jax.ShapeDtypeStruct((T * ROW_SUB, LANES), ROW_DT)],
        compiler_params=_cp("parallel"),
        name="out_proj_ln1",
    )(merged, w_out.astype(BF16), hf, ln_g.reshape(1, D), ln_b.reshape(1, D))


R_TM = 512
COMBINE_TB = 256


def _router_kernel(h_ref, whi_ref, wlo_ref, b_ref, eid_ref, wts_ref, rank_ref, cnt_ref, carry_ref):
    i = pl.program_id(0)

    @pl.when(i == 0)
    def _():
        carry_ref[...] = jnp.zeros(carry_ref.shape, F32)

    x = h_ref[...]
    tm = x.shape[0]
    xh = x.astype(BF16)
    xl = (x - xh.astype(F32)).astype(BF16)
    whi = whi_ref[...]
    logits = _dot(xh, whi) + _dot(xh, wlo_ref[...]) + _dot(xl, whi) + b_ref[...]
    lane = lax.broadcasted_iota(jnp.int32, (tm, LANES), 1)
    big = jnp.int32(4 * LANES)
    is_g = (lane >= N_EXPERTS) & (lane < N_EXPERTS + N_GROUPS)
    lg = jnp.where(is_g, logits, NEG_INF)
    gmax = jnp.max(lg, axis=-1, keepdims=True)
    gidx = jnp.min(jnp.where(lg == gmax, lane - N_EXPERTS, big), axis=-1, keepdims=True)
    g_w = 1.0 / jnp.sum(jnp.where(is_g, jnp.exp(lg - gmax), 0.0), axis=-1, keepdims=True)
    in_grp = (lane < N_EXPERTS) & ((lane // EXP_PER_GROUP) == gidx)
    le = jnp.where(in_grp, logits, NEG_INF)
    e1 = jnp.max(le, axis=-1, keepdims=True)
    i1 = jnp.min(jnp.where(le == e1, lane, big), axis=-1, keepdims=True)
    le2 = jnp.where(lane == i1, NEG_INF, le)
    e2 = jnp.max(le2, axis=-1, keepdims=True)
    i2 = jnp.min(jnp.where(le2 == e2, lane, big), axis=-1, keepdims=True)
    zsum = jnp.sum(jnp.where(in_grp, jnp.exp(le - e1), 0.0), axis=-1, keepdims=True)
    p1 = 1.0 / zsum
    p2 = jnp.exp(e2 - e1) / zsum
    w1 = g_w * p1 / (p1 + p2)
    w2 = g_w * p2 / (p1 + p2)
    oh1 = lane == i1
    oh2 = lane == i2
    ohs = (oh1 | oh2).astype(BF16)
    row = lax.broadcasted_iota(jnp.int32, (tm, tm), 0)
    col = lax.broadcasted_iota(jnp.int32, (tm, tm), 1)
    before = _dot((row > col).astype(BF16), ohs) + carry_ref[0:1, :]
    r1 = jnp.sum(jnp.where(oh1, before, 0.0), axis=-1, keepdims=True)
    r2 = jnp.sum(jnp.where(oh2, before, 0.0), axis=-1, keepdims=True)
    total = carry_ref[0:1, :] + jnp.sum(ohs.astype(F32), axis=0, keepdims=True)
    carry_ref[...] = jnp.broadcast_to(total, carry_ref.shape)
    cnt_ref[...] = jnp.broadcast_to(total, cnt_ref.shape).astype(jnp.int32)
    eid_ref[...] = jnp.where(lane == 0, i1, jnp.where(lane == 1, i2, 0))
    wts_ref[...] = jnp.where(lane == 0, w1, jnp.where(lane == 1, w2, 0.0))
    rank_ref[...] = jnp.where(lane == 0, r1, jnp.where(lane == 1, r2, 0.0)).astype(jnp.int32)


ROW_SUB = D_MODEL // LANES // 2
ROW_DT = jnp.uint32


def _row_slab(ref, r):
    return ref.at[pl.ds(pl.multiple_of(r * ROW_SUB, ROW_SUB), ROW_SUB)]


def _col_block(n, c):
    return pl.ds(c, n, stride=ROW_SUB)


def _pack_cols(x, c):
    as_bits = lambda t: lax.bitcast_convert_type(t.astype(BF16).astype(F32), ROW_DT)
    lo = as_bits(x[:, c * LANES:(c + 1) * LANES])
    hi = as_bits(x[:, (c + ROW_SUB) * LANES:(c + ROW_SUB + 1) * LANES])
    return (lo >> 16) | hi


def _unpack_cols(w):
    return (lax.bitcast_convert_type(w << 16, F32),
            lax.bitcast_convert_type(w & jnp.uint32(0xFFFF0000), F32))


EXPERT_AHEAD = 2


def _expert_kernel(be_ref, *refs):
    src_refs = refs[:EXPERT_AHEAD + 1]
    h_hbm, wg_ref, wu_ref, wd_ref, o_ref, xbuf_ref, wgb_ref, wub_ref, wdb_ref, gsem = refs[EXPERT_AHEAD + 1:]
    i = pl.program_id(0)
    n_used = be_ref[pl.num_programs(0)]
    nbuf = EXPERT_AHEAD + 1
    slot = i % nbuf

    @pl.when((i == 0) | (be_ref[i] != be_ref[jnp.maximum(i - 1, 0)]))
    def _():
        wgb_ref[...] = wg_ref[...].astype(BF16)
        wub_ref[...] = wu_ref[...].astype(BF16)
        wdb_ref[...] = wd_ref[...].astype(BF16)

    def gather_copy(tok, to_slot, t):
        return pltpu.make_async_copy(_row_slab(h_hbm, tok), _row_slab(xbuf_ref.at[to_slot], t), gsem.at[to_slot])

    def gather(idx_ref, to_slot):
        for t in range(MOE_BLOCK):
            gather_copy(idx_ref[t], to_slot, t).start(priority=t % 2)

    @pl.when(i == 0)
    def _():
        for a in range(EXPERT_AHEAD):
            gather(src_refs[a], a)

    @pl.when(i + EXPERT_AHEAD < n_used)
    def _():
        gather(src_refs[EXPERT_AHEAD], (i + EXPERT_AHEAD) % nbuf)

    @pl.when(i < n_used)
    def _():
        for t in range(MOE_BLOCK):
            gather_copy(0, slot, 0).wait()
        x_ref = xbuf_ref.at[slot]
        halves = [_unpack_cols(x_ref[_col_block(MOE_BLOCK, c), :]) for c in range(ROW_SUB)]
        x = jnp.concatenate([lo for lo, _ in halves] + [hi for _, hi in halves], axis=1).astype(BF16)
        hid = (_silu(_dot(x, wgb_ref[...])) * _dot(x, wub_ref[...])).astype(BF16)
        y = _dot(hid, wdb_ref[...])
        for c in range(ROW_SUB):
            o_ref[_col_block(MOE_BLOCK, c), :] = _pack_cols(y, c)

    @pl.when(i >= n_used)
    def _():
        o_ref[...] = jnp.zeros(o_ref.shape, o_ref.dtype)


def _combine_kernel(dest_ref, dnext_ref, h_ref, w_ref, g_ref, b_ref, yrows_hbm, of_ref, ob_ref,
                    ybuf_ref, acc_ref, sem):
    i = pl.program_id(0)
    n = pl.num_programs(0)
    slot = i % 2

    def row_copy(d, to_slot, k, t):
        return pltpu.make_async_copy(_row_slab(yrows_hbm, d), _row_slab(ybuf_ref.at[to_slot, k], t), sem.at[to_slot])

    def gather(d_ref, to_slot):
        def start(t, carry):
            for k in range(TOP_K):
                row_copy(d_ref[TOP_K * t + k], to_slot, k, t).start(priority=k)
            return carry

        lax.fori_loop(0, COMBINE_TB, start, 0, unroll=8)

    @pl.when(i == 0)
    def _():
        gather(dest_ref, 0)

    @pl.when(i + 1 < n)
    def _():
        gather(dnext_ref, 1 - slot)

    for t in range(COMBINE_TB):
        for k in range(TOP_K):
            row_copy(0, slot, k, 0).wait()
    w = w_ref[...]
    y0_ref = ybuf_ref.at[slot, 0]
    y1_ref = ybuf_ref.at[slot, 1]
    for c in range(ROW_SUB):
        cb = _col_block(COMBINE_TB, c)
        lo0, hi0 = _unpack_cols(y0_ref[cb, :])
        lo1, hi1 = _unpack_cols(y1_ref[cb, :])
        acc_ref[:, c * LANES:(c + 1) * LANES] = lo0 * w[:, 0:1] + lo1 * w[:, 1:2]
        acc_ref[:, (c + ROW_SUB) * LANES:(c + ROW_SUB + 1) * LANES] = hi0 * w[:, 0:1] + hi1 * w[:, 1:2]
    y = _layernorm_rows(ALPHA * h_ref[...] + acc_ref[...], g_ref[...], b_ref[...])
    of_ref[...] = y
    ob_ref[...] = y.astype(BF16)


def _moe_layer(hf, hp, w_rg, b_rg, w_re, b_re, w_e_gate, w_e_up, w_e_down, layer, ln_g, ln_b):
    T, D = hf.shape
    n_rows = T * TOP_K + N_EXPERTS * MOE_BLOCK
    n_blocks = n_rows // MOE_BLOCK
    w_r = jnp.concatenate([w_re, w_rg, jnp.zeros((D, LANES - N_EXPERTS - N_GROUPS), F32)], axis=-1)
    b_r = jnp.concatenate([b_re, b_rg, jnp.zeros((LANES - N_EXPERTS - N_GROUPS,), F32)]).reshape(1, LANES)
    w_hi = w_r.astype(BF16)
    w_lo = (w_r - w_hi.astype(F32)).astype(BF16)
    const = lambda i: (0, 0)
    rows = lambda i: (i, 0)
    eid, wts, rank, cnt = pl.pallas_call(
        _router_kernel,
        grid=(T // R_TM,),
        in_specs=[pl.BlockSpec((R_TM, D), rows), pl.BlockSpec((D, LANES), const),
                  pl.BlockSpec((D, LANES), const), pl.BlockSpec((1, LANES), const)],
        out_specs=[pl.BlockSpec((R_TM, LANES), rows), pl.BlockSpec((R_TM, LANES), rows),
                   pl.BlockSpec((R_TM, LANES), rows), pl.BlockSpec((8, LANES), const)],
        out_shape=[jax.ShapeDtypeStruct((T, LANES), jnp.int32), jax.ShapeDtypeStruct((T, LANES), F32),
                   jax.ShapeDtypeStruct((T, LANES), jnp.int32), jax.ShapeDtypeStruct((8, LANES), jnp.int32)],
        scratch_shapes=[pltpu.VMEM((8, LANES), F32)],
        compiler_params=_cp("arbitrary"),
        name="moe_router",
    )(hf, w_hi, w_lo, b_r)
    counts = cnt[0, :N_EXPERTS]
    padded = (counts + MOE_BLOCK - 1) // MOE_BLOCK * MOE_BLOCK
    pends = jnp.cumsum(padded)
    pstarts = pends - padded
    blk_start = jnp.arange(n_blocks, dtype=jnp.int32) * MOE_BLOCK
    blk_exp = jnp.minimum(jnp.sum((pends[None, :] <= blk_start[:, None]).astype(jnp.int32), axis=1), N_EXPERTS - 1)
    sel = eid[:, :TOP_K, None] == jnp.arange(N_EXPERTS, dtype=jnp.int32)
    dest = (jnp.sum(jnp.where(sel, pstarts, 0), axis=-1) + rank[:, :TOP_K]).astype(jnp.int32).reshape(T * TOP_K)
    flat = jnp.full((n_rows,), -1, jnp.int32).at[dest].set(jnp.arange(T * TOP_K, dtype=jnp.int32),
                                                           unique_indices=True)
    row_tok = jnp.where(flat < 0, 0, flat // TOP_K)
    assert n_blocks > EXPERT_AHEAD
    any_spec = pl.BlockSpec(memory_space=pl.ANY)
    idx_spec = lambda f: pl.BlockSpec((MOE_BLOCK,), f, memory_space=pltpu.SMEM)
    w_spec = lambda shape: pl.BlockSpec((None, None) + shape, lambda i, be: (layer, be[i], 0, 0))
    yrows = pl.pallas_call(
        _expert_kernel,
        grid_spec=pltpu.PrefetchScalarGridSpec(
            num_scalar_prefetch=1,
            grid=(n_blocks,),
            in_specs=[idx_spec(lambda i, be, a=a: (jnp.minimum(i + a, n_blocks - 1),)) for a in range(EXPERT_AHEAD + 1)]
            + [any_spec, w_spec((D, D_FF)), w_spec((D, D_FF)), w_spec((D_FF, D))],
            out_specs=pl.BlockSpec((MOE_BLOCK * ROW_SUB, LANES), lambda i, be: (i, 0)),
            scratch_shapes=[pltpu.VMEM((EXPERT_AHEAD + 1, MOE_BLOCK * ROW_SUB, LANES), ROW_DT),
                            pltpu.VMEM((D, D_FF), BF16), pltpu.VMEM((D, D_FF), BF16), pltpu.VMEM((D_FF, D), BF16),
                            pltpu.SemaphoreType.DMA((EXPERT_AHEAD + 1,))]),
        out_shape=jax.ShapeDtypeStruct((n_rows * ROW_SUB, LANES), ROW_DT),
        compiler_params=_cp("arbitrary"),
        name="moe_experts",
    )(jnp.concatenate([blk_exp, pends[-1:] // MOE_BLOCK]).astype(jnp.int32),
      *([row_tok] * (EXPERT_AHEAD + 1)), hp, w_e_gate, w_e_up, w_e_down)
    n_steps = T // COMBINE_TB
    dspec = lambda f: pl.BlockSpec((TOP_K * COMBINE_TB,), f, memory_space=pltpu.SMEM)
    return pl.pallas_call(
        _combine_kernel,
        grid=(n_steps,),
        in_specs=[dspec(lambda i: (i,)), dspec(lambda i: (jnp.minimum(i + 1, n_steps - 1),)),
                  pl.BlockSpec((COMBINE_TB, D), rows), pl.BlockSpec((COMBINE_TB, LANES), rows),
                  pl.BlockSpec((1, D), const), pl.BlockSpec((1, D), const), any_spec],
        out_specs=[pl.BlockSpec((COMBINE_TB, D), rows), pl.BlockSpec((COMBINE_TB, D), rows)],
        out_shape=[jax.ShapeDtypeStruct((T, D), F32), jax.ShapeDtypeStruct((T, D), BF16)],
        scratch_shapes=[pltpu.VMEM((2, TOP_K, COMBINE_TB * ROW_SUB, LANES), ROW_DT),
                        pltpu.VMEM((COMBINE_TB, D), F32), pltpu.SemaphoreType.DMA((2,))],
        compiler_params=_cp("arbitrary"),
        name="moe_combine_ln2",
    )(dest, dest, hf, wts, ln_g.reshape(1, D), ln_b.reshape(1, D), yrows)


def kernel(x, ln_in_g, ln_in_b, rel_bias, w_in, g_cq, w_uq, g_ckv, w_ukv, w_dw_c, b_dw_c, ln_c_g, ln_c_b,
           w_conv_d, b_conv_d, a_log_f, a_log_b, dt_bias_f, dt_bias_b, d_skip, g_norm_d, w_br, w_gate, b_gate,
           w_out, ln1_g, ln1_b, w_rg, b_rg, w_re, b_re, w_e_gate, w_e_up, w_e_down, ln2_g, ln2_b):
    B, S, D = x.shape
    T = B * S
    hf, hb = _layernorm(x.reshape(T, D), ln_in_g, ln_in_b)
    a_bias = _mixer_a_bias(rel_bias, S)
    for l in range(DEPTH):
        w_a, w_r = _in_proj_weights(w_in[l])
        qkv = _matmul(hb, w_a.astype(BF16), F32, 512, 3 * A_WIDTH, "in_proj_a")
        rest = _matmul(hb, w_r.astype(BF16), F32, 512, R_WIDTH // 2, "in_proj_rest")
        y_a = _mixer_a(qkv, a_bias, B, S)
        y_b = _mixer_b(rest, g_cq[l], w_uq[l], g_ckv[l], w_ukv[l], B, S)
        y_c = _mixer_c(rest, w_dw_c[l], b_dw_c[l], ln_c_g[l], ln_c_b[l], B, S)
        y_d = _mixer_d(rest, w_conv_d[l], b_conv_d[l], a_log_f[l], a_log_b[l], dt_bias_f[l], dt_bias_b[l],
                       d_skip[l], g_norm_d[l], B, S)
        h1f, h1p = _merge(hb, hf, (y_a, y_b, y_c, y_d), w_gate[l], b_gate[l], w_br[l], w_out[l], ln1_g[l], ln1_b[l])
        hf, hb = _moe_layer(h1f, h1p, w_rg[l], b_rg[l], w_re[l], b_re[l], w_e_gate, w_e_up, w_e_down, l,
                            ln2_g[l], ln2_b[l])
    return hf.reshape(B, S, D)
```

```python
import functools

import numpy as np
import jax
import jax.numpy as jnp
from jax import lax
from jax.experimental import pallas as pl
from jax.experimental.pallas import tpu as pltpu

F32 = jnp.float32
BF16 = jnp.bfloat16

D_MODEL = 2048
DEPTH = 2
A_HEADS = 8
A_HEAD_DIM = 64
A_WIDTH = A_HEADS * A_HEAD_DIM
A_PATTERNS = ((128, 1), (512, 4), (2048, 16))
A_BAND = 64
REL_BUCKETS = 32
REL_MAX_DIST = 1024
B_HEADS = 8
B_NOPE = 64
B_ROPE = 32
B_V = 64
B_Q_LORA = 512
B_KV_LORA = 256
ROPE_THETA = 10000.0
C_CH = 512
C_KERNEL = 31
D_HEADS = 8
D_HEAD_DIM = 64
D_INNER = D_HEADS * D_HEAD_DIM
D_STATE = 128
D_GROUPS = 2
D_CONV = 5
D_CHUNK = 128
N_BRANCH = 4
BRANCH_W = 512
N_GROUPS = 4
EXP_PER_GROUP = 8
N_EXPERTS = N_GROUPS * EXP_PER_GROUP
TOP_K = 2
D_FF = 512
MOE_BLOCK = 128
ALPHA = (2 * DEPTH) ** 0.25
EPS = 1e-5
NEG_INF = -1e30

LANES = 128
R_GLU, R_XBC, R_CQ, R_Z, R_CKV, R_KR, R_DT = 0, 1024, 2048, 2560, 3072, 3328, 3456
R_WIDTH = 3584
VMEM_LIMIT = 56 * 1024 * 1024


def _cp(*sem):
    return pltpu.CompilerParams(dimension_semantics=sem, vmem_limit_bytes=VMEM_LIMIT)


def _dot(a, b):
    return jnp.dot(a, b, preferred_element_type=F32)


def _dot_nt(a, b):
    return lax.dot_general(a, b, (((1,), (1,)), ((), ())), preferred_element_type=F32)


def _split3(x):
    hi = x.astype(BF16)
    r1 = x - hi.astype(F32)
    mid = r1.astype(BF16)
    lo = (r1 - mid.astype(F32)).astype(BF16)
    return hi, mid, lo


def _layernorm_rows(x, g, b):
    mu = jnp.mean(x, axis=-1, keepdims=True)
    xc = x - mu
    var = jnp.mean(xc * xc, axis=-1, keepdims=True)
    return xc * lax.rsqrt(var + EPS) * g + b


def _sigmoid(x):
    return 1.0 / (1.0 + jnp.exp(-x))


def _silu(x):
    return x * _sigmoid(x)


def _ln_kernel(x_ref, g_ref, b_ref, of_ref, ob_ref):
    y = _layernorm_rows(x_ref[...], g_ref[...], b_ref[...])
    of_ref[...] = y
    ob_ref[...] = y.astype(BF16)


def _layernorm(x, g, b, tm=256):
    T, D = x.shape
    return pl.pallas_call(
        _ln_kernel,
        grid=(T // tm,),
        in_specs=[pl.BlockSpec((tm, D), lambda i: (i, 0)),
                  pl.BlockSpec((1, D), lambda i: (0, 0)),
                  pl.BlockSpec((1, D), lambda i: (0, 0))],
        out_specs=[pl.BlockSpec((tm, D), lambda i: (i, 0)),
                   pl.BlockSpec((tm, D), lambda i: (i, 0))],
        out_shape=[jax.ShapeDtypeStruct((T, D), F32), jax.ShapeDtypeStruct((T, D), BF16)],
        compiler_params=_cp("parallel"),
        name="ln_in",
    )(x, g.reshape(1, D), b.reshape(1, D))


def _mm_kernel(x_ref, w_ref, o_ref):
    o_ref[...] = _dot(x_ref[...], w_ref[...]).astype(o_ref.dtype)


def _matmul(x, w, out_dtype, tm, tn, name):
    M, K = x.shape
    N = w.shape[1]
    return pl.pallas_call(
        _mm_kernel,
        grid=(N // tn, M // tm),
        in_specs=[pl.BlockSpec((tm, K), lambda j, i: (i, 0)),
                  pl.BlockSpec((K, tn), lambda j, i: (0, j))],
        out_specs=pl.BlockSpec((tm, tn), lambda j, i: (i, j)),
        out_shape=jax.ShapeDtypeStruct((M, N), out_dtype),
        compiler_params=_cp("parallel", "parallel"),
        name=name,
    )(x, w)


def _t5_bucket(rel):
    half = REL_BUCKETS // 2
    max_exact = half // 2
    n = np.abs(rel)
    large = max_exact + (np.log(np.maximum(n, 1) / max_exact) / np.log(REL_MAX_DIST / max_exact)
                         * (half - max_exact)).astype(np.int32)
    large = np.minimum(large, half - 1)
    return (rel > 0).astype(np.int32) * half + np.where(n < max_exact, n, large)


def _a_window(L):
    return min(2 * LANES, L)


def _a_bias_tiles(rel_bias, d, L):
    W = _a_window(L)
    offs = (0,) if L == LANES else (0, -A_BAND, -2 * A_BAND)
    qi = np.arange(LANES)[:, None]
    kj = np.arange(W)[None, :]
    rel = np.stack([kj - qi + off for off in offs], axis=0)
    valid = np.abs(rel) <= A_BAND
    onehot = (jnp.asarray(_t5_bucket(rel * d), jnp.int32)[..., None] == jnp.arange(REL_BUCKETS)).astype(F32)
    b = jnp.einsum('vqkb,bh->vhqk', onehot, rel_bias.astype(F32), precision=lax.Precision.HIGHEST)
    b = jnp.where(valid[:, None], b, NEG_INF)
    return b.reshape(len(offs), A_HEADS // 2, 2 * LANES, W)


A_GROUP = 2


def _attn_a_kernel(q_ref, k_ref, v_ref, b16_ref, b4_ref, b1_ref, y_ref,
                   q4_ref, k4_ref, v4_ref, m_ref, l_ref, acc_ref, tmp_ref, *, S):
    (_, d1), (_, d4), (_, d16) = A_PATTERNS
    lane = lax.broadcasted_iota(jnp.int32, (LANES, LANES), 1)
    head0 = lane < A_HEAD_DIM
    scale = A_HEAD_DIM ** -0.5

    def partial_softmax(qs, ks, vs, bias_ref):
        q2 = jnp.concatenate([jnp.where(head0, qs, 0.0), jnp.where(head0, 0.0, qs)], axis=0).astype(BF16)
        s = _dot_nt(q2, ks.astype(BF16)) * scale + bias_ref[...]
        m = jnp.max(s, axis=-1, keepdims=True)
        p = jnp.exp(s - m).astype(BF16)
        num = _dot(p, vs.astype(BF16))
        den = _dot(p, jnp.ones((vs.shape[0], LANES), BF16))
        both = lambda t: jnp.where(head0, t[:LANES], t[LANES:])
        return both(jnp.broadcast_to(m, (2 * LANES, LANES))), both(den), both(num)

    def fold(old, new):
        (m_old, l_old, a_old), (m_new, l_new, a_new) = old, new
        m = jnp.maximum(m_old, m_new)
        c_old = jnp.exp(m_old - m)
        c_new = jnp.exp(m_new - m)
        return m, c_old * l_old + c_new * l_new, c_old * a_old + c_new * a_new

    stat_refs = (m_ref, l_ref, acc_ref)

    def get(c, rows):
        return tuple(ref.at[c][rows, :] for ref in stat_refs)

    def put(c, rows, stats):
        for ref, val in zip(stat_refs, stats):
            ref.at[c][rows, :] = val

    def grouped(n, group, unit):
        def trip(g, carry):
            pending = [unit(g * group + u) for u in range(group)]
            for finish in pending:
                finish()
            return carry

        lax.fori_loop(0, n // group, trip, 0)

    L4 = S // d4
    sub = d16 // d4
    assert S // d16 == LANES and sub == d4
    for c in range(d4):
        cls = pl.ds(c, L4, stride=d4)
        q4_ref[c] = q_ref[cls, :]
        k4_ref[c] = k_ref[cls, :]
        v4_ref[c] = v_ref[cls, :]

    def unit16(t):
        c = t % d4
        rows = pl.ds(t // d4, LANES, stride=sub)
        stats = partial_softmax(q4_ref.at[c][rows, :], k4_ref.at[c][rows, :], v4_ref.at[c][rows, :], b16_ref.at[0])
        return lambda: put(c, rows, stats)

    grouped(d16, A_GROUP, unit16)

    def window(i, L):
        nqb = L // LANES
        ws = pl.multiple_of(jnp.clip(i * LANES - A_BAND, 0, L - 2 * LANES), A_BAND)
        return pl.ds(ws, 2 * LANES), jnp.where(i == 0, 0, jnp.where(i == nqb - 1, 2, 1))

    def unit4(t):
        c = t % d4
        i = t // d4
        keys, var = window(i, L4)
        rows = pl.ds(pl.multiple_of(i * LANES, LANES), LANES)
        stats = partial_softmax(q4_ref.at[c][rows, :], k4_ref.at[c][keys, :], v4_ref.at[c][keys, :], b4_ref.at[var])
        return lambda: put(c, rows, fold(get(c, rows), stats))

    grouped(d4 * (L4 // LANES), A_GROUP, unit4)

    def unit1(i):
        keys, var = window(i, S)
        rows = pl.ds(pl.multiple_of(i * LANES, LANES), LANES)
        stats = partial_softmax(q_ref[rows, :], k_ref[keys, :], v_ref[keys, :], b1_ref.at[var])

        def finish():
            part = pl.ds(pl.multiple_of(i * (LANES // d4), LANES // d4), LANES // d4)
            for n, ref in enumerate(stat_refs):
                for c in range(d4):
                    tmp_ref.at[n][pl.ds(c, LANES // d4, stride=d4), :] = ref.at[c][part, :]
            _, l, a = fold(tuple(tmp_ref[n] for n in range(len(stat_refs))), stats)
            y_ref[rows, :] = (a / l).astype(y_ref.dtype)

        return finish

    grouped(S // LANES, A_GROUP, unit1)


def _mixer_a_bias(rel_bias, S):
    return tuple(_a_bias_tiles(rel_bias, d, S // d) for _, d in A_PATTERNS)


def _mixer_a(qkv, bias, B, S):
    b1, b4, b16 = bias
    npair = A_HEADS // 2
    pair_bias = lambda t: pl.BlockSpec((t.shape[0], None) + t.shape[2:], lambda b, hp: (0, hp, 0, 0))
    slab = lambda first: pl.BlockSpec((S, LANES), lambda b, hp: (b, first + hp))
    return pl.pallas_call(
        functools.partial(_attn_a_kernel, S=S),
        grid=(B, npair),
        in_specs=[slab(0), slab(npair), slab(2 * npair), pair_bias(b16), pair_bias(b4), pair_bias(b1)],
        out_specs=pl.BlockSpec((S, LANES), lambda b, hp: (b, hp)),
        out_shape=jax.ShapeDtypeStruct((B * S, A_WIDTH), BF16),
        scratch_shapes=[pltpu.VMEM((A_PATTERNS[1][1], S // A_PATTERNS[1][1], LANES), F32)] * 6
        + [pltpu.VMEM((3, LANES, LANES), F32)],
        compiler_params=_cp("parallel", "parallel"),
        name="attn_a",
    )(qkv, qkv, qkv, b16, b4, b1)


MLA_Q_SCALE = float((B_NOPE + B_ROPE) ** -0.5 * np.log2(np.e))


def _mla_proj_kernel(cq_ref, ckv_ref, kr_ref, gq_ref, gkv_ref, wqm_ref, wqs_ref, wk_ref, wv_ref, vone_ref,
                     ek_ref, cosq_ref, sinq_ref, csk_ref, q_ref, k_ref, v_ref):
    cq = cq_ref[...]
    xq = (cq * lax.rsqrt(jnp.mean(cq * cq, axis=-1, keepdims=True) + EPS) * gq_ref[...]).astype(BF16)
    ckv = ckv_ref[...]
    xkv = (ckv * lax.rsqrt(jnp.mean(ckv * ckv, axis=-1, keepdims=True) + EPS) * gkv_ref[...]).astype(BF16)
    qm = _dot(xq, wqm_ref[...])
    qs = _dot(xq, wqs_ref[...])
    cosq = cosq_ref[...] * MLA_Q_SCALE
    sinq = sinq_ref[...] * MLA_Q_SCALE
    t = kr_ref[...] * csk_ref[...]
    t_hi = t.astype(BF16)
    t_lo = (t - t_hi.astype(F32)).astype(BF16)
    kk = _dot(xkv, wk_ref[...]) + _dot(t_hi, ek_ref[...]) + _dot(t_lo, ek_ref[...])
    for h in range(B_HEADS):
        sl = slice(h * LANES, (h + 1) * LANES)
        q_ref[:, sl] = (qm[:, sl] * cosq + qs[:, sl] * sinq).astype(BF16)
    k_ref[...] = kk.astype(BF16)
    v_ref[...] = (_dot(xkv, wv_ref[...]) + vone_ref[...]).astype(BF16)


def _mla_attn_kernel(q_ref, k_ref, v_ref, o_ref):
    outs = []
    for hh in range(2):
        sl = slice(hh * LANES, (hh + 1) * LANES)
        s = _dot_nt(q_ref[:, sl], k_ref[:, sl])
        p = jnp.exp2(s - jnp.max(s, axis=-1, keepdims=True))
        outs.append(_dot(p.astype(BF16), v_ref[:, sl]))
    lane = lax.broadcasted_iota(jnp.int32, outs[0].shape, 1)
    acc = jnp.where(lane < B_V, outs[0], outs[1])
    den = pltpu.roll(jnp.where(lane < B_V, outs[1], outs[0]), B_V, axis=1)
    o_ref[...] = (acc / den).astype(o_ref.dtype)


def _mla_tables(S):
    inv_freq = ROPE_THETA ** (-jnp.arange(0, B_ROPE, 2, dtype=F32) / B_ROPE)
    ang = jnp.arange(S, dtype=F32)[:, None] * inv_freq[None]
    cos, sin = jnp.cos(ang), jnp.sin(ang)
    cos2 = jnp.concatenate([cos, cos], axis=-1)
    sin2 = jnp.concatenate([sin, sin], axis=-1)
    ones = jnp.ones((S, B_NOPE), F32)
    zn = jnp.zeros((S, B_NOPE), F32)
    zp = jnp.zeros((S, LANES - B_NOPE - B_ROPE), F32)
    cosq = jnp.concatenate([ones, cos2, zp], axis=-1)
    sinq = jnp.concatenate([zn, sin2, zp], axis=-1)
    csk = jnp.concatenate([cos2, sin2, jnp.zeros((S, LANES - 2 * B_ROPE), F32)], axis=-1)
    return cosq, sinq, csk


def _swap_cols(w):
    half = w.shape[-1] // 2
    return jnp.concatenate([-w[..., half:], w[..., :half]], axis=-1)


def _mla_weights(w_uq, w_ukv):
    dq = B_NOPE + B_ROPE
    wq = w_uq.reshape(B_Q_LORA, B_HEADS, dq)
    zpad = jnp.zeros((B_Q_LORA, B_HEADS, LANES - dq), F32)
    wqm = jnp.concatenate([wq, zpad], axis=-1).reshape(B_Q_LORA, B_HEADS * LANES)
    wqs = jnp.concatenate([jnp.zeros((B_Q_LORA, B_HEADS, B_NOPE), F32), _swap_cols(wq[..., B_NOPE:]), zpad],
                          axis=-1).reshape(B_Q_LORA, B_HEADS * LANES)
    wkv = w_ukv.reshape(B_KV_LORA, B_HEADS, B_NOPE + B_V)
    wk = jnp.concatenate([wkv[..., :B_NOPE], jnp.zeros((B_KV_LORA, B_HEADS, LANES - B_NOPE), F32)],
                         axis=-1).reshape(B_KV_LORA, B_HEADS * LANES)
    zv = jnp.zeros((B_KV_LORA, B_HEADS // 2, LANES - B_V), F32)
    wv_h = wkv[..., B_NOPE:]
    wv = jnp.stack([jnp.concatenate([wv_h[:, 0::2], zv], axis=-1),
                    jnp.concatenate([zv, wv_h[:, 1::2]], axis=-1)], axis=2).reshape(B_KV_LORA, B_HEADS * LANES)
    lane_in_pair = np.arange(B_HEADS * LANES) % (2 * LANES)
    vone = jnp.asarray(((lane_in_pair >= B_V) & (lane_in_pair < LANES + B_V)).astype(np.float32)).reshape(1, -1)
    ek = np.zeros((LANES, B_HEADS, LANES), np.float32)
    for j in range(B_ROPE):
        ek[j, :, B_NOPE + j] = 1.0
        ek[B_ROPE + j, :, B_NOPE + j] = 1.0
    ek = jnp.asarray(ek.reshape(LANES, B_HEADS * LANES))
    return wqm.astype(BF16), wqs.astype(BF16), wk.astype(BF16), wv.astype(BF16), vone, ek.astype(BF16)


def _mixer_b(rest, g_cq, w_uq, g_ckv, w_ukv, B, S, tm=512, tq=256):
    T = B * S
    wqm, wqs, wk, wv, vone, ek = _mla_weights(w_uq, w_ukv)
    cosq, sinq, csk = _mla_tables(S)
    nst = S // tm
    QW = B_HEADS * LANES
    const = lambda i: (0, 0)
    pos = lambda i: (i % nst, 0)
    q, k, v = pl.pallas_call(
        _mla_proj_kernel,
        grid=(T // tm,),
        in_specs=[pl.BlockSpec((tm, B_Q_LORA), lambda i: (i, R_CQ // B_Q_LORA)),
                  pl.BlockSpec((tm, B_KV_LORA), lambda i: (i, R_CKV // B_KV_LORA)),
                  pl.BlockSpec((tm, LANES), lambda i: (i, R_KR // LANES)),
                  pl.BlockSpec((1, B_Q_LORA), const),
                  pl.BlockSpec((1, B_KV_LORA), const),
                  pl.BlockSpec((B_Q_LORA, QW), const),
                  pl.BlockSpec((B_Q_LORA, QW), const),
                  pl.BlockSpec((B_KV_LORA, QW), const),
                  pl.BlockSpec((B_KV_LORA, QW), const),
                  pl.BlockSpec((1, QW), const),
                  pl.BlockSpec((LANES, QW), const),
                  pl.BlockSpec((tm, LANES), pos),
                  pl.BlockSpec((tm, LANES), pos),
                  pl.BlockSpec((tm, LANES), pos)],
        out_specs=[pl.BlockSpec((tm, QW), lambda i: (i, 0)),
                   pl.BlockSpec((tm, QW), lambda i: (i, 0)),
                   pl.BlockSpec((tm, QW), lambda i: (i, 0))],
        out_shape=[jax.ShapeDtypeStruct((T, QW), BF16), jax.ShapeDtypeStruct((T, QW), BF16),
                   jax.ShapeDtypeStruct((T, QW), BF16)],
        compiler_params=_cp("parallel"),
        name="mla_proj",
    )(rest, rest, rest, g_cq.reshape(1, -1), g_ckv.reshape(1, -1), wqm, wqs, wk, wv, vone, ek, cosq, sinq, csk)
    y = pl.pallas_call(
        _mla_attn_kernel,
        grid=(B, B_HEADS // 2, S // tq),
        in_specs=[pl.BlockSpec((None, tq, 2 * LANES), lambda b, hp, i: (b, i, hp)),
                  pl.BlockSpec((None, S, 2 * LANES), lambda b, hp, i: (b, 0, hp)),
                  pl.BlockSpec((None, S, 2 * LANES), lambda b, hp, i: (b, 0, hp))],
        out_specs=pl.BlockSpec((None, tq, 2 * B_V), lambda b, hp, i: (b, i, hp)),
        out_shape=jax.ShapeDtypeStruct((B, S, B_HEADS * B_V), BF16),
        compiler_params=_cp("parallel", "parallel", "arbitrary"),
        name="mla_attn",
    )(q.reshape(B, S, QW), k.reshape(B, S, QW), v.reshape(B, S, QW))
    return y.reshape(T, B_HEADS * B_V)


C_PAD = 16
C_ROWS = 128


SUBLANES = 8


def _tap_span(first, ntaps, rows):
    return rows + ((first + ntaps - 1) // SUBLANES) * SUBLANES


def _depthwise_taps(win_ref, sh_ref, w_ref, bias, ls, first, ntaps, rows):
    acc = jnp.broadcast_to(bias, (rows, LANES))
    span = _tap_span(first, ntaps, rows)
    for ph in range(SUBLANES):
        taps = [j for j in range(ntaps) if (first + j) % SUBLANES == ph]
        if not taps:
            continue
        if len(taps) == 1:
            j = taps[0]
            acc = acc + w_ref[j:j + 1, ls] * win_ref[first + j:first + j + rows, ls]
            continue
        sh_ref[0:span, :] = win_ref[ph:ph + span, ls]
        for j in taps:
            a = (first + j) // SUBLANES * SUBLANES
            acc = acc + w_ref[j:j + 1, ls] * sh_ref[a:a + rows, :]
    return acc


def _fill_window(win_ref, load_rows, r0, i, nblk, rows, pad):
    width = win_ref.shape[1]
    win_ref[pad:pad + rows, :] = load_rows(r0, rows)

    @pl.when(i > 0)
    def _():
        win_ref[0:pad, :] = load_rows(r0 - pad, pad)

    @pl.when(i == 0)
    def _():
        win_ref[0:pad, :] = jnp.zeros((pad, width), F32)

    @pl.when(i < nblk - 1)
    def _():
        win_ref[pad + rows:pad + rows + pad, :] = load_rows(r0 + rows, pad)

    @pl.when(i == nblk - 1)
    def _():
        win_ref[pad + rows:pad + rows + pad, :] = jnp.zeros((pad, width), F32)


def _conv_c_kernel(glu_ref, w_ref, b_ref, g_ref, beta_ref, o_ref, win_ref, sh_ref, acc_ref, *, S):
    i = pl.program_id(1)
    r0 = pl.multiple_of(i * C_ROWS, C_ROWS)

    def glu_rows(start, n):
        rs = pl.ds(pl.multiple_of(start, C_PAD), n)
        return glu_ref[rs, 0:C_CH] * _sigmoid(glu_ref[rs, C_CH:2 * C_CH])

    _fill_window(win_ref, glu_rows, r0, i, S // C_ROWS, C_ROWS, C_PAD)
    first = C_PAD - C_KERNEL // 2
    for lb in range(C_CH // LANES):
        ls = slice(lb * LANES, (lb + 1) * LANES)
        acc_ref[:, ls] = _depthwise_taps(win_ref, sh_ref, w_ref, b_ref[:, ls], ls, first, C_KERNEL, C_ROWS)
    y = _layernorm_rows(acc_ref[...], g_ref[...], beta_ref[...])
    o_ref[...] = _silu(y).astype(o_ref.dtype)


def _mixer_c(rest, w_dw, b_dw, ln_g, ln_b, B, S):
    T = B * S
    const = lambda b, i: (0, 0)
    y = pl.pallas_call(
        functools.partial(_conv_c_kernel, S=S),
        grid=(B, S // C_ROWS),
        in_specs=[pl.BlockSpec((None, S, 2 * C_CH), lambda b, i: (b, 0, R_GLU // (2 * C_CH))),
                  pl.BlockSpec((C_KERNEL, C_CH), const),
                  pl.BlockSpec((1, C_CH), const),
                  pl.BlockSpec((1, C_CH), const),
                  pl.BlockSpec((1, C_CH), const)],
        out_specs=pl.BlockSpec((None, C_ROWS, C_CH), lambda b, i: (b, i, 0)),
        out_shape=jax.ShapeDtypeStruct((B, S, C_CH), BF16),
        scratch_shapes=[pltpu.VMEM((C_ROWS + 2 * C_PAD, C_CH), F32),
                        pltpu.VMEM((_tap_span(C_PAD - C_KERNEL // 2, C_KERNEL, C_ROWS), LANES), F32),
                        pltpu.VMEM((C_ROWS, C_CH), F32)],
        compiler_params=_cp("parallel", "parallel"),
        name="conformer_conv",
    )(rest.reshape(B, S, R_WIDTH), w_dw, b_dw.reshape(1, -1), ln_g.reshape(1, -1), ln_b.reshape(1, -1))
    return y.reshape(T, C_CH)


D_PAD = 8
XBC_W = D_INNER + 2 * D_GROUPS * D_STATE
N_PAIR = D_HEADS // 2


def _pair_expand(v, first):
    lane = lax.broadcasted_iota(jnp.int32, (v.shape[0], LANES), 1)
    lo = jnp.broadcast_to(v[:, first:first + 1], (v.shape[0], LANES))
    hi = jnp.broadcast_to(v[:, first + 1:first + 2], (v.shape[0], LANES))
    return jnp.where(lane < D_HEAD_DIM, lo, hi)


def _ssd_kernel(xbc_ref, z_ref, dt_ref, wc_ref, bc_ref, alog_ref, dtb_ref, dskip_ref, gn_ref, o_ref,
                win_ref, sh_ref, xc_ref, a_ref, dtv_ref, y_ref, st_ref, *, S):
    Q = D_CHUNK
    nchunk = S // Q
    N = D_STATE
    bm0 = D_INNER
    cm0 = D_INNER + D_GROUPS * N

    def conv_body(c, carry):
        r0 = pl.multiple_of(c * Q, Q)
        _fill_window(win_ref, lambda st, n: xbc_ref[pl.ds(pl.multiple_of(st, D_PAD), n), :], r0, c, nchunk, Q, D_PAD)
        for lb in range(XBC_W // LANES):
            ls = slice(lb * LANES, (lb + 1) * LANES)
            acc = _depthwise_taps(win_ref, sh_ref, wc_ref, bc_ref[:, ls], ls, D_PAD - D_CONV // 2, D_CONV, Q)
            xc_ref[pl.ds(r0, Q), ls] = _silu(acc)
        return carry

    lax.fori_loop(0, nchunk, conv_body, 0)

    lane1 = lax.broadcasted_iota(jnp.int32, (1, LANES), 1)
    a_row = jnp.where(lane1 < 2 * D_HEADS, -jnp.exp(alog_ref[...]), 0.0)
    xdt = dt_ref[...] + dtb_ref[...]
    dtv = jnp.maximum(xdt, 0.0) + jnp.log(1.0 + jnp.exp(-jnp.abs(xdt)))
    dtv_ref[...] = dtv
    a_ref[...] = dtv * a_row

    row = lax.broadcasted_iota(jnp.int32, (Q, Q), 0)
    col = lax.broadcasted_iota(jnp.int32, (Q, Q), 1)
    tril = row >= col
    triu = col >= row
    lane = col

    def scan_chunk(c, lower, off, finalize):
        r0 = pl.multiple_of(c * Q, Q)
        rows = pl.ds(r0, Q)
        mask = tril if lower else triu
        tri = mask.astype(BF16)
        a_hi, a_mid, a_lo = _split3(a_ref[rows, :])
        cs = _dot(tri, a_hi) + _dot(tri, a_mid) + _dot(tri, a_lo)
        cs_t = cs.T
        ecs = jnp.exp(cs)
        edge = Q - 1 if lower else 0
        edec = jnp.exp(cs[edge:edge + 1, :] - cs)
        dt_c = dtv_ref[rows, :]
        for g in range(D_GROUPS):
            bg = xc_ref[rows, bm0 + g * N:bm0 + (g + 1) * N]
            cg = xc_ref[rows, cm0 + g * N:cm0 + (g + 1) * N].astype(BF16)
            cb = _dot_nt(cg, bg.astype(BF16))
            bg_t = bg.T.astype(BF16)
            for pp in range(N_PAIR // D_GROUPS):
                p = g * (N_PAIR // D_GROUPS) + pp
                ps = slice(p * LANES, (p + 1) * LANES)
                xdt_p = xc_ref[rows, ps] * _pair_expand(dt_c, off + 2 * p)
                ms = []
                for hh in range(2):
                    k = off + 2 * p + hh
                    diff = jnp.broadcast_to(cs[:, k:k + 1], (Q, Q)) - cs_t[k:k + 1, :]
                    ms.append((jnp.exp(jnp.where(mask, diff, NEG_INF)) * cb).astype(BF16))
                x_lo = jnp.where(lane < D_HEAD_DIM, xdt_p, 0.0).astype(BF16)
                x_hi = jnp.where(lane >= D_HEAD_DIM, xdt_p, 0.0).astype(BF16)
                y_intra = _dot(jnp.concatenate(ms, axis=1), jnp.concatenate([x_lo, x_hi], axis=0))
                hp = st_ref[p]
                ecs_p = _pair_expand(ecs, off + 2 * p)
                y_new = y_intra + _dot(cg, hp.astype(BF16)) * ecs_p
                if lower:
                    y_ref[rows, ps] = y_new
                else:
                    y_ref[rows, ps] = y_ref[rows, ps] + y_new
                xs_p = (xdt_p * _pair_expand(edec, off + 2 * p)).astype(BF16)
                st_ref[p] = hp * ecs_p[edge:edge + 1, :] + _dot(bg_t, xs_p)
        if finalize:
            y = y_ref[rows, :] + xc_ref[rows, 0:D_INNER] * dskip_ref[...]
            gated = y * _silu(z_ref[rows, :])
            out = gated * lax.rsqrt(jnp.mean(gated * gated, axis=-1, keepdims=True) + EPS) * gn_ref[...]
            o_ref[rows, :] = out.astype(o_ref.dtype)

    st_ref[...] = jnp.zeros(st_ref.shape, F32)

    def fwd_body(c, carry):
        scan_chunk(c, True, 0, False)
        return carry

    lax.fori_loop(0, nchunk, fwd_body, 0)
    st_ref[...] = jnp.zeros(st_ref.shape, F32)

    def bwd_body(k, carry):
        scan_chunk(nchunk - 1 - k, False, D_HEADS, True)
        return carry

    lax.fori_loop(0, nchunk, bwd_body, 0)


def _mixer_d(rest, w_conv, b_conv, a_log_f, a_log_b, dt_bias_f, dt_bias_b, d_skip, g_norm, B, S):
    T = B * S
    pad16 = lambda f, b: jnp.concatenate([f, b, jnp.zeros((LANES - 2 * D_HEADS,), F32)]).reshape(1, LANES)
    const = lambda b: (0, 0)
    y = pl.pallas_call(
        functools.partial(_ssd_kernel, S=S),
        grid=(B,),
        in_specs=[pl.BlockSpec((None, S, XBC_W), lambda b: (b, 0, R_XBC // XBC_W)),
                  pl.BlockSpec((None, S, D_INNER), lambda b: (b, 0, R_Z // D_INNER)),
                  pl.BlockSpec((None, S, LANES), lambda b: (b, 0, R_DT // LANES)),
                  pl.BlockSpec((D_CONV, XBC_W), const),
                  pl.BlockSpec((1, XBC_W), const),
                  pl.BlockSpec((1, LANES), const),
                  pl.BlockSpec((1, LANES), const),
                  pl.BlockSpec((1, D_INNER), const),
                  pl.BlockSpec((1, D_INNER), const)],
        out_specs=pl.BlockSpec((None, S, D_INNER), lambda b: (b, 0, 0)),
        out_shape=jax.ShapeDtypeStruct((B, S, D_INNER), BF16),
        scratch_shapes=[pltpu.VMEM((D_CHUNK + 2 * D_PAD, XBC_W), F32),
                        pltpu.VMEM((_tap_span(D_PAD - D_CONV // 2, D_CONV, D_CHUNK), LANES), F32),
                        pltpu.VMEM((S, XBC_W), F32),
                        pltpu.VMEM((S, LANES), F32),
                        pltpu.VMEM((S, LANES), F32),
                        pltpu.VMEM((S, D_INNER), F32),
                        pltpu.VMEM((N_PAIR, D_STATE, LANES), F32)],
        compiler_params=_cp("parallel"),
        name="ssd_mixer",
    )(rest.reshape(B, S, R_WIDTH), rest.reshape(B, S, R_WIDTH), rest.reshape(B, S, R_WIDTH),
      w_conv, b_conv.reshape(1, -1), pad16(a_log_f, a_log_b), pad16(dt_bias_f, dt_bias_b),
      jnp.repeat(d_skip, D_HEAD_DIM).reshape(1, -1), g_norm.reshape(1, -1))
    return y.reshape(T, D_INNER)


def _in_proj_weights(w_in_l):
    o = np.cumsum((0, A_WIDTH, A_WIDTH, A_WIDTH, B_Q_LORA, B_KV_LORA, B_ROPE, 2 * C_CH,
                   D_INNER, D_INNER, D_GROUPS * D_STATE, D_GROUPS * D_STATE, 2 * D_HEADS)).tolist()
    seg = lambda n: w_in_l[:, o[n]:o[n + 1]]
    w_a = w_in_l[:, :o[3]]
    cq, ckv, kr, glu, z, xs, bm, cm, dt = (seg(n) for n in range(3, 12))
    zeros = lambda n: jnp.zeros((w_in_l.shape[0], n), w_in_l.dtype)
    w_r = jnp.concatenate([glu, xs, bm, cm, cq, z, ckv,
                           kr, _swap_cols(kr), zeros(LANES - 2 * B_ROPE),
                           dt, zeros(LANES - 2 * D_HEADS)], axis=-1)
    assert w_r.shape[1] == R_WIDTH
    return w_a, w_r


def _merge_gate_kernel(h_ref, ya_ref, yb_ref, yc_ref, yd_ref, wg_ref, bg_ref, wbr_ref, o_ref):
    h = h_ref[...]
    acc = None
    for i, y_ref in enumerate((ya_ref, yb_ref, yc_ref, yd_ref)):
        gate = _sigmoid(_dot(h, wg_ref[i]) + bg_ref[i])
        term = gate * _dot(y_ref[...], wbr_ref[i])
        acc = term if acc is None else acc + term
    o_ref[...] = acc.astype(o_ref.dtype)


def _out_ln_kernel(m_ref, w_ref, h_ref, g_ref, b_ref, of_ref, op_ref):
    y = _layernorm_rows(ALPHA * h_ref[...] + _dot(m_ref[...], w_ref[...]), g_ref[...], b_ref[...])
    of_ref[...] = y
    for c in range(ROW_SUB):
        op_ref[_col_block(y.shape[0], c), :] = _pack_cols(y, c)


def _merge(hb, hf, branches, w_gate, b_gate, w_br, w_out, ln_g, ln_b, tm=512, tn=512, tm2=512):
    T, D = hb.shape
    ybs = pl.BlockSpec((tm, BRANCH_W), lambda j, i: (i, 0))
    merged = pl.pallas_call(
        _merge_gate_kernel,
        grid=(D // tn, T // tm),
        in_specs=[pl.BlockSpec((tm, D), lambda j, i: (i, 0)), ybs, ybs, ybs, ybs,
                  pl.BlockSpec((N_BRANCH, D, tn), lambda j, i: (0, 0, j)),
                  pl.BlockSpec((N_BRANCH, 1, tn), lambda j, i: (0, 0, j)),
                  pl.BlockSpec((N_BRANCH, BRANCH_W, tn), lambda j, i: (0, 0, j))],
        out_specs=pl.BlockSpec((tm, tn), lambda j, i: (i, j)),
        out_shape=jax.ShapeDtypeStruct((T, D), BF16),
        compiler_params=_cp("parallel", "parallel"),
        name="merge_gate",
    )(hb, *branches, w_gate.astype(BF16), b_gate.reshape(N_BRANCH, 1, D), w_br.astype(BF16))
    const = lambda i: (0, 0)
    rows = lambda i: (i, 0)
    return pl.pallas_call(
        _out_ln_kernel,
        grid=(T // tm2,),
        in_specs=[pl.BlockSpec((tm2, D), rows), pl.BlockSpec((D, D), const), pl.BlockSpec((tm2, D), rows),
                  pl.BlockSpec((1, D), const), pl.BlockSpec((1, D), const)],
        out_specs=[pl.BlockSpec((tm2, D), rows), pl.BlockSpec((tm2 * ROW_SUB, LANES), rows)],
        out_shape=[jax.ShapeDtypeStruct((T, D), F32), jax.ShapeDtypeStruct((T * ROW_SUB, LANES), ROW_DT)],
        compiler_params=_cp("parallel"),
        name="out_proj_ln1",
    )(merged, w_out.astype(BF16), hf, ln_g.reshape(1, D), ln_b.reshape(1, D))


R_TM = 512
COMBINE_TB = 256


def _router_kernel(h_ref, whi_ref, wlo_ref, b_ref, eid_ref, wts_ref, rank_ref, cnt_ref, carry_ref):
    i = pl.program_id(0)

    @pl.when(i == 0)
    def _():
        carry_ref[...] = jnp.zeros(carry_ref.shape, F32)

    x = h_ref[...]
    tm = x.shape[0]
    xh = x.astype(BF16)
    xl = (x - xh.astype(F32)).astype(BF16)
    whi = whi_ref[...]
    logits = _dot(xh, whi) + _dot(xh, wlo_ref[...]) + _dot(xl, whi) + b_ref[...]
    lane = lax.broadcasted_iota(jnp.int32, (tm, LANES), 1)
    big = jnp.int32(4 * LANES)
    is_g = (lane >= N_EXPERTS) & (lane < N_EXPERTS + N_GROUPS)
    lg = jnp.where(is_g, logits, NEG_INF)
    gmax = jnp.max(lg, axis=-1, keepdims=True)
    gidx = jnp.min(jnp.where(lg == gmax, lane - N_EXPERTS, big), axis=-1, keepdims=True)
    g_w = 1.0 / jnp.sum(jnp.where(is_g, jnp.exp(lg - gmax), 0.0), axis=-1, keepdims=True)
    in_grp = (lane < N_EXPERTS) & ((lane // EXP_PER_GROUP) == gidx)
    le = jnp.where(in_grp, logits, NEG_INF)
    e1 = jnp.max(le, axis=-1, keepdims=True)
    i1 = jnp.min(jnp.where(le == e1, lane, big), axis=-1, keepdims=True)
    le2 = jnp.where(lane == i1, NEG_INF, le)
    e2 = jnp.max(le2, axis=-1, keepdims=True)
    i2 = jnp.min(jnp.where(le2 == e2, lane, big), axis=-1, keepdims=True)
    zsum = jnp.sum(jnp.where(in_grp, jnp.exp(le - e1), 0.0), axis=-1, keepdims=True)
    p1 = 1.0 / zsum
    p2 = jnp.exp(e2 - e1) / zsum
    w1 = g_w * p1 / (p1 + p2)
    w2 = g_w * p2 / (p1 + p2)
    oh1 = lane == i1
    oh2 = lane == i2
    ohs = (oh1 | oh2).astype(BF16)
    row = lax.broadcasted_iota(jnp.int32, (tm, tm), 0)
    col = lax.broadcasted_iota(jnp.int32, (tm, tm), 1)
    before = _dot((row > col).astype(BF16), ohs) + carry_ref[0:1, :]
    r1 = jnp.sum(jnp.where(oh1, before, 0.0), axis=-1, keepdims=True)
    r2 = jnp.sum(jnp.where(oh2, before, 0.0), axis=-1, keepdims=True)
    total = carry_ref[0:1, :] + jnp.sum(ohs.astype(F32), axis=0, keepdims=True)
    carry_ref[...] = jnp.broadcast_to(total, carry_ref.shape)
    cnt_ref[...] = jnp.broadcast_to(total, cnt_ref.shape).astype(jnp.int32)
    eid_ref[...] = jnp.where(lane == 0, i1, jnp.where(lane == 1, i2, 0))
    wts_ref[...] = jnp.where(lane == 0, w1, jnp.where(lane == 1, w2, 0.0))
    rank_ref[...] = jnp.where(lane == 0, r1, jnp.where(lane == 1, r2, 0.0)).astype(jnp.int32)


ROW_SUB = D_MODEL // LANES // 2
ROW_DT = jnp.uint32


def _row_slab(ref, r):
    return ref.at[pl.ds(pl.multiple_of(r * ROW_SUB, ROW_SUB), ROW_SUB)]


def _col_block(n, c):
    return pl.ds(c, n, stride=ROW_SUB)


def _pack_cols(x, c):
    as_bits = lambda t: lax.bitcast_convert_type(t.astype(BF16).astype(F32), ROW_DT)
    lo = as_bits(x[:, c * LANES:(c + 1) * LANES])
    hi = as_bits(x[:, (c + ROW_SUB) * LANES:(c + ROW_SUB + 1) * LANES])
    return (lo >> 16) | hi


def _unpack_cols(w):
    return (lax.bitcast_convert_type(w << 16, F32),
            lax.bitcast_convert_type(w & jnp.uint32(0xFFFF0000), F32))


EXPERT_AHEAD = 2


def _expert_kernel(be_ref, succ_ref, *refs, layer):
    src_refs = refs[:EXPERT_AHEAD + 1]
    (h_hbm, wg_hbm, wu_hbm, wd_hbm, o_ref, xbuf_ref, sg_ref, su_ref, sd_ref, wgb_ref, wub_ref, wdb_ref,
     stage_ref, gsem, wsem) = refs[EXPERT_AHEAD + 1:]
    i = pl.program_id(0)
    n_used = be_ref[pl.num_programs(0)]
    nbuf = EXPERT_AHEAD + 1
    slot = i % nbuf
    expert = be_ref[i]
    weights = ((wg_hbm, sg_ref, wgb_ref), (wu_hbm, su_ref, wub_ref), (wd_hbm, sd_ref, wdb_ref))

    def weight_copy(w, e, s):
        hbm, st_ref, _ = weights[w]
        return pltpu.make_async_copy(hbm.at[layer, e], st_ref.at[s], wsem.at[s, w])

    @pl.when(i == 0)
    def _():
        stage_ref[0] = 0
        for w in range(len(weights)):
            weight_copy(w, expert, 0).start()

    @pl.when((i < n_used) & ((i == 0) | (expert != be_ref[jnp.maximum(i - 1, 0)])))
    def _():
        s = stage_ref[0]
        nxt = succ_ref[expert]
        for w, (_, st_ref, wb_ref) in enumerate(weights):
            weight_copy(w, 0, s).wait()
            wb_ref[...] = st_ref[s].astype(BF16)

        @pl.when(nxt >= 0)
        def _():
            for w in range(len(weights)):
                weight_copy(w, nxt, 1 - s).start()

        stage_ref[0] = 1 - s

    def gather_copy(tok, to_slot, t):
        return pltpu.make_async_copy(_row_slab(h_hbm, tok), _row_slab(xbuf_ref.at[to_slot], t), gsem.at[to_slot])

    def gather(idx_ref, to_slot):
        for t in range(MOE_BLOCK):
            gather_copy(idx_ref[t], to_slot, t).start(priority=t % 2)

    @pl.when(i == 0)
    def _():
        for a in range(EXPERT_AHEAD):
            gather(src_refs[a], a)

    @pl.when(i + EXPERT_AHEAD < n_used)
    def _():
        gather(src_refs[EXPERT_AHEAD], (i + EXPERT_AHEAD) % nbuf)

    @pl.when(i < n_used)
    def _():
        for t in range(MOE_BLOCK):
            gather_copy(0, slot, 0).wait()
        x_ref = xbuf_ref.at[slot]
        halves = [_unpack_cols(x_ref[_col_block(MOE_BLOCK, c), :]) for c in range(ROW_SUB)]
        x = jnp.concatenate([lo for lo, _ in halves] + [hi for _, hi in halves], axis=1).astype(BF16)
        hid = (_silu(_dot(x, wgb_ref[...])) * _dot(x, wub_ref[...])).astype(BF16)
        y = _dot(hid, wdb_ref[...])
        for c in range(ROW_SUB):
            o_ref[_col_block(MOE_BLOCK, c), :] = _pack_cols(y, c)

    @pl.when(i >= n_used)
    def _():
        o_ref[...] = jnp.zeros(o_ref.shape, o_ref.dtype)


def _combine_kernel(dest_ref, dnext_ref, h_ref, w_ref, g_ref, b_ref, yrows_hbm, of_ref, ob_ref,
                    ybuf_ref, acc_ref, sem):
    i = pl.program_id(0)
    n = pl.num_programs(0)
    slot = i % 2

    def row_copy(d, to_slot, k, t):
        return pltpu.make_async_copy(_row_slab(yrows_hbm, d), _row_slab(ybuf_ref.at[to_slot, k], t), sem.at[to_slot])

    def gather(d_ref, to_slot):
        def start(t, carry):
            for k in range(TOP_K):
                row_copy(d_ref[TOP_K * t + k], to_slot, k, t).start(priority=k)
            return carry

        lax.fori_loop(0, COMBINE_TB, start, 0, unroll=8)

    @pl.when(i == 0)
    def _():
        gather(dest_ref, 0)

    @pl.when(i + 1 < n)
    def _():
        gather(dnext_ref, 1 - slot)

    for t in range(COMBINE_TB):
        for k in range(TOP_K):
            row_copy(0, slot, k, 0).wait()
    w = w_ref[...]
    y0_ref = ybuf_ref.at[slot, 0]
    y1_ref = ybuf_ref.at[slot, 1]
    for c in range(ROW_SUB):
        cb = _col_block(COMBINE_TB, c)
        lo0, hi0 = _unpack_cols(y0_ref[cb, :])
        lo1, hi1 = _unpack_cols(y1_ref[cb, :])
        acc_ref[:, c * LANES:(c + 1) * LANES] = lo0 * w[:, 0:1] + lo1 * w[:, 1:2]
        acc_ref[:, (c + ROW_SUB) * LANES:(c + ROW_SUB + 1) * LANES] = hi0 * w[:, 0:1] + hi1 * w[:, 1:2]
    y = _layernorm_rows(ALPHA * h_ref[...] + acc_ref[...], g_ref[...], b_ref[...])
    of_ref[...] = y
    ob_ref[...] = y.astype(BF16)


def _moe_layer(hf, hp, w_rg, b_rg, w_re, b_re, w_e_gate, w_e_up, w_e_down, layer, ln_g, ln_b):
    T, D = hf.shape
    n_rows = T * TOP_K + N_EXPERTS * MOE_BLOCK
    n_blocks = n_rows // MOE_BLOCK
    w_r = jnp.concatenate([w_re, w_rg, jnp.zeros((D, LANES - N_EXPERTS - N_GROUPS), F32)], axis=-1)
    b_r = jnp.concatenate([b_re, b_rg, jnp.zeros((LANES - N_EXPERTS - N_GROUPS,), F32)]).reshape(1, LANES)
    w_hi = w_r.astype(BF16)
    w_lo = (w_r - w_hi.astype(F32)).astype(BF16)
    const = lambda i: (0, 0)
    rows = lambda i: (i, 0)
    eid, wts, rank, cnt = pl.pallas_call(
        _router_kernel,
        grid=(T // R_TM,),
        in_specs=[pl.BlockSpec((R_TM, D), rows), pl.BlockSpec((D, LANES), const),
                  pl.BlockSpec((D, LANES), const), pl.BlockSpec((1, LANES), const)],
        out_specs=[pl.BlockSpec((R_TM, LANES), rows), pl.BlockSpec((R_TM, LANES), rows),
                   pl.BlockSpec((R_TM, LANES), rows), pl.BlockSpec((8, LANES), const)],
        out_shape=[jax.ShapeDtypeStruct((T, LANES), jnp.int32), jax.ShapeDtypeStruct((T, LANES), F32),
                   jax.ShapeDtypeStruct((T, LANES), jnp.int32), jax.ShapeDtypeStruct((8, LANES), jnp.int32)],
        scratch_shapes=[pltpu.VMEM((8, LANES), F32)],
        compiler_params=_cp("arbitrary"),
        name="moe_router",
    )(hf, w_hi, w_lo, b_r)
    counts = cnt[0, :N_EXPERTS]
    padded = (counts + MOE_BLOCK - 1) // MOE_BLOCK * MOE_BLOCK
    pends = jnp.cumsum(padded)
    pstarts = pends - padded
    blk_start = jnp.arange(n_blocks, dtype=jnp.int32) * MOE_BLOCK
    blk_exp = jnp.minimum(jnp.sum((pends[None, :] <= blk_start[:, None]).astype(jnp.int32), axis=1), N_EXPERTS - 1)
    sel = eid[:, :TOP_K, None] == jnp.arange(N_EXPERTS, dtype=jnp.int32)
    dest = (jnp.sum(jnp.where(sel, pstarts, 0), axis=-1) + rank[:, :TOP_K]).astype(jnp.int32).reshape(T * TOP_K)
    flat = jnp.full((n_rows,), -1, jnp.int32).at[dest].set(jnp.arange(T * TOP_K, dtype=jnp.int32),
                                                           unique_indices=True)
    row_tok = jnp.where(flat < 0, 0, flat // TOP_K)
    assert n_blocks > EXPERT_AHEAD
    any_spec = pl.BlockSpec(memory_space=pl.ANY)
    idx_spec = lambda f: pl.BlockSpec((MOE_BLOCK,), f, memory_space=pltpu.SMEM)
    ids = jnp.arange(N_EXPERTS, dtype=jnp.int32)
    later = (ids[None, :] > ids[:, None]) & (counts[None, :] > 0)
    succ = jnp.min(jnp.where(later, ids[None, :], N_EXPERTS), axis=1)
    succ = jnp.where(succ == N_EXPERTS, -1, succ).astype(jnp.int32)
    yrows = pl.pallas_call(
        functools.partial(_expert_kernel, layer=layer),
        grid_spec=pltpu.PrefetchScalarGridSpec(
            num_scalar_prefetch=2,
            grid=(n_blocks,),
            in_specs=[idx_spec(lambda i, *_, a=a: (jnp.minimum(i + a, n_blocks - 1),)) for a in range(EXPERT_AHEAD + 1)]
            + [any_spec] * 4,
            out_specs=pl.BlockSpec((MOE_BLOCK * ROW_SUB, LANES), lambda i, *_: (i, 0)),
            scratch_shapes=[pltpu.VMEM((EXPERT_AHEAD + 1, MOE_BLOCK * ROW_SUB, LANES), ROW_DT),
                            pltpu.VMEM((2, D, D_FF), F32), pltpu.VMEM((2, D, D_FF), F32), pltpu.VMEM((2, D_FF, D), F32),
                            pltpu.VMEM((D, D_FF), BF16), pltpu.VMEM((D, D_FF), BF16), pltpu.VMEM((D_FF, D), BF16),
                            pltpu.SMEM((1,), jnp.int32),
                            pltpu.SemaphoreType.DMA((EXPERT_AHEAD + 1,)), pltpu.SemaphoreType.DMA((2, 3))]),
        out_shape=jax.ShapeDtypeStruct((n_rows * ROW_SUB, LANES), ROW_DT),
        compiler_params=_cp("arbitrary"),
        name="moe_experts",
    )(jnp.concatenate([blk_exp, pends[-1:] // MOE_BLOCK]).astype(jnp.int32), succ,
      *([row_tok] * (EXPERT_AHEAD + 1)), hp, w_e_gate, w_e_up, w_e_down)
    n_steps = T // COMBINE_TB
    dspec = lambda f: pl.BlockSpec((TOP_K * COMBINE_TB,), f, memory_space=pltpu.SMEM)
    return pl.pallas_call(
        _combine_kernel,
        grid=(n_steps,),
        in_specs=[dspec(lambda i: (i,)), dspec(lambda i: (jnp.minimum(i + 1, n_steps - 1),)),
                  pl.BlockSpec((COMBINE_TB, D), rows), pl.BlockSpec((COMBINE_TB, LANES), rows),
                  pl.BlockSpec((1, D), const), pl.BlockSpec((1, D), const), any_spec],
        out_specs=[pl.BlockSpec((COMBINE_TB, D), rows), pl.BlockSpec((COMBINE_TB, D), rows)],
        out_shape=[jax.ShapeDtypeStruct((T, D), F32), jax.ShapeDtypeStruct((T, D), BF16)],
        scratch_shapes=[pltpu.VMEM((2, TOP_K, COMBINE_TB * ROW_SUB, LANES), ROW_DT),
                        pltpu.VMEM((COMBINE_TB, D), F32), pltpu.SemaphoreType.DMA((2,))],
        compiler_params=_cp("arbitrary"),
        name="moe_combine_ln2",
    )(dest, dest, hf, wts, ln_g.reshape(1, D), ln_b.reshape(1, D), yrows)


def kernel(x, ln_in_g, ln_in_b, rel_bias, w_in, g_cq, w_uq, g_ckv, w_ukv, w_dw_c, b_dw_c, ln_c_g, ln_c_b,
           w_conv_d, b_conv_d, a_log_f, a_log_b, dt_bias_f, dt_bias_b, d_skip, g_norm_d, w_br, w_gate, b_gate,
           w_out, ln1_g, ln1_b, w_rg, b_rg, w_re, b_re, w_e_gate, w_e_up, w_e_down, ln2_g, ln2_b):
    B, S, D = x.shape
    T = B * S
    hf, hb = _layernorm(x.reshape(T, D), ln_in_g, ln_in_b)
    a_bias = _mixer_a_bias(rel_bias, S)
    for l in range(DEPTH):
        w_a, w_r = _in_proj_weights(w_in[l])
        qkv = _matmul(hb, w_a.astype(BF16), F32, 512, 3 * A_WIDTH, "in_proj_a")
        rest = _matmul(hb, w_r.astype(BF16), F32, 512, R_WIDTH // 2, "in_proj_rest")
        y_a = _mixer_a(qkv, a_bias, B, S)
        y_b = _mixer_b(rest, g_cq[l], w_uq[l], g_ckv[l], w_ukv[l], B, S)
        y_c = _mixer_c(rest, w_dw_c[l], b_dw_c[l], ln_c_g[l], ln_c_b[l], B, S)
        y_d = _mixer_d(rest, w_conv_d[l], b_conv_d[l], a_log_f[l], a_log_b[l], dt_bias_f[l], dt_bias_b[l],
                       d_skip[l], g_norm_d[l], B, S)
        h1f, h1p = _merge(hb, hf, (y_a, y_b, y_c, y_d), w_gate[l], b_gate[l], w_br[l], w_out[l], ln1_g[l], ln1_b[l])
        hf, hb = _moe_layer(h1f, h1p, w_rg[l], b_rg[l], w_re[l], b_re[l], w_e_gate, w_e_up, w_e_down, l,
                            ln2_g[l], ln2_b[l])
    return hf.reshape(B, S, D)
```

```python
import functools

import numpy as np
import jax
import jax.numpy as jnp
from jax import lax
from jax.experimental import pallas as pl
from jax.experimental.pallas import tpu as pltpu

F32 = jnp.float32
BF16 = jnp.bfloat16

D_MODEL = 2048
DEPTH = 2
A_HEADS = 8
A_HEAD_DIM = 64
A_WIDTH = A_HEADS * A_HEAD_DIM
A_PATTERNS = ((128, 1), (512, 4), (2048, 16))
A_BAND = 64
REL_BUCKETS = 32
REL_MAX_DIST = 1024
B_HEADS = 8
B_NOPE = 64
B_ROPE = 32
B_V = 64
B_Q_LORA = 512
B_KV_LORA = 256
ROPE_THETA = 10000.0
C_CH = 512
C_KERNEL = 31
D_HEADS = 8
D_HEAD_DIM = 64
D_INNER = D_HEADS * D_HEAD_DIM
D_STATE = 128
D_GROUPS = 2
D_CONV = 5
D_CHUNK = 128
N_BRANCH = 4
BRANCH_W = 512
N_GROUPS = 4
EXP_PER_GROUP = 8
N_EXPERTS = N_GROUPS * EXP_PER_GROUP
TOP_K = 2
D_FF = 512
MOE_BLOCK = 128
ALPHA = (2 * DEPTH) ** 0.25
EPS = 1e-5
NEG_INF = -1e30

LANES = 128
R_GLU, R_XBC, R_CQ, R_Z, R_CKV, R_KR, R_DT = 0, 1024, 2048, 2560, 3072, 3328, 3456
R_WIDTH = 3584
VMEM_LIMIT = 56 * 1024 * 1024


def _cp(*sem):
    return pltpu.CompilerParams(dimension_semantics=sem, vmem_limit_bytes=VMEM_LIMIT)


def _dot(a, b):
    return jnp.dot(a, b, preferred_element_type=F32)


def _dot_nt(a, b):
    return lax.dot_general(a, b, (((1,), (1,)), ((), ())), preferred_element_type=F32)


def _split3(x):
    hi = x.astype(BF16)
    r1 = x - hi.astype(F32)
    mid = r1.astype(BF16)
    lo = (r1 - mid.astype(F32)).astype(BF16)
    return hi, mid, lo


def _layernorm_rows(x, g, b):
    mu = jnp.mean(x, axis=-1, keepdims=True)
    xc = x - mu
    var = jnp.mean(xc * xc, axis=-1, keepdims=True)
    return xc * lax.rsqrt(var + EPS) * g + b


def _sigmoid(x):
    return 1.0 / (1.0 + jnp.exp(-x))


def _silu(x):
    return x * _sigmoid(x)


def _ln_kernel(x_ref, g_ref, b_ref, of_ref, ob_ref):
    y = _layernorm_rows(x_ref[...], g_ref[...], b_ref[...])
    of_ref[...] = y
    ob_ref[...] = y.astype(BF16)


def _layernorm(x, g, b, tm=256):
    T, D = x.shape
    return pl.pallas_call(
        _ln_kernel,
        grid=(T // tm,),
        in_specs=[pl.BlockSpec((tm, D), lambda i: (i, 0)),
                  pl.BlockSpec((1, D), lambda i: (0, 0)),
                  pl.BlockSpec((1, D), lambda i: (0, 0))],
        out_specs=[pl.BlockSpec((tm, D), lambda i: (i, 0)),
                   pl.BlockSpec((tm, D), lambda i: (i, 0))],
        out_shape=[jax.ShapeDtypeStruct((T, D), F32), jax.ShapeDtypeStruct((T, D), BF16)],
        compiler_params=_cp("parallel"),
        name="ln_in",
    )(x, g.reshape(1, D), b.reshape(1, D))


def _mm_kernel(x_ref, w_ref, o_ref):
    o_ref[...] = _dot(x_ref[...], w_ref[...]).astype(o_ref.dtype)


def _matmul(x, w, out_dtype, tm, tn, name):
    M, K = x.shape
    N = w.shape[1]
    return pl.pallas_call(
        _mm_kernel,
        grid=(N // tn, M // tm),
        in_specs=[pl.BlockSpec((tm, K), lambda j, i: (i, 0)),
                  pl.BlockSpec((K, tn), lambda j, i: (0, j))],
        out_specs=pl.BlockSpec((tm, tn), lambda j, i: (i, j)),
        out_shape=jax.ShapeDtypeStruct((M, N), out_dtype),
        compiler_params=_cp("parallel", "parallel"),
        name=name,
    )(x, w)


def _t5_bucket(rel):
    half = REL_BUCKETS // 2
    max_exact = half // 2
    n = np.abs(rel)
    large = max_exact + (np.log(np.maximum(n, 1) / max_exact) / np.log(REL_MAX_DIST / max_exact)
                         * (half - max_exact)).astype(np.int32)
    large = np.minimum(large, half - 1)
    return (rel > 0).astype(np.int32) * half + np.where(n < max_exact, n, large)


def _a_window(L):
    return min(2 * LANES, L)


def _a_bias_tiles(rel_bias, d, L):
    W = _a_window(L)
    offs = (0,) if L == LANES else (0, -A_BAND, -2 * A_BAND)
    qi = np.arange(LANES)[:, None]
    kj = np.arange(W)[None, :]
    rel = np.stack([kj - qi + off for off in offs], axis=0)
    valid = np.abs(rel) <= A_BAND
    onehot = (jnp.asarray(_t5_bucket(rel * d), jnp.int32)[..., None] == jnp.arange(REL_BUCKETS)).astype(F32)
    b = jnp.einsum('vqkb,bh->vhqk', onehot, rel_bias.astype(F32), precision=lax.Precision.HIGHEST)
    b = jnp.where(valid[:, None], b, NEG_INF)
    return b.reshape(len(offs), A_HEADS // 2, 2 * LANES, W)


A_GROUP = 2


def _attn_a_kernel(q_ref, k_ref, v_ref, b16_ref, b4_ref, b1_ref, y_ref,
                   q4_ref, k4_ref, v4_ref, m_ref, l_ref, acc_ref, tmp_ref, *, S):
    (_, d1), (_, d4), (_, d16) = A_PATTERNS
    lane = lax.broadcasted_iota(jnp.int32, (LANES, LANES), 1)
    head0 = lane < A_HEAD_DIM
    scale = A_HEAD_DIM ** -0.5

    def partial_softmax(qs, ks, vs, bias_ref):
        q2 = jnp.concatenate([jnp.where(head0, qs, 0.0), jnp.where(head0, 0.0, qs)], axis=0).astype(BF16)
        s = _dot_nt(q2, ks.astype(BF16)) * scale + bias_ref[...]
        m = jnp.max(s, axis=-1, keepdims=True)
        p = jnp.exp(s - m).astype(BF16)
        num = _dot(p, vs.astype(BF16))
        den = _dot(p, jnp.ones((vs.shape[0], LANES), BF16))
        both = lambda t: jnp.where(head0, t[:LANES], t[LANES:])
        return both(jnp.broadcast_to(m, (2 * LANES, LANES))), both(den), both(num)

    def fold(old, new):
        (m_old, l_old, a_old), (m_new, l_new, a_new) = old, new
        m = jnp.maximum(m_old, m_new)
        c_old = jnp.exp(m_old - m)
        c_new = jnp.exp(m_new - m)
        return m, c_old * l_old + c_new * l_new, c_old * a_old + c_new * a_new

    stat_refs = (m_ref, l_ref, acc_ref)

    def get(c, rows):
        return tuple(ref.at[c][rows, :] for ref in stat_refs)

    def put(c, rows, stats):
        for ref, val in zip(stat_refs, stats):
            ref.at[c][rows, :] = val

    def grouped(n, group, unit):
        def trip(g, carry):
            pending = [unit(g * group + u) for u in range(group)]
            for finish in pending:
                finish()
            return carry

        lax.fori_loop(0, n // group, trip, 0)

    L4 = S // d4
    sub = d16 // d4
    assert S // d16 == LANES and sub == d4
    for c in range(d4):
        cls = pl.ds(c, L4, stride=d4)
        q4_ref[c] = q_ref[cls, :]
        k4_ref[c] = k_ref[cls, :]
        v4_ref[c] = v_ref[cls, :]

    def unit16(t):
        c = t % d4
        rows = pl.ds(t // d4, LANES, stride=sub)
        stats = partial_softmax(q4_ref.at[c][rows, :], k4_ref.at[c][rows, :], v4_ref.at[c][rows, :], b16_ref.at[0])
        return lambda: put(c, rows, stats)

    grouped(d16, A_GROUP, unit16)

    def window(i, L):
        nqb = L // LANES
        ws = pl.multiple_of(jnp.clip(i * LANES - A_BAND, 0, L - 2 * LANES), A_BAND)
        return pl.ds(ws, 2 * LANES), jnp.where(i == 0, 0, jnp.where(i == nqb - 1, 2, 1))

    def unit4(t):
        c = t % d4
        i = t // d4
        keys, var = window(i, L4)
        rows = pl.ds(pl.multiple_of(i * LANES, LANES), LANES)
        stats = partial_softmax(q4_ref.at[c][rows, :], k4_ref.at[c][keys, :], v4_ref.at[c][keys, :], b4_ref.at[var])
        return lambda: put(c, rows, fold(get(c, rows), stats))

    grouped(d4 * (L4 // LANES), A_GROUP, unit4)

    def unit1(i):
        keys, var = window(i, S)
        rows = pl.ds(pl.multiple_of(i * LANES, LANES), LANES)
        stats = partial_softmax(q_ref[rows, :], k_ref[keys, :], v_ref[keys, :], b1_ref.at[var])

        def finish():
            part = pl.ds(pl.multiple_of(i * (LANES // d4), LANES // d4), LANES // d4)
            for n, ref in enumerate(stat_refs):
                for c in range(d4):
                    tmp_ref.at[n][pl.ds(c, LANES // d4, stride=d4), :] = ref.at[c][part, :]
            _, l, a = fold(tuple(tmp_ref[n] for n in range(len(stat_refs))), stats)
            y_ref[rows, :] = (a / l).astype(y_ref.dtype)

        return finish

    grouped(S // LANES, A_GROUP, unit1)


def _mixer_a_bias(rel_bias, S):
    return tuple(_a_bias_tiles(rel_bias, d, S // d) for _, d in A_PATTERNS)


def _mixer_a(qkv, bias, B, S):
    b1, b4, b16 = bias
    npair = A_HEADS // 2
    pair_bias = lambda t: pl.BlockSpec((t.shape[0], None) + t.shape[2:], lambda b, hp: (0, hp, 0, 0))
    slab = lambda first: pl.BlockSpec((S, LANES), lambda b, hp: (b, first + hp))
    return pl.pallas_call(
        functools.partial(_attn_a_kernel, S=S),
        grid=(B, npair),
        in_specs=[slab(0), slab(npair), slab(2 * npair), pair_bias(b16), pair_bias(b4), pair_bias(b1)],
        out_specs=pl.BlockSpec((S, LANES), lambda b, hp: (b, hp)),
        out_shape=jax.ShapeDtypeStruct((B * S, A_WIDTH), BF16),
        scratch_shapes=[pltpu.VMEM((A_PATTERNS[1][1], S // A_PATTERNS[1][1], LANES), F32)] * 6
        + [pltpu.VMEM((3, LANES, LANES), F32)],
        compiler_params=_cp("parallel", "parallel"),
        name="attn_a",
    )(qkv, qkv, qkv, b16, b4, b1)


MLA_Q_SCALE = float((B_NOPE + B_ROPE) ** -0.5 * np.log2(np.e))


def _mla_proj_kernel(cq_ref, ckv_ref, kr_ref, gq_ref, gkv_ref, wqm_ref, wqs_ref, wk_ref, wv_ref, vone_ref,
                     ek_ref, cosq_ref, sinq_ref, csk_ref, q_ref, k_ref, v_ref):
    cq = cq_ref[...]
    xq = (cq * lax.rsqrt(jnp.mean(cq * cq, axis=-1, keepdims=True) + EPS) * gq_ref[...]).astype(BF16)
    ckv = ckv_ref[...]
    xkv = (ckv * lax.rsqrt(jnp.mean(ckv * ckv, axis=-1, keepdims=True) + EPS) * gkv_ref[...]).astype(BF16)
    qm = _dot(xq, wqm_ref[...])
    qs = _dot(xq, wqs_ref[...])
    cosq = cosq_ref[...] * MLA_Q_SCALE
    sinq = sinq_ref[...] * MLA_Q_SCALE
    t = kr_ref[...] * csk_ref[...]
    t_hi = t.astype(BF16)
    t_lo = (t - t_hi.astype(F32)).astype(BF16)
    kk = _dot(xkv, wk_ref[...]) + _dot(t_hi, ek_ref[...]) + _dot(t_lo, ek_ref[...])
    for h in range(B_HEADS):
        sl = slice(h * LANES, (h + 1) * LANES)
        q_ref[:, sl] = (qm[:, sl] * cosq + qs[:, sl] * sinq).astype(BF16)
    k_ref[...] = kk.astype(BF16)
    v_ref[...] = (_dot(xkv, wv_ref[...]) + vone_ref[...]).astype(BF16)


def _mla_attn_kernel(q_ref, k_ref, v_ref, o_ref):
    outs = []
    for hh in range(2):
        sl = slice(hh * LANES, (hh + 1) * LANES)
        s = _dot_nt(q_ref[:, sl], k_ref[:, sl])
        p = jnp.exp2(s - jnp.max(s, axis=-1, keepdims=True))
        outs.append(_dot(p.astype(BF16), v_ref[:, sl]))
    lane = lax.broadcasted_iota(jnp.int32, outs[0].shape, 1)
    acc = jnp.where(lane < B_V, outs[0], outs[1])
    den = pltpu.roll(jnp.where(lane < B_V, outs[1], outs[0]), B_V, axis=1)
    o_ref[...] = (acc / den).astype(o_ref.dtype)


def _mla_tables(S):
    inv_freq = ROPE_THETA ** (-jnp.arange(0, B_ROPE, 2, dtype=F32) / B_ROPE)
    ang = jnp.arange(S, dtype=F32)[:, None] * inv_freq[None]
    cos, sin = jnp.cos(ang), jnp.sin(ang)
    cos2 = jnp.concatenate([cos, cos], axis=-1)
    sin2 = jnp.concatenate([sin, sin], axis=-1)
    ones = jnp.ones((S, B_NOPE), F32)
    zn = jnp.zeros((S, B_NOPE), F32)
    zp = jnp.zeros((S, LANES - B_NOPE - B_ROPE), F32)
    cosq = jnp.concatenate([ones, cos2, zp], axis=-1)
    sinq = jnp.concatenate([zn, sin2, zp], axis=-1)
    csk = jnp.concatenate([cos2, sin2, jnp.zeros((S, LANES - 2 * B_ROPE), F32)], axis=-1)
    return cosq, sinq, csk


def _swap_cols(w):
    half = w.shape[-1] // 2
    return jnp.concatenate([-w[..., half:], w[..., :half]], axis=-1)


def _mla_weights(w_uq, w_ukv):
    dq = B_NOPE + B_ROPE
    wq = w_uq.reshape(B_Q_LORA, B_HEADS, dq)
    zpad = jnp.zeros((B_Q_LORA, B_HEADS, LANES - dq), F32)
    wqm = jnp.concatenate([wq, zpad], axis=-1).reshape(B_Q_LORA, B_HEADS * LANES)
    wqs = jnp.concatenate([jnp.zeros((B_Q_LORA, B_HEADS, B_NOPE), F32), _swap_cols(wq[..., B_NOPE:]), zpad],
                          axis=-1).reshape(B_Q_LORA, B_HEADS * LANES)
    wkv = w_ukv.reshape(B_KV_LORA, B_HEADS, B_NOPE + B_V)
    wk = jnp.concatenate([wkv[..., :B_NOPE], jnp.zeros((B_KV_LORA, B_HEADS, LANES - B_NOPE), F32)],
                         axis=-1).reshape(B_KV_LORA, B_HEADS * LANES)
    zv = jnp.zeros((B_KV_LORA, B_HEADS // 2, LANES - B_V), F32)
    wv_h = wkv[..., B_NOPE:]
    wv = jnp.stack([jnp.concatenate([wv_h[:, 0::2], zv], axis=-1),
                    jnp.concatenate([zv, wv_h[:, 1::2]], axis=-1)], axis=2).reshape(B_KV_LORA, B_HEADS * LANES)
    lane_in_pair = np.arange(B_HEADS * LANES) % (2 * LANES)
    vone = jnp.asarray(((lane_in_pair >= B_V) & (lane_in_pair < LANES + B_V)).astype(np.float32)).reshape(1, -1)
    ek = np.zeros((LANES, B_HEADS, LANES), np.float32)
    for j in range(B_ROPE):
        ek[j, :, B_NOPE + j] = 1.0
        ek[B_ROPE + j, :, B_NOPE + j] = 1.0
    ek = jnp.asarray(ek.reshape(LANES, B_HEADS * LANES))
    return wqm.astype(BF16), wqs.astype(BF16), wk.astype(BF16), wv.astype(BF16), vone, ek.astype(BF16)


def _mixer_b(rest, g_cq, w_uq, g_ckv, w_ukv, B, S, tm=512, tq=256):
    T = B * S
    wqm, wqs, wk, wv, vone, ek = _mla_weights(w_uq, w_ukv)
    cosq, sinq, csk = _mla_tables(S)
    nst = S // tm
    QW = B_HEADS * LANES
    const = lambda i: (0, 0)
    pos = lambda i: (i % nst, 0)
    q, k, v = pl.pallas_call(
        _mla_proj_kernel,
        grid=(T // tm,),
        in_specs=[pl.BlockSpec((tm, B_Q_LORA), lambda i: (i, R_CQ // B_Q_LORA)),
                  pl.BlockSpec((tm, B_KV_LORA), lambda i: (i, R_CKV // B_KV_LORA)),
                  pl.BlockSpec((tm, LANES), lambda i: (i, R_KR // LANES)),
                  pl.BlockSpec((1, B_Q_LORA), const),
                  pl.BlockSpec((1, B_KV_LORA), const),
                  pl.BlockSpec((B_Q_LORA, QW), const),
                  pl.BlockSpec((B_Q_LORA, QW), const),
                  pl.BlockSpec((B_KV_LORA, QW), const),
                  pl.BlockSpec((B_KV_LORA, QW), const),
                  pl.BlockSpec((1, QW), const),
                  pl.BlockSpec((LANES, QW), const),
                  pl.BlockSpec((tm, LANES), pos),
                  pl.BlockSpec((tm, LANES), pos),
                  pl.BlockSpec((tm, LANES), pos)],
        out_specs=[pl.BlockSpec((tm, QW), lambda i: (i, 0)),
                   pl.BlockSpec((tm, QW), lambda i: (i, 0)),
                   pl.BlockSpec((tm, QW), lambda i: (i, 0))],
        out_shape=[jax.ShapeDtypeStruct((T, QW), BF16), jax.ShapeDtypeStruct((T, QW), BF16),
                   jax.ShapeDtypeStruct((T, QW), BF16)],
        compiler_params=_cp("parallel"),
        name="mla_proj",
    )(rest, rest, rest, g_cq.reshape(1, -1), g_ckv.reshape(1, -1), wqm, wqs, wk, wv, vone, ek, cosq, sinq, csk)
    y = pl.pallas_call(
        _mla_attn_kernel,
        grid=(B, B_HEADS // 2, S // tq),
        in_specs=[pl.BlockSpec((None, tq, 2 * LANES), lambda b, hp, i: (b, i, hp)),
                  pl.BlockSpec((None, S, 2 * LANES), lambda b, hp, i: (b, 0, hp)),
                  pl.BlockSpec((None, S, 2 * LANES), lambda b, hp, i: (b, 0, hp))],
        out_specs=pl.BlockSpec((None, tq, 2 * B_V), lambda b, hp, i: (b, i, hp)),
        out_shape=jax.ShapeDtypeStruct((B, S, B_HEADS * B_V), BF16),
        compiler_params=_cp("parallel", "parallel", "arbitrary"),
        name="mla_attn",
    )(q.reshape(B, S, QW), k.reshape(B, S, QW), v.reshape(B, S, QW))
    return y.reshape(T, B_HEADS * B_V)


C_PAD = 16
C_ROWS = 128


SUBLANES = 8


def _tap_span(first, ntaps, rows):
    return rows + ((first + ntaps - 1) // SUBLANES) * SUBLANES


def _depthwise_taps(win_ref, sh_ref, w_ref, bias, ls, first, ntaps, rows):
    acc = jnp.broadcast_to(bias, (rows, LANES))
    span = _tap_span(first, ntaps, rows)
    for ph in range(SUBLANES):
        taps = [j for j in range(ntaps) if (first + j) % SUBLANES == ph]
        if not taps:
            continue
        if len(taps) == 1:
            j = taps[0]
            acc = acc + w_ref[j:j + 1, ls] * win_ref[first + j:first + j + rows, ls]
            continue
        sh_ref[0:span, :] = win_ref[ph:ph + span, ls]
        for j in taps:
            a = (first + j) // SUBLANES * SUBLANES
            acc = acc + w_ref[j:j + 1, ls] * sh_ref[a:a + rows, :]
    return acc


def _fill_window(win_ref, load_rows, r0, i, nblk, rows, pad):
    width = win_ref.shape[1]
    win_ref[pad:pad + rows, :] = load_rows(r0, rows)

    @pl.when(i > 0)
    def _():
        win_ref[0:pad, :] = load_rows(r0 - pad, pad)

    @pl.when(i == 0)
    def _():
        win_ref[0:pad, :] = jnp.zeros((pad, width), F32)

    @pl.when(i < nblk - 1)
    def _():
        win_ref[pad + rows:pad + rows + pad, :] = load_rows(r0 + rows, pad)

    @pl.when(i == nblk - 1)
    def _():
        win_ref[pad + rows:pad + rows + pad, :] = jnp.zeros((pad, width), F32)


def _conv_c_kernel(glu_ref, w_ref, b_ref, g_ref, beta_ref, o_ref, win_ref, sh_ref, acc_ref, *, S):
    i = pl.program_id(1)
    r0 = pl.multiple_of(i * C_ROWS, C_ROWS)

    def glu_rows(start, n):
        rs = pl.ds(pl.multiple_of(start, C_PAD), n)
        return glu_ref[rs, 0:C_CH] * _sigmoid(glu_ref[rs, C_CH:2 * C_CH])

    _fill_window(win_ref, glu_rows, r0, i, S // C_ROWS, C_ROWS, C_PAD)
    first = C_PAD - C_KERNEL // 2
    for lb in range(C_CH // LANES):
        ls = slice(lb * LANES, (lb + 1) * LANES)
        acc_ref[:, ls] = _depthwise_taps(win_ref, sh_ref, w_ref, b_ref[:, ls], ls, first, C_KERNEL, C_ROWS)
    y = _layernorm_rows(acc_ref[...], g_ref[...], beta_ref[...])
    o_ref[...] = _silu(y).astype(o_ref.dtype)


def _mixer_c(rest, w_dw, b_dw, ln_g, ln_b, B, S):
    T = B * S
    const = lambda b, i: (0, 0)
    y = pl.pallas_call(
        functools.partial(_conv_c_kernel, S=S),
        grid=(B, S // C_ROWS),
        in_specs=[pl.BlockSpec((None, S, 2 * C_CH), lambda b, i: (b, 0, R_GLU // (2 * C_CH))),
                  pl.BlockSpec((C_KERNEL, C_CH), const),
                  pl.BlockSpec((1, C_CH), const),
                  pl.BlockSpec((1, C_CH), const),
                  pl.BlockSpec((1, C_CH), const)],
        out_specs=pl.BlockSpec((None, C_ROWS, C_CH), lambda b, i: (b, i, 0)),
        out_shape=jax.ShapeDtypeStruct((B, S, C_CH), BF16),
        scratch_shapes=[pltpu.VMEM((C_ROWS + 2 * C_PAD, C_CH), F32),
                        pltpu.VMEM((_tap_span(C_PAD - C_KERNEL // 2, C_KERNEL, C_ROWS), LANES), F32),
                        pltpu.VMEM((C_ROWS, C_CH), F32)],
        compiler_params=_cp("parallel", "parallel"),
        name="conformer_conv",
    )(rest.reshape(B, S, R_WIDTH), w_dw, b_dw.reshape(1, -1), ln_g.reshape(1, -1), ln_b.reshape(1, -1))
    return y.reshape(T, C_CH)


D_PAD = 8
XBC_W = D_INNER + 2 * D_GROUPS * D_STATE
N_PAIR = D_HEADS // 2


def _pair_expand(v, first):
    lane = lax.broadcasted_iota(jnp.int32, (v.shape[0], LANES), 1)
    lo = jnp.broadcast_to(v[:, first:first + 1], (v.shape[0], LANES))
    hi = jnp.broadcast_to(v[:, first + 1:first + 2], (v.shape[0], LANES))
    return jnp.where(lane < D_HEAD_DIM, lo, hi)


def _ssd_kernel(xbc_ref, z_ref, dt_ref, wc_ref, bc_ref, alog_ref, dtb_ref, dskip_ref, gn_ref, o_ref,
                win_ref, sh_ref, xc_ref, a_ref, dtv_ref, y_ref, st_ref, *, S):
    Q = D_CHUNK
    nchunk = S // Q
    N = D_STATE
    bm0 = D_INNER
    cm0 = D_INNER + D_GROUPS * N

    def conv_body(c, carry):
        r0 = pl.multiple_of(c * Q, Q)
        _fill_window(win_ref, lambda st, n: xbc_ref[pl.ds(pl.multiple_of(st, D_PAD), n), :], r0, c, nchunk, Q, D_PAD)
        for lb in range(XBC_W // LANES):
            ls = slice(lb * LANES, (lb + 1) * LANES)
            acc = _depthwise_taps(win_ref, sh_ref, wc_ref, bc_ref[:, ls], ls, D_PAD - D_CONV // 2, D_CONV, Q)
            xc_ref[pl.ds(r0, Q), ls] = _silu(acc)
        return carry

    lax.fori_loop(0, nchunk, conv_body, 0)

    lane1 = lax.broadcasted_iota(jnp.int32, (1, LANES), 1)
    a_row = jnp.where(lane1 < 2 * D_HEADS, -jnp.exp(alog_ref[...]), 0.0)
    xdt = dt_ref[...] + dtb_ref[...]
    dtv = jnp.maximum(xdt, 0.0) + jnp.log(1.0 + jnp.exp(-jnp.abs(xdt)))
    dtv_ref[...] = dtv
    a_ref[...] = dtv * a_row

    row = lax.broadcasted_iota(jnp.int32, (Q, Q), 0)
    col = lax.broadcasted_iota(jnp.int32, (Q, Q), 1)
    tril = row >= col
    triu = col >= row
    lane = col

    def scan_chunk(c, lower, off, finalize):
        r0 = pl.multiple_of(c * Q, Q)
        rows = pl.ds(r0, Q)
        mask = tril if lower else triu
        tri = mask.astype(BF16)
        a_hi, a_mid, a_lo = _split3(a_ref[rows, :])
        cs = _dot(tri, a_hi) + _dot(tri, a_mid) + _dot(tri, a_lo)
        cs_t = cs.T
        ecs = jnp.exp(cs)
        edge = Q - 1 if lower else 0
        edec = jnp.exp(cs[edge:edge + 1, :] - cs)
        dt_c = dtv_ref[rows, :]
        dt_t = dt_c.T
        dte_t = (dt_c * edec).T
        for g in range(D_GROUPS):
            bg = xc_ref[rows, bm0 + g * N:bm0 + (g + 1) * N]
            cg = xc_ref[rows, cm0 + g * N:cm0 + (g + 1) * N].astype(BF16)
            cb = _dot_nt(cg, bg.astype(BF16))
            bg_t = bg.T
            for pp in range(N_PAIR // D_GROUPS):
                p = g * (N_PAIR // D_GROUPS) + pp
                ps = slice(p * LANES, (p + 1) * LANES)
                x_p = xc_ref[rows, ps]
                ms, bs = [], []
                for hh in range(2):
                    k = off + 2 * p + hh
                    diff = jnp.broadcast_to(cs[:, k:k + 1], (Q, Q)) - cs_t[k:k + 1, :]
                    ms.append((jnp.exp(jnp.where(mask, diff, NEG_INF)) * (cb * dt_t[k:k + 1, :])).astype(BF16))
                    bs.append((bg_t * dte_t[k:k + 1, :]).astype(BF16))
                x_lo = jnp.where(lane < D_HEAD_DIM, x_p, 0.0).astype(BF16)
                x_hi = jnp.where(lane >= D_HEAD_DIM, x_p, 0.0).astype(BF16)
                x2 = jnp.concatenate([x_lo, x_hi], axis=0)
                y_intra = _dot(jnp.concatenate(ms, axis=1), x2)
                hp = st_ref[p]
                ecs_p = _pair_expand(ecs, off + 2 * p)
                y_new = y_intra + _dot(cg, hp.astype(BF16)) * ecs_p
                if lower:
                    y_ref[rows, ps] = y_new
                else:
                    y_ref[rows, ps] = y_ref[rows, ps] + y_new
                st_ref[p] = hp * ecs_p[edge:edge + 1, :] + _dot(jnp.concatenate(bs, axis=1), x2)
        if finalize:
            y = y_ref[rows, :] + xc_ref[rows, 0:D_INNER] * dskip_ref[...]
            gated = y * _silu(z_ref[rows, :])
            out = gated * lax.rsqrt(jnp.mean(gated * gated, axis=-1, keepdims=True) + EPS) * gn_ref[...]
            o_ref[rows, :] = out.astype(o_ref.dtype)

    st_ref[...] = jnp.zeros(st_ref.shape, F32)

    def fwd_body(c, carry):
        scan_chunk(c, True, 0, False)
        return carry

    lax.fori_loop(0, nchunk, fwd_body, 0)
    st_ref[...] = jnp.zeros(st_ref.shape, F32)

    def bwd_body(k, carry):
        scan_chunk(nchunk - 1 - k, False, D_HEADS, True)
        return carry

    lax.fori_loop(0, nchunk, bwd_body, 0)


def _mixer_d(rest, w_conv, b_conv, a_log_f, a_log_b, dt_bias_f, dt_bias_b, d_skip, g_norm, B, S):
    T = B * S
    pad16 = lambda f, b: jnp.concatenate([f, b, jnp.zeros((LANES - 2 * D_HEADS,), F32)]).reshape(1, LANES)
    const = lambda b: (0, 0)
    y = pl.pallas_call(
        functools.partial(_ssd_kernel, S=S),
        grid=(B,),
        in_specs=[pl.BlockSpec((None, S, XBC_W), lambda b: (b, 0, R_XBC // XBC_W)),
                  pl.BlockSpec((None, S, D_INNER), lambda b: (b, 0, R_Z // D_INNER)),
                  pl.BlockSpec((None, S, LANES), lambda b: (b, 0, R_DT // LANES)),
                  pl.BlockSpec((D_CONV, XBC_W), const),
                  pl.BlockSpec((1, XBC_W), const),
                  pl.BlockSpec((1, LANES), const),
                  pl.BlockSpec((1, LANES), const),
                  pl.BlockSpec((1, D_INNER), const),
                  pl.BlockSpec((1, D_INNER), const)],
        out_specs=pl.BlockSpec((None, S, D_INNER), lambda b: (b, 0, 0)),
        out_shape=jax.ShapeDtypeStruct((B, S, D_INNER), BF16),
        scratch_shapes=[pltpu.VMEM((D_CHUNK + 2 * D_PAD, XBC_W), F32),
                        pltpu.VMEM((_tap_span(D_PAD - D_CONV // 2, D_CONV, D_CHUNK), LANES), F32),
                        pltpu.VMEM((S, XBC_W), F32),
                        pltpu.VMEM((S, LANES), F32),
                        pltpu.VMEM((S, LANES), F32),
                        pltpu.VMEM((S, D_INNER), F32),
                        pltpu.VMEM((N_PAIR, D_STATE, LANES), F32)],
        compiler_params=_cp("parallel"),
        name="ssd_mixer",
    )(rest.reshape(B, S, R_WIDTH), rest.reshape(B, S, R_WIDTH), rest.reshape(B, S, R_WIDTH),
      w_conv, b_conv.reshape(1, -1), pad16(a_log_f, a_log_b), pad16(dt_bias_f, dt_bias_b),
      jnp.repeat(d_skip, D_HEAD_DIM).reshape(1, -1), g_norm.reshape(1, -1))
    return y.reshape(T, D_INNER)


def _in_proj_weights(w_in_l):
    o = np.cumsum((0, A_WIDTH, A_WIDTH, A_WIDTH, B_Q_LORA, B_KV_LORA, B_ROPE, 2 * C_CH,
                   D_INNER, D_INNER, D_GROUPS * D_STATE, D_GROUPS * D_STATE, 2 * D_HEADS)).tolist()
    seg = lambda n: w_in_l[:, o[n]:o[n + 1]]
    w_a = w_in_l[:, :o[3]]
    cq, ckv, kr, glu, z, xs, bm, cm, dt = (seg(n) for n in range(3, 12))
    zeros = lambda n: jnp.zeros((w_in_l.shape[0], n), w_in_l.dtype)
    w_r = jnp.concatenate([glu, xs, bm, cm, cq, z, ckv,
                           kr, _swap_cols(kr), zeros(LANES - 2 * B_ROPE),
                           dt, zeros(LANES - 2 * D_HEADS)], axis=-1)
    assert w_r.shape[1] == R_WIDTH
    return w_a, w_r


def _merge_gate_kernel(h_ref, ya_ref, yb_ref, yc_ref, yd_ref, wg_ref, bg_ref, wbr_ref, o_ref):
    h = h_ref[...]
    acc = None
    for i, y_ref in enumerate((ya_ref, yb_ref, yc_ref, yd_ref)):
        gate = _sigmoid(_dot(h, wg_ref[i]) + bg_ref[i])
        term = gate * _dot(y_ref[...], wbr_ref[i])
        acc = term if acc is None else acc + term
    o_ref[...] = acc.astype(o_ref.dtype)


def _out_ln_kernel(m_ref, w_ref, h_ref, g_ref, b_ref, of_ref, op_ref):
    y = _layernorm_rows(ALPHA * h_ref[...] + _dot(m_ref[...], w_ref[...]), g_ref[...], b_ref[...])
    of_ref[...] = y
    for c in range(ROW_SUB):
        op_ref[_col_block(y.shape[0], c), :] = _pack_cols(y, c)


def _merge(hb, hf, branches, w_gate, b_gate, w_br, w_out, ln_g, ln_b, tm=512, tn=512, tm2=512):
    T, D = hb.shape
    ybs = pl.BlockSpec((tm, BRANCH_W), lambda j, i: (i, 0))
    merged = pl.pallas_call(
        _merge_gate_kernel,
        grid=(D // tn, T // tm),
        in_specs=[pl.BlockSpec((tm, D), lambda j, i: (i, 0)), ybs, ybs, ybs, ybs,
                  pl.BlockSpec((N_BRANCH, D, tn), lambda j, i: (0, 0, j)),
                  pl.BlockSpec((N_BRANCH, 1, tn), lambda j, i: (0, 0, j)),
                  pl.BlockSpec((N_BRANCH, BRANCH_W, tn), lambda j, i: (0, 0, j))],
        out_specs=pl.BlockSpec((tm, tn), lambda j, i: (i, j)),
        out_shape=jax.ShapeDtypeStruct((T, D), BF16),
        compiler_params=_cp("parallel", "parallel"),
        name="merge_gate",
    )(hb, *branches, w_gate.astype(BF16), b_gate.reshape(N_BRANCH, 1, D), w_br.astype(BF16))
    const = lambda i: (0, 0)
    rows = lambda i: (i, 0)
    return pl.pallas_call(
        _out_ln_kernel,
        grid=(T // tm2,),
        in_specs=[pl.BlockSpec((tm2, D), rows), pl.BlockSpec((D, D), const), pl.BlockSpec((tm2, D), rows),
                  pl.BlockSpec((1, D), const), pl.BlockSpec((1, D), const)],
        out_specs=[pl.BlockSpec((tm2, D), rows), pl.BlockSpec((tm2 * ROW_SUB, LANES), rows)],
        out_shape=[jax.ShapeDtypeStruct((T, D), F32), jax.ShapeDtypeStruct((T * ROW_SUB, LANES), ROW_DT)],
        compiler_params=_cp("parallel"),
        name="out_proj_ln1",
    )(merged, w_out.astype(BF16), hf, ln_g.reshape(1, D), ln_b.reshape(1, D))


R_TM = 512
COMBINE_TB = 256


def _router_kernel(h_ref, whi_ref, wlo_ref, b_ref, eid_ref, wts_ref, rank_ref, cnt_ref, carry_ref):
    i = pl.program_id(0)

    @pl.when(i == 0)
    def _():
        carry_ref[...] = jnp.zeros(carry_ref.shape, F32)

    x = h_ref[...]
    tm = x.shape[0]
    xh = x.astype(BF16)
    xl = (x - xh.astype(F32)).astype(BF16)
    whi = whi_ref[...]
    logits = _dot(xh, whi) + _dot(xh, wlo_ref[...]) + _dot(xl, whi) + b_ref[...]
    lane = lax.broadcasted_iota(jnp.int32, (tm, LANES), 1)
    big = jnp.int32(4 * LANES)
    is_g = (lane >= N_EXPERTS) & (lane < N_EXPERTS + N_GROUPS)
    lg = jnp.where(is_g, logits, NEG_INF)
    gmax = jnp.max(lg, axis=-1, keepdims=True)
    gidx = jnp.min(jnp.where(lg == gmax, lane - N_EXPERTS, big), axis=-1, keepdims=True)
    g_w = 1.0 / jnp.sum(jnp.where(is_g, jnp.exp(lg - gmax), 0.0), axis=-1, keepdims=True)
    in_grp = (lane < N_EXPERTS) & ((lane // EXP_PER_GROUP) == gidx)
    le = jnp.where(in_grp, logits, NEG_INF)
    e1 = jnp.max(le, axis=-1, keepdims=True)
    i1 = jnp.min(jnp.where(le == e1, lane, big), axis=-1, keepdims=True)
    le2 = jnp.where(lane == i1, NEG_INF, le)
    e2 = jnp.max(le2, axis=-1, keepdims=True)
    i2 = jnp.min(jnp.where(le2 == e2, lane, big), axis=-1, keepdims=True)
    zsum = jnp.sum(jnp.where(in_grp, jnp.exp(le - e1), 0.0), axis=-1, keepdims=True)
    p1 = 1.0 / zsum
    p2 = jnp.exp(e2 - e1) / zsum
    w1 = g_w * p1 / (p1 + p2)
    w2 = g_w * p2 / (p1 + p2)
    oh1 = lane == i1
    oh2 = lane == i2
    ohs = (oh1 | oh2).astype(BF16)
    row = lax.broadcasted_iota(jnp.int32, (tm, tm), 0)
    col = lax.broadcasted_iota(jnp.int32, (tm, tm), 1)
    before = _dot((row > col).astype(BF16), ohs) + carry_ref[0:1, :]
    r1 = jnp.sum(jnp.where(oh1, before, 0.0), axis=-1, keepdims=True)
    r2 = jnp.sum(jnp.where(oh2, before, 0.0), axis=-1, keepdims=True)
    total = carry_ref[0:1, :] + jnp.sum(ohs.astype(F32), axis=0, keepdims=True)
    carry_ref[...] = jnp.broadcast_to(total, carry_ref.shape)
    cnt_ref[...] = jnp.broadcast_to(total, cnt_ref.shape).astype(jnp.int32)
    eid_ref[...] = jnp.where(lane == 0, i1, jnp.where(lane == 1, i2, 0))
    wts_ref[...] = jnp.where(lane == 0, w1, jnp.where(lane == 1, w2, 0.0))
    rank_ref[...] = jnp.where(lane == 0, r1, jnp.where(lane == 1, r2, 0.0)).astype(jnp.int32)


ROW_SUB = D_MODEL // LANES // 2
ROW_DT = jnp.uint32


def _row_slab(ref, r):
    return ref.at[pl.ds(pl.multiple_of(r * ROW_SUB, ROW_SUB), ROW_SUB)]


def _col_block(n, c):
    return pl.ds(c, n, stride=ROW_SUB)


def _pack_cols(x, c):
    as_bits = lambda t: lax.bitcast_convert_type(t.astype(BF16).astype(F32), ROW_DT)
    lo = as_bits(x[:, c * LANES:(c + 1) * LANES])
    hi = as_bits(x[:, (c + ROW_SUB) * LANES:(c + ROW_SUB + 1) * LANES])
    return (lo >> 16) | hi


def _unpack_cols(w):
    return (lax.bitcast_convert_type(w << 16, F32),
            lax.bitcast_convert_type(w & jnp.uint32(0xFFFF0000), F32))


EXPERT_AHEAD = 2


def _expert_kernel(be_ref, succ_ref, *refs, layer):
    src_refs = refs[:EXPERT_AHEAD + 1]
    (h_hbm, wg_hbm, wu_hbm, wd_hbm, o_ref, xbuf_ref, sg_ref, su_ref, sd_ref, wgb_ref, wub_ref, wdb_ref,
     stage_ref, gsem, wsem) = refs[EXPERT_AHEAD + 1:]
    i = pl.program_id(0)
    n_used = be_ref[pl.num_programs(0)]
    nbuf = EXPERT_AHEAD + 1
    slot = i % nbuf
    expert = be_ref[i]
    weights = ((wg_hbm, sg_ref, wgb_ref), (wu_hbm, su_ref, wub_ref), (wd_hbm, sd_ref, wdb_ref))

    def weight_copy(w, e, s):
        hbm, st_ref, _ = weights[w]
        return pltpu.make_async_copy(hbm.at[layer, e], st_ref.at[s], wsem.at[s, w])

    @pl.when(i == 0)
    def _():
        stage_ref[0] = 0
        for w in range(len(weights)):
            weight_copy(w, expert, 0).start()

    @pl.when((i < n_used) & ((i == 0) | (expert != be_ref[jnp.maximum(i - 1, 0)])))
    def _():
        s = stage_ref[0]
        nxt = succ_ref[expert]
        for w, (_, st_ref, wb_ref) in enumerate(weights):
            weight_copy(w, 0, s).wait()
            wb_ref[...] = st_ref[s].astype(BF16)

        @pl.when(nxt >= 0)
        def _():
            for w in range(len(weights)):
                weight_copy(w, nxt, 1 - s).start()

        stage_ref[0] = 1 - s

    def gather_copy(tok, to_slot, t):
        return pltpu.make_async_copy(_row_slab(h_hbm, tok), _row_slab(xbuf_ref.at[to_slot], t), gsem.at[to_slot])

    def gather(idx_ref, to_slot):
        for t in range(MOE_BLOCK):
            gather_copy(idx_ref[t], to_slot, t).start(priority=t % 2)

    @pl.when(i == 0)
    def _():
        for a in range(EXPERT_AHEAD):
            gather(src_refs[a], a)

    def drain(which):
        for t in range(MOE_BLOCK):
            gather_copy(0, which, 0).wait()

    @pl.when(i < n_used)
    def _():
        gather(src_refs[EXPERT_AHEAD], (i + EXPERT_AHEAD) % nbuf)
        drain(slot)
        x_ref = xbuf_ref.at[slot]
        halves = [_unpack_cols(x_ref[_col_block(MOE_BLOCK, c), :]) for c in range(ROW_SUB)]
        x = jnp.concatenate([lo for lo, _ in halves] + [hi for _, hi in halves], axis=1).astype(BF16)
        hid = (_silu(_dot(x, wgb_ref[...])) * _dot(x, wub_ref[...])).astype(BF16)
        y = _dot(hid, wdb_ref[...])
        for c in range(ROW_SUB):
            o_ref[_col_block(MOE_BLOCK, c), :] = _pack_cols(y, c)

        @pl.when(i == n_used - 1)
        def _():
            for a in range(1, EXPERT_AHEAD + 1):
                drain((i + a) % nbuf)

    @pl.when(i >= n_used)
    def _():
        o_ref[...] = jnp.zeros(o_ref.shape, o_ref.dtype)


def _combine_kernel(dest_ref, dnext_ref, h_ref, w_ref, g_ref, b_ref, yrows_hbm, of_ref, ob_ref,
                    ybuf_ref, acc_ref, sem):
    i = pl.program_id(0)
    n = pl.num_programs(0)
    slot = i % 2

    def row_copy(d, to_slot, k, t):
        return pltpu.make_async_copy(_row_slab(yrows_hbm, d), _row_slab(ybuf_ref.at[to_slot, k], t), sem.at[to_slot])

    def gather(d_ref, to_slot):
        def start(t, carry):
            for k in range(TOP_K):
                row_copy(d_ref[TOP_K * t + k], to_slot, k, t).start(priority=k)
            return carry

        lax.fori_loop(0, COMBINE_TB, start, 0, unroll=8)

    @pl.when(i == 0)
    def _():
        gather(dest_ref, 0)

    @pl.when(i + 1 < n)
    def _():
        gather(dnext_ref, 1 - slot)

    for t in range(COMBINE_TB):
        for k in range(TOP_K):
            row_copy(0, slot, k, 0).wait()
    w = w_ref[...]
    y0_ref = ybuf_ref.at[slot, 0]
    y1_ref = ybuf_ref.at[slot, 1]
    for c in range(ROW_SUB):
        cb = _col_block(COMBINE_TB, c)
        lo0, hi0 = _unpack_cols(y0_ref[cb, :])
        lo1, hi1 = _unpack_cols(y1_ref[cb, :])
        acc_ref[:, c * LANES:(c + 1) * LANES] = lo0 * w[:, 0:1] + lo1 * w[:, 1:2]
        acc_ref[:, (c + ROW_SUB) * LANES:(c + ROW_SUB + 1) * LANES] = hi0 * w[:, 0:1] + hi1 * w[:, 1:2]
    y = _layernorm_rows(ALPHA * h_ref[...] + acc_ref[...], g_ref[...], b_ref[...])
    of_ref[...] = y
    ob_ref[...] = y.astype(BF16)


def _moe_layer(hf, hp, w_rg, b_rg, w_re, b_re, w_e_gate, w_e_up, w_e_down, layer, ln_g, ln_b):
    T, D = hf.shape
    n_rows = T * TOP_K + N_EXPERTS * MOE_BLOCK
    n_blocks = n_rows // MOE_BLOCK
    w_r = jnp.concatenate([w_re, w_rg, jnp.zeros((D, LANES - N_EXPERTS - N_GROUPS), F32)], axis=-1)
    b_r = jnp.concatenate([b_re, b_rg, jnp.zeros((LANES - N_EXPERTS - N_GROUPS,), F32)]).reshape(1, LANES)
    w_hi = w_r.astype(BF16)
    w_lo = (w_r - w_hi.astype(F32)).astype(BF16)
    const = lambda i: (0, 0)
    rows = lambda i: (i, 0)
    eid, wts, rank, cnt = pl.pallas_call(
        _router_kernel,
        grid=(T // R_TM,),
        in_specs=[pl.BlockSpec((R_TM, D), rows), pl.BlockSpec((D, LANES), const),
                  pl.BlockSpec((D, LANES), const), pl.BlockSpec((1, LANES), const)],
        out_specs=[pl.BlockSpec((R_TM, LANES), rows), pl.BlockSpec((R_TM, LANES), rows),
                   pl.BlockSpec((R_TM, LANES), rows), pl.BlockSpec((8, LANES), const)],
        out_shape=[jax.ShapeDtypeStruct((T, LANES), jnp.int32), jax.ShapeDtypeStruct((T, LANES), F32),
                   jax.ShapeDtypeStruct((T, LANES), jnp.int32), jax.ShapeDtypeStruct((8, LANES), jnp.int32)],
        scratch_shapes=[pltpu.VMEM((8, LANES), F32)],
        compiler_params=_cp("arbitrary"),
        name="moe_router",
    )(hf, w_hi, w_lo, b_r)
    counts = cnt[0, :N_EXPERTS]
    padded = (counts + MOE_BLOCK - 1) // MOE_BLOCK * MOE_BLOCK
    pends = jnp.cumsum(padded)
    pstarts = pends - padded
    blk_start = jnp.arange(n_blocks, dtype=jnp.int32) * MOE_BLOCK
    blk_exp = jnp.minimum(jnp.sum((pends[None, :] <= blk_start[:, None]).astype(jnp.int32), axis=1), N_EXPERTS - 1)
    sel = eid[:, :TOP_K, None] == jnp.arange(N_EXPERTS, dtype=jnp.int32)
    dest = (jnp.sum(jnp.where(sel, pstarts, 0), axis=-1) + rank[:, :TOP_K]).astype(jnp.int32).reshape(T * TOP_K)
    flat = jnp.full((n_rows,), -1, jnp.int32).at[dest].set(jnp.arange(T * TOP_K, dtype=jnp.int32),
                                                           unique_indices=True)
    row_tok = jnp.where(flat < 0, 0, flat // TOP_K)
    assert n_blocks > EXPERT_AHEAD
    any_spec = pl.BlockSpec(memory_space=pl.ANY)
    idx_spec = lambda f: pl.BlockSpec((MOE_BLOCK,), f, memory_space=pltpu.SMEM)
    ids = jnp.arange(N_EXPERTS, dtype=jnp.int32)
    later = (ids[None, :] > ids[:, None]) & (counts[None, :] > 0)
    succ = jnp.min(jnp.where(later, ids[None, :], N_EXPERTS), axis=1)
    succ = jnp.where(succ == N_EXPERTS, -1, succ).astype(jnp.int32)
    yrows = pl.pallas_call(
        functools.partial(_expert_kernel, layer=layer),
        grid_spec=pltpu.PrefetchScalarGridSpec(
            num_scalar_prefetch=2,
            grid=(n_blocks,),
            in_specs=[idx_spec(lambda i, *_, a=a: (jnp.minimum(i + a, n_blocks - 1),)) for a in range(EXPERT_AHEAD + 1)]
            + [any_spec] * 4,
            out_specs=pl.BlockSpec((MOE_BLOCK * ROW_SUB, LANES), lambda i, *_: (i, 0)),
            scratch_shapes=[pltpu.VMEM((EXPERT_AHEAD + 1, MOE_BLOCK * ROW_SUB, LANES), ROW_DT),
                            pltpu.VMEM((2, D, D_FF), F32), pltpu.VMEM((2, D, D_FF), F32), pltpu.VMEM((2, D_FF, D), F32),
                            pltpu.VMEM((D, D_FF), BF16), pltpu.VMEM((D, D_FF), BF16), pltpu.VMEM((D_FF, D), BF16),
                            pltpu.SMEM((1,), jnp.int32),
                            pltpu.SemaphoreType.DMA((EXPERT_AHEAD + 1,)), pltpu.SemaphoreType.DMA((2, 3))]),
        out_shape=jax.ShapeDtypeStruct((n_rows * ROW_SUB, LANES), ROW_DT),
        compiler_params=_cp("arbitrary"),
        name="moe_experts",
    )(jnp.concatenate([blk_exp, pends[-1:] // MOE_BLOCK]).astype(jnp.int32), succ,
      *([row_tok] * (EXPERT_AHEAD + 1)), hp, w_e_gate, w_e_up, w_e_down)
    n_steps = T // COMBINE_TB
    dspec = lambda f: pl.BlockSpec((TOP_K * COMBINE_TB,), f, memory_space=pltpu.SMEM)
    return pl.pallas_call(
        _combine_kernel,
        grid=(n_steps,),
        in_specs=[dspec(lambda i: (i,)), dspec(lambda i: (jnp.minimum(i + 1, n_steps - 1),)),
                  pl.BlockSpec((COMBINE_TB, D), rows), pl.BlockSpec((COMBINE_TB, LANES), rows),
                  pl.BlockSpec((1, D), const), pl.BlockSpec((1, D), const), any_spec],
        out_specs=[pl.BlockSpec((COMBINE_TB, D), rows), pl.BlockSpec((COMBINE_TB, D), rows)],
        out_shape=[jax.ShapeDtypeStruct((T, D), F32), jax.ShapeDtypeStruct((T, D), BF16)],
        scratch_shapes=[pltpu.VMEM((2, TOP_K, COMBINE_TB * ROW_SUB, LANES), ROW_DT),
                        pltpu.VMEM((COMBINE_TB, D), F32), pltpu.SemaphoreType.DMA((2,))],
        compiler_params=_cp("arbitrary"),
        name="moe_combine_ln2",
    )(dest, dest, hf, wts, ln_g.reshape(1, D), ln_b.reshape(1, D), yrows)


def kernel(x, ln_in_g, ln_in_b, rel_bias, w_in, g_cq, w_uq, g_ckv, w_ukv, w_dw_c, b_dw_c, ln_c_g, ln_c_b,
           w_conv_d, b_conv_d, a_log_f, a_log_b, dt_bias_f, dt_bias_b, d_skip, g_norm_d, w_br, w_gate, b_gate,
           w_out, ln1_g, ln1_b, w_rg, b_rg, w_re, b_re, w_e_gate, w_e_up, w_e_down, ln2_g, ln2_b):
    B, S, D = x.shape
    T = B * S
    hf, hb = _layernorm(x.reshape(T, D), ln_in_g, ln_in_b)
    a_bias = _mixer_a_bias(rel_bias, S)
    for l in range(DEPTH):
        w_a, w_r = _in_proj_weights(w_in[l])
        qkv = _matmul(hb, w_a.astype(BF16), F32, 512, 3 * A_WIDTH, "in_proj_a")
        rest = _matmul(hb, w_r.astype(BF16), F32, 512, R_WIDTH // 2, "in_proj_rest")
        y_a = _mixer_a(qkv, a_bias, B, S)
        y_b = _mixer_b(rest, g_cq[l], w_uq[l], g_ckv[l], w_ukv[l], B, S)
        y_c = _mixer_c(rest, w_dw_c[l], b_dw_c[l], ln_c_g[l], ln_c_b[l], B, S)
        y_d = _mixer_d(rest, w_conv_d[l], b_conv_d[l], a_log_f[l], a_log_b[l], dt_bias_f[l], dt_bias_b[l],
                       d_skip[l], g_norm_d[l], B, S)
        h1f, h1p = _merge(hb, hf, (y_a, y_b, y_c, y_d), w_gate[l], b_gate[l], w_br[l], w_out[l], ln1_g[l], ln1_b[l])
        hf, hb = _moe_layer(h1f, h1p, w_rg[l], b_rg[l], w_re[l], b_re[l], w_e_gate, w_e_up, w_e_down, l,
                            ln2_g[l], ln2_b[l])
    return hf.reshape(B, S, D)
```

```python
import functools

import numpy as np
import jax
import jax.numpy as jnp
from jax import lax
from jax.experimental import pallas as pl
from jax.experimental.pallas import tpu as pltpu

F32 = jnp.float32
BF16 = jnp.bfloat16

D_MODEL = 2048
DEPTH = 2
A_HEADS = 8
A_HEAD_DIM = 64
A_WIDTH = A_HEADS * A_HEAD_DIM
A_PATTERNS = ((128, 1), (512, 4), (2048, 16))
A_BAND = 64
REL_BUCKETS = 32
REL_MAX_DIST = 1024
B_HEADS = 8
B_NOPE = 64
B_ROPE = 32
B_V = 64
B_Q_LORA = 512
B_KV_LORA = 256
ROPE_THETA = 10000.0
C_CH = 512
C_KERNEL = 31
D_HEADS = 8
D_HEAD_DIM = 64
D_INNER = D_HEADS * D_HEAD_DIM
D_STATE = 128
D_GROUPS = 2
D_CONV = 5
D_CHUNK = 128
N_BRANCH = 4
BRANCH_W = 512
N_GROUPS = 4
EXP_PER_GROUP = 8
N_EXPERTS = N_GROUPS * EXP_PER_GROUP
TOP_K = 2
D_FF = 512
MOE_BLOCK = 256
ALPHA = (2 * DEPTH) ** 0.25
EPS = 1e-5
NEG_INF = -1e30

LANES = 128
R_GLU, R_XBC, R_CQ, R_Z, R_CKV, R_KR, R_DT = 0, 1024, 2048, 2560, 3072, 3328, 3456
R_WIDTH = 3584
VMEM_LIMIT = 56 * 1024 * 1024


def _cp(*sem):
    return pltpu.CompilerParams(dimension_semantics=sem, vmem_limit_bytes=VMEM_LIMIT)


def _dot(a, b):
    return jnp.dot(a, b, preferred_element_type=F32)


def _dot_nt(a, b):
    return lax.dot_general(a, b, (((1,), (1,)), ((), ())), preferred_element_type=F32)


def _split3(x):
    hi = x.astype(BF16)
    r1 = x - hi.astype(F32)
    mid = r1.astype(BF16)
    lo = (r1 - mid.astype(F32)).astype(BF16)
    return hi, mid, lo


def _layernorm_rows(x, g, b):
    mu = jnp.mean(x, axis=-1, keepdims=True)
    xc = x - mu
    var = jnp.mean(xc * xc, axis=-1, keepdims=True)
    return xc * lax.rsqrt(var + EPS) * g + b


def _sigmoid(x):
    return 1.0 / (1.0 + jnp.exp(-x))


def _silu(x):
    return x * _sigmoid(x)


def _ln_kernel(x_ref, g_ref, b_ref, of_ref, ob_ref):
    y = _layernorm_rows(x_ref[...], g_ref[...], b_ref[...])
    of_ref[...] = y
    ob_ref[...] = y.astype(BF16)


def _layernorm(x, g, b, tm=256):
    T, D = x.shape
    return pl.pallas_call(
        _ln_kernel,
        grid=(T // tm,),
        in_specs=[pl.BlockSpec((tm, D), lambda i: (i, 0)),
                  pl.BlockSpec((1, D), lambda i: (0, 0)),
                  pl.BlockSpec((1, D), lambda i: (0, 0))],
        out_specs=[pl.BlockSpec((tm, D), lambda i: (i, 0)),
                   pl.BlockSpec((tm, D), lambda i: (i, 0))],
        out_shape=[jax.ShapeDtypeStruct((T, D), F32), jax.ShapeDtypeStruct((T, D), BF16)],
        compiler_params=_cp("parallel"),
        name="ln_in",
    )(x, g.reshape(1, D), b.reshape(1, D))


def _mm_kernel(x_ref, w_ref, o_ref):
    o_ref[...] = _dot(x_ref[...], w_ref[...]).astype(o_ref.dtype)


def _matmul(x, w, out_dtype, tm, tn, name):
    M, K = x.shape
    N = w.shape[1]
    return pl.pallas_call(
        _mm_kernel,
        grid=(N // tn, M // tm),
        in_specs=[pl.BlockSpec((tm, K), lambda j, i: (i, 0)),
                  pl.BlockSpec((K, tn), lambda j, i: (0, j))],
        out_specs=pl.BlockSpec((tm, tn), lambda j, i: (i, j)),
        out_shape=jax.ShapeDtypeStruct((M, N), out_dtype),
        compiler_params=_cp("parallel", "parallel"),
        name=name,
    )(x, w)


def _t5_bucket(rel):
    half = REL_BUCKETS // 2
    max_exact = half // 2
    n = np.abs(rel)
    large = max_exact + (np.log(np.maximum(n, 1) / max_exact) / np.log(REL_MAX_DIST / max_exact)
                         * (half - max_exact)).astype(np.int32)
    large = np.minimum(large, half - 1)
    return (rel > 0).astype(np.int32) * half + np.where(n < max_exact, n, large)


def _a_window(L):
    return min(2 * LANES, L)


def _a_bias_tiles(rel_bias, d, L):
    W = _a_window(L)
    offs = (0,) if L == LANES else (0, -A_BAND, -2 * A_BAND)
    qi = np.arange(LANES)[:, None]
    kj = np.arange(W)[None, :]
    rel = np.stack([kj - qi + off for off in offs], axis=0)
    valid = np.abs(rel) <= A_BAND
    onehot = (jnp.asarray(_t5_bucket(rel * d), jnp.int32)[..., None] == jnp.arange(REL_BUCKETS)).astype(F32)
    b = jnp.einsum('vqkb,bh->vhqk', onehot, rel_bias.astype(F32), precision=lax.Precision.HIGHEST)
    b = jnp.where(valid[:, None], b, NEG_INF)
    return b.reshape(len(offs), A_HEADS // 2, 2 * LANES, W)


A_GROUP = 2


def _attn_a_kernel(q_ref, k_ref, v_ref, b16_ref, b4_ref, b1_ref, y_ref,
                   q4_ref, k4_ref, v4_ref, m_ref, l_ref, acc_ref, tmp_ref, *, S):
    (_, d1), (_, d4), (_, d16) = A_PATTERNS
    lane = lax.broadcasted_iota(jnp.int32, (LANES, LANES), 1)
    head0 = lane < A_HEAD_DIM
    scale = A_HEAD_DIM ** -0.5

    def partial_softmax(qs, ks, vs, bias_ref):
        q2 = jnp.concatenate([jnp.where(head0, qs, 0.0), jnp.where(head0, 0.0, qs)], axis=0).astype(BF16)
        s = _dot_nt(q2, ks.astype(BF16)) * scale + bias_ref[...]
        m = jnp.max(s, axis=-1, keepdims=True)
        p = jnp.exp(s - m).astype(BF16)
        num = _dot(p, vs.astype(BF16))
        den = _dot(p, jnp.ones((vs.shape[0], LANES), BF16))
        both = lambda t: jnp.where(head0, t[:LANES], t[LANES:])
        return both(jnp.broadcast_to(m, (2 * LANES, LANES))), both(den), both(num)

    def fold(old, new):
        (m_old, l_old, a_old), (m_new, l_new, a_new) = old, new
        m = jnp.maximum(m_old, m_new)
        c_old = jnp.exp(m_old - m)
        c_new = jnp.exp(m_new - m)
        return m, c_old * l_old + c_new * l_new, c_old * a_old + c_new * a_new

    stat_refs = (m_ref, l_ref, acc_ref)

    def get(c, rows):
        return tuple(ref.at[c][rows, :] for ref in stat_refs)

    def put(c, rows, stats):
        for ref, val in zip(stat_refs, stats):
            ref.at[c][rows, :] = val

    def grouped(n, group, unit):
        def trip(g, carry):
            pending = [unit(g * group + u) for u in range(group)]
            for finish in pending:
                finish()
            return carry

        lax.fori_loop(0, n // group, trip, 0)

    L4 = S // d4
    sub = d16 // d4
    assert S // d16 == LANES and sub == d4
    for c in range(d4):
        cls = pl.ds(c, L4, stride=d4)
        q4_ref[c] = q_ref[cls, :]
        k4_ref[c] = k_ref[cls, :]
        v4_ref[c] = v_ref[cls, :]

    def unit16(t):
        c = t % d4
        rows = pl.ds(t // d4, LANES, stride=sub)
        stats = partial_softmax(q4_ref.at[c][rows, :], k4_ref.at[c][rows, :], v4_ref.at[c][rows, :], b16_ref.at[0])
        return lambda: put(c, rows, stats)

    grouped(d16, A_GROUP, unit16)

    def window(i, L):
        nqb = L // LANES
        ws = pl.multiple_of(jnp.clip(i * LANES - A_BAND, 0, L - 2 * LANES), A_BAND)
        return pl.ds(ws, 2 * LANES), jnp.where(i == 0, 0, jnp.where(i == nqb - 1, 2, 1))

    def unit4(t):
        c = t % d4
        i = t // d4
        keys, var = window(i, L4)
        rows = pl.ds(pl.multiple_of(i * LANES, LANES), LANES)
        stats = partial_softmax(q4_ref.at[c][rows, :], k4_ref.at[c][keys, :], v4_ref.at[c][keys, :], b4_ref.at[var])
        return lambda: put(c, rows, fold(get(c, rows), stats))

    grouped(d4 * (L4 // LANES), A_GROUP, unit4)

    def unit1(i):
        keys, var = window(i, S)
        rows = pl.ds(pl.multiple_of(i * LANES, LANES), LANES)
        stats = partial_softmax(q_ref[rows, :], k_ref[keys, :], v_ref[keys, :], b1_ref.at[var])

        def finish():
            part = pl.ds(pl.multiple_of(i * (LANES // d4), LANES // d4), LANES // d4)
            for n, ref in enumerate(stat_refs):
                for c in range(d4):
                    tmp_ref.at[n][pl.ds(c, LANES // d4, stride=d4), :] = ref.at[c][part, :]
            _, l, a = fold(tuple(tmp_ref[n] for n in range(len(stat_refs))), stats)
            y_ref[rows, :] = (a / l).astype(y_ref.dtype)

        return finish

    grouped(S // LANES, A_GROUP, unit1)


def _mixer_a_bias(rel_bias, S):
    return tuple(_a_bias_tiles(rel_bias, d, S // d) for _, d in A_PATTERNS)


def _mixer_a(qkv, bias, B, S):
    b1, b4, b16 = bias
    npair = A_HEADS // 2
    pair_bias = lambda t: pl.BlockSpec((t.shape[0], None) + t.shape[2:], lambda b, hp: (0, hp, 0, 0))
    slab = lambda first: pl.BlockSpec((S, LANES), lambda b, hp: (b, first + hp))
    return pl.pallas_call(
        functools.partial(_attn_a_kernel, S=S),
        grid=(B, npair),
        in_specs=[slab(0), slab(npair), slab(2 * npair), pair_bias(b16), pair_bias(b4), pair_bias(b1)],
        out_specs=pl.BlockSpec((S, LANES), lambda b, hp: (b, hp)),
        out_shape=jax.ShapeDtypeStruct((B * S, A_WIDTH), BF16),
        scratch_shapes=[pltpu.VMEM((A_PATTERNS[1][1], S // A_PATTERNS[1][1], LANES), F32)] * 6
        + [pltpu.VMEM((3, LANES, LANES), F32)],
        compiler_params=_cp("parallel", "parallel"),
        name="attn_a",
    )(qkv, qkv, qkv, b16, b4, b1)


MLA_Q_SCALE = float((B_NOPE + B_ROPE) ** -0.5 * np.log2(np.e))


def _mla_proj_kernel(cq_ref, ckv_ref, kr_ref, gq_ref, gkv_ref, wqm_ref, wqs_ref, wk_ref, wv_ref, vone_ref,
                     ek_ref, cosq_ref, sinq_ref, csk_ref, q_ref, k_ref, v_ref):
    cq = cq_ref[...]
    xq = (cq * lax.rsqrt(jnp.mean(cq * cq, axis=-1, keepdims=True) + EPS) * gq_ref[...]).astype(BF16)
    ckv = ckv_ref[...]
    xkv = (ckv * lax.rsqrt(jnp.mean(ckv * ckv, axis=-1, keepdims=True) + EPS) * gkv_ref[...]).astype(BF16)
    qm = _dot(xq, wqm_ref[...])
    qs = _dot(xq, wqs_ref[...])
    cosq = cosq_ref[...] * MLA_Q_SCALE
    sinq = sinq_ref[...] * MLA_Q_SCALE
    t = kr_ref[...] * csk_ref[...]
    t_hi = t.astype(BF16)
    t_lo = (t - t_hi.astype(F32)).astype(BF16)
    kk = _dot(xkv, wk_ref[...]) + _dot(t_hi, ek_ref[...]) + _dot(t_lo, ek_ref[...])
    for h in range(B_HEADS):
        sl = slice(h * LANES, (h + 1) * LANES)
        q_ref[:, sl] = (qm[:, sl] * cosq + qs[:, sl] * sinq).astype(BF16)
    k_ref[...] = kk.astype(BF16)
    v_ref[...] = (_dot(xkv, wv_ref[...]) + vone_ref[...]).astype(BF16)


def _mla_attn_kernel(q_ref, k_ref, v_ref, o_ref):
    outs = []
    for hh in range(2):
        sl = slice(hh * LANES, (hh + 1) * LANES)
        s = _dot_nt(q_ref[:, sl], k_ref[:, sl])
        p = jnp.exp2(s - jnp.max(s, axis=-1, keepdims=True))
        outs.append(_dot(p.astype(BF16), v_ref[:, sl]))
    lane = lax.broadcasted_iota(jnp.int32, outs[0].shape, 1)
    acc = jnp.where(lane < B_V, outs[0], outs[1])
    den = pltpu.roll(jnp.where(lane < B_V, outs[1], outs[0]), B_V, axis=1)
    o_ref[...] = (acc / den).astype(o_ref.dtype)


def _mla_tables(S):
    inv_freq = ROPE_THETA ** (-jnp.arange(0, B_ROPE, 2, dtype=F32) / B_ROPE)
    ang = jnp.arange(S, dtype=F32)[:, None] * inv_freq[None]
    cos, sin = jnp.cos(ang), jnp.sin(ang)
    cos2 = jnp.concatenate([cos, cos], axis=-1)
    sin2 = jnp.concatenate([sin, sin], axis=-1)
    ones = jnp.ones((S, B_NOPE), F32)
    zn = jnp.zeros((S, B_NOPE), F32)
    zp = jnp.zeros((S, LANES - B_NOPE - B_ROPE), F32)
    cosq = jnp.concatenate([ones, cos2, zp], axis=-1)
    sinq = jnp.concatenate([zn, sin2, zp], axis=-1)
    csk = jnp.concatenate([cos2, sin2, jnp.zeros((S, LANES - 2 * B_ROPE), F32)], axis=-1)
    return cosq, sinq, csk


def _swap_cols(w):
    half = w.shape[-1] // 2
    return jnp.concatenate([-w[..., half:], w[..., :half]], axis=-1)


def _mla_weights(w_uq, w_ukv):
    dq = B_NOPE + B_ROPE
    wq = w_uq.reshape(B_Q_LORA, B_HEADS, dq)
    zpad = jnp.zeros((B_Q_LORA, B_HEADS, LANES - dq), F32)
    wqm = jnp.concatenate([wq, zpad], axis=-1).reshape(B_Q_LORA, B_HEADS * LANES)
    wqs = jnp.concatenate([jnp.zeros((B_Q_LORA, B_HEADS, B_NOPE), F32), _swap_cols(wq[..., B_NOPE:]), zpad],
                          axis=-1).reshape(B_Q_LORA, B_HEADS * LANES)
    wkv = w_ukv.reshape(B_KV_LORA, B_HEADS, B_NOPE + B_V)
    wk = jnp.concatenate([wkv[..., :B_NOPE], jnp.zeros((B_KV_LORA, B_HEADS, LANES - B_NOPE), F32)],
                         axis=-1).reshape(B_KV_LORA, B_HEADS * LANES)
    zv = jnp.zeros((B_KV_LORA, B_HEADS // 2, LANES - B_V), F32)
    wv_h = wkv[..., B_NOPE:]
    wv = jnp.stack([jnp.concatenate([wv_h[:, 0::2], zv], axis=-1),
                    jnp.concatenate([zv, wv_h[:, 1::2]], axis=-1)], axis=2).reshape(B_KV_LORA, B_HEADS * LANES)
    lane_in_pair = np.arange(B_HEADS * LANES) % (2 * LANES)
    vone = jnp.asarray(((lane_in_pair >= B_V) & (lane_in_pair < LANES + B_V)).astype(np.float32)).reshape(1, -1)
    ek = np.zeros((LANES, B_HEADS, LANES), np.float32)
    for j in range(B_ROPE):
        ek[j, :, B_NOPE + j] = 1.0
        ek[B_ROPE + j, :, B_NOPE + j] = 1.0
    ek = jnp.asarray(ek.reshape(LANES, B_HEADS * LANES))
    return wqm.astype(BF16), wqs.astype(BF16), wk.astype(BF16), wv.astype(BF16), vone, ek.astype(BF16)


def _mixer_b(rest, g_cq, w_uq, g_ckv, w_ukv, B, S, tm=512, tq=256):
    T = B * S
    wqm, wqs, wk, wv, vone, ek = _mla_weights(w_uq, w_ukv)
    cosq, sinq, csk = _mla_tables(S)
    nst = S // tm
    QW = B_HEADS * LANES
    const = lambda i: (0, 0)
    pos = lambda i: (i % nst, 0)
    q, k, v = pl.pallas_call(
        _mla_proj_kernel,
        grid=(T // tm,),
        in_specs=[pl.BlockSpec((tm, B_Q_LORA), lambda i: (i, R_CQ // B_Q_LORA)),
                  pl.BlockSpec((tm, B_KV_LORA), lambda i: (i, R_CKV // B_KV_LORA)),
                  pl.BlockSpec((tm, LANES), lambda i: (i, R_KR // LANES)),
                  pl.BlockSpec((1, B_Q_LORA), const),
                  pl.BlockSpec((1, B_KV_LORA), const),
                  pl.BlockSpec((B_Q_LORA, QW), const),
                  pl.BlockSpec((B_Q_LORA, QW), const),
                  pl.BlockSpec((B_KV_LORA, QW), const),
                  pl.BlockSpec((B_KV_LORA, QW), const),
                  pl.BlockSpec((1, QW), const),
                  pl.BlockSpec((LANES, QW), const),
                  pl.BlockSpec((tm, LANES), pos),
                  pl.BlockSpec((tm, LANES), pos),
                  pl.BlockSpec((tm, LANES), pos)],
        out_specs=[pl.BlockSpec((tm, QW), lambda i: (i, 0)),
                   pl.BlockSpec((tm, QW), lambda i: (i, 0)),
                   pl.BlockSpec((tm, QW), lambda i: (i, 0))],
        out_shape=[jax.ShapeDtypeStruct((T, QW), BF16), jax.ShapeDtypeStruct((T, QW), BF16),
                   jax.ShapeDtypeStruct((T, QW), BF16)],
        compiler_params=_cp("parallel"),
        name="mla_proj",
    )(rest, rest, rest, g_cq.reshape(1, -1), g_ckv.reshape(1, -1), wqm, wqs, wk, wv, vone, ek, cosq, sinq, csk)
    y = pl.pallas_call(
        _mla_attn_kernel,
        grid=(B, B_HEADS // 2, S // tq),
        in_specs=[pl.BlockSpec((None, tq, 2 * LANES), lambda b, hp, i: (b, i, hp)),
                  pl.BlockSpec((None, S, 2 * LANES), lambda b, hp, i: (b, 0, hp)),
                  pl.BlockSpec((None, S, 2 * LANES), lambda b, hp, i: (b, 0, hp))],
        out_specs=pl.BlockSpec((None, tq, 2 * B_V), lambda b, hp, i: (b, i, hp)),
        out_shape=jax.ShapeDtypeStruct((B, S, B_HEADS * B_V), BF16),
        compiler_params=_cp("parallel", "parallel", "arbitrary"),
        name="mla_attn",
    )(q.reshape(B, S, QW), k.reshape(B, S, QW), v.reshape(B, S, QW))
    return y.reshape(T, B_HEADS * B_V)


C_PAD = 16
C_ROWS = 128


SUBLANES = 8


def _tap_span(first, ntaps, rows):
    return rows + ((first + ntaps - 1) // SUBLANES) * SUBLANES


def _depthwise_taps(win_ref, sh_ref, w_ref, bias, ls, first, ntaps, rows):
    acc = jnp.broadcast_to(bias, (rows, LANES))
    span = _tap_span(first, ntaps, rows)
    for ph in range(SUBLANES):
        taps = [j for j in range(ntaps) if (first + j) % SUBLANES == ph]
        if not taps:
            continue
        if len(taps) == 1:
            j = taps[0]
            acc = acc + w_ref[j:j + 1, ls] * win_ref[first + j:first + j + rows, ls]
            continue
        sh_ref[0:span, :] = win_ref[ph:ph + span, ls]
        for j in taps:
            a = (first + j) // SUBLANES * SUBLANES
            acc = acc + w_ref[j:j + 1, ls] * sh_ref[a:a + rows, :]
    return acc


def _fill_window(win_ref, load_rows, r0, i, nblk, rows, pad):
    width = win_ref.shape[1]
    win_ref[pad:pad + rows, :] = load_rows(r0, rows)

    @pl.when(i > 0)
    def _():
        win_ref[0:pad, :] = load_rows(r0 - pad, pad)

    @pl.when(i == 0)
    def _():
        win_ref[0:pad, :] = jnp.zeros((pad, width), F32)

    @pl.when(i < nblk - 1)
    def _():
        win_ref[pad + rows:pad + rows + pad, :] = load_rows(r0 + rows, pad)

    @pl.when(i == nblk - 1)
    def _():
        win_ref[pad + rows:pad + rows + pad, :] = jnp.zeros((pad, width), F32)


def _conv_c_kernel(glu_ref, w_ref, b_ref, g_ref, beta_ref, o_ref, win_ref, sh_ref, acc_ref, *, S):
    i = pl.program_id(1)
    r0 = pl.multiple_of(i * C_ROWS, C_ROWS)

    def glu_rows(start, n):
        rs = pl.ds(pl.multiple_of(start, C_PAD), n)
        return glu_ref[rs, 0:C_CH] * _sigmoid(glu_ref[rs, C_CH:2 * C_CH])

    _fill_window(win_ref, glu_rows, r0, i, S // C_ROWS, C_ROWS, C_PAD)
    first = C_PAD - C_KERNEL // 2
    for lb in range(C_CH // LANES):
        ls = slice(lb * LANES, (lb + 1) * LANES)
        acc_ref[:, ls] = _depthwise_taps(win_ref, sh_ref, w_ref, b_ref[:, ls], ls, first, C_KERNEL, C_ROWS)
    y = _layernorm_rows(acc_ref[...], g_ref[...], beta_ref[...])
    o_ref[...] = _silu(y).astype(o_ref.dtype)


def _mixer_c(rest, w_dw, b_dw, ln_g, ln_b, B, S):
    T = B * S
    const = lambda b, i: (0, 0)
    y = pl.pallas_call(
        functools.partial(_conv_c_kernel, S=S),
        grid=(B, S // C_ROWS),
        in_specs=[pl.BlockSpec((None, S, 2 * C_CH), lambda b, i: (b, 0, R_GLU // (2 * C_CH))),
                  pl.BlockSpec((C_KERNEL, C_CH), const),
                  pl.BlockSpec((1, C_CH), const),
                  pl.BlockSpec((1, C_CH), const),
                  pl.BlockSpec((1, C_CH), const)],
        out_specs=pl.BlockSpec((None, C_ROWS, C_CH), lambda b, i: (b, i, 0)),
        out_shape=jax.ShapeDtypeStruct((B, S, C_CH), BF16),
        scratch_shapes=[pltpu.VMEM((C_ROWS + 2 * C_PAD, C_CH), F32),
                        pltpu.VMEM((_tap_span(C_PAD - C_KERNEL // 2, C_KERNEL, C_ROWS), LANES), F32),
                        pltpu.VMEM((C_ROWS, C_CH), F32)],
        compiler_params=_cp("parallel", "parallel"),
        name="conformer_conv",
    )(rest.reshape(B, S, R_WIDTH), w_dw, b_dw.reshape(1, -1), ln_g.reshape(1, -1), ln_b.reshape(1, -1))
    return y.reshape(T, C_CH)


D_PAD = 8
XBC_W = D_INNER + 2 * D_GROUPS * D_STATE
N_PAIR = D_HEADS // 2


def _pair_expand(v, first):
    lane = lax.broadcasted_iota(jnp.int32, (v.shape[0], LANES), 1)
    lo = jnp.broadcast_to(v[:, first:first + 1], (v.shape[0], LANES))
    hi = jnp.broadcast_to(v[:, first + 1:first + 2], (v.shape[0], LANES))
    return jnp.where(lane < D_HEAD_DIM, lo, hi)


def _ssd_kernel(xbc_ref, z_ref, dt_ref, wc_ref, bc_ref, alog_ref, dtb_ref, dskip_ref, gn_ref, o_ref,
                win_ref, sh_ref, xc_ref, a_ref, dtv_ref, y_ref, st_ref, *, S):
    Q = D_CHUNK
    nchunk = S // Q
    N = D_STATE
    bm0 = D_INNER
    cm0 = D_INNER + D_GROUPS * N

    def conv_body(c, carry):
        r0 = pl.multiple_of(c * Q, Q)
        _fill_window(win_ref, lambda st, n: xbc_ref[pl.ds(pl.multiple_of(st, D_PAD), n), :], r0, c, nchunk, Q, D_PAD)
        for lb in range(XBC_W // LANES):
            ls = slice(lb * LANES, (lb + 1) * LANES)
            acc = _depthwise_taps(win_ref, sh_ref, wc_ref, bc_ref[:, ls], ls, D_PAD - D_CONV // 2, D_CONV, Q)
            xc_ref[pl.ds(r0, Q), ls] = _silu(acc)
        return carry

    lax.fori_loop(0, nchunk, conv_body, 0)

    lane1 = lax.broadcasted_iota(jnp.int32, (1, LANES), 1)
    a_row = jnp.where(lane1 < 2 * D_HEADS, -jnp.exp(alog_ref[...]), 0.0)
    xdt = dt_ref[...] + dtb_ref[...]
    dtv = jnp.maximum(xdt, 0.0) + jnp.log(1.0 + jnp.exp(-jnp.abs(xdt)))
    dtv_ref[...] = dtv
    a_ref[...] = dtv * a_row

    row = lax.broadcasted_iota(jnp.int32, (Q, Q), 0)
    col = lax.broadcasted_iota(jnp.int32, (Q, Q), 1)
    tril = row >= col
    triu = col >= row
    lane = col

    def scan_chunk(c, lower, off, finalize):
        r0 = pl.multiple_of(c * Q, Q)
        rows = pl.ds(r0, Q)
        mask = tril if lower else triu
        tri = mask.astype(BF16)
        a_hi, a_mid, a_lo = _split3(a_ref[rows, :])
        cs = _dot(tri, a_hi) + _dot(tri, a_mid) + _dot(tri, a_lo)
        cs_t = cs.T
        ecs = jnp.exp(cs)
        edge = Q - 1 if lower else 0
        edec = jnp.exp(cs[edge:edge + 1, :] - cs)
        dt_c = dtv_ref[rows, :]
        dt_t = dt_c.T
        dte_t = (dt_c * edec).T
        for g in range(D_GROUPS):
            bg = xc_ref[rows, bm0 + g * N:bm0 + (g + 1) * N]
            cg = xc_ref[rows, cm0 + g * N:cm0 + (g + 1) * N].astype(BF16)
            cb = _dot_nt(cg, bg.astype(BF16))
            bg_t = bg.T
            for pp in range(N_PAIR // D_GROUPS):
                p = g * (N_PAIR // D_GROUPS) + pp
                ps = slice(p * LANES, (p + 1) * LANES)
                x_p = xc_ref[rows, ps]
                ms, bs = [], []
                for hh in range(2):
                    k = off + 2 * p + hh
                    diff = jnp.broadcast_to(cs[:, k:k + 1], (Q, Q)) - cs_t[k:k + 1, :]
                    ms.append((jnp.exp(jnp.where(mask, diff, NEG_INF)) * (cb * dt_t[k:k + 1, :])).astype(BF16))
                    bs.append((bg_t * dte_t[k:k + 1, :]).astype(BF16))
                x_lo = jnp.where(lane < D_HEAD_DIM, x_p, 0.0).astype(BF16)
                x_hi = jnp.where(lane >= D_HEAD_DIM, x_p, 0.0).astype(BF16)
                x2 = jnp.concatenate([x_lo, x_hi], axis=0)
                y_intra = _dot(jnp.concatenate(ms, axis=1), x2)
                hp = st_ref[p]
                ecs_p = _pair_expand(ecs, off + 2 * p)
                y_new = y_intra + _dot(cg, hp.astype(BF16)) * ecs_p
                if lower:
                    y_ref[rows, ps] = y_new
                else:
                    y_ref[rows, ps] = y_ref[rows, ps] + y_new
                st_ref[p] = hp * ecs_p[edge:edge + 1, :] + _dot(jnp.concatenate(bs, axis=1), x2)
        if finalize:
            y = y_ref[rows, :] + xc_ref[rows, 0:D_INNER] * dskip_ref[...]
            gated = y * _silu(z_ref[rows, :])
            out = gated * lax.rsqrt(jnp.mean(gated * gated, axis=-1, keepdims=True) + EPS) * gn_ref[...]
            o_ref[rows, :] = out.astype(o_ref.dtype)

    st_ref[...] = jnp.zeros(st_ref.shape, F32)

    def fwd_body(c, carry):
        scan_chunk(c, True, 0, False)
        return carry

    lax.fori_loop(0, nchunk, fwd_body, 0)
    st_ref[...] = jnp.zeros(st_ref.shape, F32)

    def bwd_body(k, carry):
        scan_chunk(nchunk - 1 - k, False, D_HEADS, True)
        return carry

    lax.fori_loop(0, nchunk, bwd_body, 0)


def _mixer_d(rest, w_conv, b_conv, a_log_f, a_log_b, dt_bias_f, dt_bias_b, d_skip, g_norm, B, S):
    T = B * S
    pad16 = lambda f, b: jnp.concatenate([f, b, jnp.zeros((LANES - 2 * D_HEADS,), F32)]).reshape(1, LANES)
    const = lambda b: (0, 0)
    y = pl.pallas_call(
        functools.partial(_ssd_kernel, S=S),
        grid=(B,),
        in_specs=[pl.BlockSpec((None, S, XBC_W), lambda b: (b, 0, R_XBC // XBC_W)),
                  pl.BlockSpec((None, S, D_INNER), lambda b: (b, 0, R_Z // D_INNER)),
                  pl.BlockSpec((None, S, LANES), lambda b: (b, 0, R_DT // LANES)),
                  pl.BlockSpec((D_CONV, XBC_W), const),
                  pl.BlockSpec((1, XBC_W), const),
                  pl.BlockSpec((1, LANES), const),
                  pl.BlockSpec((1, LANES), const),
                  pl.BlockSpec((1, D_INNER), const),
                  pl.BlockSpec((1, D_INNER), const)],
        out_specs=pl.BlockSpec((None, S, D_INNER), lambda b: (b, 0, 0)),
        out_shape=jax.ShapeDtypeStruct((B, S, D_INNER), BF16),
        scratch_shapes=[pltpu.VMEM((D_CHUNK + 2 * D_PAD, XBC_W), F32),
                        pltpu.VMEM((_tap_span(D_PAD - D_CONV // 2, D_CONV, D_CHUNK), LANES), F32),
                        pltpu.VMEM((S, XBC_W), F32),
                        pltpu.VMEM((S, LANES), F32),
                        pltpu.VMEM((S, LANES), F32),
                        pltpu.VMEM((S, D_INNER), F32),
                        pltpu.VMEM((N_PAIR, D_STATE, LANES), F32)],
        compiler_params=_cp("parallel"),
        name="ssd_mixer",
    )(rest.reshape(B, S, R_WIDTH), rest.reshape(B, S, R_WIDTH), rest.reshape(B, S, R_WIDTH),
      w_conv, b_conv.reshape(1, -1), pad16(a_log_f, a_log_b), pad16(dt_bias_f, dt_bias_b),
      jnp.repeat(d_skip, D_HEAD_DIM).reshape(1, -1), g_norm.reshape(1, -1))
    return y.reshape(T, D_INNER)


def _in_proj_weights(w_in_l):
    o = np.cumsum((0, A_WIDTH, A_WIDTH, A_WIDTH, B_Q_LORA, B_KV_LORA, B_ROPE, 2 * C_CH,
                   D_INNER, D_INNER, D_GROUPS * D_STATE, D_GROUPS * D_STATE, 2 * D_HEADS)).tolist()
    seg = lambda n: w_in_l[:, o[n]:o[n + 1]]
    w_a = w_in_l[:, :o[3]]
    cq, ckv, kr, glu, z, xs, bm, cm, dt = (seg(n) for n in range(3, 12))
    zeros = lambda n: jnp.zeros((w_in_l.shape[0], n), w_in_l.dtype)
    w_r = jnp.concatenate([glu, xs, bm, cm, cq, z, ckv,
                           kr, _swap_cols(kr), zeros(LANES - 2 * B_ROPE),
                           dt, zeros(LANES - 2 * D_HEADS)], axis=-1)
    assert w_r.shape[1] == R_WIDTH
    return w_a, w_r


def _merge_gate_kernel(h_ref, ya_ref, yb_ref, yc_ref, yd_ref, wg_ref, bg_ref, wbr_ref, o_ref):
    h = h_ref[...]
    acc = None
    for i, y_ref in enumerate((ya_ref, yb_ref, yc_ref, yd_ref)):
        gate = _sigmoid(_dot(h, wg_ref[i]) + bg_ref[i])
        term = gate * _dot(y_ref[...], wbr_ref[i])
        acc = term if acc is None else acc + term
    o_ref[...] = acc.astype(o_ref.dtype)


def _out_ln_kernel(m_ref, w_ref, h_ref, g_ref, b_ref, of_ref, op_ref):
    y = _layernorm_rows(ALPHA * h_ref[...] + _dot(m_ref[...], w_ref[...]), g_ref[...], b_ref[...])
    of_ref[...] = y
    for c in range(ROW_SUB):
        op_ref[_col_block(y.shape[0], c), :] = _pack_cols(y, c)


def _merge(hb, hf, branches, w_gate, b_gate, w_br, w_out, ln_g, ln_b, tm=512, tn=512, tm2=512):
    T, D = hb.shape
    ybs = pl.BlockSpec((tm, BRANCH_W), lambda j, i: (i, 0))
    merged = pl.pallas_call(
        _merge_gate_kernel,
        grid=(D // tn, T // tm),
        in_specs=[pl.BlockSpec((tm, D), lambda j, i: (i, 0)), ybs, ybs, ybs, ybs,
                  pl.BlockSpec((N_BRANCH, D, tn), lambda j, i: (0, 0, j)),
                  pl.BlockSpec((N_BRANCH, 1, tn), lambda j, i: (0, 0, j)),
                  pl.BlockSpec((N_BRANCH, BRANCH_W, tn), lambda j, i: (0, 0, j))],
        out_specs=pl.BlockSpec((tm, tn), lambda j, i: (i, j)),
        out_shape=jax.ShapeDtypeStruct((T, D), BF16),
        compiler_params=_cp("parallel", "parallel"),
        name="merge_gate",
    )(hb, *branches, w_gate.astype(BF16), b_gate.reshape(N_BRANCH, 1, D), w_br.astype(BF16))
    const = lambda i: (0, 0)
    rows = lambda i: (i, 0)
    return pl.pallas_call(
        _out_ln_kernel,
        grid=(T // tm2,),
        in_specs=[pl.BlockSpec((tm2, D), rows), pl.BlockSpec((D, D), const), pl.BlockSpec((tm2, D), rows),
                  pl.BlockSpec((1, D), const), pl.BlockSpec((1, D), const)],
        out_specs=[pl.BlockSpec((tm2, D), rows), pl.BlockSpec((tm2 * ROW_SUB, LANES), rows)],
        out_shape=[jax.ShapeDtypeStruct((T, D), F32), jax.ShapeDtypeStruct((T * ROW_SUB, LANES), ROW_DT)],
        compiler_params=_cp("parallel"),
        name="out_proj_ln1",
    )(merged, w_out.astype(BF16), hf, ln_g.reshape(1, D), ln_b.reshape(1, D))


R_TM = 512
COMBINE_TB = 256


def _router_kernel(h_ref, whi_ref, wlo_ref, b_ref, eid_ref, wts_ref, rank_ref, cnt_ref, carry_ref):
    i = pl.program_id(0)

    @pl.when(i == 0)
    def _():
        carry_ref[...] = jnp.zeros(carry_ref.shape, F32)

    x = h_ref[...]
    tm = x.shape[0]
    xh = x.astype(BF16)
    xl = (x - xh.astype(F32)).astype(BF16)
    whi = whi_ref[...]
    logits = _dot(xh, whi) + _dot(xh, wlo_ref[...]) + _dot(xl, whi) + b_ref[...]
    lane = lax.broadcasted_iota(jnp.int32, (tm, LANES), 1)
    big = jnp.int32(4 * LANES)
    is_g = (lane >= N_EXPERTS) & (lane < N_EXPERTS + N_GROUPS)
    lg = jnp.where(is_g, logits, NEG_INF)
    gmax = jnp.max(lg, axis=-1, keepdims=True)
    gidx = jnp.min(jnp.where(lg == gmax, lane - N_EXPERTS, big), axis=-1, keepdims=True)
    g_w = 1.0 / jnp.sum(jnp.where(is_g, jnp.exp(lg - gmax), 0.0), axis=-1, keepdims=True)
    in_grp = (lane < N_EXPERTS) & ((lane // EXP_PER_GROUP) == gidx)
    le = jnp.where(in_grp, logits, NEG_INF)
    e1 = jnp.max(le, axis=-1, keepdims=True)
    i1 = jnp.min(jnp.where(le == e1, lane, big), axis=-1, keepdims=True)
    le2 = jnp.where(lane == i1, NEG_INF, le)
    e2 = jnp.max(le2, axis=-1, keepdims=True)
    i2 = jnp.min(jnp.where(le2 == e2, lane, big), axis=-1, keepdims=True)
    zsum = jnp.sum(jnp.where(in_grp, jnp.exp(le - e1), 0.0), axis=-1, keepdims=True)
    p1 = 1.0 / zsum
    p2 = jnp.exp(e2 - e1) / zsum
    w1 = g_w * p1 / (p1 + p2)
    w2 = g_w * p2 / (p1 + p2)
    oh1 = lane == i1
    oh2 = lane == i2
    ohs = (oh1 | oh2).astype(BF16)
    row = lax.broadcasted_iota(jnp.int32, (tm, tm), 0)
    col = lax.broadcasted_iota(jnp.int32, (tm, tm), 1)
    before = _dot((row > col).astype(BF16), ohs) + carry_ref[0:1, :]
    r1 = jnp.sum(jnp.where(oh1, before, 0.0), axis=-1, keepdims=True)
    r2 = jnp.sum(jnp.where(oh2, before, 0.0), axis=-1, keepdims=True)
    total = carry_ref[0:1, :] + jnp.sum(ohs.astype(F32), axis=0, keepdims=True)
    carry_ref[...] = jnp.broadcast_to(total, carry_ref.shape)
    cnt_ref[...] = jnp.broadcast_to(total, cnt_ref.shape).astype(jnp.int32)
    eid_ref[...] = jnp.where(lane == 0, i1, jnp.where(lane == 1, i2, 0))
    wts_ref[...] = jnp.where(lane == 0, w1, jnp.where(lane == 1, w2, 0.0))
    rank_ref[...] = jnp.where(lane == 0, r1, jnp.where(lane == 1, r2, 0.0)).astype(jnp.int32)


ROW_SUB = D_MODEL // LANES // 2
ROW_DT = jnp.uint32


def _row_slab(ref, r):
    return ref.at[pl.ds(pl.multiple_of(r * ROW_SUB, ROW_SUB), ROW_SUB)]


def _col_block(n, c):
    return pl.ds(c, n, stride=ROW_SUB)


def _pack_cols(x, c):
    as_bits = lambda t: lax.bitcast_convert_type(t.astype(BF16).astype(F32), ROW_DT)
    lo = as_bits(x[:, c * LANES:(c + 1) * LANES])
    hi = as_bits(x[:, (c + ROW_SUB) * LANES:(c + ROW_SUB + 1) * LANES])
    return (lo >> 16) | hi


def _unpack_cols(w):
    return (lax.bitcast_convert_type(w << 16, F32),
            lax.bitcast_convert_type(w & jnp.uint32(0xFFFF0000), F32))


EXPERT_AHEAD = 2


def _expert_kernel(be_ref, succ_ref, *refs, layer):
    src_refs = refs[:EXPERT_AHEAD + 1]
    (h_hbm, wg_hbm, wu_hbm, wd_hbm, o_ref, xbuf_ref, sg_ref, su_ref, sd_ref, wgb_ref, wub_ref, wdb_ref,
     stage_ref, gsem, wsem) = refs[EXPERT_AHEAD + 1:]
    i = pl.program_id(0)
    n_used = be_ref[pl.num_programs(0)]
    nbuf = EXPERT_AHEAD + 1
    slot = i % nbuf
    expert = be_ref[i]
    weights = ((wg_hbm, sg_ref, wgb_ref), (wu_hbm, su_ref, wub_ref), (wd_hbm, sd_ref, wdb_ref))

    def weight_copy(w, e, s):
        hbm, st_ref, _ = weights[w]
        return pltpu.make_async_copy(hbm.at[layer, e], st_ref.at[s], wsem.at[s, w])

    @pl.when(i == 0)
    def _():
        stage_ref[0] = 0
        for w in range(len(weights)):
            weight_copy(w, expert, 0).start()

    @pl.when((i < n_used) & ((i == 0) | (expert != be_ref[jnp.maximum(i - 1, 0)])))
    def _():
        s = stage_ref[0]
        nxt = succ_ref[expert]
        for w, (_, st_ref, wb_ref) in enumerate(weights):
            weight_copy(w, 0, s).wait()
            wb_ref[...] = st_ref[s].astype(BF16)

        @pl.when(nxt >= 0)
        def _():
            for w in range(len(weights)):
                weight_copy(w, nxt, 1 - s).start()

        stage_ref[0] = 1 - s

    def gather_copy(src_row, to_slot, t):
        src = h_hbm.at[pl.ds(pl.multiple_of(src_row, ROW_SUB), ROW_SUB)]
        return pltpu.make_async_copy(src, _row_slab(xbuf_ref.at[to_slot], t), gsem.at[to_slot])

    def gather(idx_ref, to_slot):
        for t in range(MOE_BLOCK):
            gather_copy(idx_ref[t], to_slot, t).start(priority=t % 2)

    @pl.when(i == 0)
    def _():
        for a in range(EXPERT_AHEAD):
            gather(src_refs[a], a)

    @pl.when(i + EXPERT_AHEAD < n_used)
    def _():
        gather(src_refs[EXPERT_AHEAD], (i + EXPERT_AHEAD) % nbuf)

    @pl.when(i < n_used)
    def _():
        for t in range(MOE_BLOCK):
            gather_copy(0, slot, 0).wait()
        x_ref = xbuf_ref.at[slot]
        halves = [_unpack_cols(x_ref[_col_block(MOE_BLOCK, c), :]) for c in range(ROW_SUB)]
        x = jnp.concatenate([lo for lo, _ in halves] + [hi for _, hi in halves], axis=1).astype(BF16)
        hid = (_silu(_dot(x, wgb_ref[...])) * _dot(x, wub_ref[...])).astype(BF16)
        y = _dot(hid, wdb_ref[...])
        for c in range(ROW_SUB):
            o_ref[_col_block(MOE_BLOCK, c), :] = _pack_cols(y, c)

    @pl.when(i >= n_used)
    def _():
        o_ref[...] = jnp.zeros(o_ref.shape, o_ref.dtype)


def _combine_kernel(dest_ref, dnext_ref, h_ref, w_ref, g_ref, b_ref, yrows_hbm, of_ref, ob_ref,
                    ybuf_ref, acc_ref, sem):
    i = pl.program_id(0)
    n = pl.num_programs(0)
    slot = i % 2

    def row_copy(d, to_slot, k, t):
        return pltpu.make_async_copy(_row_slab(yrows_hbm, d), _row_slab(ybuf_ref.at[to_slot, k], t), sem.at[to_slot])

    def gather(d_ref, to_slot):
        def start(t, carry):
            for k in range(TOP_K):
                row_copy(d_ref[TOP_K * t + k], to_slot, k, t).start(priority=k)
            return carry

        lax.fori_loop(0, COMBINE_TB, start, 0, unroll=8)

    @pl.when(i == 0)
    def _():
        gather(dest_ref, 0)

    @pl.when(i + 1 < n)
    def _():
        gather(dnext_ref, 1 - slot)

    for t in range(COMBINE_TB):
        for k in range(TOP_K):
            row_copy(0, slot, k, 0).wait()
    w = w_ref[...]
    y0_ref = ybuf_ref.at[slot, 0]
    y1_ref = ybuf_ref.at[slot, 1]
    for c in range(ROW_SUB):
        cb = _col_block(COMBINE_TB, c)
        lo0, hi0 = _unpack_cols(y0_ref[cb, :])
        lo1, hi1 = _unpack_cols(y1_ref[cb, :])
        acc_ref[:, c * LANES:(c + 1) * LANES] = lo0 * w[:, 0:1] + lo1 * w[:, 1:2]
        acc_ref[:, (c + ROW_SUB) * LANES:(c + ROW_SUB + 1) * LANES] = hi0 * w[:, 0:1] + hi1 * w[:, 1:2]
    y = _layernorm_rows(ALPHA * h_ref[...] + acc_ref[...], g_ref[...], b_ref[...])
    of_ref[...] = y
    ob_ref[...] = y.astype(BF16)


def _moe_layer(hf, hp, w_rg, b_rg, w_re, b_re, w_e_gate, w_e_up, w_e_down, layer, ln_g, ln_b):
    T, D = hf.shape
    n_rows = T * TOP_K + N_EXPERTS * MOE_BLOCK
    n_blocks = n_rows // MOE_BLOCK
    w_r = jnp.concatenate([w_re, w_rg, jnp.zeros((D, LANES - N_EXPERTS - N_GROUPS), F32)], axis=-1)
    b_r = jnp.concatenate([b_re, b_rg, jnp.zeros((LANES - N_EXPERTS - N_GROUPS,), F32)]).reshape(1, LANES)
    w_hi = w_r.astype(BF16)
    w_lo = (w_r - w_hi.astype(F32)).astype(BF16)
    const = lambda i: (0, 0)
    rows = lambda i: (i, 0)
    eid, wts, rank, cnt = pl.pallas_call(
        _router_kernel,
        grid=(T // R_TM,),
        in_specs=[pl.BlockSpec((R_TM, D), rows), pl.BlockSpec((D, LANES), const),
                  pl.BlockSpec((D, LANES), const), pl.BlockSpec((1, LANES), const)],
        out_specs=[pl.BlockSpec((R_TM, LANES), rows), pl.BlockSpec((R_TM, LANES), rows),
                   pl.BlockSpec((R_TM, LANES), rows), pl.BlockSpec((8, LANES), const)],
        out_shape=[jax.ShapeDtypeStruct((T, LANES), jnp.int32), jax.ShapeDtypeStruct((T, LANES), F32),
                   jax.ShapeDtypeStruct((T, LANES), jnp.int32), jax.ShapeDtypeStruct((8, LANES), jnp.int32)],
        scratch_shapes=[pltpu.VMEM((8, LANES), F32)],
        compiler_params=_cp("arbitrary"),
        name="moe_router",
    )(hf, w_hi, w_lo, b_r)
    counts = cnt[0, :N_EXPERTS]
    padded = (counts + MOE_BLOCK - 1) // MOE_BLOCK * MOE_BLOCK
    pends = jnp.cumsum(padded)
    pstarts = pends - padded
    blk_start = jnp.arange(n_blocks, dtype=jnp.int32) * MOE_BLOCK
    blk_exp = jnp.minimum(jnp.sum((pends[None, :] <= blk_start[:, None]).astype(jnp.int32), axis=1), N_EXPERTS - 1)
    sel = eid[:, :TOP_K, None] == jnp.arange(N_EXPERTS, dtype=jnp.int32)
    dest = (jnp.sum(jnp.where(sel, pstarts, 0), axis=-1) + rank[:, :TOP_K]).astype(jnp.int32).reshape(T * TOP_K)
    flat = jnp.full((n_rows,), -1, jnp.int32).at[dest].set(jnp.arange(T * TOP_K, dtype=jnp.int32),
                                                           unique_indices=True)
    row_tok = jnp.where(flat < 0, 0, flat // TOP_K) * ROW_SUB
    assert n_blocks > EXPERT_AHEAD
    any_spec = pl.BlockSpec(memory_space=pl.ANY)
    idx_spec = lambda f: pl.BlockSpec((MOE_BLOCK,), f, memory_space=pltpu.SMEM)
    ids = jnp.arange(N_EXPERTS, dtype=jnp.int32)
    later = (ids[None, :] > ids[:, None]) & (counts[None, :] > 0)
    succ = jnp.min(jnp.where(later, ids[None, :], N_EXPERTS), axis=1)
    succ = jnp.where(succ == N_EXPERTS, -1, succ).astype(jnp.int32)
    yrows = pl.pallas_call(
        functools.partial(_expert_kernel, layer=layer),
        grid_spec=pltpu.PrefetchScalarGridSpec(
            num_scalar_prefetch=2,
            grid=(n_blocks,),
            in_specs=[idx_spec(lambda i, *_, a=a: (jnp.minimum(i + a, n_blocks - 1),)) for a in range(EXPERT_AHEAD + 1)]
            + [any_spec] * 4,
            out_specs=pl.BlockSpec((MOE_BLOCK * ROW_SUB, LANES), lambda i, *_: (i, 0)),
            scratch_shapes=[pltpu.VMEM((EXPERT_AHEAD + 1, MOE_BLOCK * ROW_SUB, LANES), ROW_DT),
                            pltpu.VMEM((2, D, D_FF), F32), pltpu.VMEM((2, D, D_FF), F32), pltpu.VMEM((2, D_FF, D), F32),
                            pltpu.VMEM((D, D_FF), BF16), pltpu.VMEM((D, D_FF), BF16), pltpu.VMEM((D_FF, D), BF16),
                            pltpu.SMEM((1,), jnp.int32),
                            pltpu.SemaphoreType.DMA((EXPERT_AHEAD + 1,)), pltpu.SemaphoreType.DMA((2, 3))]),
        out_shape=jax.ShapeDtypeStruct((n_rows * ROW_SUB, LANES), ROW_DT),
        compiler_params=_cp("arbitrary"),
        name="moe_experts",
    )(jnp.concatenate([blk_exp, pends[-1:] // MOE_BLOCK]).astype(jnp.int32), succ,
      *([row_tok] * (EXPERT_AHEAD + 1)), hp, w_e_gate, w_e_up, w_e_down)
    n_steps = T // COMBINE_TB
    dspec = lambda f: pl.BlockSpec((TOP_K * COMBINE_TB,), f, memory_space=pltpu.SMEM)
    return pl.pallas_call(
        _combine_kernel,
        grid=(n_steps,),
        in_specs=[dspec(lambda i: (i,)), dspec(lambda i: (jnp.minimum(i + 1, n_steps - 1),)),
                  pl.BlockSpec((COMBINE_TB, D), rows), pl.BlockSpec((COMBINE_TB, LANES), rows),
                  pl.BlockSpec((1, D), const), pl.BlockSpec((1, D), const), any_spec],
        out_specs=[pl.BlockSpec((COMBINE_TB, D), rows), pl.BlockSpec((COMBINE_TB, D), rows)],
        out_shape=[jax.ShapeDtypeStruct((T, D), F32), jax.ShapeDtypeStruct((T, D), BF16)],
        scratch_shapes=[pltpu.VMEM((2, TOP_K, COMBINE_TB * ROW_SUB, LANES), ROW_DT),
                        pltpu.VMEM((COMBINE_TB, D), F32), pltpu.SemaphoreType.DMA((2,))],
        compiler_params=_cp("arbitrary"),
        name="moe_combine_ln2",
    )(dest, dest, hf, wts, ln_g.reshape(1, D), ln_b.reshape(1, D), yrows)


def kernel(x, ln_in_g, ln_in_b, rel_bias, w_in, g_cq, w_uq, g_ckv, w_ukv, w_dw_c, b_dw_c, ln_c_g, ln_c_b,
           w_conv_d, b_conv_d, a_log_f, a_log_b, dt_bias_f, dt_bias_b, d_skip, g_norm_d, w_br, w_gate, b_gate,
           w_out, ln1_g, ln1_b, w_rg, b_rg, w_re, b_re, w_e_gate, w_e_up, w_e_down, ln2_g, ln2_b):
    B, S, D = x.shape
    T = B * S
    hf, hb = _layernorm(x.reshape(T, D), ln_in_g, ln_in_b)
    a_bias = _mixer_a_bias(rel_bias, S)
    for l in range(DEPTH):
        w_a, w_r = _in_proj_weights(w_in[l])
        qkv = _matmul(hb, w_a.astype(BF16), F32, 512, 3 * A_WIDTH, "in_proj_a")
        rest = _matmul(hb, w_r.astype(BF16), F32, 512, R_WIDTH // 2, "in_proj_rest")
        y_a = _mixer_a(qkv, a_bias, B, S)
        y_b = _mixer_b(rest, g_cq[l], w_uq[l], g_ckv[l], w_ukv[l], B, S)
        y_c = _mixer_c(rest, w_dw_c[l], b_dw_c[l], ln_c_g[l], ln_c_b[l], B, S)
        y_d = _mixer_d(rest, w_conv_d[l], b_conv_d[l], a_log_f[l], a_log_b[l], dt_bias_f[l], dt_bias_b[l],
                       d_skip[l], g_norm_d[l], B, S)
        h1f, h1p = _merge(hb, hf, (y_a, y_b, y_c, y_d), w_gate[l], b_gate[l], w_br[l], w_out[l], ln1_g[l], ln1_b[l])
        hf, hb = _moe_layer(h1f, h1p, w_rg[l], b_rg[l], w_re[l], b_re[l], w_e_gate, w_e_up, w_e_down, l,
                            ln2_g[l], ln2_b[l])
    return hf.reshape(B, S, D)
```

```python
import functools

import numpy as np
import jax
import jax.numpy as jnp
from jax import lax
from jax.experimental import pallas as pl
from jax.experimental.pallas import tpu as pltpu

F32 = jnp.float32
BF16 = jnp.bfloat16

D_MODEL = 2048
DEPTH = 2
A_HEADS = 8
A_HEAD_DIM = 64
A_WIDTH = A_HEADS * A_HEAD_DIM
A_PATTERNS = ((128, 1), (512, 4), (2048, 16))
A_BAND = 64
REL_BUCKETS = 32
REL_MAX_DIST = 1024
B_HEADS = 8
B_NOPE = 64
B_ROPE = 32
B_V = 64
B_Q_LORA = 512
B_KV_LORA = 256
ROPE_THETA = 10000.0
C_CH = 512
C_KERNEL = 31
D_HEADS = 8
D_HEAD_DIM = 64
D_INNER = D_HEADS * D_HEAD_DIM
D_STATE = 128
D_GROUPS = 2
D_CONV = 5
D_CHUNK = 128
N_BRANCH = 4
BRANCH_W = 512
N_GROUPS = 4
EXP_PER_GROUP = 8
N_EXPERTS = N_GROUPS * EXP_PER_GROUP
TOP_K = 2
D_FF = 512
MOE_BLOCK = 256
ALPHA = (2 * DEPTH) ** 0.25
EPS = 1e-5
NEG_INF = -1e30

LANES = 128
R_GLU, R_XBC, R_CQ, R_Z, R_CKV, R_KR, R_DT = 0, 1024, 2048, 2560, 3072, 3328, 3456
R_WIDTH = 3584
VMEM_LIMIT = 56 * 1024 * 1024


def _cp(*sem):
    return pltpu.CompilerParams(dimension_semantics=sem, vmem_limit_bytes=VMEM_LIMIT)


def _dot(a, b):
    return jnp.dot(a, b, preferred_element_type=F32)


def _dot_nt(a, b):
    return lax.dot_general(a, b, (((1,), (1,)), ((), ())), preferred_element_type=F32)


def _split3(x):
    hi = x.astype(BF16)
    r1 = x - hi.astype(F32)
    mid = r1.astype(BF16)
    lo = (r1 - mid.astype(F32)).astype(BF16)
    return hi, mid, lo


def _layernorm_rows(x, g, b):
    mu = jnp.mean(x, axis=-1, keepdims=True)
    xc = x - mu
    var = jnp.mean(xc * xc, axis=-1, keepdims=True)
    return xc * lax.rsqrt(var + EPS) * g + b


def _sigmoid(x):
    return 1.0 / (1.0 + jnp.exp(-x))


def _silu(x):
    return x * _sigmoid(x)


def _ln_kernel(x_ref, g_ref, b_ref, of_ref, ob_ref):
    y = _layernorm_rows(x_ref[...], g_ref[...], b_ref[...])
    of_ref[...] = y
    ob_ref[...] = y.astype(BF16)


def _layernorm(x, g, b, tm=256):
    T, D = x.shape
    return pl.pallas_call(
        _ln_kernel,
        grid=(T // tm,),
        in_specs=[pl.BlockSpec((tm, D), lambda i: (i, 0)),
                  pl.BlockSpec((1, D), lambda i: (0, 0)),
                  pl.BlockSpec((1, D), lambda i: (0, 0))],
        out_specs=[pl.BlockSpec((tm, D), lambda i: (i, 0)),
                   pl.BlockSpec((tm, D), lambda i: (i, 0))],
        out_shape=[jax.ShapeDtypeStruct((T, D), F32), jax.ShapeDtypeStruct((T, D), BF16)],
        compiler_params=_cp("parallel"),
        name="ln_in",
    )(x, g.reshape(1, D), b.reshape(1, D))


def _mm_kernel(x_ref, w_ref, o_ref):
    o_ref[...] = _dot(x_ref[...], w_ref[...]).astype(o_ref.dtype)


def _matmul(x, w, out_dtype, tm, tn, name):
    M, K = x.shape
    N = w.shape[1]
    return pl.pallas_call(
        _mm_kernel,
        grid=(N // tn, M // tm),
        in_specs=[pl.BlockSpec((tm, K), lambda j, i: (i, 0)),
                  pl.BlockSpec((K, tn), lambda j, i: (0, j))],
        out_specs=pl.BlockSpec((tm, tn), lambda j, i: (i, j)),
        out_shape=jax.ShapeDtypeStruct((M, N), out_dtype),
        compiler_params=_cp("parallel", "parallel"),
        name=name,
    )(x, w)


def _t5_bucket(rel):
    half = REL_BUCKETS // 2
    max_exact = half // 2
    n = np.abs(rel)
    large = max_exact + (np.log(np.maximum(n, 1) / max_exact) / np.log(REL_MAX_DIST / max_exact)
                         * (half - max_exact)).astype(np.int32)
    large = np.minimum(large, half - 1)
    return (rel > 0).astype(np.int32) * half + np.where(n < max_exact, n, large)


def _a_window(L):
    return min(2 * LANES, L)


def _a_bias_tiles(rel_bias, d, L):
    W = _a_window(L)
    offs = (0,) if L == LANES else (0, -A_BAND, -2 * A_BAND)
    qi = np.arange(LANES)[:, None]
    kj = np.arange(W)[None, :]
    rel = np.stack([kj - qi + off for off in offs], axis=0)
    valid = np.abs(rel) <= A_BAND
    onehot = (jnp.asarray(_t5_bucket(rel * d), jnp.int32)[..., None] == jnp.arange(REL_BUCKETS)).astype(F32)
    b = jnp.einsum('vqkb,bh->vhqk', onehot, rel_bias.astype(F32), precision=lax.Precision.HIGHEST)
    b = jnp.where(valid[:, None], b, NEG_INF)
    return b.reshape(len(offs), A_HEADS // 2, 2 * LANES, W)


A_GROUP = 2


def _attn_a_kernel(q_ref, k_ref, v_ref, b16_ref, b4_ref, b1_ref, y_ref,
                   q4_ref, k4_ref, v4_ref, m_ref, l_ref, acc_ref, tmp_ref, *, S):
    (_, d1), (_, d4), (_, d16) = A_PATTERNS
    lane = lax.broadcasted_iota(jnp.int32, (LANES, LANES), 1)
    head0 = lane < A_HEAD_DIM
    scale = A_HEAD_DIM ** -0.5

    def partial_softmax(qs, ks, vs, bias_ref):
        q2 = jnp.concatenate([jnp.where(head0, qs, 0.0), jnp.where(head0, 0.0, qs)], axis=0).astype(BF16)
        s = _dot_nt(q2, ks.astype(BF16)) * scale + bias_ref[...]
        m = jnp.max(s, axis=-1, keepdims=True)
        p = jnp.exp(s - m).astype(BF16)
        num = _dot(p, vs.astype(BF16))
        den = _dot(p, jnp.ones((vs.shape[0], LANES), BF16))
        both = lambda t: jnp.where(head0, t[:LANES], t[LANES:])
        return both(jnp.broadcast_to(m, (2 * LANES, LANES))), both(den), both(num)

    def fold(old, new):
        (m_old, l_old, a_old), (m_new, l_new, a_new) = old, new
        m = jnp.maximum(m_old, m_new)
        c_old = jnp.exp(m_old - m)
        c_new = jnp.exp(m_new - m)
        return m, c_old * l_old + c_new * l_new, c_old * a_old + c_new * a_new

    stat_refs = (m_ref, l_ref, acc_ref)

    def get(c, rows):
        return tuple(ref.at[c][rows, :] for ref in stat_refs)

    def put(c, rows, stats):
        for ref, val in zip(stat_refs, stats):
            ref.at[c][rows, :] = val

    def grouped(n, group, unit):
        def trip(g, carry):
            pending = [unit(g * group + u) for u in range(group)]
            for finish in pending:
                finish()
            return carry

        lax.fori_loop(0, n // group, trip, 0)

    L4 = S // d4
    sub = d16 // d4
    assert S // d16 == LANES and sub == d4
    for c in range(d4):
        cls = pl.ds(c, L4, stride=d4)
        q4_ref[c] = q_ref[cls, :]
        k4_ref[c] = k_ref[cls, :]
        v4_ref[c] = v_ref[cls, :]

    def unit16(t):
        c = t % d4
        rows = pl.ds(t // d4, LANES, stride=sub)
        stats = partial_softmax(q4_ref.at[c][rows, :], k4_ref.at[c][rows, :], v4_ref.at[c][rows, :], b16_ref.at[0])
        return lambda: put(c, rows, stats)

    grouped(d16, A_GROUP, unit16)

    def window(i, L):
        nqb = L // LANES
        ws = pl.multiple_of(jnp.clip(i * LANES - A_BAND, 0, L - 2 * LANES), A_BAND)
        return pl.ds(ws, 2 * LANES), jnp.where(i == 0, 0, jnp.where(i == nqb - 1, 2, 1))

    def unit4(t):
        c = t % d4
        i = t // d4
        keys, var = window(i, L4)
        rows = pl.ds(pl.multiple_of(i * LANES, LANES), LANES)
        stats = partial_softmax(q4_ref.at[c][rows, :], k4_ref.at[c][keys, :], v4_ref.at[c][keys, :], b4_ref.at[var])
        return lambda: put(c, rows, fold(get(c, rows), stats))

    grouped(d4 * (L4 // LANES), A_GROUP, unit4)

    def unit1(i):
        keys, var = window(i, S)
        rows = pl.ds(pl.multiple_of(i * LANES, LANES), LANES)
        stats = partial_softmax(q_ref[rows, :], k_ref[keys, :], v_ref[keys, :], b1_ref.at[var])

        def finish():
            part = pl.ds(pl.multiple_of(i * (LANES // d4), LANES // d4), LANES // d4)
            for n, ref in enumerate(stat_refs):
                for c in range(d4):
                    tmp_ref.at[n][pl.ds(c, LANES // d4, stride=d4), :] = ref.at[c][part, :]
            _, l, a = fold(tuple(tmp_ref[n] for n in range(len(stat_refs))), stats)
            y_ref[rows, :] = (a / l).astype(y_ref.dtype)

        return finish

    grouped(S // LANES, A_GROUP, unit1)


def _mixer_a_bias(rel_bias, S):
    return tuple(_a_bias_tiles(rel_bias, d, S // d) for _, d in A_PATTERNS)


def _mixer_a(qkv, bias, B, S):
    b1, b4, b16 = bias
    npair = A_HEADS // 2
    pair_bias = lambda t: pl.BlockSpec((t.shape[0], None) + t.shape[2:], lambda b, hp: (0, hp, 0, 0))
    slab = lambda first: pl.BlockSpec((S, LANES), lambda b, hp: (b, first + hp))
    return pl.pallas_call(
        functools.partial(_attn_a_kernel, S=S),
        grid=(B, npair),
        in_specs=[slab(0), slab(npair), slab(2 * npair), pair_bias(b16), pair_bias(b4), pair_bias(b1)],
        out_specs=pl.BlockSpec((S, LANES), lambda b, hp: (b, hp)),
        out_shape=jax.ShapeDtypeStruct((B * S, A_WIDTH), BF16),
        scratch_shapes=[pltpu.VMEM((A_PATTERNS[1][1], S // A_PATTERNS[1][1], LANES), F32)] * 6
        + [pltpu.VMEM((3, LANES, LANES), F32)],
        compiler_params=_cp("parallel", "parallel"),
        name="attn_a",
    )(qkv, qkv, qkv, b16, b4, b1)


MLA_Q_SCALE = float((B_NOPE + B_ROPE) ** -0.5 * np.log2(np.e))


def _mla_proj_kernel(cq_ref, ckv_ref, kr_ref, gq_ref, gkv_ref, wqm_ref, wqs_ref, wk_ref, wv_ref, vone_ref,
                     ek_ref, cosq_ref, sinq_ref, csk_ref, q_ref, k_ref, v_ref):
    cq = cq_ref[...]
    xq = (cq * lax.rsqrt(jnp.mean(cq * cq, axis=-1, keepdims=True) + EPS) * gq_ref[...]).astype(BF16)
    ckv = ckv_ref[...]
    xkv = (ckv * lax.rsqrt(jnp.mean(ckv * ckv, axis=-1, keepdims=True) + EPS) * gkv_ref[...]).astype(BF16)
    qm = _dot(xq, wqm_ref[...])
    qs = _dot(xq, wqs_ref[...])
    cosq = cosq_ref[...] * MLA_Q_SCALE
    sinq = sinq_ref[...] * MLA_Q_SCALE
    t = kr_ref[...] * csk_ref[...]
    t_hi = t.astype(BF16)
    t_lo = (t - t_hi.astype(F32)).astype(BF16)
    kk = _dot(xkv, wk_ref[...]) + _dot(t_hi, ek_ref[...]) + _dot(t_lo, ek_ref[...])
    for h in range(B_HEADS):
        sl = slice(h * LANES, (h + 1) * LANES)
        q_ref[:, sl] = (qm[:, sl] * cosq + qs[:, sl] * sinq).astype(BF16)
    k_ref[...] = kk.astype(BF16)
    v_ref[...] = (_dot(xkv, wv_ref[...]) + vone_ref[...]).astype(BF16)


def _mla_attn_kernel(q_ref, k_ref, v_ref, o_ref):
    outs = []
    for hh in range(2):
        sl = slice(hh * LANES, (hh + 1) * LANES)
        s = _dot_nt(q_ref[:, sl], k_ref[:, sl])
        p = jnp.exp2(s - jnp.max(s, axis=-1, keepdims=True))
        outs.append(_dot(p.astype(BF16), v_ref[:, sl]))
    lane = lax.broadcasted_iota(jnp.int32, outs[0].shape, 1)
    acc = jnp.where(lane < B_V, outs[0], outs[1])
    den = pltpu.roll(jnp.where(lane < B_V, outs[1], outs[0]), B_V, axis=1)
    o_ref[...] = (acc / den).astype(o_ref.dtype)


def _mla_tables(S):
    inv_freq = ROPE_THETA ** (-jnp.arange(0, B_ROPE, 2, dtype=F32) / B_ROPE)
    ang = jnp.arange(S, dtype=F32)[:, None] * inv_freq[None]
    cos, sin = jnp.cos(ang), jnp.sin(ang)
    cos2 = jnp.concatenate([cos, cos], axis=-1)
    sin2 = jnp.concatenate([sin, sin], axis=-1)
    ones = jnp.ones((S, B_NOPE), F32)
    zn = jnp.zeros((S, B_NOPE), F32)
    zp = jnp.zeros((S, LANES - B_NOPE - B_ROPE), F32)
    cosq = jnp.concatenate([ones, cos2, zp], axis=-1)
    sinq = jnp.concatenate([zn, sin2, zp], axis=-1)
    csk = jnp.concatenate([cos2, sin2, jnp.zeros((S, LANES - 2 * B_ROPE), F32)], axis=-1)
    return cosq, sinq, csk


def _swap_cols(w):
    half = w.shape[-1] // 2
    return jnp.concatenate([-w[..., half:], w[..., :half]], axis=-1)


def _mla_weights(w_uq, w_ukv):
    dq = B_NOPE + B_ROPE
    wq = w_uq.reshape(B_Q_LORA, B_HEADS, dq)
    zpad = jnp.zeros((B_Q_LORA, B_HEADS, LANES - dq), F32)
    wqm = jnp.concatenate([wq, zpad], axis=-1).reshape(B_Q_LORA, B_HEADS * LANES)
    wqs = jnp.concatenate([jnp.zeros((B_Q_LORA, B_HEADS, B_NOPE), F32), _swap_cols(wq[..., B_NOPE:]), zpad],
                          axis=-1).reshape(B_Q_LORA, B_HEADS * LANES)
    wkv = w_ukv.reshape(B_KV_LORA, B_HEADS, B_NOPE + B_V)
    wk = jnp.concatenate([wkv[..., :B_NOPE], jnp.zeros((B_KV_LORA, B_HEADS, LANES - B_NOPE), F32)],
                         axis=-1).reshape(B_KV_LORA, B_HEADS * LANES)
    zv = jnp.zeros((B_KV_LORA, B_HEADS // 2, LANES - B_V), F32)
    wv_h = wkv[..., B_NOPE:]
    wv = jnp.stack([jnp.concatenate([wv_h[:, 0::2], zv], axis=-1),
                    jnp.concatenate([zv, wv_h[:, 1::2]], axis=-1)], axis=2).reshape(B_KV_LORA, B_HEADS * LANES)
    lane_in_pair = np.arange(B_HEADS * LANES) % (2 * LANES)
    vone = jnp.asarray(((lane_in_pair >= B_V) & (lane_in_pair < LANES + B_V)).astype(np.float32)).reshape(1, -1)
    ek = np.zeros((LANES, B_HEADS, LANES), np.float32)
    for j in range(B_ROPE):
        ek[j, :, B_NOPE + j] = 1.0
        ek[B_ROPE + j, :, B_NOPE + j] = 1.0
    ek = jnp.asarray(ek.reshape(LANES, B_HEADS * LANES))
    return wqm.astype(BF16), wqs.astype(BF16), wk.astype(BF16), wv.astype(BF16), vone, ek.astype(BF16)


def _mixer_b(rest, g_cq, w_uq, g_ckv, w_ukv, B, S, tm=512, tq=256):
    T = B * S
    wqm, wqs, wk, wv, vone, ek = _mla_weights(w_uq, w_ukv)
    cosq, sinq, csk = _mla_tables(S)
    nst = S // tm
    QW = B_HEADS * LANES
    const = lambda i: (0, 0)
    pos = lambda i: (i % nst, 0)
    q, k, v = pl.pallas_call(
        _mla_proj_kernel,
        grid=(T // tm,),
        in_specs=[pl.BlockSpec((tm, B_Q_LORA), lambda i: (i, R_CQ // B_Q_LORA)),
                  pl.BlockSpec((tm, B_KV_LORA), lambda i: (i, R_CKV // B_KV_LORA)),
                  pl.BlockSpec((tm, LANES), lambda i: (i, R_KR // LANES)),
                  pl.BlockSpec((1, B_Q_LORA), const),
                  pl.BlockSpec((1, B_KV_LORA), const),
                  pl.BlockSpec((B_Q_LORA, QW), const),
                  pl.BlockSpec((B_Q_LORA, QW), const),
                  pl.BlockSpec((B_KV_LORA, QW), const),
                  pl.BlockSpec((B_KV_LORA, QW), const),
                  pl.BlockSpec((1, QW), const),
                  pl.BlockSpec((LANES, QW), const),
                  pl.BlockSpec((tm, LANES), pos),
                  pl.BlockSpec((tm, LANES), pos),
                  pl.BlockSpec((tm, LANES), pos)],
        out_specs=[pl.BlockSpec((tm, QW), lambda i: (i, 0)),
                   pl.BlockSpec((tm, QW), lambda i: (i, 0)),
                   pl.BlockSpec((tm, QW), lambda i: (i, 0))],
        out_shape=[jax.ShapeDtypeStruct((T, QW), BF16), jax.ShapeDtypeStruct((T, QW), BF16),
                   jax.ShapeDtypeStruct((T, QW), BF16)],
        compiler_params=_cp("parallel"),
        name="mla_proj",
    )(rest, rest, rest, g_cq.reshape(1, -1), g_ckv.reshape(1, -1), wqm, wqs, wk, wv, vone, ek, cosq, sinq, csk)
    y = pl.pallas_call(
        _mla_attn_kernel,
        grid=(B, B_HEADS // 2, S // tq),
        in_specs=[pl.BlockSpec((None, tq, 2 * LANES), lambda b, hp, i: (b, i, hp)),
                  pl.BlockSpec((None, S, 2 * LANES), lambda b, hp, i: (b, 0, hp)),
                  pl.BlockSpec((None, S, 2 * LANES), lambda b, hp, i: (b, 0, hp))],
        out_specs=pl.BlockSpec((None, tq, 2 * B_V), lambda b, hp, i: (b, i, hp)),
        out_shape=jax.ShapeDtypeStruct((B, S, B_HEADS * B_V), BF16),
        compiler_params=_cp("parallel", "parallel", "arbitrary"),
        name="mla_attn",
    )(q.reshape(B, S, QW), k.reshape(B, S, QW), v.reshape(B, S, QW))
    return y.reshape(T, B_HEADS * B_V)


C_PAD = 16
C_ROWS = 128


SUBLANES = 8


def _tap_span(first, ntaps, rows):
    return rows + ((first + ntaps - 1) // SUBLANES) * SUBLANES


def _depthwise_taps(win_ref, sh_ref, w_ref, bias, ls, first, ntaps, rows):
    acc = jnp.broadcast_to(bias, (rows, LANES))
    span = _tap_span(first, ntaps, rows)
    for ph in range(SUBLANES):
        taps = [j for j in range(ntaps) if (first + j) % SUBLANES == ph]
        if not taps:
            continue
        if len(taps) == 1:
            j = taps[0]
            acc = acc + w_ref[j:j + 1, ls] * win_ref[first + j:first + j + rows, ls]
            continue
        sh_ref[0:span, :] = win_ref[ph:ph + span, ls]
        for j in taps:
            a = (first + j) // SUBLANES * SUBLANES
            acc = acc + w_ref[j:j + 1, ls] * sh_ref[a:a + rows, :]
    return acc


def _fill_window(win_ref, load_rows, r0, i, nblk, rows, pad):
    width = win_ref.shape[1]
    win_ref[pad:pad + rows, :] = load_rows(r0, rows)

    @pl.when(i > 0)
    def _():
        win_ref[0:pad, :] = load_rows(r0 - pad, pad)

    @pl.when(i == 0)
    def _():
        win_ref[0:pad, :] = jnp.zeros((pad, width), F32)

    @pl.when(i < nblk - 1)
    def _():
        win_ref[pad + rows:pad + rows + pad, :] = load_rows(r0 + rows, pad)

    @pl.when(i == nblk - 1)
    def _():
        win_ref[pad + rows:pad + rows + pad, :] = jnp.zeros((pad, width), F32)


def _conv_c_kernel(glu_ref, w_ref, b_ref, g_ref, beta_ref, o_ref, win_ref, sh_ref, acc_ref, *, S):
    i = pl.program_id(1)
    r0 = pl.multiple_of(i * C_ROWS, C_ROWS)

    def glu_rows(start, n):
        rs = pl.ds(pl.multiple_of(start, C_PAD), n)
        return glu_ref[rs, 0:C_CH] * _sigmoid(glu_ref[rs, C_CH:2 * C_CH])

    _fill_window(win_ref, glu_rows, r0, i, S // C_ROWS, C_ROWS, C_PAD)
    first = C_PAD - C_KERNEL // 2
    for lb in range(C_CH // LANES):
        ls = slice(lb * LANES, (lb + 1) * LANES)
        acc_ref[:, ls] = _depthwise_taps(win_ref, sh_ref, w_ref, b_ref[:, ls], ls, first, C_KERNEL, C_ROWS)
    y = _layernorm_rows(acc_ref[...], g_ref[...], beta_ref[...])
    o_ref[...] = _silu(y).astype(o_ref.dtype)


def _mixer_c(rest, w_dw, b_dw, ln_g, ln_b, B, S):
    T = B * S
    const = lambda b, i: (0, 0)
    y = pl.pallas_call(
        functools.partial(_conv_c_kernel, S=S),
        grid=(B, S // C_ROWS),
        in_specs=[pl.BlockSpec((None, S, 2 * C_CH), lambda b, i: (b, 0, R_GLU // (2 * C_CH))),
                  pl.BlockSpec((C_KERNEL, C_CH), const),
                  pl.BlockSpec((1, C_CH), const),
                  pl.BlockSpec((1, C_CH), const),
                  pl.BlockSpec((1, C_CH), const)],
        out_specs=pl.BlockSpec((None, C_ROWS, C_CH), lambda b, i: (b, i, 0)),
        out_shape=jax.ShapeDtypeStruct((B, S, C_CH), BF16),
        scratch_shapes=[pltpu.VMEM((C_ROWS + 2 * C_PAD, C_CH), F32),
                        pltpu.VMEM((_tap_span(C_PAD - C_KERNEL // 2, C_KERNEL, C_ROWS), LANES), F32),
                        pltpu.VMEM((C_ROWS, C_CH), F32)],
        compiler_params=_cp("parallel", "parallel"),
        name="conformer_conv",
    )(rest.reshape(B, S, R_WIDTH), w_dw, b_dw.reshape(1, -1), ln_g.reshape(1, -1), ln_b.reshape(1, -1))
    return y.reshape(T, C_CH)


D_PAD = 8
XBC_W = D_INNER + 2 * D_GROUPS * D_STATE
N_PAIR = D_HEADS // 2


def _pair_expand(v, first):
    lane = lax.broadcasted_iota(jnp.int32, (v.shape[0], LANES), 1)
    lo = jnp.broadcast_to(v[:, first:first + 1], (v.shape[0], LANES))
    hi = jnp.broadcast_to(v[:, first + 1:first + 2], (v.shape[0], LANES))
    return jnp.where(lane < D_HEAD_DIM, lo, hi)


def _ssd_kernel(xbc_ref, z_ref, dt_ref, wc_ref, bc_ref, alog_ref, dtb_ref, dskip_ref, gn_ref, o_ref,
                win_ref, sh_ref, xc_ref, a_ref, dtv_ref, y_ref, st_ref, *, S):
    Q = D_CHUNK
    nchunk = S // Q
    N = D_STATE
    bm0 = D_INNER
    cm0 = D_INNER + D_GROUPS * N

    def conv_body(c, carry):
        r0 = pl.multiple_of(c * Q, Q)
        _fill_window(win_ref, lambda st, n: xbc_ref[pl.ds(pl.multiple_of(st, D_PAD), n), :], r0, c, nchunk, Q, D_PAD)
        for lb in range(XBC_W // LANES):
            ls = slice(lb * LANES, (lb + 1) * LANES)
            acc = _depthwise_taps(win_ref, sh_ref, wc_ref, bc_ref[:, ls], ls, D_PAD - D_CONV // 2, D_CONV, Q)
            xc_ref[pl.ds(r0, Q), ls] = _silu(acc)
        return carry

    lax.fori_loop(0, nchunk, conv_body, 0)

    lane1 = lax.broadcasted_iota(jnp.int32, (1, LANES), 1)
    a_row = jnp.where(lane1 < 2 * D_HEADS, -jnp.exp(alog_ref[...]), 0.0)
    xdt = dt_ref[...] + dtb_ref[...]
    dtv = jnp.maximum(xdt, 0.0) + jnp.log(1.0 + jnp.exp(-jnp.abs(xdt)))
    dtv_ref[...] = dtv
    a_ref[...] = dtv * a_row

    row = lax.broadcasted_iota(jnp.int32, (Q, Q), 0)
    col = lax.broadcasted_iota(jnp.int32, (Q, Q), 1)
    tril = row >= col
    triu = col >= row
    lane = col

    def scan_chunk(c, lower, off, finalize):
        r0 = pl.multiple_of(c * Q, Q)
        rows = pl.ds(r0, Q)
        mask = tril if lower else triu
        tri = mask.astype(BF16)
        a_hi, a_mid, a_lo = _split3(a_ref[rows, :])
        cs = _dot(tri, a_hi) + _dot(tri, a_mid) + _dot(tri, a_lo)
        cs_t = cs.T
        ecs = jnp.exp(cs)
        edge = Q - 1 if lower else 0
        edec = jnp.exp(cs[edge:edge + 1, :] - cs)
        dt_c = dtv_ref[rows, :]
        dt_t = dt_c.T
        dte_t = (dt_c * edec).T
        for g in range(D_GROUPS):
            bg = xc_ref[rows, bm0 + g * N:bm0 + (g + 1) * N]
            cg = xc_ref[rows, cm0 + g * N:cm0 + (g + 1) * N].astype(BF16)
            cb = _dot_nt(cg, bg.astype(BF16))
            bg_t = bg.T
            for pp in range(N_PAIR // D_GROUPS):
                p = g * (N_PAIR // D_GROUPS) + pp
                ps = slice(p * LANES, (p + 1) * LANES)
                x_p = xc_ref[rows, ps]
                ms, bs = [], []
                for hh in range(2):
                    k = off + 2 * p + hh
                    diff = jnp.broadcast_to(cs[:, k:k + 1], (Q, Q)) - cs_t[k:k + 1, :]
                    ms.append((jnp.exp(jnp.where(mask, diff, NEG_INF)) * (cb * dt_t[k:k + 1, :])).astype(BF16))
                    bs.append((bg_t * dte_t[k:k + 1, :]).astype(BF16))
                x_lo = jnp.where(lane < D_HEAD_DIM, x_p, 0.0).astype(BF16)
                x_hi = jnp.where(lane >= D_HEAD_DIM, x_p, 0.0).astype(BF16)
                x2 = jnp.concatenate([x_lo, x_hi], axis=0)
                y_intra = _dot(jnp.concatenate(ms, axis=1), x2)
                hp = st_ref[p]
                ecs_p = _pair_expand(ecs, off + 2 * p)
                y_new = y_intra + _dot(cg, hp.astype(BF16)) * ecs_p
                if lower:
                    y_ref[rows, ps] = y_new
                else:
                    y_ref[rows, ps] = y_ref[rows, ps] + y_new
                st_ref[p] = hp * ecs_p[edge:edge + 1, :] + _dot(jnp.concatenate(bs, axis=1), x2)
        if finalize:
            y = y_ref[rows, :] + xc_ref[rows, 0:D_INNER] * dskip_ref[...]
            gated = y * _silu(z_ref[rows, :])
            out = gated * lax.rsqrt(jnp.mean(gated * gated, axis=-1, keepdims=True) + EPS) * gn_ref[...]
            o_ref[rows, :] = out.astype(o_ref.dtype)

    st_ref[...] = jnp.zeros(st_ref.shape, F32)

    def fwd_body(c, carry):
        scan_chunk(c, True, 0, False)
        return carry

    lax.fori_loop(0, nchunk, fwd_body, 0)
    st_ref[...] = jnp.zeros(st_ref.shape, F32)

    def bwd_body(k, carry):
        scan_chunk(nchunk - 1 - k, False, D_HEADS, True)
        return carry

    lax.fori_loop(0, nchunk, bwd_body, 0)


def _mixer_d(rest, w_conv, b_conv, a_log_f, a_log_b, dt_bias_f, dt_bias_b, d_skip, g_norm, B, S):
    T = B * S
    pad16 = lambda f, b: jnp.concatenate([f, b, jnp.zeros((LANES - 2 * D_HEADS,), F32)]).reshape(1, LANES)
    const = lambda b: (0, 0)
    y = pl.pallas_call(
        functools.partial(_ssd_kernel, S=S),
        grid=(B,),
        in_specs=[pl.BlockSpec((None, S, XBC_W), lambda b: (b, 0, R_XBC // XBC_W)),
                  pl.BlockSpec((None, S, D_INNER), lambda b: (b, 0, R_Z // D_INNER)),
                  pl.BlockSpec((None, S, LANES), lambda b: (b, 0, R_DT // LANES)),
                  pl.BlockSpec((D_CONV, XBC_W), const),
                  pl.BlockSpec((1, XBC_W), const),
                  pl.BlockSpec((1, LANES), const),
                  pl.BlockSpec((1, LANES), const),
                  pl.BlockSpec((1, D_INNER), const),
                  pl.BlockSpec((1, D_INNER), const)],
        out_specs=pl.BlockSpec((None, S, D_INNER), lambda b: (b, 0, 0)),
        out_shape=jax.ShapeDtypeStruct((B, S, D_INNER), BF16),
        scratch_shapes=[pltpu.VMEM((D_CHUNK + 2 * D_PAD, XBC_W), F32),
                        pltpu.VMEM((_tap_span(D_PAD - D_CONV // 2, D_CONV, D_CHUNK), LANES), F32),
                        pltpu.VMEM((S, XBC_W), F32),
                        pltpu.VMEM((S, LANES), F32),
                        pltpu.VMEM((S, LANES), F32),
                        pltpu.VMEM((S, D_INNER), F32),
                        pltpu.VMEM((N_PAIR, D_STATE, LANES), F32)],
        compiler_params=_cp("parallel"),
        name="ssd_mixer",
    )(rest.reshape(B, S, R_WIDTH), rest.reshape(B, S, R_WIDTH), rest.reshape(B, S, R_WIDTH),
      w_conv, b_conv.reshape(1, -1), pad16(a_log_f, a_log_b), pad16(dt_bias_f, dt_bias_b),
      jnp.repeat(d_skip, D_HEAD_DIM).reshape(1, -1), g_norm.reshape(1, -1))
    return y.reshape(T, D_INNER)


def _in_proj_weights(w_in_l):
    o = np.cumsum((0, A_WIDTH, A_WIDTH, A_WIDTH, B_Q_LORA, B_KV_LORA, B_ROPE, 2 * C_CH,
                   D_INNER, D_INNER, D_GROUPS * D_STATE, D_GROUPS * D_STATE, 2 * D_HEADS)).tolist()
    seg = lambda n: w_in_l[:, o[n]:o[n + 1]]
    w_a = w_in_l[:, :o[3]]
    cq, ckv, kr, glu, z, xs, bm, cm, dt = (seg(n) for n in range(3, 12))
    zeros = lambda n: jnp.zeros((w_in_l.shape[0], n), w_in_l.dtype)
    w_r = jnp.concatenate([glu, xs, bm, cm, cq, z, ckv,
                           kr, _swap_cols(kr), zeros(LANES - 2 * B_ROPE),
                           dt, zeros(LANES - 2 * D_HEADS)], axis=-1)
    assert w_r.shape[1] == R_WIDTH
    return w_a, w_r


def _merge_gate_kernel(h_ref, ya_ref, yb_ref, yc_ref, yd_ref, wg_ref, bg_ref, wbr_ref, o_ref):
    h = h_ref[...]
    acc = None
    for i, y_ref in enumerate((ya_ref, yb_ref, yc_ref, yd_ref)):
        gate = _sigmoid(_dot(h, wg_ref[i]) + bg_ref[i])
        term = gate * _dot(y_ref[...], wbr_ref[i])
        acc = term if acc is None else acc + term
    o_ref[...] = acc.astype(o_ref.dtype)


def _out_ln_kernel(m_ref, w_ref, h_ref, g_ref, b_ref, of_ref, op_ref):
    y = _layernorm_rows(ALPHA * h_ref[...] + _dot(m_ref[...], w_ref[...]), g_ref[...], b_ref[...])
    of_ref[...] = y
    for c in range(ROW_SUB):
        op_ref[_col_block(y.shape[0], c), :] = _pack_cols(y, c)


def _merge(hb, hf, branches, w_gate, b_gate, w_br, w_out, ln_g, ln_b, tm=512, tn=512, tm2=512):
    T, D = hb.shape
    ybs = pl.BlockSpec((tm, BRANCH_W), lambda j, i: (i, 0))
    merged = pl.pallas_call(
        _merge_gate_kernel,
        grid=(D // tn, T // tm),
        in_specs=[pl.BlockSpec((tm, D), lambda j, i: (i, 0)), ybs, ybs, ybs, ybs,
                  pl.BlockSpec((N_BRANCH, D, tn), lambda j, i: (0, 0, j)),
                  pl.BlockSpec((N_BRANCH, 1, tn), lambda j, i: (0, 0, j)),
                  pl.BlockSpec((N_BRANCH, BRANCH_W, tn), lambda j, i: (0, 0, j))],
        out_specs=pl.BlockSpec((tm, tn), lambda j, i: (i, j)),
        out_shape=jax.ShapeDtypeStruct((T, D), BF16),
        compiler_params=_cp("parallel", "parallel"),
        name="merge_gate",
    )(hb, *branches, w_gate.astype(BF16), b_gate.reshape(N_BRANCH, 1, D), w_br.astype(BF16))
    const = lambda i: (0, 0)
    rows = lambda i: (i, 0)
    return pl.pallas_call(
        _out_ln_kernel,
        grid=(T // tm2,),
        in_specs=[pl.BlockSpec((tm2, D), rows), pl.BlockSpec((D, D), const), pl.BlockSpec((tm2, D), rows),
                  pl.BlockSpec((1, D), const), pl.BlockSpec((1, D), const)],
        out_specs=[pl.BlockSpec((tm2, D), rows), pl.BlockSpec((tm2 * ROW_SUB, LANES), rows)],
        out_shape=[jax.ShapeDtypeStruct((T, D), F32), jax.ShapeDtypeStruct((T * ROW_SUB, LANES), ROW_DT)],
        compiler_params=_cp("parallel"),
        name="out_proj_ln1",
    )(merged, w_out.astype(BF16), hf, ln_g.reshape(1, D), ln_b.reshape(1, D))


R_TM = 512
COMBINE_TB = 256


def _router_kernel(h_ref, whi_ref, wlo_ref, b_ref, eid_ref, wts_ref, rank_ref, cnt_ref, carry_ref):
    i = pl.program_id(0)

    @pl.when(i == 0)
    def _():
        carry_ref[...] = jnp.zeros(carry_ref.shape, F32)

    x = h_ref[...]
    tm = x.shape[0]
    xh = x.astype(BF16)
    xl = (x - xh.astype(F32)).astype(BF16)
    whi = whi_ref[...]
    logits = _dot(xh, whi) + _dot(xh, wlo_ref[...]) + _dot(xl, whi) + b_ref[...]
    lane = lax.broadcasted_iota(jnp.int32, (tm, LANES), 1)
    big = jnp.int32(4 * LANES)
    is_g = (lane >= N_EXPERTS) & (lane < N_EXPERTS + N_GROUPS)
    lg = jnp.where(is_g, logits, NEG_INF)
    gmax = jnp.max(lg, axis=-1, keepdims=True)
    gidx = jnp.min(jnp.where(lg == gmax, lane - N_EXPERTS, big), axis=-1, keepdims=True)
    g_w = 1.0 / jnp.sum(jnp.where(is_g, jnp.exp(lg - gmax), 0.0), axis=-1, keepdims=True)
    in_grp = (lane < N_EXPERTS) & ((lane // EXP_PER_GROUP) == gidx)
    le = jnp.where(in_grp, logits, NEG_INF)
    e1 = jnp.max(le, axis=-1, keepdims=True)
    i1 = jnp.min(jnp.where(le == e1, lane, big), axis=-1, keepdims=True)
    le2 = jnp.where(lane == i1, NEG_INF, le)
    e2 = jnp.max(le2, axis=-1, keepdims=True)
    i2 = jnp.min(jnp.where(le2 == e2, lane, big), axis=-1, keepdims=True)
    zsum = jnp.sum(jnp.where(in_grp, jnp.exp(le - e1), 0.0), axis=-1, keepdims=True)
    p1 = 1.0 / zsum
    p2 = jnp.exp(e2 - e1) / zsum
    w1 = g_w * p1 / (p1 + p2)
    w2 = g_w * p2 / (p1 + p2)
    oh1 = lane == i1
    oh2 = lane == i2
    ohs = (oh1 | oh2).astype(BF16)
    row = lax.broadcasted_iota(jnp.int32, (tm, tm), 0)
    col = lax.broadcasted_iota(jnp.int32, (tm, tm), 1)
    before = _dot((row > col).astype(BF16), ohs) + carry_ref[0:1, :]
    r1 = jnp.sum(jnp.where(oh1, before, 0.0), axis=-1, keepdims=True)
    r2 = jnp.sum(jnp.where(oh2, before, 0.0), axis=-1, keepdims=True)
    total = carry_ref[0:1, :] + jnp.sum(ohs.astype(F32), axis=0, keepdims=True)
    carry_ref[...] = jnp.broadcast_to(total, carry_ref.shape)
    cnt_ref[...] = jnp.broadcast_to(total, cnt_ref.shape).astype(jnp.int32)
    eid_ref[...] = jnp.where(lane == 0, i1, jnp.where(lane == 1, i2, 0))
    wts_ref[...] = jnp.where(lane == 0, w1, jnp.where(lane == 1, w2, 0.0))
    rank_ref[...] = jnp.where(lane == 0, r1, jnp.where(lane == 1, r2, 0.0)).astype(jnp.int32)


ROW_SUB = D_MODEL // LANES // 2
ROW_DT = jnp.uint32


def _row_slab(ref, r):
    return ref.at[pl.ds(pl.multiple_of(r * ROW_SUB, ROW_SUB), ROW_SUB)]


def _col_block(n, c):
    return pl.ds(c, n, stride=ROW_SUB)


def _pack_cols(x, c):
    as_bits = lambda t: lax.bitcast_convert_type(t.astype(BF16).astype(F32), ROW_DT)
    lo = as_bits(x[:, c * LANES:(c + 1) * LANES])
    hi = as_bits(x[:, (c + ROW_SUB) * LANES:(c + ROW_SUB + 1) * LANES])
    return (lo >> 16) | hi


def _unpack_cols(w):
    return (lax.bitcast_convert_type(w << 16, F32),
            lax.bitcast_convert_type(w & jnp.uint32(0xFFFF0000), F32))


EXPERT_AHEAD = 2


def _expert_kernel(be_ref, succ_ref, *refs, layer):
    src_refs = refs[:EXPERT_AHEAD + 1]
    (h_hbm, wg_hbm, wu_hbm, wd_hbm, o_ref, xbuf_ref, sg_ref, su_ref, sd_ref, wgb_ref, wub_ref, wdb_ref,
     stage_ref, gsem, wsem) = refs[EXPERT_AHEAD + 1:]
    i = pl.program_id(0)
    n_used = be_ref[pl.num_programs(0)]
    nbuf = EXPERT_AHEAD + 1
    slot = i % nbuf
    expert = be_ref[i]
    weights = ((wg_hbm, sg_ref, wgb_ref), (wu_hbm, su_ref, wub_ref), (wd_hbm, sd_ref, wdb_ref))

    def weight_copy(w, e, s):
        hbm, st_ref, _ = weights[w]
        return pltpu.make_async_copy(hbm.at[layer, e], st_ref.at[s], wsem.at[s, w])

    @pl.when(i == 0)
    def _():
        stage_ref[0] = 0
        for w in range(len(weights)):
            weight_copy(w, expert, 0).start()

    @pl.when((i < n_used) & ((i == 0) | (expert != be_ref[jnp.maximum(i - 1, 0)])))
    def _():
        s = stage_ref[0]
        nxt = succ_ref[expert]
        for w, (_, st_ref, wb_ref) in enumerate(weights):
            weight_copy(w, 0, s).wait()
            wb_ref[...] = st_ref[s].astype(BF16)

        @pl.when(nxt >= 0)
        def _():
            for w in range(len(weights)):
                weight_copy(w, nxt, 1 - s).start(priority=1)

        stage_ref[0] = 1 - s

    def gather_copy(src_row, to_slot, t):
        src = h_hbm.at[pl.ds(pl.multiple_of(src_row, ROW_SUB), ROW_SUB)]
        return pltpu.make_async_copy(src, _row_slab(xbuf_ref.at[to_slot], t), gsem.at[to_slot])

    def gather(idx_ref, to_slot):
        for t in range(MOE_BLOCK):
            gather_copy(idx_ref[t], to_slot, t).start()

    @pl.when(i == 0)
    def _():
        for a in range(EXPERT_AHEAD):
            gather(src_refs[a], a)

    @pl.when(i + EXPERT_AHEAD < n_used)
    def _():
        gather(src_refs[EXPERT_AHEAD], (i + EXPERT_AHEAD) % nbuf)

    @pl.when(i < n_used)
    def _():
        for t in range(MOE_BLOCK):
            gather_copy(0, slot, 0).wait()
        x_ref = xbuf_ref.at[slot]
        halves = [_unpack_cols(x_ref[_col_block(MOE_BLOCK, c), :]) for c in range(ROW_SUB)]
        x = jnp.concatenate([lo for lo, _ in halves] + [hi for _, hi in halves], axis=1).astype(BF16)
        hid = (_silu(_dot(x, wgb_ref[...])) * _dot(x, wub_ref[...])).astype(BF16)
        y = _dot(hid, wdb_ref[...])
        for c in range(ROW_SUB):
            o_ref[_col_block(MOE_BLOCK, c), :] = _pack_cols(y, c)

    @pl.when(i >= n_used)
    def _():
        o_ref[...] = jnp.zeros(o_ref.shape, o_ref.dtype)


def _combine_kernel(dest_ref, dnext_ref, h_ref, w_ref, g_ref, b_ref, yrows_hbm, of_ref, ob_ref,
                    ybuf_ref, acc_ref, sem):
    i = pl.program_id(0)
    n = pl.num_programs(0)
    slot = i % 2

    def row_copy(d, to_slot, k, t):
        return pltpu.make_async_copy(_row_slab(yrows_hbm, d), _row_slab(ybuf_ref.at[to_slot, k], t), sem.at[to_slot])

    def gather(d_ref, to_slot):
        def start(t, carry):
            for k in range(TOP_K):
                row_copy(d_ref[TOP_K * t + k], to_slot, k, t).start(priority=k)
            return carry

        lax.fori_loop(0, COMBINE_TB, start, 0, unroll=8)

    @pl.when(i == 0)
    def _():
        gather(dest_ref, 0)

    @pl.when(i + 1 < n)
    def _():
        gather(dnext_ref, 1 - slot)

    for t in range(COMBINE_TB):
        for k in range(TOP_K):
            row_copy(0, slot, k, 0).wait()
    w = w_ref[...]
    y0_ref = ybuf_ref.at[slot, 0]
    y1_ref = ybuf_ref.at[slot, 1]
    for c in range(ROW_SUB):
        cb = _col_block(COMBINE_TB, c)
        lo0, hi0 = _unpack_cols(y0_ref[cb, :])
        lo1, hi1 = _unpack_cols(y1_ref[cb, :])
        acc_ref[:, c * LANES:(c + 1) * LANES] = lo0 * w[:, 0:1] + lo1 * w[:, 1:2]
        acc_ref[:, (c + ROW_SUB) * LANES:(c + ROW_SUB + 1) * LANES] = hi0 * w[:, 0:1] + hi1 * w[:, 1:2]
    y = _layernorm_rows(ALPHA * h_ref[...] + acc_ref[...], g_ref[...], b_ref[...])
    of_ref[...] = y
    ob_ref[...] = y.astype(BF16)


def _moe_layer(hf, hp, w_rg, b_rg, w_re, b_re, w_e_gate, w_e_up, w_e_down, layer, ln_g, ln_b):
    T, D = hf.shape
    n_rows = T * TOP_K + N_EXPERTS * MOE_BLOCK
    n_blocks = n_rows // MOE_BLOCK
    w_r = jnp.concatenate([w_re, w_rg, jnp.zeros((D, LANES - N_EXPERTS - N_GROUPS), F32)], axis=-1)
    b_r = jnp.concatenate([b_re, b_rg, jnp.zeros((LANES - N_EXPERTS - N_GROUPS,), F32)]).reshape(1, LANES)
    w_hi = w_r.astype(BF16)
    w_lo = (w_r - w_hi.astype(F32)).astype(BF16)
    const = lambda i: (0, 0)
    rows = lambda i: (i, 0)
    eid, wts, rank, cnt = pl.pallas_call(
        _router_kernel,
        grid=(T // R_TM,),
        in_specs=[pl.BlockSpec((R_TM, D), rows), pl.BlockSpec((D, LANES), const),
                  pl.BlockSpec((D, LANES), const), pl.BlockSpec((1, LANES), const)],
        out_specs=[pl.BlockSpec((R_TM, LANES), rows), pl.BlockSpec((R_TM, LANES), rows),
                   pl.BlockSpec((R_TM, LANES), rows), pl.BlockSpec((8, LANES), const)],
        out_shape=[jax.ShapeDtypeStruct((T, LANES), jnp.int32), jax.ShapeDtypeStruct((T, LANES), F32),
                   jax.ShapeDtypeStruct((T, LANES), jnp.int32), jax.ShapeDtypeStruct((8, LANES), jnp.int32)],
        scratch_shapes=[pltpu.VMEM((8, LANES), F32)],
        compiler_params=_cp("arbitrary"),
        name="moe_router",
    )(hf, w_hi, w_lo, b_r)
    counts = cnt[0, :N_EXPERTS]
    padded = (counts + MOE_BLOCK - 1) // MOE_BLOCK * MOE_BLOCK
    pends = jnp.cumsum(padded)
    pstarts = pends - padded
    blk_start = jnp.arange(n_blocks, dtype=jnp.int32) * MOE_BLOCK
    blk_exp = jnp.minimum(jnp.sum((pends[None, :] <= blk_start[:, None]).astype(jnp.int32), axis=1), N_EXPERTS - 1)
    sel = eid[:, :TOP_K, None] == jnp.arange(N_EXPERTS, dtype=jnp.int32)
    dest = (jnp.sum(jnp.where(sel, pstarts, 0), axis=-1) + rank[:, :TOP_K]).astype(jnp.int32).reshape(T * TOP_K)
    flat = jnp.full((n_rows,), -1, jnp.int32).at[dest].set(jnp.arange(T * TOP_K, dtype=jnp.int32),
                                                           unique_indices=True)
    row_tok = jnp.where(flat < 0, 0, flat // TOP_K) * ROW_SUB
    assert n_blocks > EXPERT_AHEAD
    any_spec = pl.BlockSpec(memory_space=pl.ANY)
    idx_spec = lambda f: pl.BlockSpec((MOE_BLOCK,), f, memory_space=pltpu.SMEM)
    ids = jnp.arange(N_EXPERTS, dtype=jnp.int32)
    later = (ids[None, :] > ids[:, None]) & (counts[None, :] > 0)
    succ = jnp.min(jnp.where(later, ids[None, :], N_EXPERTS), axis=1)
    succ = jnp.where(succ == N_EXPERTS, -1, succ).astype(jnp.int32)
    yrows = pl.pallas_call(
        functools.partial(_expert_kernel, layer=layer),
        grid_spec=pltpu.PrefetchScalarGridSpec(
            num_scalar_prefetch=2,
            grid=(n_blocks,),
            in_specs=[idx_spec(lambda i, *_, a=a: (jnp.minimum(i + a, n_blocks - 1),)) for a in range(EXPERT_AHEAD + 1)]
            + [any_spec] * 4,
            out_specs=pl.BlockSpec((MOE_BLOCK * ROW_SUB, LANES), lambda i, *_: (i, 0)),
            scratch_shapes=[pltpu.VMEM((EXPERT_AHEAD + 1, MOE_BLOCK * ROW_SUB, LANES), ROW_DT),
                            pltpu.VMEM((2, D, D_FF), F32), pltpu.VMEM((2, D, D_FF), F32), pltpu.VMEM((2, D_FF, D), F32),
                            pltpu.VMEM((D, D_FF), BF16), pltpu.VMEM((D, D_FF), BF16), pltpu.VMEM((D_FF, D), BF16),
                            pltpu.SMEM((1,), jnp.int32),
                            pltpu.SemaphoreType.DMA((EXPERT_AHEAD + 1,)), pltpu.SemaphoreType.DMA((2, 3))]),
        out_shape=jax.ShapeDtypeStruct((n_rows * ROW_SUB, LANES), ROW_DT),
        compiler_params=_cp("arbitrary"),
        name="moe_experts",
    )(jnp.concatenate([blk_exp, pends[-1:] // MOE_BLOCK]).astype(jnp.int32), succ,
      *([row_tok] * (EXPERT_AHEAD + 1)), hp, w_e_gate, w_e_up, w_e_down)
    n_steps = T // COMBINE_TB
    dspec = lambda f: pl.BlockSpec((TOP_K * COMBINE_TB,), f, memory_space=pltpu.SMEM)
    return pl.pallas_call(
        _combine_kernel,
        grid=(n_steps,),
        in_specs=[dspec(lambda i: (i,)), dspec(lambda i: (jnp.minimum(i + 1, n_steps - 1),)),
                  pl.BlockSpec((COMBINE_TB, D), rows), pl.BlockSpec((COMBINE_TB, LANES), rows),
                  pl.BlockSpec((1, D), const), pl.BlockSpec((1, D), const), any_spec],
        out_specs=[pl.BlockSpec((COMBINE_TB, D), rows), pl.BlockSpec((COMBINE_TB, D), rows)],
        out_shape=[jax.ShapeDtypeStruct((T, D), F32), jax.ShapeDtypeStruct((T, D), BF16)],
        scratch_shapes=[pltpu.VMEM((2, TOP_K, COMBINE_TB * ROW_SUB, LANES), ROW_DT),
                        pltpu.VMEM((COMBINE_TB, D), F32), pltpu.SemaphoreType.DMA((2,))],
        compiler_params=_cp("arbitrary"),
        name="moe_combine_ln2",
    )(dest, dest, hf, wts, ln_g.reshape(1, D), ln_b.reshape(1, D), yrows)


def kernel(x, ln_in_g, ln_in_b, rel_bias, w_in, g_cq, w_uq, g_ckv, w_ukv, w_dw_c, b_dw_c, ln_c_g, ln_c_b,
           w_conv_d, b_conv_d, a_log_f, a_log_b, dt_bias_f, dt_bias_b, d_skip, g_norm_d, w_br, w_gate, b_gate,
           w_out, ln1_g, ln1_b, w_rg, b_rg, w_re, b_re, w_e_gate, w_e_up, w_e_down, ln2_g, ln2_b):
    B, S, D = x.shape
    T = B * S
    hf, hb = _layernorm(x.reshape(T, D), ln_in_g, ln_in_b)
    a_bias = _mixer_a_bias(rel_bias, S)
    for l in range(DEPTH):
        w_a, w_r = _in_proj_weights(w_in[l])
        qkv = _matmul(hb, w_a.astype(BF16), F32, 512, 3 * A_WIDTH, "in_proj_a")
        rest = _matmul(hb, w_r.astype(BF16), F32, 512, R_WIDTH // 2, "in_proj_rest")
        y_a = _mixer_a(qkv, a_bias, B, S)
        y_b = _mixer_b(rest, g_cq[l], w_uq[l], g_ckv[l], w_ukv[l], B, S)
        y_c = _mixer_c(rest, w_dw_c[l], b_dw_c[l], ln_c_g[l], ln_c_b[l], B, S)
        y_d = _mixer_d(rest, w_conv_d[l], b_conv_d[l], a_log_f[l], a_log_b[l], dt_bias_f[l], dt_bias_b[l],
                       d_skip[l], g_norm_d[l], B, S)
        h1f, h1p = _merge(hb, hf, (y_a, y_b, y_c, y_d), w_gate[l], b_gate[l], w_br[l], w_out[l], ln1_g[l], ln1_b[l])
        hf, hb = _moe_layer(h1f, h1p, w_rg[l], b_rg[l], w_re[l], b_re[l], w_e_gate, w_e_up, w_e_down, l,
                            ln2_g[l], ln2_b[l])
    return hf.reshape(B, S, D)
```

```python
import functools

import numpy as np
import jax
import jax.numpy as jnp
from jax import lax
from jax.experimental import pallas as pl
from jax.experimental.pallas import tpu as pltpu

F32 = jnp.float32
BF16 = jnp.bfloat16

D_MODEL = 2048
DEPTH = 2
A_HEADS = 8
A_HEAD_DIM = 64
A_WIDTH = A_HEADS * A_HEAD_DIM
A_PATTERNS = ((128, 1), (512, 4), (2048, 16))
A_BAND = 64
REL_BUCKETS = 32
REL_MAX_DIST = 1024
B_HEADS = 8
B_NOPE = 64
B_ROPE = 32
B_V = 64
B_Q_LORA = 512
B_KV_LORA = 256
ROPE_THETA = 10000.0
C_CH = 512
C_KERNEL = 31
D_HEADS = 8
D_HEAD_DIM = 64
D_INNER = D_HEADS * D_HEAD_DIM
D_STATE = 128
D_GROUPS = 2
D_CONV = 5
D_CHUNK = 128
N_BRANCH = 4
BRANCH_W = 512
N_GROUPS = 4
EXP_PER_GROUP = 8
N_EXPERTS = N_GROUPS * EXP_PER_GROUP
TOP_K = 2
D_FF = 512
MOE_BLOCK = 256
ALPHA = (2 * DEPTH) ** 0.25
EPS = 1e-5
NEG_INF = -1e30

LANES = 128
R_GLU, R_XBC, R_CQ, R_Z, R_CKV, R_KR, R_DT = 0, 1024, 2048, 2560, 3072, 3328, 3456
R_WIDTH = 3584
VMEM_LIMIT = 56 * 1024 * 1024


def _cp(*sem):
    return pltpu.CompilerParams(dimension_semantics=sem, vmem_limit_bytes=VMEM_LIMIT)


def _dot(a, b):
    return jnp.dot(a, b, preferred_element_type=F32)


def _dot_nt(a, b):
    return lax.dot_general(a, b, (((1,), (1,)), ((), ())), preferred_element_type=F32)


def _split3(x):
    hi = x.astype(BF16)
    r1 = x - hi.astype(F32)
    mid = r1.astype(BF16)
    lo = (r1 - mid.astype(F32)).astype(BF16)
    return hi, mid, lo


def _layernorm_rows(x, g, b):
    mu = jnp.mean(x, axis=-1, keepdims=True)
    xc = x - mu
    var = jnp.mean(xc * xc, axis=-1, keepdims=True)
    return xc * lax.rsqrt(var + EPS) * g + b


def _sigmoid(x):
    return 1.0 / (1.0 + jnp.exp(-x))


def _silu(x):
    return x * _sigmoid(x)


def _ln_kernel(x_ref, g_ref, b_ref, of_ref, ob_ref):
    y = _layernorm_rows(x_ref[...], g_ref[...], b_ref[...])
    of_ref[...] = y
    ob_ref[...] = y.astype(BF16)


def _layernorm(x, g, b, tm=256):
    T, D = x.shape
    return pl.pallas_call(
        _ln_kernel,
        grid=(T // tm,),
        in_specs=[pl.BlockSpec((tm, D), lambda i: (i, 0)),
                  pl.BlockSpec((1, D), lambda i: (0, 0)),
                  pl.BlockSpec((1, D), lambda i: (0, 0))],
        out_specs=[pl.BlockSpec((tm, D), lambda i: (i, 0)),
                   pl.BlockSpec((tm, D), lambda i: (i, 0))],
        out_shape=[jax.ShapeDtypeStruct((T, D), F32), jax.ShapeDtypeStruct((T, D), BF16)],
        compiler_params=_cp("parallel"),
        name="ln_in",
    )(x, g.reshape(1, D), b.reshape(1, D))


def _mm_kernel(x_ref, w_ref, o_ref):
    o_ref[...] = _dot(x_ref[...], w_ref[...]).astype(o_ref.dtype)


def _matmul(x, w, out_dtype, tm, tn, name):
    M, K = x.shape
    N = w.shape[1]
    return pl.pallas_call(
        _mm_kernel,
        grid=(N // tn, M // tm),
        in_specs=[pl.BlockSpec((tm, K), lambda j, i: (i, 0)),
                  pl.BlockSpec((K, tn), lambda j, i: (0, j))],
        out_specs=pl.BlockSpec((tm, tn), lambda j, i: (i, j)),
        out_shape=jax.ShapeDtypeStruct((M, N), out_dtype),
        compiler_params=_cp("parallel", "parallel"),
        name=name,
    )(x, w)


def _t5_bucket(rel):
    half = REL_BUCKETS // 2
    max_exact = half // 2
    n = np.abs(rel)
    large = max_exact + (np.log(np.maximum(n, 1) / max_exact) / np.log(REL_MAX_DIST / max_exact)
                         * (half - max_exact)).astype(np.int32)
    large = np.minimum(large, half - 1)
    return (rel > 0).astype(np.int32) * half + np.where(n < max_exact, n, large)


def _a_window(L):
    return min(2 * LANES, L)


def _a_bias_tiles(rel_bias, d, L):
    W = _a_window(L)
    offs = (0,) if L == LANES else (0, -A_BAND, -2 * A_BAND)
    qi = np.arange(LANES)[:, None]
    kj = np.arange(W)[None, :]
    rel = np.stack([kj - qi + off for off in offs], axis=0)
    valid = np.abs(rel) <= A_BAND
    onehot = (jnp.asarray(_t5_bucket(rel * d), jnp.int32)[..., None] == jnp.arange(REL_BUCKETS)).astype(F32)
    b = jnp.einsum('vqkb,bh->vhqk', onehot, rel_bias.astype(F32), precision=lax.Precision.HIGHEST)
    b = jnp.where(valid[:, None], b, NEG_INF)
    return b.reshape(len(offs), A_HEADS // 2, 2 * LANES, W)


A_GROUP = 4


def _attn_a_kernel(q_ref, k_ref, v_ref, b16_ref, b4_ref, b1_ref, y_ref,
                   q4_ref, k4_ref, v4_ref, m_ref, l_ref, acc_ref, tmp_ref, *, S):
    (_, d1), (_, d4), (_, d16) = A_PATTERNS
    lane = lax.broadcasted_iota(jnp.int32, (LANES, LANES), 1)
    head0 = lane < A_HEAD_DIM
    scale = A_HEAD_DIM ** -0.5

    def partial_softmax(qs, ks, vs, bias_ref):
        q2 = jnp.concatenate([jnp.where(head0, qs, 0.0), jnp.where(head0, 0.0, qs)], axis=0).astype(BF16)
        s = _dot_nt(q2, ks.astype(BF16)) * scale + bias_ref[...]
        m = jnp.max(s, axis=-1, keepdims=True)
        p = jnp.exp(s - m).astype(BF16)
        num = _dot(p, vs.astype(BF16))
        den = _dot(p, jnp.ones((vs.shape[0], LANES), BF16))
        both = lambda t: jnp.where(head0, t[:LANES], t[LANES:])
        return both(jnp.broadcast_to(m, (2 * LANES, LANES))), both(den), both(num)

    def fold(old, new):
        (m_old, l_old, a_old), (m_new, l_new, a_new) = old, new
        m = jnp.maximum(m_old, m_new)
        c_old = jnp.exp(m_old - m)
        c_new = jnp.exp(m_new - m)
        return m, c_old * l_old + c_new * l_new, c_old * a_old + c_new * a_new

    stat_refs = (m_ref, l_ref, acc_ref)

    def get(c, rows):
        return tuple(ref.at[c][rows, :] for ref in stat_refs)

    def put(c, rows, stats):
        for ref, val in zip(stat_refs, stats):
            ref.at[c][rows, :] = val

    def grouped(n, group, unit):
        def trip(g, carry):
            pending = [unit(g * group + u) for u in range(group)]
            for finish in pending:
                finish()
            return carry

        lax.fori_loop(0, n // group, trip, 0)

    L4 = S // d4
    sub = d16 // d4
    assert S // d16 == LANES and sub == d4
    for c in range(d4):
        cls = pl.ds(c, L4, stride=d4)
        q4_ref[c] = q_ref[cls, :]
        k4_ref[c] = k_ref[cls, :]
        v4_ref[c] = v_ref[cls, :]

    def unit16(t):
        c = t % d4
        rows = pl.ds(t // d4, LANES, stride=sub)
        stats = partial_softmax(q4_ref.at[c][rows, :], k4_ref.at[c][rows, :], v4_ref.at[c][rows, :], b16_ref.at[0])
        return lambda: put(c, rows, stats)

    grouped(d16, A_GROUP, unit16)

    def window(i, L):
        nqb = L // LANES
        ws = pl.multiple_of(jnp.clip(i * LANES - A_BAND, 0, L - 2 * LANES), A_BAND)
        return pl.ds(ws, 2 * LANES), jnp.where(i == 0, 0, jnp.where(i == nqb - 1, 2, 1))

    def unit4(t):
        c = t % d4
        i = t // d4
        keys, var = window(i, L4)
        rows = pl.ds(pl.multiple_of(i * LANES, LANES), LANES)
        stats = partial_softmax(q4_ref.at[c][rows, :], k4_ref.at[c][keys, :], v4_ref.at[c][keys, :], b4_ref.at[var])
        return lambda: put(c, rows, fold(get(c, rows), stats))

    grouped(d4 * (L4 // LANES), A_GROUP, unit4)

    def unit1(i):
        keys, var = window(i, S)
        rows = pl.ds(pl.multiple_of(i * LANES, LANES), LANES)
        stats = partial_softmax(q_ref[rows, :], k_ref[keys, :], v_ref[keys, :], b1_ref.at[var])

        def finish():
            part = pl.ds(pl.multiple_of(i * (LANES // d4), LANES // d4), LANES // d4)
            for n, ref in enumerate(stat_refs):
                for c in range(d4):
                    tmp_ref.at[n][pl.ds(c, LANES // d4, stride=d4), :] = ref.at[c][part, :]
            _, l, a = fold(tuple(tmp_ref[n] for n in range(len(stat_refs))), stats)
            y_ref[rows, :] = (a / l).astype(y_ref.dtype)

        return finish

    grouped(S // LANES, A_GROUP, unit1)


def _mixer_a_bias(rel_bias, S):
    return tuple(_a_bias_tiles(rel_bias, d, S // d) for _, d in A_PATTERNS)


def _mixer_a(qkv, bias, B, S):
    b1, b4, b16 = bias
    npair = A_HEADS // 2
    pair_bias = lambda t: pl.BlockSpec((t.shape[0], None) + t.shape[2:], lambda b, hp: (0, hp, 0, 0))
    slab = lambda first: pl.BlockSpec((S, LANES), lambda b, hp: (b, first + hp))
    return pl.pallas_call(
        functools.partial(_attn_a_kernel, S=S),
        grid=(B, npair),
        in_specs=[slab(0), slab(npair), slab(2 * npair), pair_bias(b16), pair_bias(b4), pair_bias(b1)],
        out_specs=pl.BlockSpec((S, LANES), lambda b, hp: (b, hp)),
        out_shape=jax.ShapeDtypeStruct((B * S, A_WIDTH), BF16),
        scratch_shapes=[pltpu.VMEM((A_PATTERNS[1][1], S // A_PATTERNS[1][1], LANES), F32)] * 6
        + [pltpu.VMEM((3, LANES, LANES), F32)],
        compiler_params=_cp("parallel", "parallel"),
        name="attn_a",
    )(qkv, qkv, qkv, b16, b4, b1)


MLA_Q_SCALE = float((B_NOPE + B_ROPE) ** -0.5 * np.log2(np.e))


def _mla_proj_kernel(cq_ref, ckv_ref, kr_ref, gq_ref, gkv_ref, wqm_ref, wqs_ref, wk_ref, wv_ref, vone_ref,
                     ek_ref, cosq_ref, sinq_ref, csk_ref, q_ref, k_ref, v_ref):
    cq = cq_ref[...]
    xq = (cq * lax.rsqrt(jnp.mean(cq * cq, axis=-1, keepdims=True) + EPS) * gq_ref[...]).astype(BF16)
    ckv = ckv_ref[...]
    xkv = (ckv * lax.rsqrt(jnp.mean(ckv * ckv, axis=-1, keepdims=True) + EPS) * gkv_ref[...]).astype(BF16)
    qm = _dot(xq, wqm_ref[...])
    qs = _dot(xq, wqs_ref[...])
    cosq = cosq_ref[...] * MLA_Q_SCALE
    sinq = sinq_ref[...] * MLA_Q_SCALE
    t = kr_ref[...] * csk_ref[...]
    t_hi = t.astype(BF16)
    t_lo = (t - t_hi.astype(F32)).astype(BF16)
    kk = _dot(xkv, wk_ref[...]) + _dot(t_hi, ek_ref[...]) + _dot(t_lo, ek_ref[...])
    for h in range(B_HEADS):
        sl = slice(h * LANES, (h + 1) * LANES)
        q_ref[:, sl] = (qm[:, sl] * cosq + qs[:, sl] * sinq).astype(BF16)
    k_ref[...] = kk.astype(BF16)
    v_ref[...] = (_dot(xkv, wv_ref[...]) + vone_ref[...]).astype(BF16)


def _mla_attn_kernel(q_ref, k_ref, v_ref, o_ref):
    outs = []
    for hh in range(2):
        sl = slice(hh * LANES, (hh + 1) * LANES)
        s = _dot_nt(q_ref[:, sl], k_ref[:, sl])
        p = jnp.exp2(s - jnp.max(s, axis=-1, keepdims=True))
        outs.append(_dot(p.astype(BF16), v_ref[:, sl]))
    lane = lax.broadcasted_iota(jnp.int32, outs[0].shape, 1)
    acc = jnp.where(lane < B_V, outs[0], outs[1])
    den = pltpu.roll(jnp.where(lane < B_V, outs[1], outs[0]), B_V, axis=1)
    o_ref[...] = (acc / den).astype(o_ref.dtype)


def _mla_tables(S):
    inv_freq = ROPE_THETA ** (-jnp.arange(0, B_ROPE, 2, dtype=F32) / B_ROPE)
    ang = jnp.arange(S, dtype=F32)[:, None] * inv_freq[None]
    cos, sin = jnp.cos(ang), jnp.sin(ang)
    cos2 = jnp.concatenate([cos, cos], axis=-1)
    sin2 = jnp.concatenate([sin, sin], axis=-1)
    ones = jnp.ones((S, B_NOPE), F32)
    zn = jnp.zeros((S, B_NOPE), F32)
    zp = jnp.zeros((S, LANES - B_NOPE - B_ROPE), F32)
    cosq = jnp.concatenate([ones, cos2, zp], axis=-1)
    sinq = jnp.concatenate([zn, sin2, zp], axis=-1)
    csk = jnp.concatenate([cos2, sin2, jnp.zeros((S, LANES - 2 * B_ROPE), F32)], axis=-1)
    return cosq, sinq, csk


def _swap_cols(w):
    half = w.shape[-1] // 2
    return jnp.concatenate([-w[..., half:], w[..., :half]], axis=-1)


def _mla_weights(w_uq, w_ukv):
    dq = B_NOPE + B_ROPE
    wq = w_uq.reshape(B_Q_LORA, B_HEADS, dq)
    zpad = jnp.zeros((B_Q_LORA, B_HEADS, LANES - dq), F32)
    wqm = jnp.concatenate([wq, zpad], axis=-1).reshape(B_Q_LORA, B_HEADS * LANES)
    wqs = jnp.concatenate([jnp.zeros((B_Q_LORA, B_HEADS, B_NOPE), F32), _swap_cols(wq[..., B_NOPE:]), zpad],
                          axis=-1).reshape(B_Q_LORA, B_HEADS * LANES)
    wkv = w_ukv.reshape(B_KV_LORA, B_HEADS, B_NOPE + B_V)
    wk = jnp.concatenate([wkv[..., :B_NOPE], jnp.zeros((B_KV_LORA, B_HEADS, LANES - B_NOPE), F32)],
                         axis=-1).reshape(B_KV_LORA, B_HEADS * LANES)
    zv = jnp.zeros((B_KV_LORA, B_HEADS // 2, LANES - B_V), F32)
    wv_h = wkv[..., B_NOPE:]
    wv = jnp.stack([jnp.concatenate([wv_h[:, 0::2], zv], axis=-1),
                    jnp.concatenate([zv, wv_h[:, 1::2]], axis=-1)], axis=2).reshape(B_KV_LORA, B_HEADS * LANES)
    lane_in_pair = np.arange(B_HEADS * LANES) % (2 * LANES)
    vone = jnp.asarray(((lane_in_pair >= B_V) & (lane_in_pair < LANES + B_V)).astype(np.float32)).reshape(1, -1)
    ek = np.zeros((LANES, B_HEADS, LANES), np.float32)
    for j in range(B_ROPE):
        ek[j, :, B_NOPE + j] = 1.0
        ek[B_ROPE + j, :, B_NOPE + j] = 1.0
    ek = jnp.asarray(ek.reshape(LANES, B_HEADS * LANES))
    return wqm.astype(BF16), wqs.astype(BF16), wk.astype(BF16), wv.astype(BF16), vone, ek.astype(BF16)


def _mixer_b(rest, g_cq, w_uq, g_ckv, w_ukv, B, S, tm=512, tq=256):
    T = B * S
    wqm, wqs, wk, wv, vone, ek = _mla_weights(w_uq, w_ukv)
    cosq, sinq, csk = _mla_tables(S)
    nst = S // tm
    QW = B_HEADS * LANES
    const = lambda i: (0, 0)
    pos = lambda i: (i % nst, 0)
    q, k, v = pl.pallas_call(
        _mla_proj_kernel,
        grid=(T // tm,),
        in_specs=[pl.BlockSpec((tm, B_Q_LORA), lambda i: (i, R_CQ // B_Q_LORA)),
                  pl.BlockSpec((tm, B_KV_LORA), lambda i: (i, R_CKV // B_KV_LORA)),
                  pl.BlockSpec((tm, LANES), lambda i: (i, R_KR // LANES)),
                  pl.BlockSpec((1, B_Q_LORA), const),
                  pl.BlockSpec((1, B_KV_LORA), const),
                  pl.BlockSpec((B_Q_LORA, QW), const),
                  pl.BlockSpec((B_Q_LORA, QW), const),
                  pl.BlockSpec((B_KV_LORA, QW), const),
                  pl.BlockSpec((B_KV_LORA, QW), const),
                  pl.BlockSpec((1, QW), const),
                  pl.BlockSpec((LANES, QW), const),
                  pl.BlockSpec((tm, LANES), pos),
                  pl.BlockSpec((tm, LANES), pos),
                  pl.BlockSpec((tm, LANES), pos)],
        out_specs=[pl.BlockSpec((tm, QW), lambda i: (i, 0)),
                   pl.BlockSpec((tm, QW), lambda i: (i, 0)),
                   pl.BlockSpec((tm, QW), lambda i: (i, 0))],
        out_shape=[jax.ShapeDtypeStruct((T, QW), BF16), jax.ShapeDtypeStruct((T, QW), BF16),
                   jax.ShapeDtypeStruct((T, QW), BF16)],
        compiler_params=_cp("parallel"),
        name="mla_proj",
    )(rest, rest, rest, g_cq.reshape(1, -1), g_ckv.reshape(1, -1), wqm, wqs, wk, wv, vone, ek, cosq, sinq, csk)
    y = pl.pallas_call(
        _mla_attn_kernel,
        grid=(B, B_HEADS // 2, S // tq),
        in_specs=[pl.BlockSpec((None, tq, 2 * LANES), lambda b, hp, i: (b, i, hp)),
                  pl.BlockSpec((None, S, 2 * LANES), lambda b, hp, i: (b, 0, hp)),
                  pl.BlockSpec((None, S, 2 * LANES), lambda b, hp, i: (b, 0, hp))],
        out_specs=pl.BlockSpec((None, tq, 2 * B_V), lambda b, hp, i: (b, i, hp)),
        out_shape=jax.ShapeDtypeStruct((B, S, B_HEADS * B_V), BF16),
        compiler_params=_cp("parallel", "parallel", "arbitrary"),
        name="mla_attn",
    )(q.reshape(B, S, QW), k.reshape(B, S, QW), v.reshape(B, S, QW))
    return y.reshape(T, B_HEADS * B_V)


C_PAD = 16
C_ROWS = 128


SUBLANES = 8


def _tap_span(first, ntaps, rows):
    return rows + ((first + ntaps - 1) // SUBLANES) * SUBLANES


def _depthwise_taps(win_ref, sh_ref, w_ref, bias, ls, first, ntaps, rows):
    acc = jnp.broadcast_to(bias, (rows, LANES))
    span = _tap_span(first, ntaps, rows)
    for ph in range(SUBLANES):
        taps = [j for j in range(ntaps) if (first + j) % SUBLANES == ph]
        if not taps:
            continue
        if len(taps) == 1:
            j = taps[0]
            acc = acc + w_ref[j:j + 1, ls] * win_ref[first + j:first + j + rows, ls]
            continue
        sh_ref[0:span, :] = win_ref[ph:ph + span, ls]
        for j in taps:
            a = (first + j) // SUBLANES * SUBLANES
            acc = acc + w_ref[j:j + 1, ls] * sh_ref[a:a + rows, :]
    return acc


def _fill_window(win_ref, load_rows, r0, i, nblk, rows, pad):
    width = win_ref.shape[1]
    win_ref[pad:pad + rows, :] = load_rows(r0, rows)

    @pl.when(i > 0)
    def _():
        win_ref[0:pad, :] = load_rows(r0 - pad, pad)

    @pl.when(i == 0)
    def _():
        win_ref[0:pad, :] = jnp.zeros((pad, width), F32)

    @pl.when(i < nblk - 1)
    def _():
        win_ref[pad + rows:pad + rows + pad, :] = load_rows(r0 + rows, pad)

    @pl.when(i == nblk - 1)
    def _():
        win_ref[pad + rows:pad + rows + pad, :] = jnp.zeros((pad, width), F32)


def _conv_c_kernel(glu_ref, w_ref, b_ref, g_ref, beta_ref, o_ref, win_ref, sh_ref, acc_ref, *, S):
    i = pl.program_id(1)
    r0 = pl.multiple_of(i * C_ROWS, C_ROWS)

    def glu_rows(start, n):
        rs = pl.ds(pl.multiple_of(start, C_PAD), n)
        return glu_ref[rs, 0:C_CH] * _sigmoid(glu_ref[rs, C_CH:2 * C_CH])

    _fill_window(win_ref, glu_rows, r0, i, S // C_ROWS, C_ROWS, C_PAD)
    first = C_PAD - C_KERNEL // 2
    for lb in range(C_CH // LANES):
        ls = slice(lb * LANES, (lb + 1) * LANES)
        acc_ref[:, ls] = _depthwise_taps(win_ref, sh_ref, w_ref, b_ref[:, ls], ls, first, C_KERNEL, C_ROWS)
    y = _layernorm_rows(acc_ref[...], g_ref[...], beta_ref[...])
    o_ref[...] = _silu(y).astype(o_ref.dtype)


def _mixer_c(rest, w_dw, b_dw, ln_g, ln_b, B, S):
    T = B * S
    const = lambda b, i: (0, 0)
    y = pl.pallas_call(
        functools.partial(_conv_c_kernel, S=S),
        grid=(B, S // C_ROWS),
        in_specs=[pl.BlockSpec((None, S, 2 * C_CH), lambda b, i: (b, 0, R_GLU // (2 * C_CH))),
                  pl.BlockSpec((C_KERNEL, C_CH), const),
                  pl.BlockSpec((1, C_CH), const),
                  pl.BlockSpec((1, C_CH), const),
                  pl.BlockSpec((1, C_CH), const)],
        out_specs=pl.BlockSpec((None, C_ROWS, C_CH), lambda b, i: (b, i, 0)),
        out_shape=jax.ShapeDtypeStruct((B, S, C_CH), BF16),
        scratch_shapes=[pltpu.VMEM((C_ROWS + 2 * C_PAD, C_CH), F32),
                        pltpu.VMEM((_tap_span(C_PAD - C_KERNEL // 2, C_KERNEL, C_ROWS), LANES), F32),
                        pltpu.VMEM((C_ROWS, C_CH), F32)],
        compiler_params=_cp("parallel", "parallel"),
        name="conformer_conv",
    )(rest.reshape(B, S, R_WIDTH), w_dw, b_dw.reshape(1, -1), ln_g.reshape(1, -1), ln_b.reshape(1, -1))
    return y.reshape(T, C_CH)


D_PAD = 8
XBC_W = D_INNER + 2 * D_GROUPS * D_STATE
N_PAIR = D_HEADS // 2


def _pair_expand(v, first):
    lane = lax.broadcasted_iota(jnp.int32, (v.shape[0], LANES), 1)
    lo = jnp.broadcast_to(v[:, first:first + 1], (v.shape[0], LANES))
    hi = jnp.broadcast_to(v[:, first + 1:first + 2], (v.shape[0], LANES))
    return jnp.where(lane < D_HEAD_DIM, lo, hi)


def _ssd_kernel(xbc_ref, z_ref, dt_ref, wc_ref, bc_ref, alog_ref, dtb_ref, dskip_ref, gn_ref, o_ref,
                win_ref, sh_ref, xc_ref, a_ref, dtv_ref, y_ref, st_ref, *, S):
    Q = D_CHUNK
    nchunk = S // Q
    N = D_STATE
    bm0 = D_INNER
    cm0 = D_INNER + D_GROUPS * N

    def conv_body(c, carry):
        r0 = pl.multiple_of(c * Q, Q)
        _fill_window(win_ref, lambda st, n: xbc_ref[pl.ds(pl.multiple_of(st, D_PAD), n), :], r0, c, nchunk, Q, D_PAD)
        for lb in range(XBC_W // LANES):
            ls = slice(lb * LANES, (lb + 1) * LANES)
            acc = _depthwise_taps(win_ref, sh_ref, wc_ref, bc_ref[:, ls], ls, D_PAD - D_CONV // 2, D_CONV, Q)
            xc_ref[pl.ds(r0, Q), ls] = _silu(acc)
        return carry

    lax.fori_loop(0, nchunk, conv_body, 0)

    lane1 = lax.broadcasted_iota(jnp.int32, (1, LANES), 1)
    a_row = jnp.where(lane1 < 2 * D_HEADS, -jnp.exp(alog_ref[...]), 0.0)
    xdt = dt_ref[...] + dtb_ref[...]
    dtv = jnp.maximum(xdt, 0.0) + jnp.log(1.0 + jnp.exp(-jnp.abs(xdt)))
    dtv_ref[...] = dtv
    a_ref[...] = dtv * a_row

    row = lax.broadcasted_iota(jnp.int32, (Q, Q), 0)
    col = lax.broadcasted_iota(jnp.int32, (Q, Q), 1)
    tril = row >= col
    triu = col >= row
    lane = col

    def scan_chunk(c, lower, off, finalize):
        r0 = pl.multiple_of(c * Q, Q)
        rows = pl.ds(r0, Q)
        mask = tril if lower else triu
        tri = mask.astype(BF16)
        a_hi, a_mid, a_lo = _split3(a_ref[rows, :])
        cs = _dot(tri, a_hi) + _dot(tri, a_mid) + _dot(tri, a_lo)
        cs_t = cs.T
        ecs = jnp.exp(cs)
        edge = Q - 1 if lower else 0
        edec = jnp.exp(cs[edge:edge + 1, :] - cs)
        dt_c = dtv_ref[rows, :]
        dt_t = dt_c.T
        dte_t = (dt_c * edec).T
        for g in range(D_GROUPS):
            bg = xc_ref[rows, bm0 + g * N:bm0 + (g + 1) * N]
            cg = xc_ref[rows, cm0 + g * N:cm0 + (g + 1) * N].astype(BF16)
            cb = _dot_nt(cg, bg.astype(BF16))
            bg_t = bg.T
            for pp in range(N_PAIR // D_GROUPS):
                p = g * (N_PAIR // D_GROUPS) + pp
                ps = slice(p * LANES, (p + 1) * LANES)
                x_p = xc_ref[rows, ps]
                ms, bs = [], []
                for hh in range(2):
                    k = off + 2 * p + hh
                    diff = jnp.broadcast_to(cs[:, k:k + 1], (Q, Q)) - cs_t[k:k + 1, :]
                    ms.append((jnp.exp(jnp.where(mask, diff, NEG_INF)) * (cb * dt_t[k:k + 1, :])).astype(BF16))
                    bs.append((bg_t * dte_t[k:k + 1, :]).astype(BF16))
                x_lo = jnp.where(lane < D_HEAD_DIM, x_p, 0.0).astype(BF16)
                x_hi = jnp.where(lane >= D_HEAD_DIM, x_p, 0.0).astype(BF16)
                x2 = jnp.concatenate([x_lo, x_hi], axis=0)
                y_intra = _dot(jnp.concatenate(ms, axis=1), x2)
                hp = st_ref[p]
                ecs_p = _pair_expand(ecs, off + 2 * p)
                y_new = y_intra + _dot(cg, hp.astype(BF16)) * ecs_p
                if lower:
                    y_ref[rows, ps] = y_new
                else:
                    y_ref[rows, ps] = y_ref[rows, ps] + y_new
                st_ref[p] = hp * ecs_p[edge:edge + 1, :] + _dot(jnp.concatenate(bs, axis=1), x2)
        if finalize:
            y = y_ref[rows, :] + xc_ref[rows, 0:D_INNER] * dskip_ref[...]
            gated = y * _silu(z_ref[rows, :])
            out = gated * lax.rsqrt(jnp.mean(gated * gated, axis=-1, keepdims=True) + EPS) * gn_ref[...]
            o_ref[rows, :] = out.astype(o_ref.dtype)

    st_ref[...] = jnp.zeros(st_ref.shape, F32)

    def fwd_body(c, carry):
        scan_chunk(c, True, 0, False)
        return carry

    lax.fori_loop(0, nchunk, fwd_body, 0)
    st_ref[...] = jnp.zeros(st_ref.shape, F32)

    def bwd_body(k, carry):
        scan_chunk(nchunk - 1 - k, False, D_HEADS, True)
        return carry

    lax.fori_loop(0, nchunk, bwd_body, 0)


def _mixer_d(rest, w_conv, b_conv, a_log_f, a_log_b, dt_bias_f, dt_bias_b, d_skip, g_norm, B, S):
    T = B * S
    pad16 = lambda f, b: jnp.concatenate([f, b, jnp.zeros((LANES - 2 * D_HEADS,), F32)]).reshape(1, LANES)
    const = lambda b: (0, 0)
    y = pl.pallas_call(
        functools.partial(_ssd_kernel, S=S),
        grid=(B,),
        in_specs=[pl.BlockSpec((None, S, XBC_W), lambda b: (b, 0, R_XBC // XBC_W)),
                  pl.BlockSpec((None, S, D_INNER), lambda b: (b, 0, R_Z // D_INNER)),
                  pl.BlockSpec((None, S, LANES), lambda b: (b, 0, R_DT // LANES)),
                  pl.BlockSpec((D_CONV, XBC_W), const),
                  pl.BlockSpec((1, XBC_W), const),
                  pl.BlockSpec((1, LANES), const),
                  pl.BlockSpec((1, LANES), const),
                  pl.BlockSpec((1, D_INNER), const),
                  pl.BlockSpec((1, D_INNER), const)],
        out_specs=pl.BlockSpec((None, S, D_INNER), lambda b: (b, 0, 0)),
        out_shape=jax.ShapeDtypeStruct((B, S, D_INNER), BF16),
        scratch_shapes=[pltpu.VMEM((D_CHUNK + 2 * D_PAD, XBC_W), F32),
                        pltpu.VMEM((_tap_span(D_PAD - D_CONV // 2, D_CONV, D_CHUNK), LANES), F32),
                        pltpu.VMEM((S, XBC_W), F32),
                        pltpu.VMEM((S, LANES), F32),
                        pltpu.VMEM((S, LANES), F32),
                        pltpu.VMEM((S, D_INNER), F32),
                        pltpu.VMEM((N_PAIR, D_STATE, LANES), F32)],
        compiler_params=_cp("parallel"),
        name="ssd_mixer",
    )(rest.reshape(B, S, R_WIDTH), rest.reshape(B, S, R_WIDTH), rest.reshape(B, S, R_WIDTH),
      w_conv, b_conv.reshape(1, -1), pad16(a_log_f, a_log_b), pad16(dt_bias_f, dt_bias_b),
      jnp.repeat(d_skip, D_HEAD_DIM).reshape(1, -1), g_norm.reshape(1, -1))
    return y.reshape(T, D_INNER)


def _in_proj_weights(w_in_l):
    o = np.cumsum((0, A_WIDTH, A_WIDTH, A_WIDTH, B_Q_LORA, B_KV_LORA, B_ROPE, 2 * C_CH,
                   D_INNER, D_INNER, D_GROUPS * D_STATE, D_GROUPS * D_STATE, 2 * D_HEADS)).tolist()
    seg = lambda n: w_in_l[:, o[n]:o[n + 1]]
    w_a = w_in_l[:, :o[3]]
    cq, ckv, kr, glu, z, xs, bm, cm, dt = (seg(n) for n in range(3, 12))
    zeros = lambda n: jnp.zeros((w_in_l.shape[0], n), w_in_l.dtype)
    w_r = jnp.concatenate([glu, xs, bm, cm, cq, z, ckv,
                           kr, _swap_cols(kr), zeros(LANES - 2 * B_ROPE),
                           dt, zeros(LANES - 2 * D_HEADS)], axis=-1)
    assert w_r.shape[1] == R_WIDTH
    return w_a, w_r


def _merge_gate_kernel(h_ref, ya_ref, yb_ref, yc_ref, yd_ref, wg_ref, bg_ref, wbr_ref, o_ref):
    h = h_ref[...]
    acc = None
    for i, y_ref in enumerate((ya_ref, yb_ref, yc_ref, yd_ref)):
        gate = _sigmoid(_dot(h, wg_ref[i]) + bg_ref[i])
        term = gate * _dot(y_ref[...], wbr_ref[i])
        acc = term if acc is None else acc + term
    o_ref[...] = acc.astype(o_ref.dtype)


def _out_ln_kernel(m_ref, w_ref, h_ref, g_ref, b_ref, of_ref, op_ref):
    y = _layernorm_rows(ALPHA * h_ref[...] + _dot(m_ref[...], w_ref[...]), g_ref[...], b_ref[...])
    of_ref[...] = y
    for c in range(ROW_SUB):
        op_ref[_col_block(y.shape[0], c), :] = _pack_cols(y, c)


def _merge(hb, hf, branches, w_gate, b_gate, w_br, w_out, ln_g, ln_b, tm=512, tn=512, tm2=512):
    T, D = hb.shape
    ybs = pl.BlockSpec((tm, BRANCH_W), lambda j, i: (i, 0))
    merged = pl.pallas_call(
        _merge_gate_kernel,
        grid=(D // tn, T // tm),
        in_specs=[pl.BlockSpec((tm, D), lambda j, i: (i, 0)), ybs, ybs, ybs, ybs,
                  pl.BlockSpec((N_BRANCH, D, tn), lambda j, i: (0, 0, j)),
                  pl.BlockSpec((N_BRANCH, 1, tn), lambda j, i: (0, 0, j)),
                  pl.BlockSpec((N_BRANCH, BRANCH_W, tn), lambda j, i: (0, 0, j))],
        out_specs=pl.BlockSpec((tm, tn), lambda j, i: (i, j)),
        out_shape=jax.ShapeDtypeStruct((T, D), BF16),
        compiler_params=_cp("parallel", "parallel"),
        name="merge_gate",
    )(hb, *branches, w_gate.astype(BF16), b_gate.reshape(N_BRANCH, 1, D), w_br.astype(BF16))
    const = lambda i: (0, 0)
    rows = lambda i: (i, 0)
    return pl.pallas_call(
        _out_ln_kernel,
        grid=(T // tm2,),
        in_specs=[pl.BlockSpec((tm2, D), rows), pl.BlockSpec((D, D), const), pl.BlockSpec((tm2, D), rows),
                  pl.BlockSpec((1, D), const), pl.BlockSpec((1, D), const)],
        out_specs=[pl.BlockSpec((tm2, D), rows), pl.BlockSpec((tm2 * ROW_SUB, LANES), rows)],
        out_shape=[jax.ShapeDtypeStruct((T, D), F32), jax.ShapeDtypeStruct((T * ROW_SUB, LANES), ROW_DT)],
        compiler_params=_cp("parallel"),
        name="out_proj_ln1",
    )(merged, w_out.astype(BF16), hf, ln_g.reshape(1, D), ln_b.reshape(1, D))


R_TM = 512
COMBINE_TB = 256


def _router_kernel(h_ref, whi_ref, wlo_ref, b_ref, eid_ref, wts_ref, rank_ref, cnt_ref, carry_ref):
    i = pl.program_id(0)

    @pl.when(i == 0)
    def _():
        carry_ref[...] = jnp.zeros(carry_ref.shape, F32)

    x = h_ref[...]
    tm = x.shape[0]
    xh = x.astype(BF16)
    xl = (x - xh.astype(F32)).astype(BF16)
    whi = whi_ref[...]
    logits = _dot(xh, whi) + _dot(xh, wlo_ref[...]) + _dot(xl, whi) + b_ref[...]
    lane = lax.broadcasted_iota(jnp.int32, (tm, LANES), 1)
    big = jnp.int32(4 * LANES)
    is_g = (lane >= N_EXPERTS) & (lane < N_EXPERTS + N_GROUPS)
    lg = jnp.where(is_g, logits, NEG_INF)
    gmax = jnp.max(lg, axis=-1, keepdims=True)
    gidx = jnp.min(jnp.where(lg == gmax, lane - N_EXPERTS, big), axis=-1, keepdims=True)
    g_w = 1.0 / jnp.sum(jnp.where(is_g, jnp.exp(lg - gmax), 0.0), axis=-1, keepdims=True)
    in_grp = (lane < N_EXPERTS) & ((lane // EXP_PER_GROUP) == gidx)
    le = jnp.where(in_grp, logits, NEG_INF)
    e1 = jnp.max(le, axis=-1, keepdims=True)
    i1 = jnp.min(jnp.where(le == e1, lane, big), axis=-1, keepdims=True)
    le2 = jnp.where(lane == i1, NEG_INF, le)
    e2 = jnp.max(le2, axis=-1, keepdims=True)
    i2 = jnp.min(jnp.where(le2 == e2, lane, big), axis=-1, keepdims=True)
    zsum = jnp.sum(jnp.where(in_grp, jnp.exp(le - e1), 0.0), axis=-1, keepdims=True)
    p1 = 1.0 / zsum
    p2 = jnp.exp(e2 - e1) / zsum
    w1 = g_w * p1 / (p1 + p2)
    w2 = g_w * p2 / (p1 + p2)
    oh1 = lane == i1
    oh2 = lane == i2
    ohs = (oh1 | oh2).astype(BF16)
    row = lax.broadcasted_iota(jnp.int32, (tm, tm), 0)
    col = lax.broadcasted_iota(jnp.int32, (tm, tm), 1)
    before = _dot((row > col).astype(BF16), ohs) + carry_ref[0:1, :]
    r1 = jnp.sum(jnp.where(oh1, before, 0.0), axis=-1, keepdims=True)
    r2 = jnp.sum(jnp.where(oh2, before, 0.0), axis=-1, keepdims=True)
    total = carry_ref[0:1, :] + jnp.sum(ohs.astype(F32), axis=0, keepdims=True)
    carry_ref[...] = jnp.broadcast_to(total, carry_ref.shape)
    cnt_ref[...] = jnp.broadcast_to(total, cnt_ref.shape).astype(jnp.int32)
    eid_ref[...] = jnp.where(lane == 0, i1, jnp.where(lane == 1, i2, 0))
    wts_ref[...] = jnp.where(lane == 0, w1, jnp.where(lane == 1, w2, 0.0))
    rank_ref[...] = jnp.where(lane == 0, r1, jnp.where(lane == 1, r2, 0.0)).astype(jnp.int32)


ROW_SUB = D_MODEL // LANES // 2
ROW_DT = jnp.uint32


def _row_slab(ref, r):
    return ref.at[pl.ds(pl.multiple_of(r * ROW_SUB, ROW_SUB), ROW_SUB)]


def _col_block(n, c):
    return pl.ds(c, n, stride=ROW_SUB)


def _pack_cols(x, c):
    as_bits = lambda t: lax.bitcast_convert_type(t.astype(BF16).astype(F32), ROW_DT)
    lo = as_bits(x[:, c * LANES:(c + 1) * LANES])
    hi = as_bits(x[:, (c + ROW_SUB) * LANES:(c + ROW_SUB + 1) * LANES])
    return (lo >> 16) | hi


def _unpack_cols(w):
    return (lax.bitcast_convert_type(w << 16, F32),
            lax.bitcast_convert_type(w & jnp.uint32(0xFFFF0000), F32))


EXPERT_AHEAD = 1


def _expert_kernel(be_ref, succ_ref, *refs, layer):
    src_refs = refs[:EXPERT_AHEAD + 1]
    (h_hbm, wg_hbm, wu_hbm, wd_hbm, o_ref, xbuf_ref, sg_ref, su_ref, sd_ref, wgb_ref, wub_ref, wdb_ref,
     stage_ref, gsem, wsem) = refs[EXPERT_AHEAD + 1:]
    i = pl.program_id(0)
    n_used = be_ref[pl.num_programs(0)]
    nbuf = EXPERT_AHEAD + 1
    slot = i % nbuf
    expert = be_ref[i]
    weights = ((wg_hbm, sg_ref, wgb_ref), (wu_hbm, su_ref, wub_ref), (wd_hbm, sd_ref, wdb_ref))

    def weight_copy(w, e, s):
        hbm, st_ref, _ = weights[w]
        return pltpu.make_async_copy(hbm.at[layer, e], st_ref.at[s], wsem.at[s, w])

    @pl.when(i == 0)
    def _():
        stage_ref[0] = 0
        for w in range(len(weights)):
            weight_copy(w, expert, 0).start()

    @pl.when((i < n_used) & ((i == 0) | (expert != be_ref[jnp.maximum(i - 1, 0)])))
    def _():
        s = stage_ref[0]
        nxt = succ_ref[expert]
        for w, (_, st_ref, wb_ref) in enumerate(weights):
            weight_copy(w, 0, s).wait()
            wb_ref[...] = st_ref[s].astype(BF16)

        @pl.when(nxt >= 0)
        def _():
            for w in range(len(weights)):
                weight_copy(w, nxt, 1 - s).start(priority=1)

        stage_ref[0] = 1 - s

    def gather_copy(src_row, to_slot, t):
        src = h_hbm.at[pl.ds(pl.multiple_of(src_row, ROW_SUB), ROW_SUB)]
        return pltpu.make_async_copy(src, _row_slab(xbuf_ref.at[to_slot], t), gsem.at[to_slot])

    def gather(idx_ref, to_slot):
        for t in range(MOE_BLOCK):
            gather_copy(idx_ref[t], to_slot, t).start()

    @pl.when(i == 0)
    def _():
        for a in range(EXPERT_AHEAD):
            gather(src_refs[a], a)

    @pl.when(i + EXPERT_AHEAD < n_used)
    def _():
        gather(src_refs[EXPERT_AHEAD], (i + EXPERT_AHEAD) % nbuf)

    @pl.when(i < n_used)
    def _():
        for t in range(MOE_BLOCK):
            gather_copy(0, slot, 0).wait()
        x_ref = xbuf_ref.at[slot]
        halves = [_unpack_cols(x_ref[_col_block(MOE_BLOCK, c), :]) for c in range(ROW_SUB)]
        x = jnp.concatenate([lo for lo, _ in halves] + [hi for _, hi in halves], axis=1).astype(BF16)
        hid = (_silu(_dot(x, wgb_ref[...])) * _dot(x, wub_ref[...])).astype(BF16)
        y = _dot(hid, wdb_ref[...])
        for c in range(ROW_SUB):
            o_ref[_col_block(MOE_BLOCK, c), :] = _pack_cols(y, c)

    @pl.when(i >= n_used)
    def _():
        o_ref[...] = jnp.zeros(o_ref.shape, o_ref.dtype)


def _combine_kernel(dest_ref, dnext_ref, h_ref, w_ref, g_ref, b_ref, yrows_hbm, of_ref, ob_ref,
                    ybuf_ref, acc_ref, sem):
    i = pl.program_id(0)
    n = pl.num_programs(0)
    slot = i % 2

    def row_copy(d, to_slot, k, t):
        return pltpu.make_async_copy(_row_slab(yrows_hbm, d), _row_slab(ybuf_ref.at[to_slot, k], t), sem.at[to_slot])

    def gather(d_ref, to_slot):
        def start(t, carry):
            for k in range(TOP_K):
                row_copy(d_ref[TOP_K * t + k], to_slot, k, t).start(priority=k)
            return carry

        lax.fori_loop(0, COMBINE_TB, start, 0, unroll=8)

    @pl.when(i == 0)
    def _():
        gather(dest_ref, 0)

    @pl.when(i + 1 < n)
    def _():
        gather(dnext_ref, 1 - slot)

    for t in range(COMBINE_TB):
        for k in range(TOP_K):
            row_copy(0, slot, k, 0).wait()
    w = w_ref[...]
    y0_ref = ybuf_ref.at[slot, 0]
    y1_ref = ybuf_ref.at[slot, 1]
    for c in range(ROW_SUB):
        cb = _col_block(COMBINE_TB, c)
        lo0, hi0 = _unpack_cols(y0_ref[cb, :])
        lo1, hi1 = _unpack_cols(y1_ref[cb, :])
        acc_ref[:, c * LANES:(c + 1) * LANES] = lo0 * w[:, 0:1] + lo1 * w[:, 1:2]
        acc_ref[:, (c + ROW_SUB) * LANES:(c + ROW_SUB + 1) * LANES] = hi0 * w[:, 0:1] + hi1 * w[:, 1:2]
    y = _layernorm_rows(ALPHA * h_ref[...] + acc_ref[...], g_ref[...], b_ref[...])
    of_ref[...] = y
    ob_ref[...] = y.astype(BF16)


def _moe_layer(hf, hp, w_rg, b_rg, w_re, b_re, w_e_gate, w_e_up, w_e_down, layer, ln_g, ln_b):
    T, D = hf.shape
    n_rows = T * TOP_K + N_EXPERTS * MOE_BLOCK
    n_blocks = n_rows // MOE_BLOCK
    w_r = jnp.concatenate([w_re, w_rg, jnp.zeros((D, LANES - N_EXPERTS - N_GROUPS), F32)], axis=-1)
    b_r = jnp.concatenate([b_re, b_rg, jnp.zeros((LANES - N_EXPERTS - N_GROUPS,), F32)]).reshape(1, LANES)
    w_hi = w_r.astype(BF16)
    w_lo = (w_r - w_hi.astype(F32)).astype(BF16)
    const = lambda i: (0, 0)
    rows = lambda i: (i, 0)
    eid, wts, rank, cnt = pl.pallas_call(
        _router_kernel,
        grid=(T // R_TM,),
        in_specs=[pl.BlockSpec((R_TM, D), rows), pl.BlockSpec((D, LANES), const),
                  pl.BlockSpec((D, LANES), const), pl.BlockSpec((1, LANES), const)],
        out_specs=[pl.BlockSpec((R_TM, LANES), rows), pl.BlockSpec((R_TM, LANES), rows),
                   pl.BlockSpec((R_TM, LANES), rows), pl.BlockSpec((8, LANES), const)],
        out_shape=[jax.ShapeDtypeStruct((T, LANES), jnp.int32), jax.ShapeDtypeStruct((T, LANES), F32),
                   jax.ShapeDtypeStruct((T, LANES), jnp.int32), jax.ShapeDtypeStruct((8, LANES), jnp.int32)],
        scratch_shapes=[pltpu.VMEM((8, LANES), F32)],
        compiler_params=_cp("arbitrary"),
        name="moe_router",
    )(hf, w_hi, w_lo, b_r)
    counts = cnt[0, :N_EXPERTS]
    padded = (counts + MOE_BLOCK - 1) // MOE_BLOCK * MOE_BLOCK
    pends = jnp.cumsum(padded)
    pstarts = pends - padded
    blk_start = jnp.arange(n_blocks, dtype=jnp.int32) * MOE_BLOCK
    blk_exp = jnp.minimum(jnp.sum((pends[None, :] <= blk_start[:, None]).astype(jnp.int32), axis=1), N_EXPERTS - 1)
    sel = eid[:, :TOP_K, None] == jnp.arange(N_EXPERTS, dtype=jnp.int32)
    dest = (jnp.sum(jnp.where(sel, pstarts, 0), axis=-1) + rank[:, :TOP_K]).astype(jnp.int32).reshape(T * TOP_K)
    flat = jnp.full((n_rows,), -1, jnp.int32).at[dest].set(jnp.arange(T * TOP_K, dtype=jnp.int32),
                                                           unique_indices=True)
    row_tok = jnp.where(flat < 0, 0, flat // TOP_K) * ROW_SUB
    assert n_blocks > EXPERT_AHEAD
    any_spec = pl.BlockSpec(memory_space=pl.ANY)
    idx_spec = lambda f: pl.BlockSpec((MOE_BLOCK,), f, memory_space=pltpu.SMEM)
    ids = jnp.arange(N_EXPERTS, dtype=jnp.int32)
    later = (ids[None, :] > ids[:, None]) & (counts[None, :] > 0)
    succ = jnp.min(jnp.where(later, ids[None, :], N_EXPERTS), axis=1)
    succ = jnp.where(succ == N_EXPERTS, -1, succ).astype(jnp.int32)
    yrows = pl.pallas_call(
        functools.partial(_expert_kernel, layer=layer),
        grid_spec=pltpu.PrefetchScalarGridSpec(
            num_scalar_prefetch=2,
            grid=(n_blocks,),
            in_specs=[idx_spec(lambda i, *_, a=a: (jnp.minimum(i + a, n_blocks - 1),)) for a in range(EXPERT_AHEAD + 1)]
            + [any_spec] * 4,
            out_specs=pl.BlockSpec((MOE_BLOCK * ROW_SUB, LANES), lambda i, *_: (i, 0)),
            scratch_shapes=[pltpu.VMEM((EXPERT_AHEAD + 1, MOE_BLOCK * ROW_SUB, LANES), ROW_DT),
                            pltpu.VMEM((2, D, D_FF), F32), pltpu.VMEM((2, D, D_FF), F32), pltpu.VMEM((2, D_FF, D), F32),
                            pltpu.VMEM((D, D_FF), BF16), pltpu.VMEM((D, D_FF), BF16), pltpu.VMEM((D_FF, D), BF16),
                            pltpu.SMEM((1,), jnp.int32),
                            pltpu.SemaphoreType.DMA((EXPERT_AHEAD + 1,)), pltpu.SemaphoreType.DMA((2, 3))]),
        out_shape=jax.ShapeDtypeStruct((n_rows * ROW_SUB, LANES), ROW_DT),
        compiler_params=_cp("arbitrary"),
        name="moe_experts",
    )(jnp.concatenate([blk_exp, pends[-1:] // MOE_BLOCK]).astype(jnp.int32), succ,
      *([row_tok] * (EXPERT_AHEAD + 1)), hp, w_e_gate, w_e_up, w_e_down)
    n_steps = T // COMBINE_TB
    dspec = lambda f: pl.BlockSpec((TOP_K * COMBINE_TB,), f, memory_space=pltpu.SMEM)
    return pl.pallas_call(
        _combine_kernel,
        grid=(n_steps,),
        in_specs=[dspec(lambda i: (i,)), dspec(lambda i: (jnp.minimum(i + 1, n_steps - 1),)),
                  pl.BlockSpec((COMBINE_TB, D), rows), pl.BlockSpec((COMBINE_TB, LANES), rows),
                  pl.BlockSpec((1, D), const), pl.BlockSpec((1, D), const), any_spec],
        out_specs=[pl.BlockSpec((COMBINE_TB, D), rows), pl.BlockSpec((COMBINE_TB, D), rows)],
        out_shape=[jax.ShapeDtypeStruct((T, D), F32), jax.ShapeDtypeStruct((T, D), BF16)],
        scratch_shapes=[pltpu.VMEM((2, TOP_K, COMBINE_TB * ROW_SUB, LANES), ROW_DT),
                        pltpu.VMEM((COMBINE_TB, D), F32), pltpu.SemaphoreType.DMA((2,))],
        compiler_params=_cp("arbitrary"),
        name="moe_combine_ln2",
    )(dest, dest, hf, wts, ln_g.reshape(1, D), ln_b.reshape(1, D), yrows)


def kernel(x, ln_in_g, ln_in_b, rel_bias, w_in, g_cq, w_uq, g_ckv, w_ukv, w_dw_c, b_dw_c, ln_c_g, ln_c_b,
           w_conv_d, b_conv_d, a_log_f, a_log_b, dt_bias_f, dt_bias_b, d_skip, g_norm_d, w_br, w_gate, b_gate,
           w_out, ln1_g, ln1_b, w_rg, b_rg, w_re, b_re, w_e_gate, w_e_up, w_e_down, ln2_g, ln2_b):
    B, S, D = x.shape
    T = B * S
    hf, hb = _layernorm(x.reshape(T, D), ln_in_g, ln_in_b)
    a_bias = _mixer_a_bias(rel_bias, S)
    for l in range(DEPTH):
        w_a, w_r = _in_proj_weights(w_in[l])
        qkv = _matmul(hb, w_a.astype(BF16), F32, 512, 3 * A_WIDTH, "in_proj_a")
        rest = _matmul(hb, w_r.astype(BF16), F32, 512, R_WIDTH // 2, "in_proj_rest")
        y_a = _mixer_a(qkv, a_bias, B, S)
        y_b = _mixer_b(rest, g_cq[l], w_uq[l], g_ckv[l], w_ukv[l], B, S)
        y_c = _mixer_c(rest, w_dw_c[l], b_dw_c[l], ln_c_g[l], ln_c_b[l], B, S)
        y_d = _mixer_d(rest, w_conv_d[l], b_conv_d[l], a_log_f[l], a_log_b[l], dt_bias_f[l], dt_bias_b[l],
                       d_skip[l], g_norm_d[l], B, S)
        h1f, h1p = _merge(hb, hf, (y_a, y_b, y_c, y_d), w_gate[l], b_gate[l], w_br[l], w_out[l], ln1_g[l], ln1_b[l])
        hf, hb = _moe_layer(h1f, h1p, w_rg[l], b_rg[l], w_re[l], b_re[l], w_e_gate, w_e_up, w_e_down, l,
                            ln2_g[l], ln2_b[l])
    return hf.reshape(B, S, D)
```

```python
import functools

import numpy as np
import jax
import jax.numpy as jnp
from jax import lax
from jax.experimental import pallas as pl
from jax.experimental.pallas import tpu as pltpu

F32 = jnp.float32
BF16 = jnp.bfloat16

D_MODEL = 2048
DEPTH = 2
A_HEADS = 8
A_HEAD_DIM = 64
A_WIDTH = A_HEADS * A_HEAD_DIM
A_PATTERNS = ((128, 1), (512, 4), (2048, 16))
A_BAND = 64
REL_BUCKETS = 32
REL_MAX_DIST = 1024
B_HEADS = 8
B_NOPE = 64
B_ROPE = 32
B_V = 64
B_Q_LORA = 512
B_KV_LORA = 256
ROPE_THETA = 10000.0
C_CH = 512
C_KERNEL = 31
D_HEADS = 8
D_HEAD_DIM = 64
D_INNER = D_HEADS * D_HEAD_DIM
D_STATE = 128
D_GROUPS = 2
D_CONV = 5
D_CHUNK = 128
N_BRANCH = 4
BRANCH_W = 512
N_GROUPS = 4
EXP_PER_GROUP = 8
N_EXPERTS = N_GROUPS * EXP_PER_GROUP
TOP_K = 2
D_FF = 512
MOE_BLOCK = 256
ALPHA = (2 * DEPTH) ** 0.25
EPS = 1e-5
NEG_INF = -1e30

LANES = 128
R_GLU, R_XBC, R_CQ, R_Z, R_CKV, R_KR, R_DT = 0, 1024, 2048, 2560, 3072, 3328, 3456
R_WIDTH = 3584
VMEM_LIMIT = 56 * 1024 * 1024


def _cp(*sem):
    return pltpu.CompilerParams(dimension_semantics=sem, vmem_limit_bytes=VMEM_LIMIT)


def _dot(a, b):
    return jnp.dot(a, b, preferred_element_type=F32)


def _dot_nt(a, b):
    return lax.dot_general(a, b, (((1,), (1,)), ((), ())), preferred_element_type=F32)


def _split3(x):
    hi = x.astype(BF16)
    r1 = x - hi.astype(F32)
    mid = r1.astype(BF16)
    lo = (r1 - mid.astype(F32)).astype(BF16)
    return hi, mid, lo


def _layernorm_rows(x, g, b):
    mu = jnp.mean(x, axis=-1, keepdims=True)
    xc = x - mu
    var = jnp.mean(xc * xc, axis=-1, keepdims=True)
    return xc * lax.rsqrt(var + EPS) * g + b


def _sigmoid(x):
    return 1.0 / (1.0 + jnp.exp(-x))


def _silu(x):
    return x * _sigmoid(x)


def _ln_kernel(x_ref, g_ref, b_ref, of_ref, ob_ref):
    y = _layernorm_rows(x_ref[...], g_ref[...], b_ref[...])
    of_ref[...] = y
    ob_ref[...] = y.astype(BF16)


def _layernorm(x, g, b, tm=256):
    T, D = x.shape
    return pl.pallas_call(
        _ln_kernel,
        grid=(T // tm,),
        in_specs=[pl.BlockSpec((tm, D), lambda i: (i, 0)),
                  pl.BlockSpec((1, D), lambda i: (0, 0)),
                  pl.BlockSpec((1, D), lambda i: (0, 0))],
        out_specs=[pl.BlockSpec((tm, D), lambda i: (i, 0)),
                   pl.BlockSpec((tm, D), lambda i: (i, 0))],
        out_shape=[jax.ShapeDtypeStruct((T, D), F32), jax.ShapeDtypeStruct((T, D), BF16)],
        compiler_params=_cp("parallel"),
        name="ln_in",
    )(x, g.reshape(1, D), b.reshape(1, D))


def _mm_kernel(x_ref, w_ref, o_ref):
    o_ref[...] = _dot(x_ref[...], w_ref[...]).astype(o_ref.dtype)


def _matmul(x, w, out_dtype, tm, tn, name):
    M, K = x.shape
    N = w.shape[1]
    return pl.pallas_call(
        _mm_kernel,
        grid=(N // tn, M // tm),
        in_specs=[pl.BlockSpec((tm, K), lambda j, i: (i, 0)),
                  pl.BlockSpec((K, tn), lambda j, i: (0, j))],
        out_specs=pl.BlockSpec((tm, tn), lambda j, i: (i, j)),
        out_shape=jax.ShapeDtypeStruct((M, N), out_dtype),
        compiler_params=_cp("parallel", "parallel"),
        name=name,
    )(x, w)


def _t5_bucket(rel):
    half = REL_BUCKETS // 2
    max_exact = half // 2
    n = np.abs(rel)
    large = max_exact + (np.log(np.maximum(n, 1) / max_exact) / np.log(REL_MAX_DIST / max_exact)
                         * (half - max_exact)).astype(np.int32)
    large = np.minimum(large, half - 1)
    return (rel > 0).astype(np.int32) * half + np.where(n < max_exact, n, large)


def _a_window(L):
    return min(2 * LANES, L)


def _a_bias_tiles(rel_bias, d, L):
    W = _a_window(L)
    offs = (0,) if L == LANES else (0, -A_BAND, -2 * A_BAND)
    qi = np.arange(LANES)[:, None]
    kj = np.arange(W)[None, :]
    rel = np.stack([kj - qi + off for off in offs], axis=0)
    valid = np.abs(rel) <= A_BAND
    onehot = (jnp.asarray(_t5_bucket(rel * d), jnp.int32)[..., None] == jnp.arange(REL_BUCKETS)).astype(F32)
    b = jnp.einsum('vqkb,bh->vhqk', onehot, rel_bias.astype(F32), precision=lax.Precision.HIGHEST)
    b = jnp.where(valid[:, None], b, NEG_INF)
    return b.reshape(len(offs), A_HEADS // 2, 2 * LANES, W)


A_GROUP = 2


def _attn_a_kernel(q_ref, k_ref, v_ref, b16_ref, b4_ref, b1_ref, y_ref,
                   q4_ref, k4_ref, v4_ref, m_ref, l_ref, acc_ref, tmp_ref, *, S):
    (_, d1), (_, d4), (_, d16) = A_PATTERNS
    lane = lax.broadcasted_iota(jnp.int32, (LANES, LANES), 1)
    head0 = lane < A_HEAD_DIM
    scale = A_HEAD_DIM ** -0.5

    def partial_softmax(qs, ks, vs, bias_ref):
        q2 = jnp.concatenate([jnp.where(head0, qs, 0.0), jnp.where(head0, 0.0, qs)], axis=0).astype(BF16)
        s = _dot_nt(q2, ks.astype(BF16)) * scale + bias_ref[...]
        m = jnp.max(s, axis=-1, keepdims=True)
        p = jnp.exp(s - m).astype(BF16)
        num = _dot(p, vs.astype(BF16))
        den = _dot(p, jnp.ones((vs.shape[0], LANES), BF16))
        both = lambda t: jnp.where(head0, t[:LANES], t[LANES:])
        return both(jnp.broadcast_to(m, (2 * LANES, LANES))), both(den), both(num)

    def fold(old, new):
        (m_old, l_old, a_old), (m_new, l_new, a_new) = old, new
        m = jnp.maximum(m_old, m_new)
        c_old = jnp.exp(m_old - m)
        c_new = jnp.exp(m_new - m)
        return m, c_old * l_old + c_new * l_new, c_old * a_old + c_new * a_new

    stat_refs = (m_ref, l_ref, acc_ref)

    def get(c, rows):
        return tuple(ref.at[c][rows, :] for ref in stat_refs)

    def put(c, rows, stats):
        for ref, val in zip(stat_refs, stats):
            ref.at[c][rows, :] = val

    def grouped(n, group, unit):
        def trip(g, carry):
            pending = [unit(g * group + u) for u in range(group)]
            for finish in pending:
                finish()
            return carry

        lax.fori_loop(0, n // group, trip, 0)

    L4 = S // d4
    sub = d16 // d4
    assert S // d16 == LANES and sub == d4
    for c in range(d4):
        cls = pl.ds(c, L4, stride=d4)
        q4_ref[c] = q_ref[cls, :]
        k4_ref[c] = k_ref[cls, :]
        v4_ref[c] = v_ref[cls, :]

    def unit16(t):
        c = t % d4
        rows = pl.ds(t // d4, LANES, stride=sub)
        stats = partial_softmax(q4_ref.at[c][rows, :], k4_ref.at[c][rows, :], v4_ref.at[c][rows, :], b16_ref.at[0])
        return lambda: put(c, rows, stats)

    grouped(d16, A_GROUP, unit16)

    def window(i, L):
        nqb = L // LANES
        ws = pl.multiple_of(jnp.clip(i * LANES - A_BAND, 0, L - 2 * LANES), A_BAND)
        return pl.ds(ws, 2 * LANES), jnp.where(i == 0, 0, jnp.where(i == nqb - 1, 2, 1))

    def unit4(t):
        c = t % d4
        i = t // d4
        keys, var = window(i, L4)
        rows = pl.ds(pl.multiple_of(i * LANES, LANES), LANES)
        stats = partial_softmax(q4_ref.at[c][rows, :], k4_ref.at[c][keys, :], v4_ref.at[c][keys, :], b4_ref.at[var])
        return lambda: put(c, rows, fold(get(c, rows), stats))

    grouped(d4 * (L4 // LANES), A_GROUP, unit4)

    def unit1(i):
        keys, var = window(i, S)
        rows = pl.ds(pl.multiple_of(i * LANES, LANES), LANES)
        stats = partial_softmax(q_ref[rows, :], k_ref[keys, :], v_ref[keys, :], b1_ref.at[var])

        def finish():
            part = pl.ds(pl.multiple_of(i * (LANES // d4), LANES // d4), LANES // d4)
            for n, ref in enumerate(stat_refs):
                for c in range(d4):
                    tmp_ref.at[n][pl.ds(c, LANES // d4, stride=d4), :] = ref.at[c][part, :]
            _, l, a = fold(tuple(tmp_ref[n] for n in range(len(stat_refs))), stats)
            y_ref[rows, :] = (a / l).astype(y_ref.dtype)

        return finish

    grouped(S // LANES, A_GROUP, unit1)


def _mixer_a_bias(rel_bias, S):
    return tuple(_a_bias_tiles(rel_bias, d, S // d) for _, d in A_PATTERNS)


def _mixer_a(qkv, bias, B, S):
    b1, b4, b16 = bias
    npair = A_HEADS // 2
    pair_bias = lambda t: pl.BlockSpec((t.shape[0], None) + t.shape[2:], lambda b, hp: (0, hp, 0, 0))
    slab = lambda first: pl.BlockSpec((S, LANES), lambda b, hp: (b, first + hp))
    return pl.pallas_call(
        functools.partial(_attn_a_kernel, S=S),
        grid=(B, npair),
        in_specs=[slab(0), slab(npair), slab(2 * npair), pair_bias(b16), pair_bias(b4), pair_bias(b1)],
        out_specs=pl.BlockSpec((S, LANES), lambda b, hp: (b, hp)),
        out_shape=jax.ShapeDtypeStruct((B * S, A_WIDTH), BF16),
        scratch_shapes=[pltpu.VMEM((A_PATTERNS[1][1], S // A_PATTERNS[1][1], LANES), F32)] * 6
        + [pltpu.VMEM((3, LANES, LANES), F32)],
        compiler_params=_cp("parallel", "parallel"),
        name="attn_a",
    )(qkv, qkv, qkv, b16, b4, b1)


MLA_Q_SCALE = float((B_NOPE + B_ROPE) ** -0.5 * np.log2(np.e))


def _mla_proj_kernel(cq_ref, ckv_ref, kr_ref, gq_ref, gkv_ref, wqm_ref, wqs_ref, wk_ref, wv_ref, vone_ref,
                     ek_ref, cosq_ref, sinq_ref, csk_ref, q_ref, k_ref, v_ref):
    cq = cq_ref[...]
    xq = (cq * lax.rsqrt(jnp.mean(cq * cq, axis=-1, keepdims=True) + EPS) * gq_ref[...]).astype(BF16)
    ckv = ckv_ref[...]
    xkv = (ckv * lax.rsqrt(jnp.mean(ckv * ckv, axis=-1, keepdims=True) + EPS) * gkv_ref[...]).astype(BF16)
    qm = _dot(xq, wqm_ref[...])
    qs = _dot(xq, wqs_ref[...])
    cosq = cosq_ref[...] * MLA_Q_SCALE
    sinq = sinq_ref[...] * MLA_Q_SCALE
    t = kr_ref[...] * csk_ref[...]
    t_hi = t.astype(BF16)
    t_lo = (t - t_hi.astype(F32)).astype(BF16)
    kk = _dot(xkv, wk_ref[...]) + _dot(t_hi, ek_ref[...]) + _dot(t_lo, ek_ref[...])
    for h in range(B_HEADS):
        sl = slice(h * LANES, (h + 1) * LANES)
        q_ref[:, sl] = (qm[:, sl] * cosq + qs[:, sl] * sinq).astype(BF16)
    k_ref[...] = kk.astype(BF16)
    v_ref[...] = (_dot(xkv, wv_ref[...]) + vone_ref[...]).astype(BF16)


def _mla_attn_kernel(q_ref, k_ref, v_ref, o_ref):
    outs = []
    for hh in range(2):
        sl = slice(hh * LANES, (hh + 1) * LANES)
        s = _dot_nt(q_ref[:, sl], k_ref[:, sl])
        p = jnp.exp2(s - jnp.max(s, axis=-1, keepdims=True))
        outs.append(_dot(p.astype(BF16), v_ref[:, sl]))
    lane = lax.broadcasted_iota(jnp.int32, outs[0].shape, 1)
    acc = jnp.where(lane < B_V, outs[0], outs[1])
    den = pltpu.roll(jnp.where(lane < B_V, outs[1], outs[0]), B_V, axis=1)
    o_ref[...] = (acc / den).astype(o_ref.dtype)


def _mla_tables(S):
    inv_freq = ROPE_THETA ** (-jnp.arange(0, B_ROPE, 2, dtype=F32) / B_ROPE)
    ang = jnp.arange(S, dtype=F32)[:, None] * inv_freq[None]
    cos, sin = jnp.cos(ang), jnp.sin(ang)
    cos2 = jnp.concatenate([cos, cos], axis=-1)
    sin2 = jnp.concatenate([sin, sin], axis=-1)
    ones = jnp.ones((S, B_NOPE), F32)
    zn = jnp.zeros((S, B_NOPE), F32)
    zp = jnp.zeros((S, LANES - B_NOPE - B_ROPE), F32)
    cosq = jnp.concatenate([ones, cos2, zp], axis=-1)
    sinq = jnp.concatenate([zn, sin2, zp], axis=-1)
    csk = jnp.concatenate([cos2, sin2, jnp.zeros((S, LANES - 2 * B_ROPE), F32)], axis=-1)
    return cosq, sinq, csk


def _swap_cols(w):
    half = w.shape[-1] // 2
    return jnp.concatenate([-w[..., half:], w[..., :half]], axis=-1)


def _mla_weights(w_uq, w_ukv):
    dq = B_NOPE + B_ROPE
    wq = w_uq.reshape(B_Q_LORA, B_HEADS, dq)
    zpad = jnp.zeros((B_Q_LORA, B_HEADS, LANES - dq), F32)
    wqm = jnp.concatenate([wq, zpad], axis=-1).reshape(B_Q_LORA, B_HEADS * LANES)
    wqs = jnp.concatenate([jnp.zeros((B_Q_LORA, B_HEADS, B_NOPE), F32), _swap_cols(wq[..., B_NOPE:]), zpad],
                          axis=-1).reshape(B_Q_LORA, B_HEADS * LANES)
    wkv = w_ukv.reshape(B_KV_LORA, B_HEADS, B_NOPE + B_V)
    wk = jnp.concatenate([wkv[..., :B_NOPE], jnp.zeros((B_KV_LORA, B_HEADS, LANES - B_NOPE), F32)],
                         axis=-1).reshape(B_KV_LORA, B_HEADS * LANES)
    zv = jnp.zeros((B_KV_LORA, B_HEADS // 2, LANES - B_V), F32)
    wv_h = wkv[..., B_NOPE:]
    wv = jnp.stack([jnp.concatenate([wv_h[:, 0::2], zv], axis=-1),
                    jnp.concatenate([zv, wv_h[:, 1::2]], axis=-1)], axis=2).reshape(B_KV_LORA, B_HEADS * LANES)
    lane_in_pair = np.arange(B_HEADS * LANES) % (2 * LANES)
    vone = jnp.asarray(((lane_in_pair >= B_V) & (lane_in_pair < LANES + B_V)).astype(np.float32)).reshape(1, -1)
    ek = np.zeros((LANES, B_HEADS, LANES), np.float32)
    for j in range(B_ROPE):
        ek[j, :, B_NOPE + j] = 1.0
        ek[B_ROPE + j, :, B_NOPE + j] = 1.0
    ek = jnp.asarray(ek.reshape(LANES, B_HEADS * LANES))
    return wqm.astype(BF16), wqs.astype(BF16), wk.astype(BF16), wv.astype(BF16), vone, ek.astype(BF16)


def _mixer_b(rest, g_cq, w_uq, g_ckv, w_ukv, B, S, tm=512, tq=256):
    T = B * S
    wqm, wqs, wk, wv, vone, ek = _mla_weights(w_uq, w_ukv)
    cosq, sinq, csk = _mla_tables(S)
    nst = S // tm
    QW = B_HEADS * LANES
    const = lambda i: (0, 0)
    pos = lambda i: (i % nst, 0)
    q, k, v = pl.pallas_call(
        _mla_proj_kernel,
        grid=(T // tm,),
        in_specs=[pl.BlockSpec((tm, B_Q_LORA), lambda i: (i, R_CQ // B_Q_LORA)),
                  pl.BlockSpec((tm, B_KV_LORA), lambda i: (i, R_CKV // B_KV_LORA)),
                  pl.BlockSpec((tm, LANES), lambda i: (i, R_KR // LANES)),
                  pl.BlockSpec((1, B_Q_LORA), const),
                  pl.BlockSpec((1, B_KV_LORA), const),
                  pl.BlockSpec((B_Q_LORA, QW), const),
                  pl.BlockSpec((B_Q_LORA, QW), const),
                  pl.BlockSpec((B_KV_LORA, QW), const),
                  pl.BlockSpec((B_KV_LORA, QW), const),
                  pl.BlockSpec((1, QW), const),
                  pl.BlockSpec((LANES, QW), const),
                  pl.BlockSpec((tm, LANES), pos),
                  pl.BlockSpec((tm, LANES), pos),
                  pl.BlockSpec((tm, LANES), pos)],
        out_specs=[pl.BlockSpec((tm, QW), lambda i: (i, 0)),
                   pl.BlockSpec((tm, QW), lambda i: (i, 0)),
                   pl.BlockSpec((tm, QW), lambda i: (i, 0))],
        out_shape=[jax.ShapeDtypeStruct((T, QW), BF16), jax.ShapeDtypeStruct((T, QW), BF16),
                   jax.ShapeDtypeStruct((T, QW), BF16)],
        compiler_params=_cp("parallel"),
        name="mla_proj",
    )(rest, rest, rest, g_cq.reshape(1, -1), g_ckv.reshape(1, -1), wqm, wqs, wk, wv, vone, ek, cosq, sinq, csk)
    y = pl.pallas_call(
        _mla_attn_kernel,
        grid=(B, B_HEADS // 2, S // tq),
        in_specs=[pl.BlockSpec((None, tq, 2 * LANES), lambda b, hp, i: (b, i, hp)),
                  pl.BlockSpec((None, S, 2 * LANES), lambda b, hp, i: (b, 0, hp)),
                  pl.BlockSpec((None, S, 2 * LANES), lambda b, hp, i: (b, 0, hp))],
        out_specs=pl.BlockSpec((None, tq, 2 * B_V), lambda b, hp, i: (b, i, hp)),
        out_shape=jax.ShapeDtypeStruct((B, S, B_HEADS * B_V), BF16),
        compiler_params=_cp("parallel", "parallel", "arbitrary"),
        name="mla_attn",
    )(q.reshape(B, S, QW), k.reshape(B, S, QW), v.reshape(B, S, QW))
    return y.reshape(T, B_HEADS * B_V)


C_PAD = 16
C_ROWS = 128


SUBLANES = 8


def _tap_span(first, ntaps, rows):
    return rows + ((first + ntaps - 1) // SUBLANES) * SUBLANES


def _depthwise_taps(win_ref, sh_ref, w_ref, bias, ls, first, ntaps, rows):
    acc = jnp.broadcast_to(bias, (rows, LANES))
    span = _tap_span(first, ntaps, rows)
    for ph in range(SUBLANES):
        taps = [j for j in range(ntaps) if (first + j) % SUBLANES == ph]
        if not taps:
            continue
        if len(taps) == 1:
            j = taps[0]
            acc = acc + w_ref[j:j + 1, ls] * win_ref[first + j:first + j + rows, ls]
            continue
        sh_ref[0:span, :] = win_ref[ph:ph + span, ls]
        for j in taps:
            a = (first + j) // SUBLANES * SUBLANES
            acc = acc + w_ref[j:j + 1, ls] * sh_ref[a:a + rows, :]
    return acc


def _fill_window(win_ref, load_rows, r0, i, nblk, rows, pad):
    width = win_ref.shape[1]
    win_ref[pad:pad + rows, :] = load_rows(r0, rows)

    @pl.when(i > 0)
    def _():
        win_ref[0:pad, :] = load_rows(r0 - pad, pad)

    @pl.when(i == 0)
    def _():
        win_ref[0:pad, :] = jnp.zeros((pad, width), F32)

    @pl.when(i < nblk - 1)
    def _():
        win_ref[pad + rows:pad + rows + pad, :] = load_rows(r0 + rows, pad)

    @pl.when(i == nblk - 1)
    def _():
        win_ref[pad + rows:pad + rows + pad, :] = jnp.zeros((pad, width), F32)


def _conv_c_kernel(glu_ref, w_ref, b_ref, g_ref, beta_ref, o_ref, win_ref, sh_ref, acc_ref, *, S):
    i = pl.program_id(1)
    r0 = pl.multiple_of(i * C_ROWS, C_ROWS)

    def glu_rows(start, n):
        rs = pl.ds(pl.multiple_of(start, C_PAD), n)
        return glu_ref[rs, 0:C_CH] * _sigmoid(glu_ref[rs, C_CH:2 * C_CH])

    _fill_window(win_ref, glu_rows, r0, i, S // C_ROWS, C_ROWS, C_PAD)
    first = C_PAD - C_KERNEL // 2
    for lb in range(C_CH // LANES):
        ls = slice(lb * LANES, (lb + 1) * LANES)
        acc_ref[:, ls] = _depthwise_taps(win_ref, sh_ref, w_ref, b_ref[:, ls], ls, first, C_KERNEL, C_ROWS)
    y = _layernorm_rows(acc_ref[...], g_ref[...], beta_ref[...])
    o_ref[...] = _silu(y).astype(o_ref.dtype)


def _mixer_c(rest, w_dw, b_dw, ln_g, ln_b, B, S):
    T = B * S
    const = lambda b, i: (0, 0)
    y = pl.pallas_call(
        functools.partial(_conv_c_kernel, S=S),
        grid=(B, S // C_ROWS),
        in_specs=[pl.BlockSpec((None, S, 2 * C_CH), lambda b, i: (b, 0, R_GLU // (2 * C_CH))),
                  pl.BlockSpec((C_KERNEL, C_CH), const),
                  pl.BlockSpec((1, C_CH), const),
                  pl.BlockSpec((1, C_CH), const),
                  pl.BlockSpec((1, C_CH), const)],
        out_specs=pl.BlockSpec((None, C_ROWS, C_CH), lambda b, i: (b, i, 0)),
        out_shape=jax.ShapeDtypeStruct((B, S, C_CH), BF16),
        scratch_shapes=[pltpu.VMEM((C_ROWS + 2 * C_PAD, C_CH), F32),
                        pltpu.VMEM((_tap_span(C_PAD - C_KERNEL // 2, C_KERNEL, C_ROWS), LANES), F32),
                        pltpu.VMEM((C_ROWS, C_CH), F32)],
        compiler_params=_cp("parallel", "parallel"),
        name="conformer_conv",
    )(rest.reshape(B, S, R_WIDTH), w_dw, b_dw.reshape(1, -1), ln_g.reshape(1, -1), ln_b.reshape(1, -1))
    return y.reshape(T, C_CH)


D_PAD = 8
XBC_W = D_INNER + 2 * D_GROUPS * D_STATE
N_PAIR = D_HEADS // 2


def _pair_expand(v, first):
    lane = lax.broadcasted_iota(jnp.int32, (v.shape[0], LANES), 1)
    lo = jnp.broadcast_to(v[:, first:first + 1], (v.shape[0], LANES))
    hi = jnp.broadcast_to(v[:, first + 1:first + 2], (v.shape[0], LANES))
    return jnp.where(lane < D_HEAD_DIM, lo, hi)


def _ssd_kernel(xbc_ref, z_ref, dt_ref, wc_ref, bc_ref, alog_ref, dtb_ref, dskip_ref, gn_ref, o_ref,
                win_ref, sh_ref, xc_ref, a_ref, dtv_ref, y_ref, st_ref, *, S):
    Q = D_CHUNK
    nchunk = S // Q
    N = D_STATE
    bm0 = D_INNER
    cm0 = D_INNER + D_GROUPS * N

    def conv_body(c, carry):
        r0 = pl.multiple_of(c * Q, Q)
        _fill_window(win_ref, lambda st, n: xbc_ref[pl.ds(pl.multiple_of(st, D_PAD), n), :], r0, c, nchunk, Q, D_PAD)
        for lb in range(XBC_W // LANES):
            ls = slice(lb * LANES, (lb + 1) * LANES)
            acc = _depthwise_taps(win_ref, sh_ref, wc_ref, bc_ref[:, ls], ls, D_PAD - D_CONV // 2, D_CONV, Q)
            xc_ref[pl.ds(r0, Q), ls] = _silu(acc)
        return carry

    lax.fori_loop(0, nchunk, conv_body, 0)

    lane1 = lax.broadcasted_iota(jnp.int32, (1, LANES), 1)
    a_row = jnp.where(lane1 < 2 * D_HEADS, -jnp.exp(alog_ref[...]), 0.0)
    xdt = dt_ref[...] + dtb_ref[...]
    dtv = jnp.maximum(xdt, 0.0) + jnp.log(1.0 + jnp.exp(-jnp.abs(xdt)))
    dtv_ref[...] = dtv
    a_ref[...] = dtv * a_row

    row = lax.broadcasted_iota(jnp.int32, (Q, Q), 0)
    col = lax.broadcasted_iota(jnp.int32, (Q, Q), 1)
    tril = row >= col
    triu = col >= row
    lane = col

    def scan_chunk(c, lower, off, finalize):
        r0 = pl.multiple_of(c * Q, Q)
        rows = pl.ds(r0, Q)
        mask = tril if lower else triu
        tri = mask.astype(BF16)
        a_hi, a_mid, a_lo = _split3(a_ref[rows, :])
        cs = _dot(tri, a_hi) + _dot(tri, a_mid) + _dot(tri, a_lo)
        cs_t = cs.T
        ecs = jnp.exp(cs)
        edge = Q - 1 if lower else 0
        edec = jnp.exp(cs[edge:edge + 1, :] - cs)
        dt_c = dtv_ref[rows, :]
        dt_t = dt_c.T
        dte_t = (dt_c * edec).T
        for g in range(D_GROUPS):
            bg = xc_ref[rows, bm0 + g * N:bm0 + (g + 1) * N]
            cg = xc_ref[rows, cm0 + g * N:cm0 + (g + 1) * N].astype(BF16)
            cb = _dot_nt(cg, bg.astype(BF16))
            bg_t = bg.T
            for pp in range(N_PAIR // D_GROUPS):
                p = g * (N_PAIR // D_GROUPS) + pp
                ps = slice(p * LANES, (p + 1) * LANES)
                x_p = xc_ref[rows, ps]
                ms, bs = [], []
                for hh in range(2):
                    k = off + 2 * p + hh
                    diff = jnp.broadcast_to(cs[:, k:k + 1], (Q, Q)) - cs_t[k:k + 1, :]
                    ms.append((jnp.exp(jnp.where(mask, diff, NEG_INF)) * (cb * dt_t[k:k + 1, :])).astype(BF16))
                    bs.append((bg_t * dte_t[k:k + 1, :]).astype(BF16))
                x_lo = jnp.where(lane < D_HEAD_DIM, x_p, 0.0).astype(BF16)
                x_hi = jnp.where(lane >= D_HEAD_DIM, x_p, 0.0).astype(BF16)
                x2 = jnp.concatenate([x_lo, x_hi], axis=0)
                y_intra = _dot(jnp.concatenate(ms, axis=1), x2)
                hp = st_ref[p]
                ecs_p = _pair_expand(ecs, off + 2 * p)
                y_new = y_intra + _dot(cg, hp.astype(BF16)) * ecs_p
                if lower:
                    y_ref[rows, ps] = y_new
                else:
                    y_ref[rows, ps] = y_ref[rows, ps] + y_new
                st_ref[p] = hp * ecs_p[edge:edge + 1, :] + _dot(jnp.concatenate(bs, axis=1), x2)
        if finalize:
            y = y_ref[rows, :] + xc_ref[rows, 0:D_INNER] * dskip_ref[...]
            gated = y * _silu(z_ref[rows, :])
            out = gated * lax.rsqrt(jnp.mean(gated * gated, axis=-1, keepdims=True) + EPS) * gn_ref[...]
            o_ref[rows, :] = out.astype(o_ref.dtype)

    st_ref[...] = jnp.zeros(st_ref.shape, F32)

    def fwd_body(c, carry):
        scan_chunk(c, True, 0, False)
        return carry

    lax.fori_loop(0, nchunk, fwd_body, 0)
    st_ref[...] = jnp.zeros(st_ref.shape, F32)

    def bwd_body(k, carry):
        scan_chunk(nchunk - 1 - k, False, D_HEADS, True)
        return carry

    lax.fori_loop(0, nchunk, bwd_body, 0)


def _mixer_d(rest, w_conv, b_conv, a_log_f, a_log_b, dt_bias_f, dt_bias_b, d_skip, g_norm, B, S):
    T = B * S
    pad16 = lambda f, b: jnp.concatenate([f, b, jnp.zeros((LANES - 2 * D_HEADS,), F32)]).reshape(1, LANES)
    const = lambda b: (0, 0)
    y = pl.pallas_call(
        functools.partial(_ssd_kernel, S=S),
        grid=(B,),
        in_specs=[pl.BlockSpec((None, S, XBC_W), lambda b: (b, 0, R_XBC // XBC_W)),
                  pl.BlockSpec((None, S, D_INNER), lambda b: (b, 0, R_Z // D_INNER)),
                  pl.BlockSpec((None, S, LANES), lambda b: (b, 0, R_DT // LANES)),
                  pl.BlockSpec((D_CONV, XBC_W), const),
                  pl.BlockSpec((1, XBC_W), const),
                  pl.BlockSpec((1, LANES), const),
                  pl.BlockSpec((1, LANES), const),
                  pl.BlockSpec((1, D_INNER), const),
                  pl.BlockSpec((1, D_INNER), const)],
        out_specs=pl.BlockSpec((None, S, D_INNER), lambda b: (b, 0, 0)),
        out_shape=jax.ShapeDtypeStruct((B, S, D_INNER), BF16),
        scratch_shapes=[pltpu.VMEM((D_CHUNK + 2 * D_PAD, XBC_W), F32),
                        pltpu.VMEM((_tap_span(D_PAD - D_CONV // 2, D_CONV, D_CHUNK), LANES), F32),
                        pltpu.VMEM((S, XBC_W), F32),
                        pltpu.VMEM((S, LANES), F32),
                        pltpu.VMEM((S, LANES), F32),
                        pltpu.VMEM((S, D_INNER), F32),
                        pltpu.VMEM((N_PAIR, D_STATE, LANES), F32)],
        compiler_params=_cp("parallel"),
        name="ssd_mixer",
    )(rest.reshape(B, S, R_WIDTH), rest.reshape(B, S, R_WIDTH), rest.reshape(B, S, R_WIDTH),
      w_conv, b_conv.reshape(1, -1), pad16(a_log_f, a_log_b), pad16(dt_bias_f, dt_bias_b),
      jnp.repeat(d_skip, D_HEAD_DIM).reshape(1, -1), g_norm.reshape(1, -1))
    return y.reshape(T, D_INNER)


def _in_proj_weights(w_in_l):
    o = np.cumsum((0, A_WIDTH, A_WIDTH, A_WIDTH, B_Q_LORA, B_KV_LORA, B_ROPE, 2 * C_CH,
                   D_INNER, D_INNER, D_GROUPS * D_STATE, D_GROUPS * D_STATE, 2 * D_HEADS)).tolist()
    seg = lambda n: w_in_l[:, o[n]:o[n + 1]]
    w_a = w_in_l[:, :o[3]]
    cq, ckv, kr, glu, z, xs, bm, cm, dt = (seg(n) for n in range(3, 12))
    zeros = lambda n: jnp.zeros((w_in_l.shape[0], n), w_in_l.dtype)
    w_r = jnp.concatenate([glu, xs, bm, cm, cq, z, ckv,
                           kr, _swap_cols(kr), zeros(LANES - 2 * B_ROPE),
                           dt, zeros(LANES - 2 * D_HEADS)], axis=-1)
    assert w_r.shape[1] == R_WIDTH
    return w_a, w_r


def _merge_gate_kernel(h_ref, ya_ref, yb_ref, yc_ref, yd_ref, wg_ref, bg_ref, wbr_ref, o_ref):
    h = h_ref[...]
    acc = None
    for i, y_ref in enumerate((ya_ref, yb_ref, yc_ref, yd_ref)):
        gate = _sigmoid(_dot(h, wg_ref[i]) + bg_ref[i])
        term = gate * _dot(y_ref[...], wbr_ref[i])
        acc = term if acc is None else acc + term
    o_ref[...] = acc.astype(o_ref.dtype)


def _out_ln_kernel(m_ref, w_ref, h_ref, g_ref, b_ref, of_ref, op_ref):
    y = _layernorm_rows(ALPHA * h_ref[...] + _dot(m_ref[...], w_ref[...]), g_ref[...], b_ref[...])
    of_ref[...] = y
    for c in range(ROW_SUB):
        op_ref[_col_block(y.shape[0], c), :] = _pack_cols(y, c)


def _merge(hb, hf, branches, w_gate, b_gate, w_br, w_out, ln_g, ln_b, tm=512, tn=512, tm2=512):
    T, D = hb.shape
    ybs = pl.BlockSpec((tm, BRANCH_W), lambda j, i: (i, 0))
    merged = pl.pallas_call(
        _merge_gate_kernel,
        grid=(D // tn, T // tm),
        in_specs=[pl.BlockSpec((tm, D), lambda j, i: (i, 0)), ybs, ybs, ybs, ybs,
                  pl.BlockSpec((N_BRANCH, D, tn), lambda j, i: (0, 0, j)),
                  pl.BlockSpec((N_BRANCH, 1, tn), lambda j, i: (0, 0, j)),
                  pl.BlockSpec((N_BRANCH, BRANCH_W, tn), lambda j, i: (0, 0, j))],
        out_specs=pl.BlockSpec((tm, tn), lambda j, i: (i, j)),
        out_shape=jax.ShapeDtypeStruct((T, D), BF16),
        compiler_params=_cp("parallel", "parallel"),
        name="merge_gate",
    )(hb, *branches, w_gate.astype(BF16), b_gate.reshape(N_BRANCH, 1, D), w_br.astype(BF16))
    const = lambda i: (0, 0)
    rows = lambda i: (i, 0)
    return pl.pallas_call(
        _out_ln_kernel,
        grid=(T // tm2,),
        in_specs=[pl.BlockSpec((tm2, D), rows), pl.BlockSpec((D, D), const), pl.BlockSpec((tm2, D), rows),
                  pl.BlockSpec((1, D), const), pl.BlockSpec((1, D), const)],
        out_specs=[pl.BlockSpec((tm2, D), rows), pl.BlockSpec((tm2 * ROW_SUB, LANES), rows)],
        out_shape=[jax.ShapeDtypeStruct((T, D), F32), jax.ShapeDtypeStruct((T * ROW_SUB, LANES), ROW_DT)],
        compiler_params=_cp("parallel"),
        name="out_proj_ln1",
    )(merged, w_out.astype(BF16), hf, ln_g.reshape(1, D), ln_b.reshape(1, D))


R_TM = 512
COMBINE_TB = 256


def _router_kernel(h_ref, whi_ref, wlo_ref, b_ref, eid_ref, wts_ref, rank_ref, cnt_ref, carry_ref):
    i = pl.program_id(0)

    @pl.when(i == 0)
    def _():
        carry_ref[...] = jnp.zeros(carry_ref.shape, F32)

    x = h_ref[...]
    tm = x.shape[0]
    xh = x.astype(BF16)
    xl = (x - xh.astype(F32)).astype(BF16)
    whi = whi_ref[...]
    logits = _dot(xh, whi) + _dot(xh, wlo_ref[...]) + _dot(xl, whi) + b_ref[...]
    lane = lax.broadcasted_iota(jnp.int32, (tm, LANES), 1)
    big = jnp.int32(4 * LANES)
    is_g = (lane >= N_EXPERTS) & (lane < N_EXPERTS + N_GROUPS)
    lg = jnp.where(is_g, logits, NEG_INF)
    gmax = jnp.max(lg, axis=-1, keepdims=True)
    gidx = jnp.min(jnp.where(lg == gmax, lane - N_EXPERTS, big), axis=-1, keepdims=True)
    g_w = 1.0 / jnp.sum(jnp.where(is_g, jnp.exp(lg - gmax), 0.0), axis=-1, keepdims=True)
    in_grp = (lane < N_EXPERTS) & ((lane // EXP_PER_GROUP) == gidx)
    le = jnp.where(in_grp, logits, NEG_INF)
    e1 = jnp.max(le, axis=-1, keepdims=True)
    i1 = jnp.min(jnp.where(le == e1, lane, big), axis=-1, keepdims=True)
    le2 = jnp.where(lane == i1, NEG_INF, le)
    e2 = jnp.max(le2, axis=-1, keepdims=True)
    i2 = jnp.min(jnp.where(le2 == e2, lane, big), axis=-1, keepdims=True)
    zsum = jnp.sum(jnp.where(in_grp, jnp.exp(le - e1), 0.0), axis=-1, keepdims=True)
    p1 = 1.0 / zsum
    p2 = jnp.exp(e2 - e1) / zsum
    w1 = g_w * p1 / (p1 + p2)
    w2 = g_w * p2 / (p1 + p2)
    oh1 = lane == i1
    oh2 = lane == i2
    ohs = (oh1 | oh2).astype(BF16)
    row = lax.broadcasted_iota(jnp.int32, (tm, tm), 0)
    col = lax.broadcasted_iota(jnp.int32, (tm, tm), 1)
    before = _dot((row > col).astype(BF16), ohs) + carry_ref[0:1, :]
    r1 = jnp.sum(jnp.where(oh1, before, 0.0), axis=-1, keepdims=True)
    r2 = jnp.sum(jnp.where(oh2, before, 0.0), axis=-1, keepdims=True)
    total = carry_ref[0:1, :] + jnp.sum(ohs.astype(F32), axis=0, keepdims=True)
    carry_ref[...] = jnp.broadcast_to(total, carry_ref.shape)
    cnt_ref[...] = jnp.broadcast_to(total, cnt_ref.shape).astype(jnp.int32)
    eid_ref[...] = jnp.where(lane == 0, i1, jnp.where(lane == 1, i2, 0))
    wts_ref[...] = jnp.where(lane == 0, w1, jnp.where(lane == 1, w2, 0.0))
    rank_ref[...] = jnp.where(lane == 0, r1, jnp.where(lane == 1, r2, 0.0)).astype(jnp.int32)


ROW_SUB = D_MODEL // LANES // 2
ROW_DT = jnp.uint32


def _row_slab(ref, r):
    return ref.at[pl.ds(pl.multiple_of(r * ROW_SUB, ROW_SUB), ROW_SUB)]


def _col_block(n, c):
    return pl.ds(c, n, stride=ROW_SUB)


def _pack_cols(x, c):
    as_bits = lambda t: lax.bitcast_convert_type(t.astype(BF16).astype(F32), ROW_DT)
    lo = as_bits(x[:, c * LANES:(c + 1) * LANES])
    hi = as_bits(x[:, (c + ROW_SUB) * LANES:(c + ROW_SUB + 1) * LANES])
    return (lo >> 16) | hi


def _unpack_cols(w):
    return (lax.bitcast_convert_type(w << 16, F32),
            lax.bitcast_convert_type(w & jnp.uint32(0xFFFF0000), F32))


EXPERT_AHEAD = 3


def _expert_kernel(be_ref, succ_ref, *refs, layer):
    src_refs = refs[:EXPERT_AHEAD + 1]
    (h_hbm, wg_hbm, wu_hbm, wd_hbm, o_ref, xbuf_ref, sg_ref, su_ref, sd_ref, wgb_ref, wub_ref, wdb_ref,
     stage_ref, gsem, wsem) = refs[EXPERT_AHEAD + 1:]
    i = pl.program_id(0)
    n_used = be_ref[pl.num_programs(0)]
    nbuf = EXPERT_AHEAD + 1
    slot = i % nbuf
    expert = be_ref[i]
    weights = ((wg_hbm, sg_ref, wgb_ref), (wu_hbm, su_ref, wub_ref), (wd_hbm, sd_ref, wdb_ref))

    def weight_copy(w, e, s):
        hbm, st_ref, _ = weights[w]
        return pltpu.make_async_copy(hbm.at[layer, e], st_ref.at[s], wsem.at[s, w])

    @pl.when(i == 0)
    def _():
        stage_ref[0] = 0
        for w in range(len(weights)):
            weight_copy(w, expert, 0).start()

    @pl.when((i < n_used) & ((i == 0) | (expert != be_ref[jnp.maximum(i - 1, 0)])))
    def _():
        s = stage_ref[0]
        nxt = succ_ref[expert]
        for w, (_, st_ref, wb_ref) in enumerate(weights):
            weight_copy(w, 0, s).wait()
            wb_ref[...] = st_ref[s].astype(BF16)

        @pl.when(nxt >= 0)
        def _():
            for w in range(len(weights)):
                weight_copy(w, nxt, 1 - s).start(priority=1)

        stage_ref[0] = 1 - s

    def gather_copy(src_row, to_slot, t):
        src = h_hbm.at[pl.ds(pl.multiple_of(src_row, ROW_SUB), ROW_SUB)]
        return pltpu.make_async_copy(src, _row_slab(xbuf_ref.at[to_slot], t), gsem.at[to_slot])

    def gather(idx_ref, to_slot):
        for t in range(MOE_BLOCK):
            gather_copy(idx_ref[t], to_slot, t).start()

    @pl.when(i == 0)
    def _():
        for a in range(EXPERT_AHEAD):
            gather(src_refs[a], a)

    @pl.when(i + EXPERT_AHEAD < n_used)
    def _():
        gather(src_refs[EXPERT_AHEAD], (i + EXPERT_AHEAD) % nbuf)

    @pl.when(i < n_used)
    def _():
        for t in range(MOE_BLOCK):
            gather_copy(0, slot, 0).wait()
        x_ref = xbuf_ref.at[slot]
        halves = [_unpack_cols(x_ref[_col_block(MOE_BLOCK, c), :]) for c in range(ROW_SUB)]
        x = jnp.concatenate([lo for lo, _ in halves] + [hi for _, hi in halves], axis=1).astype(BF16)
        hid = (_silu(_dot(x, wgb_ref[...])) * _dot(x, wub_ref[...])).astype(BF16)
        y = _dot(hid, wdb_ref[...])
        for c in range(ROW_SUB):
            o_ref[_col_block(MOE_BLOCK, c), :] = _pack_cols(y, c)

    @pl.when(i >= n_used)
    def _():
        o_ref[...] = jnp.zeros(o_ref.shape, o_ref.dtype)


def _combine_kernel(dest_ref, dnext_ref, h_ref, w_ref, g_ref, b_ref, yrows_hbm, of_ref, ob_ref,
                    ybuf_ref, acc_ref, sem):
    i = pl.program_id(0)
    n = pl.num_programs(0)
    slot = i % 2

    def row_copy(d, to_slot, k, t):
        return pltpu.make_async_copy(_row_slab(yrows_hbm, d), _row_slab(ybuf_ref.at[to_slot, k], t), sem.at[to_slot])

    def gather(d_ref, to_slot):
        def start(t, carry):
            for k in range(TOP_K):
                row_copy(d_ref[TOP_K * t + k], to_slot, k, t).start(priority=k)
            return carry

        lax.fori_loop(0, COMBINE_TB, start, 0, unroll=8)

    @pl.when(i == 0)
    def _():
        gather(dest_ref, 0)

    @pl.when(i + 1 < n)
    def _():
        gather(dnext_ref, 1 - slot)

    for t in range(COMBINE_TB):
        for k in range(TOP_K):
            row_copy(0, slot, k, 0).wait()
    w = w_ref[...]
    y0_ref = ybuf_ref.at[slot, 0]
    y1_ref = ybuf_ref.at[slot, 1]
    for c in range(ROW_SUB):
        cb = _col_block(COMBINE_TB, c)
        lo0, hi0 = _unpack_cols(y0_ref[cb, :])
        lo1, hi1 = _unpack_cols(y1_ref[cb, :])
        acc_ref[:, c * LANES:(c + 1) * LANES] = lo0 * w[:, 0:1] + lo1 * w[:, 1:2]
        acc_ref[:, (c + ROW_SUB) * LANES:(c + ROW_SUB + 1) * LANES] = hi0 * w[:, 0:1] + hi1 * w[:, 1:2]
    y = _layernorm_rows(ALPHA * h_ref[...] + acc_ref[...], g_ref[...], b_ref[...])
    of_ref[...] = y
    ob_ref[...] = y.astype(BF16)


def _moe_layer(hf, hp, w_rg, b_rg, w_re, b_re, w_e_gate, w_e_up, w_e_down, layer, ln_g, ln_b):
    T, D = hf.shape
    n_rows = T * TOP_K + N_EXPERTS * MOE_BLOCK
    n_blocks = n_rows // MOE_BLOCK
    w_r = jnp.concatenate([w_re, w_rg, jnp.zeros((D, LANES - N_EXPERTS - N_GROUPS), F32)], axis=-1)
    b_r = jnp.concatenate([b_re, b_rg, jnp.zeros((LANES - N_EXPERTS - N_GROUPS,), F32)]).reshape(1, LANES)
    w_hi = w_r.astype(BF16)
    w_lo = (w_r - w_hi.astype(F32)).astype(BF16)
    const = lambda i: (0, 0)
    rows = lambda i: (i, 0)
    eid, wts, rank, cnt = pl.pallas_call(
        _router_kernel,
        grid=(T // R_TM,),
        in_specs=[pl.BlockSpec((R_TM, D), rows), pl.BlockSpec((D, LANES), const),
                  pl.BlockSpec((D, LANES), const), pl.BlockSpec((1, LANES), const)],
        out_specs=[pl.BlockSpec((R_TM, LANES), rows), pl.BlockSpec((R_TM, LANES), rows),
                   pl.BlockSpec((R_TM, LANES), rows), pl.BlockSpec((8, LANES), const)],
        out_shape=[jax.ShapeDtypeStruct((T, LANES), jnp.int32), jax.ShapeDtypeStruct((T, LANES), F32),
                   jax.ShapeDtypeStruct((T, LANES), jnp.int32), jax.ShapeDtypeStruct((8, LANES), jnp.int32)],
        scratch_shapes=[pltpu.VMEM((8, LANES), F32)],
        compiler_params=_cp("arbitrary"),
        name="moe_router",
    )(hf, w_hi, w_lo, b_r)
    counts = cnt[0, :N_EXPERTS]
    padded = (counts + MOE_BLOCK - 1) // MOE_BLOCK * MOE_BLOCK
    pends = jnp.cumsum(padded)
    pstarts = pends - padded
    blk_start = jnp.arange(n_blocks, dtype=jnp.int32) * MOE_BLOCK
    blk_exp = jnp.minimum(jnp.sum((pends[None, :] <= blk_start[:, None]).astype(jnp.int32), axis=1), N_EXPERTS - 1)
    sel = eid[:, :TOP_K, None] == jnp.arange(N_EXPERTS, dtype=jnp.int32)
    dest = (jnp.sum(jnp.where(sel, pstarts, 0), axis=-1) + rank[:, :TOP_K]).astype(jnp.int32).reshape(T * TOP_K)
    flat = jnp.full((n_rows,), -1, jnp.int32).at[dest].set(jnp.arange(T * TOP_K, dtype=jnp.int32),
                                                           unique_indices=True)
    row_tok = jnp.where(flat < 0, 0, flat // TOP_K) * ROW_SUB
    assert n_blocks > EXPERT_AHEAD
    any_spec = pl.BlockSpec(memory_space=pl.ANY)
    idx_spec = lambda f: pl.BlockSpec((MOE_BLOCK,), f, memory_space=pltpu.SMEM)
    ids = jnp.arange(N_EXPERTS, dtype=jnp.int32)
    later = (ids[None, :] > ids[:, None]) & (counts[None, :] > 0)
    succ = jnp.min(jnp.where(later, ids[None, :], N_EXPERTS), axis=1)
    succ = jnp.where(succ == N_EXPERTS, -1, succ).astype(jnp.int32)
    yrows = pl.pallas_call(
        functools.partial(_expert_kernel, layer=layer),
        grid_spec=pltpu.PrefetchScalarGridSpec(
            num_scalar_prefetch=2,
            grid=(n_blocks,),
            in_specs=[idx_spec(lambda i, *_, a=a: (jnp.minimum(i + a, n_blocks - 1),)) for a in range(EXPERT_AHEAD + 1)]
            + [any_spec] * 4,
            out_specs=pl.BlockSpec((MOE_BLOCK * ROW_SUB, LANES), lambda i, *_: (i, 0)),
            scratch_shapes=[pltpu.VMEM((EXPERT_AHEAD + 1, MOE_BLOCK * ROW_SUB, LANES), ROW_DT),
                            pltpu.VMEM((2, D, D_FF), F32), pltpu.VMEM((2, D, D_FF), F32), pltpu.VMEM((2, D_FF, D), F32),
                            pltpu.VMEM((D, D_FF), BF16), pltpu.VMEM((D, D_FF), BF16), pltpu.VMEM((D_FF, D), BF16),
                            pltpu.SMEM((1,), jnp.int32),
                            pltpu.SemaphoreType.DMA((EXPERT_AHEAD + 1,)), pltpu.SemaphoreType.DMA((2, 3))]),
        out_shape=jax.ShapeDtypeStruct((n_rows * ROW_SUB, LANES), ROW_DT),
        compiler_params=_cp("arbitrary"),
        name="moe_experts",
    )(jnp.concatenate([blk_exp, pends[-1:] // MOE_BLOCK]).astype(jnp.int32), succ,
      *([row_tok] * (EXPERT_AHEAD + 1)), hp, w_e_gate, w_e_up, w_e_down)
    n_steps = T // COMBINE_TB
    dspec = lambda f: pl.BlockSpec((TOP_K * COMBINE_TB,), f, memory_space=pltpu.SMEM)
    return pl.pallas_call(
        _combine_kernel,
        grid=(n_steps,),
        in_specs=[dspec(lambda i: (i,)), dspec(lambda i: (jnp.minimum(i + 1, n_steps - 1),)),
                  pl.BlockSpec((COMBINE_TB, D), rows), pl.BlockSpec((COMBINE_TB, LANES), rows),
                  pl.BlockSpec((1, D), const), pl.BlockSpec((1, D), const), any_spec],
        out_specs=[pl.BlockSpec((COMBINE_TB, D), rows), pl.BlockSpec((COMBINE_TB, D), rows)],
        out_shape=[jax.ShapeDtypeStruct((T, D), F32), jax.ShapeDtypeStruct((T, D), BF16)],
        scratch_shapes=[pltpu.VMEM((2, TOP_K, COMBINE_TB * ROW_SUB, LANES), ROW_DT),
                        pltpu.VMEM((COMBINE_TB, D), F32), pltpu.SemaphoreType.DMA((2,))],
        compiler_params=_cp("arbitrary"),
        name="moe_combine_ln2",
    )(dest, dest, hf, wts, ln_g.reshape(1, D), ln_b.reshape(1, D), yrows)


def kernel(x, ln_in_g, ln_in_b, rel_bias, w_in, g_cq, w_uq, g_ckv, w_ukv, w_dw_c, b_dw_c, ln_c_g, ln_c_b,
           w_conv_d, b_conv_d, a_log_f, a_log_b, dt_bias_f, dt_bias_b, d_skip, g_norm_d, w_br, w_gate, b_gate,
           w_out, ln1_g, ln1_b, w_rg, b_rg, w_re, b_re, w_e_gate, w_e_up, w_e_down, ln2_g, ln2_b):
    B, S, D = x.shape
    T = B * S
    hf, hb = _layernorm(x.reshape(T, D), ln_in_g, ln_in_b)
    a_bias = _mixer_a_bias(rel_bias, S)
    for l in range(DEPTH):
        w_a, w_r = _in_proj_weights(w_in[l])
        qkv = _matmul(hb, w_a.astype(BF16), F32, 512, 3 * A_WIDTH, "in_proj_a")
        rest = _matmul(hb, w_r.astype(BF16), F32, 512, R_WIDTH // 2, "in_proj_rest")
        y_a = _mixer_a(qkv, a_bias, B, S)
        y_b = _mixer_b(rest, g_cq[l], w_uq[l], g_ckv[l], w_ukv[l], B, S)
        y_c = _mixer_c(rest, w_dw_c[l], b_dw_c[l], ln_c_g[l], ln_c_b[l], B, S)
        y_d = _mixer_d(rest, w_conv_d[l], b_conv_d[l], a_log_f[l], a_log_b[l], dt_bias_f[l], dt_bias_b[l],
                       d_skip[l], g_norm_d[l], B, S)
        h1f, h1p = _merge(hb, hf, (y_a, y_b, y_c, y_d), w_gate[l], b_gate[l], w_br[l], w_out[l], ln1_g[l], ln1_b[l])
        hf, hb = _moe_layer(h1f, h1p, w_rg[l], b_rg[l], w_re[l], b_re[l], w_e_gate, w_e_up, w_e_down, l,
                            ln2_g[l], ln2_b[l])
    return hf.reshape(B, S, D)
```

```python
import functools

import numpy as np
import jax
import jax.numpy as jnp
from jax import lax
from jax.experimental import pallas as pl
from jax.experimental.pallas import tpu as pltpu

F32 = jnp.float32
BF16 = jnp.bfloat16

D_MODEL = 2048
DEPTH = 2
A_HEADS = 8
A_HEAD_DIM = 64
A_WIDTH = A_HEADS * A_HEAD_DIM
A_PATTERNS = ((128, 1), (512, 4), (2048, 16))
A_BAND = 64
REL_BUCKETS = 32
REL_MAX_DIST = 1024
B_HEADS = 8
B_NOPE = 64
B_ROPE = 32
B_V = 64
B_Q_LORA = 512
B_KV_LORA = 256
ROPE_THETA = 10000.0
C_CH = 512
C_KERNEL = 31
D_HEADS = 8
D_HEAD_DIM = 64
D_INNER = D_HEADS * D_HEAD_DIM
D_STATE = 128
D_GROUPS = 2
D_CONV = 5
D_CHUNK = 128
N_BRANCH = 4
BRANCH_W = 512
N_GROUPS = 4
EXP_PER_GROUP = 8
N_EXPERTS = N_GROUPS * EXP_PER_GROUP
TOP_K = 2
D_FF = 512
MOE_BLOCK = 256
ALPHA = (2 * DEPTH) ** 0.25
EPS = 1e-5
NEG_INF = -1e30

LANES = 128
R_GLU, R_XBC, R_CQ, R_Z, R_CKV, R_KR, R_DT = 0, 1024, 2048, 2560, 3072, 3328, 3456
R_WIDTH = 3584
VMEM_LIMIT = 56 * 1024 * 1024


def _cp(*sem):
    return pltpu.CompilerParams(dimension_semantics=sem, vmem_limit_bytes=VMEM_LIMIT)


def _dot(a, b):
    return jnp.dot(a, b, preferred_element_type=F32)


def _dot_nt(a, b):
    return lax.dot_general(a, b, (((1,), (1,)), ((), ())), preferred_element_type=F32)


def _split3(x):
    hi = x.astype(BF16)
    r1 = x - hi.astype(F32)
    mid = r1.astype(BF16)
    lo = (r1 - mid.astype(F32)).astype(BF16)
    return hi, mid, lo


def _layernorm_rows(x, g, b):
    mu = jnp.mean(x, axis=-1, keepdims=True)
    xc = x - mu
    var = jnp.mean(xc * xc, axis=-1, keepdims=True)
    return xc * lax.rsqrt(var + EPS) * g + b


def _sigmoid(x):
    return 1.0 / (1.0 + jnp.exp(-x))


def _silu(x):
    return x * _sigmoid(x)


def _ln_kernel(x_ref, g_ref, b_ref, of_ref, ob_ref):
    y = _layernorm_rows(x_ref[...], g_ref[...], b_ref[...])
    of_ref[...] = y
    ob_ref[...] = y.astype(BF16)


def _layernorm(x, g, b, tm=256):
    T, D = x.shape
    return pl.pallas_call(
        _ln_kernel,
        grid=(T // tm,),
        in_specs=[pl.BlockSpec((tm, D), lambda i: (i, 0)),
                  pl.BlockSpec((1, D), lambda i: (0, 0)),
                  pl.BlockSpec((1, D), lambda i: (0, 0))],
        out_specs=[pl.BlockSpec((tm, D), lambda i: (i, 0)),
                   pl.BlockSpec((tm, D), lambda i: (i, 0))],
        out_shape=[jax.ShapeDtypeStruct((T, D), F32), jax.ShapeDtypeStruct((T, D), BF16)],
        compiler_params=_cp("parallel"),
        name="ln_in",
    )(x, g.reshape(1, D), b.reshape(1, D))


def _mm_kernel(x_ref, w_ref, o_ref):
    o_ref[...] = _dot(x_ref[...], w_ref[...]).astype(o_ref.dtype)


def _matmul(x, w, out_dtype, tm, tn, name):
    M, K = x.shape
    N = w.shape[1]
    return pl.pallas_call(
        _mm_kernel,
        grid=(N // tn, M // tm),
        in_specs=[pl.BlockSpec((tm, K), lambda j, i: (i, 0)),
                  pl.BlockSpec((K, tn), lambda j, i: (0, j))],
        out_specs=pl.BlockSpec((tm, tn), lambda j, i: (i, j)),
        out_shape=jax.ShapeDtypeStruct((M, N), out_dtype),
        compiler_params=_cp("parallel", "parallel"),
        name=name,
    )(x, w)


def _t5_bucket(rel):
    half = REL_BUCKETS // 2
    max_exact = half // 2
    n = np.abs(rel)
    large = max_exact + (np.log(np.maximum(n, 1) / max_exact) / np.log(REL_MAX_DIST / max_exact)
                         * (half - max_exact)).astype(np.int32)
    large = np.minimum(large, half - 1)
    return (rel > 0).astype(np.int32) * half + np.where(n < max_exact, n, large)


def _a_window(L):
    return min(2 * LANES, L)


def _a_bias_tiles(rel_bias, d, L):
    W = _a_window(L)
    offs = (0,) if L == LANES else (0, -A_BAND, -2 * A_BAND)
    qi = np.arange(LANES)[:, None]
    kj = np.arange(W)[None, :]
    rel = np.stack([kj - qi + off for off in offs], axis=0)
    valid = np.abs(rel) <= A_BAND
    onehot = (jnp.asarray(_t5_bucket(rel * d), jnp.int32)[..., None] == jnp.arange(REL_BUCKETS)).astype(F32)
    b = jnp.einsum('vqkb,bh->vhqk', onehot, rel_bias.astype(F32), precision=lax.Precision.HIGHEST)
    b = jnp.where(valid[:, None], b, NEG_INF)
    return b.reshape(len(offs), A_HEADS // 2, 2 * LANES, W)


A_GROUP = 2


def _attn_a_kernel(q_ref, k_ref, v_ref, b16_ref, b4_ref, b1_ref, y_ref,
                   q4_ref, k4_ref, v4_ref, m_ref, l_ref, acc_ref, tmp_ref, *, S):
    (_, d1), (_, d4), (_, d16) = A_PATTERNS
    lane = lax.broadcasted_iota(jnp.int32, (LANES, LANES), 1)
    head0 = lane < A_HEAD_DIM
    scale = A_HEAD_DIM ** -0.5

    def partial_softmax(qs, ks, vs, bias_ref):
        q2 = jnp.concatenate([jnp.where(head0, qs, 0.0), jnp.where(head0, 0.0, qs)], axis=0).astype(BF16)
        s = _dot_nt(q2, ks.astype(BF16)) * scale + bias_ref[...]
        m = jnp.max(s, axis=-1, keepdims=True)
        p = jnp.exp(s - m).astype(BF16)
        num = _dot(p, vs.astype(BF16))
        den = _dot(p, jnp.ones((vs.shape[0], LANES), BF16))
        both = lambda t: jnp.where(head0, t[:LANES], t[LANES:])
        return both(jnp.broadcast_to(m, (2 * LANES, LANES))), both(den), both(num)

    def fold(old, new):
        (m_old, l_old, a_old), (m_new, l_new, a_new) = old, new
        m = jnp.maximum(m_old, m_new)
        c_old = jnp.exp(m_old - m)
        c_new = jnp.exp(m_new - m)
        return m, c_old * l_old + c_new * l_new, c_old * a_old + c_new * a_new

    stat_refs = (m_ref, l_ref, acc_ref)

    def get(c, rows):
        return tuple(ref.at[c][rows, :] for ref in stat_refs)

    def put(c, rows, stats):
        for ref, val in zip(stat_refs, stats):
            ref.at[c][rows, :] = val

    def grouped(n, group, unit):
        def trip(g, carry):
            pending = [unit(g * group + u) for u in range(group)]
            for finish in pending:
                finish()
            return carry

        lax.fori_loop(0, n // group, trip, 0)

    L4 = S // d4
    sub = d16 // d4
    assert S // d16 == LANES and sub == d4
    for c in range(d4):
        cls = pl.ds(c, L4, stride=d4)
        q4_ref[c] = q_ref[cls, :]
        k4_ref[c] = k_ref[cls, :]
        v4_ref[c] = v_ref[cls, :]

    def unit16(t):
        c = t % d4
        rows = pl.ds(t // d4, LANES, stride=sub)
        stats = partial_softmax(q4_ref.at[c][rows, :], k4_ref.at[c][rows, :], v4_ref.at[c][rows, :], b16_ref.at[0])
        return lambda: put(c, rows, stats)

    grouped(d16, A_GROUP, unit16)

    def window(i, L):
        nqb = L // LANES
        ws = pl.multiple_of(jnp.clip(i * LANES - A_BAND, 0, L - 2 * LANES), A_BAND)
        return pl.ds(ws, 2 * LANES), jnp.where(i == 0, 0, jnp.where(i == nqb - 1, 2, 1))

    def unit4(t):
        c = t % d4
        i = t // d4
        keys, var = window(i, L4)
        rows = pl.ds(pl.multiple_of(i * LANES, LANES), LANES)
        stats = partial_softmax(q4_ref.at[c][rows, :], k4_ref.at[c][keys, :], v4_ref.at[c][keys, :], b4_ref.at[var])
        return lambda: put(c, rows, fold(get(c, rows), stats))

    grouped(d4 * (L4 // LANES), A_GROUP, unit4)

    def unit1(i):
        keys, var = window(i, S)
        rows = pl.ds(pl.multiple_of(i * LANES, LANES), LANES)
        stats = partial_softmax(q_ref[rows, :], k_ref[keys, :], v_ref[keys, :], b1_ref.at[var])

        def finish():
            part = pl.ds(pl.multiple_of(i * (LANES // d4), LANES // d4), LANES // d4)
            for n, ref in enumerate(stat_refs):
                for c in range(d4):
                    tmp_ref.at[n][pl.ds(c, LANES // d4, stride=d4), :] = ref.at[c][part, :]
            _, l, a = fold(tuple(tmp_ref[n] for n in range(len(stat_refs))), stats)
            y_ref[rows, :] = (a / l).astype(y_ref.dtype)

        return finish

    grouped(S // LANES, A_GROUP, unit1)


def _mixer_a_bias(rel_bias, S):
    return tuple(_a_bias_tiles(rel_bias, d, S // d) for _, d in A_PATTERNS)


def _mixer_a(qkv, bias, B, S):
    b1, b4, b16 = bias
    npair = A_HEADS // 2
    pair_bias = lambda t: pl.BlockSpec((t.shape[0], None) + t.shape[2:], lambda b, hp: (0, hp, 0, 0))
    slab = lambda first: pl.BlockSpec((S, LANES), lambda b, hp: (b, first + hp))
    return pl.pallas_call(
        functools.partial(_attn_a_kernel, S=S),
        grid=(B, npair),
        in_specs=[slab(0), slab(npair), slab(2 * npair), pair_bias(b16), pair_bias(b4), pair_bias(b1)],
        out_specs=pl.BlockSpec((S, LANES), lambda b, hp: (b, hp)),
        out_shape=jax.ShapeDtypeStruct((B * S, A_WIDTH), BF16),
        scratch_shapes=[pltpu.VMEM((A_PATTERNS[1][1], S // A_PATTERNS[1][1], LANES), F32)] * 6
        + [pltpu.VMEM((3, LANES, LANES), F32)],
        compiler_params=_cp("parallel", "parallel"),
        name="attn_a",
    )(qkv, qkv, qkv, b16, b4, b1)


MLA_Q_SCALE = float((B_NOPE + B_ROPE) ** -0.5 * np.log2(np.e))


def _mla_proj_kernel(cq_ref, ckv_ref, kr_ref, gq_ref, gkv_ref, wqm_ref, wqs_ref, wk_ref, wv_ref, vone_ref,
                     ek_ref, cosq_ref, sinq_ref, csk_ref, q_ref, k_ref, v_ref):
    cq = cq_ref[...]
    xq = (cq * lax.rsqrt(jnp.mean(cq * cq, axis=-1, keepdims=True) + EPS) * gq_ref[...]).astype(BF16)
    ckv = ckv_ref[...]
    xkv = (ckv * lax.rsqrt(jnp.mean(ckv * ckv, axis=-1, keepdims=True) + EPS) * gkv_ref[...]).astype(BF16)
    qm = _dot(xq, wqm_ref[...])
    qs = _dot(xq, wqs_ref[...])
    cosq = cosq_ref[...] * MLA_Q_SCALE
    sinq = sinq_ref[...] * MLA_Q_SCALE
    t = kr_ref[...] * csk_ref[...]
    t_hi = t.astype(BF16)
    t_lo = (t - t_hi.astype(F32)).astype(BF16)
    kk = _dot(xkv, wk_ref[...]) + _dot(t_hi, ek_ref[...]) + _dot(t_lo, ek_ref[...])
    for h in range(B_HEADS):
        sl = slice(h * LANES, (h + 1) * LANES)
        q_ref[:, sl] = (qm[:, sl] * cosq + qs[:, sl] * sinq).astype(BF16)
    k_ref[...] = kk.astype(BF16)
    v_ref[...] = (_dot(xkv, wv_ref[...]) + vone_ref[...]).astype(BF16)


MLA_QSUB = 4


def _mla_attn_kernel(q_ref, k_ref, v_ref, o_ref):
    tq = q_ref.shape[0] // MLA_QSUB
    lane = lax.broadcasted_iota(jnp.int32, (tq, LANES), 1)
    for j in range(MLA_QSUB):
        rows = slice(j * tq, (j + 1) * tq)
        outs = []
        for hh in range(2):
            sl = slice(hh * LANES, (hh + 1) * LANES)
            s = _dot_nt(q_ref[rows, sl], k_ref[:, sl])
            p = jnp.exp2(s - jnp.max(s, axis=-1, keepdims=True))
            outs.append(_dot(p.astype(BF16), v_ref[:, sl]))
        acc = jnp.where(lane < B_V, outs[0], outs[1])
        den = pltpu.roll(jnp.where(lane < B_V, outs[1], outs[0]), B_V, axis=1)
        o_ref[rows, :] = (acc / den).astype(o_ref.dtype)


def _mla_tables(S):
    inv_freq = ROPE_THETA ** (-jnp.arange(0, B_ROPE, 2, dtype=F32) / B_ROPE)
    ang = jnp.arange(S, dtype=F32)[:, None] * inv_freq[None]
    cos, sin = jnp.cos(ang), jnp.sin(ang)
    cos2 = jnp.concatenate([cos, cos], axis=-1)
    sin2 = jnp.concatenate([sin, sin], axis=-1)
    ones = jnp.ones((S, B_NOPE), F32)
    zn = jnp.zeros((S, B_NOPE), F32)
    zp = jnp.zeros((S, LANES - B_NOPE - B_ROPE), F32)
    cosq = jnp.concatenate([ones, cos2, zp], axis=-1)
    sinq = jnp.concatenate([zn, sin2, zp], axis=-1)
    csk = jnp.concatenate([cos2, sin2, jnp.zeros((S, LANES - 2 * B_ROPE), F32)], axis=-1)
    return cosq, sinq, csk


def _swap_cols(w):
    half = w.shape[-1] // 2
    return jnp.concatenate([-w[..., half:], w[..., :half]], axis=-1)


def _mla_weights(w_uq, w_ukv):
    dq = B_NOPE + B_ROPE
    wq = w_uq.reshape(B_Q_LORA, B_HEADS, dq)
    zpad = jnp.zeros((B_Q_LORA, B_HEADS, LANES - dq), F32)
    wqm = jnp.concatenate([wq, zpad], axis=-1).reshape(B_Q_LORA, B_HEADS * LANES)
    wqs = jnp.concatenate([jnp.zeros((B_Q_LORA, B_HEADS, B_NOPE), F32), _swap_cols(wq[..., B_NOPE:]), zpad],
                          axis=-1).reshape(B_Q_LORA, B_HEADS * LANES)
    wkv = w_ukv.reshape(B_KV_LORA, B_HEADS, B_NOPE + B_V)
    wk = jnp.concatenate([wkv[..., :B_NOPE], jnp.zeros((B_KV_LORA, B_HEADS, LANES - B_NOPE), F32)],
                         axis=-1).reshape(B_KV_LORA, B_HEADS * LANES)
    zv = jnp.zeros((B_KV_LORA, B_HEADS // 2, LANES - B_V), F32)
    wv_h = wkv[..., B_NOPE:]
    wv = jnp.stack([jnp.concatenate([wv_h[:, 0::2], zv], axis=-1),
                    jnp.concatenate([zv, wv_h[:, 1::2]], axis=-1)], axis=2).reshape(B_KV_LORA, B_HEADS * LANES)
    lane_in_pair = np.arange(B_HEADS * LANES) % (2 * LANES)
    vone = jnp.asarray(((lane_in_pair >= B_V) & (lane_in_pair < LANES + B_V)).astype(np.float32)).reshape(1, -1)
    ek = np.zeros((LANES, B_HEADS, LANES), np.float32)
    for j in range(B_ROPE):
        ek[j, :, B_NOPE + j] = 1.0
        ek[B_ROPE + j, :, B_NOPE + j] = 1.0
    ek = jnp.asarray(ek.reshape(LANES, B_HEADS * LANES))
    return wqm.astype(BF16), wqs.astype(BF16), wk.astype(BF16), wv.astype(BF16), vone, ek.astype(BF16)


def _mixer_b(rest, g_cq, w_uq, g_ckv, w_ukv, B, S, tm=512, tq=256 * MLA_QSUB):
    T = B * S
    wqm, wqs, wk, wv, vone, ek = _mla_weights(w_uq, w_ukv)
    cosq, sinq, csk = _mla_tables(S)
    nst = S // tm
    QW = B_HEADS * LANES
    const = lambda i: (0, 0)
    pos = lambda i: (i % nst, 0)
    q, k, v = pl.pallas_call(
        _mla_proj_kernel,
        grid=(T // tm,),
        in_specs=[pl.BlockSpec((tm, B_Q_LORA), lambda i: (i, R_CQ // B_Q_LORA)),
                  pl.BlockSpec((tm, B_KV_LORA), lambda i: (i, R_CKV // B_KV_LORA)),
                  pl.BlockSpec((tm, LANES), lambda i: (i, R_KR // LANES)),
                  pl.BlockSpec((1, B_Q_LORA), const),
                  pl.BlockSpec((1, B_KV_LORA), const),
                  pl.BlockSpec((B_Q_LORA, QW), const),
                  pl.BlockSpec((B_Q_LORA, QW), const),
                  pl.BlockSpec((B_KV_LORA, QW), const),
                  pl.BlockSpec((B_KV_LORA, QW), const),
                  pl.BlockSpec((1, QW), const),
                  pl.BlockSpec((LANES, QW), const),
                  pl.BlockSpec((tm, LANES), pos),
                  pl.BlockSpec((tm, LANES), pos),
                  pl.BlockSpec((tm, LANES), pos)],
        out_specs=[pl.BlockSpec((tm, QW), lambda i: (i, 0)),
                   pl.BlockSpec((tm, QW), lambda i: (i, 0)),
                   pl.BlockSpec((tm, QW), lambda i: (i, 0))],
        out_shape=[jax.ShapeDtypeStruct((T, QW), BF16), jax.ShapeDtypeStruct((T, QW), BF16),
                   jax.ShapeDtypeStruct((T, QW), BF16)],
        compiler_params=_cp("parallel"),
        name="mla_proj",
    )(rest, rest, rest, g_cq.reshape(1, -1), g_ckv.reshape(1, -1), wqm, wqs, wk, wv, vone, ek, cosq, sinq, csk)
    y = pl.pallas_call(
        _mla_attn_kernel,
        grid=(B, B_HEADS // 2, S // tq),
        in_specs=[pl.BlockSpec((None, tq, 2 * LANES), lambda b, hp, i: (b, i, hp)),
                  pl.BlockSpec((None, S, 2 * LANES), lambda b, hp, i: (b, 0, hp)),
                  pl.BlockSpec((None, S, 2 * LANES), lambda b, hp, i: (b, 0, hp))],
        out_specs=pl.BlockSpec((None, tq, 2 * B_V), lambda b, hp, i: (b, i, hp)),
        out_shape=jax.ShapeDtypeStruct((B, S, B_HEADS * B_V), BF16),
        compiler_params=_cp("parallel", "parallel", "arbitrary"),
        name="mla_attn",
    )(q.reshape(B, S, QW), k.reshape(B, S, QW), v.reshape(B, S, QW))
    return y.reshape(T, B_HEADS * B_V)


C_PAD = 16
C_ROWS = 128


SUBLANES = 8


def _tap_span(first, ntaps, rows):
    return rows + ((first + ntaps - 1) // SUBLANES) * SUBLANES


def _depthwise_taps(win_ref, sh_ref, w_ref, bias, ls, first, ntaps, rows):
    acc = jnp.broadcast_to(bias, (rows, LANES))
    span = _tap_span(first, ntaps, rows)
    for ph in range(SUBLANES):
        taps = [j for j in range(ntaps) if (first + j) % SUBLANES == ph]
        if not taps:
            continue
        if len(taps) == 1:
            j = taps[0]
            acc = acc + w_ref[j:j + 1, ls] * win_ref[first + j:first + j + rows, ls]
            continue
        sh_ref[0:span, :] = win_ref[ph:ph + span, ls]
        for j in taps:
            a = (first + j) // SUBLANES * SUBLANES
            acc = acc + w_ref[j:j + 1, ls] * sh_ref[a:a + rows, :]
    return acc


def _fill_window(win_ref, load_rows, r0, i, nblk, rows, pad):
    width = win_ref.shape[1]
    win_ref[pad:pad + rows, :] = load_rows(r0, rows)

    @pl.when(i > 0)
    def _():
        win_ref[0:pad, :] = load_rows(r0 - pad, pad)

    @pl.when(i == 0)
    def _():
        win_ref[0:pad, :] = jnp.zeros((pad, width), F32)

    @pl.when(i < nblk - 1)
    def _():
        win_ref[pad + rows:pad + rows + pad, :] = load_rows(r0 + rows, pad)

    @pl.when(i == nblk - 1)
    def _():
        win_ref[pad + rows:pad + rows + pad, :] = jnp.zeros((pad, width), F32)


def _conv_c_kernel(glu_ref, w_ref, b_ref, g_ref, beta_ref, o_ref, win_ref, sh_ref, acc_ref, *, S):
    i = pl.program_id(1)
    r0 = pl.multiple_of(i * C_ROWS, C_ROWS)

    def glu_rows(start, n):
        rs = pl.ds(pl.multiple_of(start, C_PAD), n)
        return glu_ref[rs, 0:C_CH] * _sigmoid(glu_ref[rs, C_CH:2 * C_CH])

    _fill_window(win_ref, glu_rows, r0, i, S // C_ROWS, C_ROWS, C_PAD)
    first = C_PAD - C_KERNEL // 2
    for lb in range(C_CH // LANES):
        ls = slice(lb * LANES, (lb + 1) * LANES)
        acc_ref[:, ls] = _depthwise_taps(win_ref, sh_ref, w_ref, b_ref[:, ls], ls, first, C_KERNEL, C_ROWS)
    y = _layernorm_rows(acc_ref[...], g_ref[...], beta_ref[...])
    o_ref[...] = _silu(y).astype(o_ref.dtype)


def _mixer_c(rest, w_dw, b_dw, ln_g, ln_b, B, S):
    T = B * S
    const = lambda b, i: (0, 0)
    y = pl.pallas_call(
        functools.partial(_conv_c_kernel, S=S),
        grid=(B, S // C_ROWS),
        in_specs=[pl.BlockSpec((None, S, 2 * C_CH), lambda b, i: (b, 0, R_GLU // (2 * C_CH))),
                  pl.BlockSpec((C_KERNEL, C_CH), const),
                  pl.BlockSpec((1, C_CH), const),
                  pl.BlockSpec((1, C_CH), const),
                  pl.BlockSpec((1, C_CH), const)],
        out_specs=pl.BlockSpec((None, C_ROWS, C_CH), lambda b, i: (b, i, 0)),
        out_shape=jax.ShapeDtypeStruct((B, S, C_CH), BF16),
        scratch_shapes=[pltpu.VMEM((C_ROWS + 2 * C_PAD, C_CH), F32),
                        pltpu.VMEM((_tap_span(C_PAD - C_KERNEL // 2, C_KERNEL, C_ROWS), LANES), F32),
                        pltpu.VMEM((C_ROWS, C_CH), F32)],
        compiler_params=_cp("parallel", "parallel"),
        name="conformer_conv",
    )(rest.reshape(B, S, R_WIDTH), w_dw, b_dw.reshape(1, -1), ln_g.reshape(1, -1), ln_b.reshape(1, -1))
    return y.reshape(T, C_CH)


D_PAD = 8
XBC_W = D_INNER + 2 * D_GROUPS * D_STATE
N_PAIR = D_HEADS // 2


def _pair_expand(v, first):
    lane = lax.broadcasted_iota(jnp.int32, (v.shape[0], LANES), 1)
    lo = jnp.broadcast_to(v[:, first:first + 1], (v.shape[0], LANES))
    hi = jnp.broadcast_to(v[:, first + 1:first + 2], (v.shape[0], LANES))
    return jnp.where(lane < D_HEAD_DIM, lo, hi)


def _ssd_kernel(xbc_ref, z_ref, dt_ref, wc_ref, bc_ref, alog_ref, dtb_ref, dskip_ref, gn_ref, o_ref,
                win_ref, sh_ref, xc_ref, a_ref, dtv_ref, y_ref, st_ref, *, S):
    Q = D_CHUNK
    nchunk = S // Q
    N = D_STATE
    bm0 = D_INNER
    cm0 = D_INNER + D_GROUPS * N

    def conv_body(c, carry):
        r0 = pl.multiple_of(c * Q, Q)
        _fill_window(win_ref, lambda st, n: xbc_ref[pl.ds(pl.multiple_of(st, D_PAD), n), :], r0, c, nchunk, Q, D_PAD)
        for lb in range(XBC_W // LANES):
            ls = slice(lb * LANES, (lb + 1) * LANES)
            acc = _depthwise_taps(win_ref, sh_ref, wc_ref, bc_ref[:, ls], ls, D_PAD - D_CONV // 2, D_CONV, Q)
            xc_ref[pl.ds(r0, Q), ls] = _silu(acc)
        return carry

    lax.fori_loop(0, nchunk, conv_body, 0)

    lane1 = lax.broadcasted_iota(jnp.int32, (1, LANES), 1)
    a_row = jnp.where(lane1 < 2 * D_HEADS, -jnp.exp(alog_ref[...]), 0.0)
    xdt = dt_ref[...] + dtb_ref[...]
    dtv = jnp.maximum(xdt, 0.0) + jnp.log(1.0 + jnp.exp(-jnp.abs(xdt)))
    dtv_ref[...] = dtv
    a_ref[...] = dtv * a_row

    row = lax.broadcasted_iota(jnp.int32, (Q, Q), 0)
    col = lax.broadcasted_iota(jnp.int32, (Q, Q), 1)
    tril = row >= col
    triu = col >= row
    lane = col

    def scan_chunk(c, lower, off, finalize):
        r0 = pl.multiple_of(c * Q, Q)
        rows = pl.ds(r0, Q)
        mask = tril if lower else triu
        tri = mask.astype(BF16)
        a_hi, a_mid, a_lo = _split3(a_ref[rows, :])
        cs = _dot(tri, a_hi) + _dot(tri, a_mid) + _dot(tri, a_lo)
        cs_t = cs.T
        ecs = jnp.exp(cs)
        edge = Q - 1 if lower else 0
        edec = jnp.exp(cs[edge:edge + 1, :] - cs)
        dt_c = dtv_ref[rows, :]
        dt_t = dt_c.T
        dte_t = (dt_c * edec).T
        for g in range(D_GROUPS):
            bg = xc_ref[rows, bm0 + g * N:bm0 + (g + 1) * N]
            cg = xc_ref[rows, cm0 + g * N:cm0 + (g + 1) * N].astype(BF16)
            cb = _dot_nt(cg, bg.astype(BF16))
            bg_t = bg.T
            for pp in range(N_PAIR // D_GROUPS):
                p = g * (N_PAIR // D_GROUPS) + pp
                ps = slice(p * LANES, (p + 1) * LANES)
                x_p = xc_ref[rows, ps]
                ms, bs = [], []
                for hh in range(2):
                    k = off + 2 * p + hh
                    diff = jnp.broadcast_to(cs[:, k:k + 1], (Q, Q)) - cs_t[k:k + 1, :]
                    ms.append((jnp.exp(jnp.where(mask, diff, NEG_INF)) * (cb * dt_t[k:k + 1, :])).astype(BF16))
                    bs.append((bg_t * dte_t[k:k + 1, :]).astype(BF16))
                x_lo = jnp.where(lane < D_HEAD_DIM, x_p, 0.0).astype(BF16)
                x_hi = jnp.where(lane >= D_HEAD_DIM, x_p, 0.0).astype(BF16)
                x2 = jnp.concatenate([x_lo, x_hi], axis=0)
                y_intra = _dot(jnp.concatenate(ms, axis=1), x2)
                hp = st_ref[p]
                ecs_p = _pair_expand(ecs, off + 2 * p)
                y_new = y_intra + _dot(cg, hp.astype(BF16)) * ecs_p
                if lower:
                    y_ref[rows, ps] = y_new
                else:
                    y_ref[rows, ps] = y_ref[rows, ps] + y_new
                st_ref[p] = hp * ecs_p[edge:edge + 1, :] + _dot(jnp.concatenate(bs, axis=1), x2)
        if finalize:
            y = y_ref[rows, :] + xc_ref[rows, 0:D_INNER] * dskip_ref[...]
            gated = y * _silu(z_ref[rows, :])
            out = gated * lax.rsqrt(jnp.mean(gated * gated, axis=-1, keepdims=True) + EPS) * gn_ref[...]
            o_ref[rows, :] = out.astype(o_ref.dtype)

    st_ref[...] = jnp.zeros(st_ref.shape, F32)

    def fwd_body(c, carry):
        scan_chunk(c, True, 0, False)
        return carry

    lax.fori_loop(0, nchunk, fwd_body, 0)
    st_ref[...] = jnp.zeros(st_ref.shape, F32)

    def bwd_body(k, carry):
        scan_chunk(nchunk - 1 - k, False, D_HEADS, True)
        return carry

    lax.fori_loop(0, nchunk, bwd_body, 0)


def _mixer_d(rest, w_conv, b_conv, a_log_f, a_log_b, dt_bias_f, dt_bias_b, d_skip, g_norm, B, S):
    T = B * S
    pad16 = lambda f, b: jnp.concatenate([f, b, jnp.zeros((LANES - 2 * D_HEADS,), F32)]).reshape(1, LANES)
    const = lambda b: (0, 0)
    y = pl.pallas_call(
        functools.partial(_ssd_kernel, S=S),
        grid=(B,),
        in_specs=[pl.BlockSpec((None, S, XBC_W), lambda b: (b, 0, R_XBC // XBC_W)),
                  pl.BlockSpec((None, S, D_INNER), lambda b: (b, 0, R_Z // D_INNER)),
                  pl.BlockSpec((None, S, LANES), lambda b: (b, 0, R_DT // LANES)),
                  pl.BlockSpec((D_CONV, XBC_W), const),
                  pl.BlockSpec((1, XBC_W), const),
                  pl.BlockSpec((1, LANES), const),
                  pl.BlockSpec((1, LANES), const),
                  pl.BlockSpec((1, D_INNER), const),
                  pl.BlockSpec((1, D_INNER), const)],
        out_specs=pl.BlockSpec((None, S, D_INNER), lambda b: (b, 0, 0)),
        out_shape=jax.ShapeDtypeStruct((B, S, D_INNER), BF16),
        scratch_shapes=[pltpu.VMEM((D_CHUNK + 2 * D_PAD, XBC_W), F32),
                        pltpu.VMEM((_tap_span(D_PAD - D_CONV // 2, D_CONV, D_CHUNK), LANES), F32),
                        pltpu.VMEM((S, XBC_W), F32),
                        pltpu.VMEM((S, LANES), F32),
                        pltpu.VMEM((S, LANES), F32),
                        pltpu.VMEM((S, D_INNER), F32),
                        pltpu.VMEM((N_PAIR, D_STATE, LANES), F32)],
        compiler_params=_cp("parallel"),
        name="ssd_mixer",
    )(rest.reshape(B, S, R_WIDTH), rest.reshape(B, S, R_WIDTH), rest.reshape(B, S, R_WIDTH),
      w_conv, b_conv.reshape(1, -1), pad16(a_log_f, a_log_b), pad16(dt_bias_f, dt_bias_b),
      jnp.repeat(d_skip, D_HEAD_DIM).reshape(1, -1), g_norm.reshape(1, -1))
    return y.reshape(T, D_INNER)


def _in_proj_weights(w_in_l):
    o = np.cumsum((0, A_WIDTH, A_WIDTH, A_WIDTH, B_Q_LORA, B_KV_LORA, B_ROPE, 2 * C_CH,
                   D_INNER, D_INNER, D_GROUPS * D_STATE, D_GROUPS * D_STATE, 2 * D_HEADS)).tolist()
    seg = lambda n: w_in_l[:, o[n]:o[n + 1]]
    w_a = w_in_l[:, :o[3]]
    cq, ckv, kr, glu, z, xs, bm, cm, dt = (seg(n) for n in range(3, 12))
    zeros = lambda n: jnp.zeros((w_in_l.shape[0], n), w_in_l.dtype)
    w_r = jnp.concatenate([glu, xs, bm, cm, cq, z, ckv,
                           kr, _swap_cols(kr), zeros(LANES - 2 * B_ROPE),
                           dt, zeros(LANES - 2 * D_HEADS)], axis=-1)
    assert w_r.shape[1] == R_WIDTH
    return w_a, w_r


def _merge_gate_kernel(h_ref, ya_ref, yb_ref, yc_ref, yd_ref, wg_ref, bg_ref, wbr_ref, o_ref):
    h = h_ref[...]
    acc = None
    for i, y_ref in enumerate((ya_ref, yb_ref, yc_ref, yd_ref)):
        gate = _sigmoid(_dot(h, wg_ref[i]) + bg_ref[i])
        term = gate * _dot(y_ref[...], wbr_ref[i])
        acc = term if acc is None else acc + term
    o_ref[...] = acc.astype(o_ref.dtype)


def _out_ln_kernel(m_ref, w_ref, h_ref, g_ref, b_ref, of_ref, op_ref):
    y = _layernorm_rows(ALPHA * h_ref[...] + _dot(m_ref[...], w_ref[...]), g_ref[...], b_ref[...])
    of_ref[...] = y
    for c in range(ROW_SUB):
        op_ref[_col_block(y.shape[0], c), :] = _pack_cols(y, c)


def _merge(hb, hf, branches, w_gate, b_gate, w_br, w_out, ln_g, ln_b, tm=512, tn=512, tm2=512):
    T, D = hb.shape
    ybs = pl.BlockSpec((tm, BRANCH_W), lambda j, i: (i, 0))
    merged = pl.pallas_call(
        _merge_gate_kernel,
        grid=(D // tn, T // tm),
        in_specs=[pl.BlockSpec((tm, D), lambda j, i: (i, 0)), ybs, ybs, ybs, ybs,
                  pl.BlockSpec((N_BRANCH, D, tn), lambda j, i: (0, 0, j)),
                  pl.BlockSpec((N_BRANCH, 1, tn), lambda j, i: (0, 0, j)),
                  pl.BlockSpec((N_BRANCH, BRANCH_W, tn), lambda j, i: (0, 0, j))],
        out_specs=pl.BlockSpec((tm, tn), lambda j, i: (i, j)),
        out_shape=jax.ShapeDtypeStruct((T, D), BF16),
        compiler_params=_cp("parallel", "parallel"),
        name="merge_gate",
    )(hb, *branches, w_gate.astype(BF16), b_gate.reshape(N_BRANCH, 1, D), w_br.astype(BF16))
    const = lambda i: (0, 0)
    rows = lambda i: (i, 0)
    return pl.pallas_call(
        _out_ln_kernel,
        grid=(T // tm2,),
        in_specs=[pl.BlockSpec((tm2, D), rows), pl.BlockSpec((D, D), const), pl.BlockSpec((tm2, D), rows),
                  pl.BlockSpec((1, D), const), pl.BlockSpec((1, D), const)],
        out_specs=[pl.BlockSpec((tm2, D), rows), pl.BlockSpec((tm2 * ROW_SUB, LANES), rows)],
        out_shape=[jax.ShapeDtypeStruct((T, D), F32), jax.ShapeDtypeStruct((T * ROW_SUB, LANES), ROW_DT)],
        compiler_params=_cp("parallel"),
        name="out_proj_ln1",
    )(merged, w_out.astype(BF16), hf, ln_g.reshape(1, D), ln_b.reshape(1, D))


R_TM = 512
COMBINE_TB = 256


def _router_kernel(h_ref, whi_ref, wlo_ref, b_ref, eid_ref, wts_ref, rank_ref, cnt_ref, carry_ref):
    i = pl.program_id(0)

    @pl.when(i == 0)
    def _():
        carry_ref[...] = jnp.zeros(carry_ref.shape, F32)

    x = h_ref[...]
    tm = x.shape[0]
    xh = x.astype(BF16)
    xl = (x - xh.astype(F32)).astype(BF16)
    whi = whi_ref[...]
    logits = _dot(xh, whi) + _dot(xh, wlo_ref[...]) + _dot(xl, whi) + b_ref[...]
    lane = lax.broadcasted_iota(jnp.int32, (tm, LANES), 1)
    big = jnp.int32(4 * LANES)
    is_g = (lane >= N_EXPERTS) & (lane < N_EXPERTS + N_GROUPS)
    lg = jnp.where(is_g, logits, NEG_INF)
    gmax = jnp.max(lg, axis=-1, keepdims=True)
    gidx = jnp.min(jnp.where(lg == gmax, lane - N_EXPERTS, big), axis=-1, keepdims=True)
    g_w = 1.0 / jnp.sum(jnp.where(is_g, jnp.exp(lg - gmax), 0.0), axis=-1, keepdims=True)
    in_grp = (lane < N_EXPERTS) & ((lane // EXP_PER_GROUP) == gidx)
    le = jnp.where(in_grp, logits, NEG_INF)
    e1 = jnp.max(le, axis=-1, keepdims=True)
    i1 = jnp.min(jnp.where(le == e1, lane, big), axis=-1, keepdims=True)
    le2 = jnp.where(lane == i1, NEG_INF, le)
    e2 = jnp.max(le2, axis=-1, keepdims=True)
    i2 = jnp.min(jnp.where(le2 == e2, lane, big), axis=-1, keepdims=True)
    zsum = jnp.sum(jnp.where(in_grp, jnp.exp(le - e1), 0.0), axis=-1, keepdims=True)
    p1 = 1.0 / zsum
    p2 = jnp.exp(e2 - e1) / zsum
    w1 = g_w * p1 / (p1 + p2)
    w2 = g_w * p2 / (p1 + p2)
    oh1 = lane == i1
    oh2 = lane == i2
    ohs = (oh1 | oh2).astype(BF16)
    row = lax.broadcasted_iota(jnp.int32, (tm, tm), 0)
    col = lax.broadcasted_iota(jnp.int32, (tm, tm), 1)
    before = _dot((row > col).astype(BF16), ohs) + carry_ref[0:1, :]
    r1 = jnp.sum(jnp.where(oh1, before, 0.0), axis=-1, keepdims=True)
    r2 = jnp.sum(jnp.where(oh2, before, 0.0), axis=-1, keepdims=True)
    total = carry_ref[0:1, :] + jnp.sum(ohs.astype(F32), axis=0, keepdims=True)
    carry_ref[...] = jnp.broadcast_to(total, carry_ref.shape)
    cnt_ref[...] = jnp.broadcast_to(total, cnt_ref.shape).astype(jnp.int32)
    eid_ref[...] = jnp.where(lane == 0, i1, jnp.where(lane == 1, i2, 0))
    wts_ref[...] = jnp.where(lane == 0, w1, jnp.where(lane == 1, w2, 0.0))
    rank_ref[...] = jnp.where(lane == 0, r1, jnp.where(lane == 1, r2, 0.0)).astype(jnp.int32)


ROW_SUB = D_MODEL // LANES // 2
ROW_DT = jnp.uint32


def _row_slab(ref, r):
    return ref.at[pl.ds(pl.multiple_of(r * ROW_SUB, ROW_SUB), ROW_SUB)]


def _col_block(n, c):
    return pl.ds(c, n, stride=ROW_SUB)


def _pack_cols(x, c):
    as_bits = lambda t: lax.bitcast_convert_type(t.astype(BF16).astype(F32), ROW_DT)
    lo = as_bits(x[:, c * LANES:(c + 1) * LANES])
    hi = as_bits(x[:, (c + ROW_SUB) * LANES:(c + ROW_SUB + 1) * LANES])
    return (lo >> 16) | hi


def _unpack_cols(w):
    return (lax.bitcast_convert_type(w << 16, F32),
            lax.bitcast_convert_type(w & jnp.uint32(0xFFFF0000), F32))


EXPERT_AHEAD = 3


def _expert_kernel(be_ref, succ_ref, *refs, layer):
    src_refs = refs[:EXPERT_AHEAD + 1]
    (h_hbm, wg_hbm, wu_hbm, wd_hbm, o_ref, xbuf_ref, sg_ref, su_ref, sd_ref, wgb_ref, wub_ref, wdb_ref,
     stage_ref, gsem, wsem) = refs[EXPERT_AHEAD + 1:]
    i = pl.program_id(0)
    n_used = be_ref[pl.num_programs(0)]
    nbuf = EXPERT_AHEAD + 1
    slot = i % nbuf
    expert = be_ref[i]
    weights = ((wg_hbm, sg_ref, wgb_ref), (wu_hbm, su_ref, wub_ref), (wd_hbm, sd_ref, wdb_ref))

    def weight_copy(w, e, s):
        hbm, st_ref, _ = weights[w]
        return pltpu.make_async_copy(hbm.at[layer, e], st_ref.at[s], wsem.at[s, w])

    @pl.when(i == 0)
    def _():
        stage_ref[0] = 0
        for w in range(len(weights)):
            weight_copy(w, expert, 0).start()

    @pl.when((i < n_used) & ((i == 0) | (expert != be_ref[jnp.maximum(i - 1, 0)])))
    def _():
        s = stage_ref[0]
        nxt = succ_ref[expert]
        for w, (_, st_ref, wb_ref) in enumerate(weights):
            weight_copy(w, 0, s).wait()
            wb_ref[...] = st_ref[s].astype(BF16)

        @pl.when(nxt >= 0)
        def _():
            for w in range(len(weights)):
                weight_copy(w, nxt, 1 - s).start(priority=1)

        stage_ref[0] = 1 - s

    def gather_copy(src_row, to_slot, t):
        src = h_hbm.at[pl.ds(pl.multiple_of(src_row, ROW_SUB), ROW_SUB)]
        return pltpu.make_async_copy(src, _row_slab(xbuf_ref.at[to_slot], t), gsem.at[to_slot])

    def gather(idx_ref, to_slot):
        for t in range(MOE_BLOCK):
            gather_copy(idx_ref[t], to_slot, t).start()

    @pl.when(i == 0)
    def _():
        for a in range(EXPERT_AHEAD):
            gather(src_refs[a], a)

    @pl.when(i + EXPERT_AHEAD < n_used)
    def _():
        gather(src_refs[EXPERT_AHEAD], (i + EXPERT_AHEAD) % nbuf)

    @pl.when(i < n_used)
    def _():
        for t in range(MOE_BLOCK):
            gather_copy(0, slot, 0).wait()
        x_ref = xbuf_ref.at[slot]
        halves = [_unpack_cols(x_ref[_col_block(MOE_BLOCK, c), :]) for c in range(ROW_SUB)]
        x = jnp.concatenate([lo for lo, _ in halves] + [hi for _, hi in halves], axis=1).astype(BF16)
        hid = (_silu(_dot(x, wgb_ref[...])) * _dot(x, wub_ref[...])).astype(BF16)
        y = _dot(hid, wdb_ref[...])
        for c in range(ROW_SUB):
            o_ref[_col_block(MOE_BLOCK, c), :] = _pack_cols(y, c)

    @pl.when(i >= n_used)
    def _():
        o_ref[...] = jnp.zeros(o_ref.shape, o_ref.dtype)


def _combine_kernel(dest_ref, dnext_ref, h_ref, w_ref, g_ref, b_ref, yrows_hbm, of_ref, ob_ref,
                    ybuf_ref, acc_ref, sem):
    i = pl.program_id(0)
    n = pl.num_programs(0)
    slot = i % 2

    def row_copy(d, to_slot, k, t):
        return pltpu.make_async_copy(_row_slab(yrows_hbm, d), _row_slab(ybuf_ref.at[to_slot, k], t), sem.at[to_slot])

    def gather(d_ref, to_slot):
        def start(t, carry):
            for k in range(TOP_K):
                row_copy(d_ref[TOP_K * t + k], to_slot, k, t).start(priority=k)
            return carry

        lax.fori_loop(0, COMBINE_TB, start, 0, unroll=8)

    @pl.when(i == 0)
    def _():
        gather(dest_ref, 0)

    @pl.when(i + 1 < n)
    def _():
        gather(dnext_ref, 1 - slot)

    for t in range(COMBINE_TB):
        for k in range(TOP_K):
            row_copy(0, slot, k, 0).wait()
    w = w_ref[...]
    y0_ref = ybuf_ref.at[slot, 0]
    y1_ref = ybuf_ref.at[slot, 1]
    for c in range(ROW_SUB):
        cb = _col_block(COMBINE_TB, c)
        lo0, hi0 = _unpack_cols(y0_ref[cb, :])
        lo1, hi1 = _unpack_cols(y1_ref[cb, :])
        acc_ref[:, c * LANES:(c + 1) * LANES] = lo0 * w[:, 0:1] + lo1 * w[:, 1:2]
        acc_ref[:, (c + ROW_SUB) * LANES:(c + ROW_SUB + 1) * LANES] = hi0 * w[:, 0:1] + hi1 * w[:, 1:2]
    y = _layernorm_rows(ALPHA * h_ref[...] + acc_ref[...], g_ref[...], b_ref[...])
    of_ref[...] = y
    ob_ref[...] = y.astype(BF16)


def _moe_layer(hf, hp, w_rg, b_rg, w_re, b_re, w_e_gate, w_e_up, w_e_down, layer, ln_g, ln_b):
    T, D = hf.shape
    n_rows = T * TOP_K + N_EXPERTS * MOE_BLOCK
    n_blocks = n_rows // MOE_BLOCK
    w_r = jnp.concatenate([w_re, w_rg, jnp.zeros((D, LANES - N_EXPERTS - N_GROUPS), F32)], axis=-1)
    b_r = jnp.concatenate([b_re, b_rg, jnp.zeros((LANES - N_EXPERTS - N_GROUPS,), F32)]).reshape(1, LANES)
    w_hi = w_r.astype(BF16)
    w_lo = (w_r - w_hi.astype(F32)).astype(BF16)
    const = lambda i: (0, 0)
    rows = lambda i: (i, 0)
    eid, wts, rank, cnt = pl.pallas_call(
        _router_kernel,
        grid=(T // R_TM,),
        in_specs=[pl.BlockSpec((R_TM, D), rows), pl.BlockSpec((D, LANES), const),
                  pl.BlockSpec((D, LANES), const), pl.BlockSpec((1, LANES), const)],
        out_specs=[pl.BlockSpec((R_TM, LANES), rows), pl.BlockSpec((R_TM, LANES), rows),
                   pl.BlockSpec((R_TM, LANES), rows), pl.BlockSpec((8, LANES), const)],
        out_shape=[jax.ShapeDtypeStruct((T, LANES), jnp.int32), jax.ShapeDtypeStruct((T, LANES), F32),
                   jax.ShapeDtypeStruct((T, LANES), jnp.int32), jax.ShapeDtypeStruct((8, LANES), jnp.int32)],
        scratch_shapes=[pltpu.VMEM((8, LANES), F32)],
        compiler_params=_cp("arbitrary"),
        name="moe_router",
    )(hf, w_hi, w_lo, b_r)
    counts = cnt[0, :N_EXPERTS]
    padded = (counts + MOE_BLOCK - 1) // MOE_BLOCK * MOE_BLOCK
    pends = jnp.cumsum(padded)
    pstarts = pends - padded
    blk_start = jnp.arange(n_blocks, dtype=jnp.int32) * MOE_BLOCK
    blk_exp = jnp.minimum(jnp.sum((pends[None, :] <= blk_start[:, None]).astype(jnp.int32), axis=1), N_EXPERTS - 1)
    sel = eid[:, :TOP_K, None] == jnp.arange(N_EXPERTS, dtype=jnp.int32)
    dest = (jnp.sum(jnp.where(sel, pstarts, 0), axis=-1) + rank[:, :TOP_K]).astype(jnp.int32).reshape(T * TOP_K)
    flat = jnp.full((n_rows,), -1, jnp.int32).at[dest].set(jnp.arange(T * TOP_K, dtype=jnp.int32),
                                                           unique_indices=True)
    row_tok = jnp.where(flat < 0, 0, flat // TOP_K) * ROW_SUB
    assert n_blocks > EXPERT_AHEAD
    any_spec = pl.BlockSpec(memory_space=pl.ANY)
    idx_spec = lambda f: pl.BlockSpec((MOE_BLOCK,), f, memory_space=pltpu.SMEM)
    ids = jnp.arange(N_EXPERTS, dtype=jnp.int32)
    later = (ids[None, :] > ids[:, None]) & (counts[None, :] > 0)
    succ = jnp.min(jnp.where(later, ids[None, :], N_EXPERTS), axis=1)
    succ = jnp.where(succ == N_EXPERTS, -1, succ).astype(jnp.int32)
    yrows = pl.pallas_call(
        functools.partial(_expert_kernel, layer=layer),
        grid_spec=pltpu.PrefetchScalarGridSpec(
            num_scalar_prefetch=2,
            grid=(n_blocks,),
            in_specs=[idx_spec(lambda i, *_, a=a: (jnp.minimum(i + a, n_blocks - 1),)) for a in range(EXPERT_AHEAD + 1)]
            + [any_spec] * 4,
            out_specs=pl.BlockSpec((MOE_BLOCK * ROW_SUB, LANES), lambda i, *_: (i, 0)),
            scratch_shapes=[pltpu.VMEM((EXPERT_AHEAD + 1, MOE_BLOCK * ROW_SUB, LANES), ROW_DT),
                            pltpu.VMEM((2, D, D_FF), F32), pltpu.VMEM((2, D, D_FF), F32), pltpu.VMEM((2, D_FF, D), F32),
                            pltpu.VMEM((D, D_FF), BF16), pltpu.VMEM((D, D_FF), BF16), pltpu.VMEM((D_FF, D), BF16),
                            pltpu.SMEM((1,), jnp.int32),
                            pltpu.SemaphoreType.DMA((EXPERT_AHEAD + 1,)), pltpu.SemaphoreType.DMA((2, 3))]),
        out_shape=jax.ShapeDtypeStruct((n_rows * ROW_SUB, LANES), ROW_DT),
        compiler_params=_cp("arbitrary"),
        name="moe_experts",
    )(jnp.concatenate([blk_exp, pends[-1:] // MOE_BLOCK]).astype(jnp.int32), succ,
      *([row_tok] * (EXPERT_AHEAD + 1)), hp, w_e_gate, w_e_up, w_e_down)
    n_steps = T // COMBINE_TB
    dspec = lambda f: pl.BlockSpec((TOP_K * COMBINE_TB,), f, memory_space=pltpu.SMEM)
    return pl.pallas_call(
        _combine_kernel,
        grid=(n_steps,),
        in_specs=[dspec(lambda i: (i,)), dspec(lambda i: (jnp.minimum(i + 1, n_steps - 1),)),
                  pl.BlockSpec((COMBINE_TB, D), rows), pl.BlockSpec((COMBINE_TB, LANES), rows),
                  pl.BlockSpec((1, D), const), pl.BlockSpec((1, D), const), any_spec],
        out_specs=[pl.BlockSpec((COMBINE_TB, D), rows), pl.BlockSpec((COMBINE_TB, D), rows)],
        out_shape=[jax.ShapeDtypeStruct((T, D), F32), jax.ShapeDtypeStruct((T, D), BF16)],
        scratch_shapes=[pltpu.VMEM((2, TOP_K, COMBINE_TB * ROW_SUB, LANES), ROW_DT),
                        pltpu.VMEM((COMBINE_TB, D), F32), pltpu.SemaphoreType.DMA((2,))],
        compiler_params=_cp("arbitrary"),
        name="moe_combine_ln2",
    )(dest, dest, hf, wts, ln_g.reshape(1, D), ln_b.reshape(1, D), yrows)


def kernel(x, ln_in_g, ln_in_b, rel_bias, w_in, g_cq, w_uq, g_ckv, w_ukv, w_dw_c, b_dw_c, ln_c_g, ln_c_b,
           w_conv_d, b_conv_d, a_log_f, a_log_b, dt_bias_f, dt_bias_b, d_skip, g_norm_d, w_br, w_gate, b_gate,
           w_out, ln1_g, ln1_b, w_rg, b_rg, w_re, b_re, w_e_gate, w_e_up, w_e_down, ln2_g, ln2_b):
    B, S, D = x.shape
    T = B * S
    hf, hb = _layernorm(x.reshape(T, D), ln_in_g, ln_in_b)
    a_bias = _mixer_a_bias(rel_bias, S)
    for l in range(DEPTH):
        w_a, w_r = _in_proj_weights(w_in[l])
        qkv = _matmul(hb, w_a.astype(BF16), F32, 512, 3 * A_WIDTH, "in_proj_a")
        rest = _matmul(hb, w_r.astype(BF16), F32, 512, R_WIDTH // 2, "in_proj_rest")
        y_a = _mixer_a(qkv, a_bias, B, S)
        y_b = _mixer_b(rest, g_cq[l], w_uq[l], g_ckv[l], w_ukv[l], B, S)
        y_c = _mixer_c(rest, w_dw_c[l], b_dw_c[l], ln_c_g[l], ln_c_b[l], B, S)
        y_d = _mixer_d(rest, w_conv_d[l], b_conv_d[l], a_log_f[l], a_log_b[l], dt_bias_f[l], dt_bias_b[l],
                       d_skip[l], g_norm_d[l], B, S)
        h1f, h1p = _merge(hb, hf, (y_a, y_b, y_c, y_d), w_gate[l], b_gate[l], w_br[l], w_out[l], ln1_g[l], ln1_b[l])
        hf, hb = _moe_layer(h1f, h1p, w_rg[l], b_rg[l], w_re[l], b_re[l], w_e_gate, w_e_up, w_e_down, l,
                            ln2_g[l], ln2_b[l])
    return hf.reshape(B, S, D)
```

```python
import functools

import numpy as np
import jax
import jax.numpy as jnp
from jax import lax
from jax.experimental import pallas as pl
from jax.experimental.pallas import tpu as pltpu

F32 = jnp.float32
BF16 = jnp.bfloat16

D_MODEL = 2048
DEPTH = 2
A_HEADS = 8
A_HEAD_DIM = 64
A_WIDTH = A_HEADS * A_HEAD_DIM
A_PATTERNS = ((128, 1), (512, 4), (2048, 16))
A_BAND = 64
REL_BUCKETS = 32
REL_MAX_DIST = 1024
B_HEADS = 8
B_NOPE = 64
B_ROPE = 32
B_V = 64
B_Q_LORA = 512
B_KV_LORA = 256
ROPE_THETA = 10000.0
C_CH = 512
C_KERNEL = 31
D_HEADS = 8
D_HEAD_DIM = 64
D_INNER = D_HEADS * D_HEAD_DIM
D_STATE = 128
D_GROUPS = 2
D_CONV = 5
D_CHUNK = 128
N_BRANCH = 4
BRANCH_W = 512
N_GROUPS = 4
EXP_PER_GROUP = 8
N_EXPERTS = N_GROUPS * EXP_PER_GROUP
TOP_K = 2
D_FF = 512
MOE_BLOCK = 256
ALPHA = (2 * DEPTH) ** 0.25
EPS = 1e-5
NEG_INF = -1e30

LANES = 128
R_GLU, R_XBC, R_CQ, R_Z, R_CKV, R_KR, R_DT = 0, 1024, 2048, 2560, 3072, 3328, 3456
R_WIDTH = 3584
VMEM_LIMIT = 56 * 1024 * 1024


def _cp(*sem):
    return pltpu.CompilerParams(dimension_semantics=sem, vmem_limit_bytes=VMEM_LIMIT)


def _dot(a, b):
    return jnp.dot(a, b, preferred_element_type=F32)


def _dot_nt(a, b):
    return lax.dot_general(a, b, (((1,), (1,)), ((), ())), preferred_element_type=F32)


def _split3(x):
    hi = x.astype(BF16)
    r1 = x - hi.astype(F32)
    mid = r1.astype(BF16)
    lo = (r1 - mid.astype(F32)).astype(BF16)
    return hi, mid, lo


def _layernorm_rows(x, g, b):
    mu = jnp.mean(x, axis=-1, keepdims=True)
    xc = x - mu
    var = jnp.mean(xc * xc, axis=-1, keepdims=True)
    return xc * lax.rsqrt(var + EPS) * g + b


def _sigmoid(x):
    return 1.0 / (1.0 + jnp.exp(-x))


def _silu(x):
    return x * _sigmoid(x)


def _ln_kernel(x_ref, g_ref, b_ref, of_ref, ob_ref):
    y = _layernorm_rows(x_ref[...], g_ref[...], b_ref[...])
    of_ref[...] = y
    ob_ref[...] = y.astype(BF16)


def _layernorm(x, g, b, tm=256):
    T, D = x.shape
    return pl.pallas_call(
        _ln_kernel,
        grid=(T // tm,),
        in_specs=[pl.BlockSpec((tm, D), lambda i: (i, 0)),
                  pl.BlockSpec((1, D), lambda i: (0, 0)),
                  pl.BlockSpec((1, D), lambda i: (0, 0))],
        out_specs=[pl.BlockSpec((tm, D), lambda i: (i, 0)),
                   pl.BlockSpec((tm, D), lambda i: (i, 0))],
        out_shape=[jax.ShapeDtypeStruct((T, D), F32), jax.ShapeDtypeStruct((T, D), BF16)],
        compiler_params=_cp("parallel"),
        name="ln_in",
    )(x, g.reshape(1, D), b.reshape(1, D))


def _mm_kernel(x_ref, w_ref, o_ref):
    o_ref[...] = _dot(x_ref[...], w_ref[...]).astype(o_ref.dtype)


def _matmul(x, w, out_dtype, tm, tn, name):
    M, K = x.shape
    N = w.shape[1]
    return pl.pallas_call(
        _mm_kernel,
        grid=(N // tn, M // tm),
        in_specs=[pl.BlockSpec((tm, K), lambda j, i: (i, 0)),
                  pl.BlockSpec((K, tn), lambda j, i: (0, j))],
        out_specs=pl.BlockSpec((tm, tn), lambda j, i: (i, j)),
        out_shape=jax.ShapeDtypeStruct((M, N), out_dtype),
        compiler_params=_cp("parallel", "parallel"),
        name=name,
    )(x, w)


def _t5_bucket(rel):
    half = REL_BUCKETS // 2
    max_exact = half // 2
    n = np.abs(rel)
    large = max_exact + (np.log(np.maximum(n, 1) / max_exact) / np.log(REL_MAX_DIST / max_exact)
                         * (half - max_exact)).astype(np.int32)
    large = np.minimum(large, half - 1)
    return (rel > 0).astype(np.int32) * half + np.where(n < max_exact, n, large)


def _a_window(L):
    return min(2 * LANES, L)


def _a_bias_tiles(rel_bias, d, L):
    W = _a_window(L)
    offs = (0,) if L == LANES else (0, -A_BAND, -2 * A_BAND)
    qi = np.arange(LANES)[:, None]
    kj = np.arange(W)[None, :]
    rel = np.stack([kj - qi + off for off in offs], axis=0)
    valid = np.abs(rel) <= A_BAND
    onehot = (jnp.asarray(_t5_bucket(rel * d), jnp.int32)[..., None] == jnp.arange(REL_BUCKETS)).astype(F32)
    b = jnp.einsum('vqkb,bh->vhqk', onehot, rel_bias.astype(F32), precision=lax.Precision.HIGHEST)
    b = jnp.where(valid[:, None], b, NEG_INF)
    return b.reshape(len(offs), A_HEADS // 2, 2 * LANES, W)


A_GROUP = 2


def _attn_a_kernel(q_ref, k_ref, v_ref, b16_ref, b4_ref, b1_ref, y_ref,
                   q4_ref, k4_ref, v4_ref, m_ref, l_ref, acc_ref, tmp_ref, *, S):
    (_, d1), (_, d4), (_, d16) = A_PATTERNS
    lane = lax.broadcasted_iota(jnp.int32, (LANES, LANES), 1)
    head0 = lane < A_HEAD_DIM
    scale = A_HEAD_DIM ** -0.5

    def partial_softmax(qs, ks, vs, bias_ref):
        q2 = jnp.concatenate([jnp.where(head0, qs, 0.0), jnp.where(head0, 0.0, qs)], axis=0).astype(BF16)
        s = _dot_nt(q2, ks.astype(BF16)) * scale + bias_ref[...]
        m = jnp.max(s, axis=-1, keepdims=True)
        p = jnp.exp(s - m).astype(BF16)
        num = _dot(p, vs.astype(BF16))
        den = _dot(p, jnp.ones((vs.shape[0], LANES), BF16))
        both = lambda t: jnp.where(head0, t[:LANES], t[LANES:])
        return both(jnp.broadcast_to(m, (2 * LANES, LANES))), both(den), both(num)

    def fold(old, new):
        (m_old, l_old, a_old), (m_new, l_new, a_new) = old, new
        m = jnp.maximum(m_old, m_new)
        c_old = jnp.exp(m_old - m)
        c_new = jnp.exp(m_new - m)
        return m, c_old * l_old + c_new * l_new, c_old * a_old + c_new * a_new

    stat_refs = (m_ref, l_ref, acc_ref)

    def get(c, rows):
        return tuple(ref.at[c][rows, :] for ref in stat_refs)

    def put(c, rows, stats):
        for ref, val in zip(stat_refs, stats):
            ref.at[c][rows, :] = val

    def grouped(n, group, unit):
        def trip(g, carry):
            pending = [unit(g * group + u) for u in range(group)]
            for finish in pending:
                finish()
            return carry

        lax.fori_loop(0, n // group, trip, 0)

    L4 = S // d4
    sub = d16 // d4
    assert S // d16 == LANES and sub == d4
    for c in range(d4):
        cls = pl.ds(c, L4, stride=d4)
        q4_ref[c] = q_ref[cls, :]
        k4_ref[c] = k_ref[cls, :]
        v4_ref[c] = v_ref[cls, :]

    def unit16(t):
        c = t % d4
        rows = pl.ds(t // d4, LANES, stride=sub)
        stats = partial_softmax(q4_ref.at[c][rows, :], k4_ref.at[c][rows, :], v4_ref.at[c][rows, :], b16_ref.at[0])
        return lambda: put(c, rows, stats)

    grouped(d16, A_GROUP, unit16)

    def window(i, L):
        nqb = L // LANES
        ws = pl.multiple_of(jnp.clip(i * LANES - A_BAND, 0, L - 2 * LANES), A_BAND)
        return pl.ds(ws, 2 * LANES), jnp.where(i == 0, 0, jnp.where(i == nqb - 1, 2, 1))

    def unit4(t):
        c = t % d4
        i = t // d4
        keys, var = window(i, L4)
        rows = pl.ds(pl.multiple_of(i * LANES, LANES), LANES)
        stats = partial_softmax(q4_ref.at[c][rows, :], k4_ref.at[c][keys, :], v4_ref.at[c][keys, :], b4_ref.at[var])
        return lambda: put(c, rows, fold(get(c, rows), stats))

    grouped(d4 * (L4 // LANES), A_GROUP, unit4)

    def unit1(i):
        keys, var = window(i, S)
        rows = pl.ds(pl.multiple_of(i * LANES, LANES), LANES)
        stats = partial_softmax(q_ref[rows, :], k_ref[keys, :], v_ref[keys, :], b1_ref.at[var])

        def finish():
            part = pl.ds(pl.multiple_of(i * (LANES // d4), LANES // d4), LANES // d4)
            for n, ref in enumerate(stat_refs):
                for c in range(d4):
                    tmp_ref.at[n][pl.ds(c, LANES // d4, stride=d4), :] = ref.at[c][part, :]
            _, l, a = fold(tuple(tmp_ref[n] for n in range(len(stat_refs))), stats)
            y_ref[rows, :] = (a / l).astype(y_ref.dtype)

        return finish

    grouped(S // LANES, A_GROUP, unit1)


def _mixer_a_bias(rel_bias, S):
    return tuple(_a_bias_tiles(rel_bias, d, S // d) for _, d in A_PATTERNS)


def _mixer_a(qkv, bias, B, S):
    b1, b4, b16 = bias
    npair = A_HEADS // 2
    pair_bias = lambda t: pl.BlockSpec((t.shape[0], None) + t.shape[2:], lambda b, hp: (0, hp, 0, 0))
    slab = lambda first: pl.BlockSpec((S, LANES), lambda b, hp: (b, first + hp))
    return pl.pallas_call(
        functools.partial(_attn_a_kernel, S=S),
        grid=(B, npair),
        in_specs=[slab(0), slab(npair), slab(2 * npair), pair_bias(b16), pair_bias(b4), pair_bias(b1)],
        out_specs=pl.BlockSpec((S, LANES), lambda b, hp: (b, hp)),
        out_shape=jax.ShapeDtypeStruct((B * S, A_WIDTH), BF16),
        scratch_shapes=[pltpu.VMEM((A_PATTERNS[1][1], S // A_PATTERNS[1][1], LANES), F32)] * 6
        + [pltpu.VMEM((3, LANES, LANES), F32)],
        compiler_params=_cp("parallel", "parallel"),
        name="attn_a",
    )(qkv, qkv, qkv, b16, b4, b1)


MLA_Q_SCALE = float((B_NOPE + B_ROPE) ** -0.5 * np.log2(np.e))


def _mla_proj_kernel(cq_ref, ckv_ref, kr_ref, gq_ref, gkv_ref, wqm_ref, wqs_ref, wk_ref, wv_ref, vone_ref,
                     ek_ref, cosq_ref, sinq_ref, csk_ref, q_ref, k_ref, v_ref):
    cq = cq_ref[...]
    xq = (cq * lax.rsqrt(jnp.mean(cq * cq, axis=-1, keepdims=True) + EPS) * gq_ref[...]).astype(BF16)
    ckv = ckv_ref[...]
    xkv = (ckv * lax.rsqrt(jnp.mean(ckv * ckv, axis=-1, keepdims=True) + EPS) * gkv_ref[...]).astype(BF16)
    qm = _dot(xq, wqm_ref[...])
    qs = _dot(xq, wqs_ref[...])
    cosq = cosq_ref[...] * MLA_Q_SCALE
    sinq = sinq_ref[...] * MLA_Q_SCALE
    t = kr_ref[...] * csk_ref[...]
    t_hi = t.astype(BF16)
    t_lo = (t - t_hi.astype(F32)).astype(BF16)
    kk = _dot(xkv, wk_ref[...]) + _dot(t_hi, ek_ref[...]) + _dot(t_lo, ek_ref[...])
    for h in range(B_HEADS):
        sl = slice(h * LANES, (h + 1) * LANES)
        q_ref[:, sl] = (qm[:, sl] * cosq + qs[:, sl] * sinq).astype(BF16)
    k_ref[...] = kk.astype(BF16)
    v_ref[...] = (_dot(xkv, wv_ref[...]) + vone_ref[...]).astype(BF16)


MLA_QSUB = 8


def _mla_attn_kernel(q_ref, k_ref, v_ref, o_ref):
    tq = q_ref.shape[0] // MLA_QSUB
    lane = lax.broadcasted_iota(jnp.int32, (tq, LANES), 1)
    for j in range(MLA_QSUB):
        rows = slice(j * tq, (j + 1) * tq)
        outs = []
        for hh in range(2):
            sl = slice(hh * LANES, (hh + 1) * LANES)
            s = _dot_nt(q_ref[rows, sl], k_ref[:, sl])
            p = jnp.exp2(s - jnp.max(s, axis=-1, keepdims=True))
            outs.append(_dot(p.astype(BF16), v_ref[:, sl]))
        acc = jnp.where(lane < B_V, outs[0], outs[1])
        den = pltpu.roll(jnp.where(lane < B_V, outs[1], outs[0]), B_V, axis=1)
        o_ref[rows, :] = (acc / den).astype(o_ref.dtype)


def _mla_tables(S):
    inv_freq = ROPE_THETA ** (-jnp.arange(0, B_ROPE, 2, dtype=F32) / B_ROPE)
    ang = jnp.arange(S, dtype=F32)[:, None] * inv_freq[None]
    cos, sin = jnp.cos(ang), jnp.sin(ang)
    cos2 = jnp.concatenate([cos, cos], axis=-1)
    sin2 = jnp.concatenate([sin, sin], axis=-1)
    ones = jnp.ones((S, B_NOPE), F32)
    zn = jnp.zeros((S, B_NOPE), F32)
    zp = jnp.zeros((S, LANES - B_NOPE - B_ROPE), F32)
    cosq = jnp.concatenate([ones, cos2, zp], axis=-1)
    sinq = jnp.concatenate([zn, sin2, zp], axis=-1)
    csk = jnp.concatenate([cos2, sin2, jnp.zeros((S, LANES - 2 * B_ROPE), F32)], axis=-1)
    return cosq, sinq, csk


def _swap_cols(w):
    half = w.shape[-1] // 2
    return jnp.concatenate([-w[..., half:], w[..., :half]], axis=-1)


def _mla_weights(w_uq, w_ukv):
    dq = B_NOPE + B_ROPE
    wq = w_uq.reshape(B_Q_LORA, B_HEADS, dq)
    zpad = jnp.zeros((B_Q_LORA, B_HEADS, LANES - dq), F32)
    wqm = jnp.concatenate([wq, zpad], axis=-1).reshape(B_Q_LORA, B_HEADS * LANES)
    wqs = jnp.concatenate([jnp.zeros((B_Q_LORA, B_HEADS, B_NOPE), F32), _swap_cols(wq[..., B_NOPE:]), zpad],
                          axis=-1).reshape(B_Q_LORA, B_HEADS * LANES)
    wkv = w_ukv.reshape(B_KV_LORA, B_HEADS, B_NOPE + B_V)
    wk = jnp.concatenate([wkv[..., :B_NOPE], jnp.zeros((B_KV_LORA, B_HEADS, LANES - B_NOPE), F32)],
                         axis=-1).reshape(B_KV_LORA, B_HEADS * LANES)
    zv = jnp.zeros((B_KV_LORA, B_HEADS // 2, LANES - B_V), F32)
    wv_h = wkv[..., B_NOPE:]
    wv = jnp.stack([jnp.concatenate([wv_h[:, 0::2], zv], axis=-1),
                    jnp.concatenate([zv, wv_h[:, 1::2]], axis=-1)], axis=2).reshape(B_KV_LORA, B_HEADS * LANES)
    lane_in_pair = np.arange(B_HEADS * LANES) % (2 * LANES)
    vone = jnp.asarray(((lane_in_pair >= B_V) & (lane_in_pair < LANES + B_V)).astype(np.float32)).reshape(1, -1)
    ek = np.zeros((LANES, B_HEADS, LANES), np.float32)
    for j in range(B_ROPE):
        ek[j, :, B_NOPE + j] = 1.0
        ek[B_ROPE + j, :, B_NOPE + j] = 1.0
    ek = jnp.asarray(ek.reshape(LANES, B_HEADS * LANES))
    return wqm.astype(BF16), wqs.astype(BF16), wk.astype(BF16), wv.astype(BF16), vone, ek.astype(BF16)


def _mixer_b(rest, g_cq, w_uq, g_ckv, w_ukv, B, S, tm=512, tq=256 * MLA_QSUB):
    T = B * S
    wqm, wqs, wk, wv, vone, ek = _mla_weights(w_uq, w_ukv)
    cosq, sinq, csk = _mla_tables(S)
    nst = S // tm
    QW = B_HEADS * LANES
    const = lambda i: (0, 0)
    pos = lambda i: (i % nst, 0)
    q, k, v = pl.pallas_call(
        _mla_proj_kernel,
        grid=(T // tm,),
        in_specs=[pl.BlockSpec((tm, B_Q_LORA), lambda i: (i, R_CQ // B_Q_LORA)),
                  pl.BlockSpec((tm, B_KV_LORA), lambda i: (i, R_CKV // B_KV_LORA)),
                  pl.BlockSpec((tm, LANES), lambda i: (i, R_KR // LANES)),
                  pl.BlockSpec((1, B_Q_LORA), const),
                  pl.BlockSpec((1, B_KV_LORA), const),
                  pl.BlockSpec((B_Q_LORA, QW), const),
                  pl.BlockSpec((B_Q_LORA, QW), const),
                  pl.BlockSpec((B_KV_LORA, QW), const),
                  pl.BlockSpec((B_KV_LORA, QW), const),
                  pl.BlockSpec((1, QW), const),
                  pl.BlockSpec((LANES, QW), const),
                  pl.BlockSpec((tm, LANES), pos),
                  pl.BlockSpec((tm, LANES), pos),
                  pl.BlockSpec((tm, LANES), pos)],
        out_specs=[pl.BlockSpec((tm, QW), lambda i: (i, 0)),
                   pl.BlockSpec((tm, QW), lambda i: (i, 0)),
                   pl.BlockSpec((tm, QW), lambda i: (i, 0))],
        out_shape=[jax.ShapeDtypeStruct((T, QW), BF16), jax.ShapeDtypeStruct((T, QW), BF16),
                   jax.ShapeDtypeStruct((T, QW), BF16)],
        compiler_params=_cp("parallel"),
        name="mla_proj",
    )(rest, rest, rest, g_cq.reshape(1, -1), g_ckv.reshape(1, -1), wqm, wqs, wk, wv, vone, ek, cosq, sinq, csk)
    y = pl.pallas_call(
        _mla_attn_kernel,
        grid=(B, B_HEADS // 2, S // tq),
        in_specs=[pl.BlockSpec((None, tq, 2 * LANES), lambda b, hp, i: (b, i, hp)),
                  pl.BlockSpec((None, S, 2 * LANES), lambda b, hp, i: (b, 0, hp)),
                  pl.BlockSpec((None, S, 2 * LANES), lambda b, hp, i: (b, 0, hp))],
        out_specs=pl.BlockSpec((None, tq, 2 * B_V), lambda b, hp, i: (b, i, hp)),
        out_shape=jax.ShapeDtypeStruct((B, S, B_HEADS * B_V), BF16),
        compiler_params=_cp("parallel", "parallel", "arbitrary"),
        name="mla_attn",
    )(q.reshape(B, S, QW), k.reshape(B, S, QW), v.reshape(B, S, QW))
    return y.reshape(T, B_HEADS * B_V)


C_PAD = 16
C_ROWS = 128


SUBLANES = 8


def _tap_span(first, ntaps, rows):
    return rows + ((first + ntaps - 1) // SUBLANES) * SUBLANES


def _depthwise_taps(win_ref, sh_ref, w_ref, bias, ls, first, ntaps, rows):
    acc = jnp.broadcast_to(bias, (rows, LANES))
    span = _tap_span(first, ntaps, rows)
    for ph in range(SUBLANES):
        taps = [j for j in range(ntaps) if (first + j) % SUBLANES == ph]
        if not taps:
            continue
        if len(taps) == 1:
            j = taps[0]
            acc = acc + w_ref[j:j + 1, ls] * win_ref[first + j:first + j + rows, ls]
            continue
        sh_ref[0:span, :] = win_ref[ph:ph + span, ls]
        for j in taps:
            a = (first + j) // SUBLANES * SUBLANES
            acc = acc + w_ref[j:j + 1, ls] * sh_ref[a:a + rows, :]
    return acc


def _fill_window(win_ref, load_rows, r0, i, nblk, rows, pad):
    width = win_ref.shape[1]
    win_ref[pad:pad + rows, :] = load_rows(r0, rows)

    @pl.when(i > 0)
    def _():
        win_ref[0:pad, :] = load_rows(r0 - pad, pad)

    @pl.when(i == 0)
    def _():
        win_ref[0:pad, :] = jnp.zeros((pad, width), F32)

    @pl.when(i < nblk - 1)
    def _():
        win_ref[pad + rows:pad + rows + pad, :] = load_rows(r0 + rows, pad)

    @pl.when(i == nblk - 1)
    def _():
        win_ref[pad + rows:pad + rows + pad, :] = jnp.zeros((pad, width), F32)


def _conv_c_kernel(glu_ref, w_ref, b_ref, g_ref, beta_ref, o_ref, win_ref, sh_ref, acc_ref, *, S):
    i = pl.program_id(1)
    r0 = pl.multiple_of(i * C_ROWS, C_ROWS)

    def glu_rows(start, n):
        rs = pl.ds(pl.multiple_of(start, C_PAD), n)
        return glu_ref[rs, 0:C_CH] * _sigmoid(glu_ref[rs, C_CH:2 * C_CH])

    _fill_window(win_ref, glu_rows, r0, i, S // C_ROWS, C_ROWS, C_PAD)
    first = C_PAD - C_KERNEL // 2
    for lb in range(C_CH // LANES):
        ls = slice(lb * LANES, (lb + 1) * LANES)
        acc_ref[:, ls] = _depthwise_taps(win_ref, sh_ref, w_ref, b_ref[:, ls], ls, first, C_KERNEL, C_ROWS)
    y = _layernorm_rows(acc_ref[...], g_ref[...], beta_ref[...])
    o_ref[...] = _silu(y).astype(o_ref.dtype)


def _mixer_c(rest, w_dw, b_dw, ln_g, ln_b, B, S):
    T = B * S
    const = lambda b, i: (0, 0)
    y = pl.pallas_call(
        functools.partial(_conv_c_kernel, S=S),
        grid=(B, S // C_ROWS),
        in_specs=[pl.BlockSpec((None, S, 2 * C_CH), lambda b, i: (b, 0, R_GLU // (2 * C_CH))),
                  pl.BlockSpec((C_KERNEL, C_CH), const),
                  pl.BlockSpec((1, C_CH), const),
                  pl.BlockSpec((1, C_CH), const),
                  pl.BlockSpec((1, C_CH), const)],
        out_specs=pl.BlockSpec((None, C_ROWS, C_CH), lambda b, i: (b, i, 0)),
        out_shape=jax.ShapeDtypeStruct((B, S, C_CH), BF16),
        scratch_shapes=[pltpu.VMEM((C_ROWS + 2 * C_PAD, C_CH), F32),
                        pltpu.VMEM((_tap_span(C_PAD - C_KERNEL // 2, C_KERNEL, C_ROWS), LANES), F32),
                        pltpu.VMEM((C_ROWS, C_CH), F32)],
        compiler_params=_cp("parallel", "parallel"),
        name="conformer_conv",
    )(rest.reshape(B, S, R_WIDTH), w_dw, b_dw.reshape(1, -1), ln_g.reshape(1, -1), ln_b.reshape(1, -1))
    return y.reshape(T, C_CH)


D_PAD = 8
XBC_W = D_INNER + 2 * D_GROUPS * D_STATE
N_PAIR = D_HEADS // 2


def _pair_expand(v, first):
    lane = lax.broadcasted_iota(jnp.int32, (v.shape[0], LANES), 1)
    lo = jnp.broadcast_to(v[:, first:first + 1], (v.shape[0], LANES))
    hi = jnp.broadcast_to(v[:, first + 1:first + 2], (v.shape[0], LANES))
    return jnp.where(lane < D_HEAD_DIM, lo, hi)


def _ssd_kernel(xbc_ref, z_ref, dt_ref, wc_ref, bc_ref, alog_ref, dtb_ref, dskip_ref, gn_ref, o_ref,
                win_ref, sh_ref, xc_ref, a_ref, dtv_ref, y_ref, st_ref, *, S):
    Q = D_CHUNK
    nchunk = S // Q
    N = D_STATE
    bm0 = D_INNER
    cm0 = D_INNER + D_GROUPS * N

    def conv_body(c, carry):
        r0 = pl.multiple_of(c * Q, Q)
        _fill_window(win_ref, lambda st, n: xbc_ref[pl.ds(pl.multiple_of(st, D_PAD), n), :], r0, c, nchunk, Q, D_PAD)
        for lb in range(XBC_W // LANES):
            ls = slice(lb * LANES, (lb + 1) * LANES)
            acc = _depthwise_taps(win_ref, sh_ref, wc_ref, bc_ref[:, ls], ls, D_PAD - D_CONV // 2, D_CONV, Q)
            xc_ref[pl.ds(r0, Q), ls] = _silu(acc)
        return carry

    lax.fori_loop(0, nchunk, conv_body, 0)

    lane1 = lax.broadcasted_iota(jnp.int32, (1, LANES), 1)
    a_row = jnp.where(lane1 < 2 * D_HEADS, -jnp.exp(alog_ref[...]), 0.0)
    xdt = dt_ref[...] + dtb_ref[...]
    dtv = jnp.maximum(xdt, 0.0) + jnp.log(1.0 + jnp.exp(-jnp.abs(xdt)))
    dtv_ref[...] = dtv
    a_ref[...] = dtv * a_row

    row = lax.broadcasted_iota(jnp.int32, (Q, Q), 0)
    col = lax.broadcasted_iota(jnp.int32, (Q, Q), 1)
    tril = row >= col
    triu = col >= row
    lane = col

    def scan_chunk(c, lower, off, finalize):
        r0 = pl.multiple_of(c * Q, Q)
        rows = pl.ds(r0, Q)
        mask = tril if lower else triu
        tri = mask.astype(BF16)
        a_hi, a_mid, a_lo = _split3(a_ref[rows, :])
        cs = _dot(tri, a_hi) + _dot(tri, a_mid) + _dot(tri, a_lo)
        cs_t = cs.T
        ecs = jnp.exp(cs)
        edge = Q - 1 if lower else 0
        edec = jnp.exp(cs[edge:edge + 1, :] - cs)
        dt_c = dtv_ref[rows, :]
        dt_t = dt_c.T
        dte_t = (dt_c * edec).T
        for g in range(D_GROUPS):
            bg = xc_ref[rows, bm0 + g * N:bm0 + (g + 1) * N]
            cg = xc_ref[rows, cm0 + g * N:cm0 + (g + 1) * N].astype(BF16)
            cb = _dot_nt(cg, bg.astype(BF16))
            bg_t = bg.T
            for pp in range(N_PAIR // D_GROUPS):
                p = g * (N_PAIR // D_GROUPS) + pp
                ps = slice(p * LANES, (p + 1) * LANES)
                x_p = xc_ref[rows, ps]
                ms, bs = [], []
                for hh in range(2):
                    k = off + 2 * p + hh
                    diff = jnp.broadcast_to(cs[:, k:k + 1], (Q, Q)) - cs_t[k:k + 1, :]
                    ms.append((jnp.exp(jnp.where(mask, diff, NEG_INF)) * (cb * dt_t[k:k + 1, :])).astype(BF16))
                    bs.append((bg_t * dte_t[k:k + 1, :]).astype(BF16))
                x_lo = jnp.where(lane < D_HEAD_DIM, x_p, 0.0).astype(BF16)
                x_hi = jnp.where(lane >= D_HEAD_DIM, x_p, 0.0).astype(BF16)
                x2 = jnp.concatenate([x_lo, x_hi], axis=0)
                y_intra = _dot(jnp.concatenate(ms, axis=1), x2)
                hp = st_ref[p]
                ecs_p = _pair_expand(ecs, off + 2 * p)
                y_new = y_intra + _dot(cg, hp.astype(BF16)) * ecs_p
                if lower:
                    y_ref[rows, ps] = y_new
                else:
                    y_ref[rows, ps] = y_ref[rows, ps] + y_new
                st_ref[p] = hp * ecs_p[edge:edge + 1, :] + _dot(jnp.concatenate(bs, axis=1), x2)
        if finalize:
            y = y_ref[rows, :] + xc_ref[rows, 0:D_INNER] * dskip_ref[...]
            gated = y * _silu(z_ref[rows, :])
            out = gated * lax.rsqrt(jnp.mean(gated * gated, axis=-1, keepdims=True) + EPS) * gn_ref[...]
            o_ref[rows, :] = out.astype(o_ref.dtype)

    st_ref[...] = jnp.zeros(st_ref.shape, F32)

    def fwd_body(c, carry):
        scan_chunk(c, True, 0, False)
        return carry

    lax.fori_loop(0, nchunk, fwd_body, 0)
    st_ref[...] = jnp.zeros(st_ref.shape, F32)

    def bwd_body(k, carry):
        scan_chunk(nchunk - 1 - k, False, D_HEADS, True)
        return carry

    lax.fori_loop(0, nchunk, bwd_body, 0)


def _mixer_d(rest, w_conv, b_conv, a_log_f, a_log_b, dt_bias_f, dt_bias_b, d_skip, g_norm, B, S):
    T = B * S
    pad16 = lambda f, b: jnp.concatenate([f, b, jnp.zeros((LANES - 2 * D_HEADS,), F32)]).reshape(1, LANES)
    const = lambda b: (0, 0)
    y = pl.pallas_call(
        functools.partial(_ssd_kernel, S=S),
        grid=(B,),
        in_specs=[pl.BlockSpec((None, S, XBC_W), lambda b: (b, 0, R_XBC // XBC_W)),
                  pl.BlockSpec((None, S, D_INNER), lambda b: (b, 0, R_Z // D_INNER)),
                  pl.BlockSpec((None, S, LANES), lambda b: (b, 0, R_DT // LANES)),
                  pl.BlockSpec((D_CONV, XBC_W), const),
                  pl.BlockSpec((1, XBC_W), const),
                  pl.BlockSpec((1, LANES), const),
                  pl.BlockSpec((1, LANES), const),
                  pl.BlockSpec((1, D_INNER), const),
                  pl.BlockSpec((1, D_INNER), const)],
        out_specs=pl.BlockSpec((None, S, D_INNER), lambda b: (b, 0, 0)),
        out_shape=jax.ShapeDtypeStruct((B, S, D_INNER), BF16),
        scratch_shapes=[pltpu.VMEM((D_CHUNK + 2 * D_PAD, XBC_W), F32),
                        pltpu.VMEM((_tap_span(D_PAD - D_CONV // 2, D_CONV, D_CHUNK), LANES), F32),
                        pltpu.VMEM((S, XBC_W), F32),
                        pltpu.VMEM((S, LANES), F32),
                        pltpu.VMEM((S, LANES), F32),
                        pltpu.VMEM((S, D_INNER), F32),
                        pltpu.VMEM((N_PAIR, D_STATE, LANES), F32)],
        compiler_params=_cp("parallel"),
        name="ssd_mixer",
    )(rest.reshape(B, S, R_WIDTH), rest.reshape(B, S, R_WIDTH), rest.reshape(B, S, R_WIDTH),
      w_conv, b_conv.reshape(1, -1), pad16(a_log_f, a_log_b), pad16(dt_bias_f, dt_bias_b),
      jnp.repeat(d_skip, D_HEAD_DIM).reshape(1, -1), g_norm.reshape(1, -1))
    return y.reshape(T, D_INNER)


def _in_proj_weights(w_in_l):
    o = np.cumsum((0, A_WIDTH, A_WIDTH, A_WIDTH, B_Q_LORA, B_KV_LORA, B_ROPE, 2 * C_CH,
                   D_INNER, D_INNER, D_GROUPS * D_STATE, D_GROUPS * D_STATE, 2 * D_HEADS)).tolist()
    seg = lambda n: w_in_l[:, o[n]:o[n + 1]]
    w_a = w_in_l[:, :o[3]]
    cq, ckv, kr, glu, z, xs, bm, cm, dt = (seg(n) for n in range(3, 12))
    zeros = lambda n: jnp.zeros((w_in_l.shape[0], n), w_in_l.dtype)
    w_r = jnp.concatenate([glu, xs, bm, cm, cq, z, ckv,
                           kr, _swap_cols(kr), zeros(LANES - 2 * B_ROPE),
                           dt, zeros(LANES - 2 * D_HEADS)], axis=-1)
    assert w_r.shape[1] == R_WIDTH
    return w_a, w_r


def _merge_gate_kernel(h_ref, ya_ref, yb_ref, yc_ref, yd_ref, wg_ref, bg_ref, wbr_ref, o_ref):
    h = h_ref[...]
    acc = None
    for i, y_ref in enumerate((ya_ref, yb_ref, yc_ref, yd_ref)):
        gate = _sigmoid(_dot(h, wg_ref[i]) + bg_ref[i])
        term = gate * _dot(y_ref[...], wbr_ref[i])
        acc = term if acc is None else acc + term
    o_ref[...] = acc.astype(o_ref.dtype)


def _out_ln_kernel(m_ref, w_ref, h_ref, g_ref, b_ref, of_ref, op_ref):
    y = _layernorm_rows(ALPHA * h_ref[...] + _dot(m_ref[...], w_ref[...]), g_ref[...], b_ref[...])
    of_ref[...] = y
    for c in range(ROW_SUB):
        op_ref[_col_block(y.shape[0], c), :] = _pack_cols(y, c)


def _merge(hb, hf, branches, w_gate, b_gate, w_br, w_out, ln_g, ln_b, tm=512, tn=512, tm2=512):
    T, D = hb.shape
    ybs = pl.BlockSpec((tm, BRANCH_W), lambda j, i: (i, 0))
    merged = pl.pallas_call(
        _merge_gate_kernel,
        grid=(D // tn, T // tm),
        in_specs=[pl.BlockSpec((tm, D), lambda j, i: (i, 0)), ybs, ybs, ybs, ybs,
                  pl.BlockSpec((N_BRANCH, D, tn), lambda j, i: (0, 0, j)),
                  pl.BlockSpec((N_BRANCH, 1, tn), lambda j, i: (0, 0, j)),
                  pl.BlockSpec((N_BRANCH, BRANCH_W, tn), lambda j, i: (0, 0, j))],
        out_specs=pl.BlockSpec((tm, tn), lambda j, i: (i, j)),
        out_shape=jax.ShapeDtypeStruct((T, D), BF16),
        compiler_params=_cp("parallel", "parallel"),
        name="merge_gate",
    )(hb, *branches, w_gate.astype(BF16), b_gate.reshape(N_BRANCH, 1, D), w_br.astype(BF16))
    const = lambda i: (0, 0)
    rows = lambda i: (i, 0)
    return pl.pallas_call(
        _out_ln_kernel,
        grid=(T // tm2,),
        in_specs=[pl.BlockSpec((tm2, D), rows), pl.BlockSpec((D, D), const), pl.BlockSpec((tm2, D), rows),
                  pl.BlockSpec((1, D), const), pl.BlockSpec((1, D), const)],
        out_specs=[pl.BlockSpec((tm2, D), rows), pl.BlockSpec((tm2 * ROW_SUB, LANES), rows)],
        out_shape=[jax.ShapeDtypeStruct((T, D), F32), jax.ShapeDtypeStruct((T * ROW_SUB, LANES), ROW_DT)],
        compiler_params=_cp("parallel"),
        name="out_proj_ln1",
    )(merged, w_out.astype(BF16), hf, ln_g.reshape(1, D), ln_b.reshape(1, D))


R_TM = 512
COMBINE_TB = 256


def _router_kernel(h_ref, whi_ref, wlo_ref, b_ref, eid_ref, wts_ref, rank_ref, cnt_ref, carry_ref):
    i = pl.program_id(0)

    @pl.when(i == 0)
    def _():
        carry_ref[...] = jnp.zeros(carry_ref.shape, F32)

    x = h_ref[...]
    tm = x.shape[0]
    xh = x.astype(BF16)
    xl = (x - xh.astype(F32)).astype(BF16)
    whi = whi_ref[...]
    logits = _dot(xh, whi) + _dot(xh, wlo_ref[...]) + _dot(xl, whi) + b_ref[...]
    lane = lax.broadcasted_iota(jnp.int32, (tm, LANES), 1)
    big = jnp.int32(4 * LANES)
    is_g = (lane >= N_EXPERTS) & (lane < N_EXPERTS + N_GROUPS)
    lg = jnp.where(is_g, logits, NEG_INF)
    gmax = jnp.max(lg, axis=-1, keepdims=True)
    gidx = jnp.min(jnp.where(lg == gmax, lane - N_EXPERTS, big), axis=-1, keepdims=True)
    g_w = 1.0 / jnp.sum(jnp.where(is_g, jnp.exp(lg - gmax), 0.0), axis=-1, keepdims=True)
    in_grp = (lane < N_EXPERTS) & ((lane // EXP_PER_GROUP) == gidx)
    le = jnp.where(in_grp, logits, NEG_INF)
    e1 = jnp.max(le, axis=-1, keepdims=True)
    i1 = jnp.min(jnp.where(le == e1, lane, big), axis=-1, keepdims=True)
    le2 = jnp.where(lane == i1, NEG_INF, le)
    e2 = jnp.max(le2, axis=-1, keepdims=True)
    i2 = jnp.min(jnp.where(le2 == e2, lane, big), axis=-1, keepdims=True)
    zsum = jnp.sum(jnp.where(in_grp, jnp.exp(le - e1), 0.0), axis=-1, keepdims=True)
    p1 = 1.0 / zsum
    p2 = jnp.exp(e2 - e1) / zsum
    w1 = g_w * p1 / (p1 + p2)
    w2 = g_w * p2 / (p1 + p2)
    oh1 = lane == i1
    oh2 = lane == i2
    ohs = (oh1 | oh2).astype(BF16)
    row = lax.broadcasted_iota(jnp.int32, (tm, tm), 0)
    col = lax.broadcasted_iota(jnp.int32, (tm, tm), 1)
    before = _dot((row > col).astype(BF16), ohs) + carry_ref[0:1, :]
    r1 = jnp.sum(jnp.where(oh1, before, 0.0), axis=-1, keepdims=True)
    r2 = jnp.sum(jnp.where(oh2, before, 0.0), axis=-1, keepdims=True)
    total = carry_ref[0:1, :] + jnp.sum(ohs.astype(F32), axis=0, keepdims=True)
    carry_ref[...] = jnp.broadcast_to(total, carry_ref.shape)
    cnt_ref[...] = jnp.broadcast_to(total, cnt_ref.shape).astype(jnp.int32)
    eid_ref[...] = jnp.where(lane == 0, i1, jnp.where(lane == 1, i2, 0))
    wts_ref[...] = jnp.where(lane == 0, w1, jnp.where(lane == 1, w2, 0.0))
    rank_ref[...] = jnp.where(lane == 0, r1, jnp.where(lane == 1, r2, 0.0)).astype(jnp.int32)


ROW_SUB = D_MODEL // LANES // 2
ROW_DT = jnp.uint32


def _row_slab(ref, r):
    return ref.at[pl.ds(pl.multiple_of(r * ROW_SUB, ROW_SUB), ROW_SUB)]


def _col_block(n, c):
    return pl.ds(c, n, stride=ROW_SUB)


def _pack_cols(x, c):
    as_bits = lambda t: lax.bitcast_convert_type(t.astype(BF16).astype(F32), ROW_DT)
    lo = as_bits(x[:, c * LANES:(c + 1) * LANES])
    hi = as_bits(x[:, (c + ROW_SUB) * LANES:(c + ROW_SUB + 1) * LANES])
    return (lo >> 16) | hi


def _unpack_cols(w):
    return (lax.bitcast_convert_type(w << 16, F32),
            lax.bitcast_convert_type(w & jnp.uint32(0xFFFF0000), F32))


EXPERT_AHEAD = 3


def _expert_kernel(be_ref, succ_ref, *refs, layer):
    src_refs = refs[:EXPERT_AHEAD + 1]
    (h_hbm, wg_hbm, wu_hbm, wd_hbm, o_ref, xbuf_ref, sg_ref, su_ref, sd_ref, wgb_ref, wub_ref, wdb_ref,
     stage_ref, gsem, wsem) = refs[EXPERT_AHEAD + 1:]
    i = pl.program_id(0)
    n_used = be_ref[pl.num_programs(0)]
    nbuf = EXPERT_AHEAD + 1
    slot = i % nbuf
    expert = be_ref[i]
    weights = ((wg_hbm, sg_ref, wgb_ref), (wu_hbm, su_ref, wub_ref), (wd_hbm, sd_ref, wdb_ref))

    def weight_copy(w, e, s):
        hbm, st_ref, _ = weights[w]
        return pltpu.make_async_copy(hbm.at[layer, e], st_ref.at[s], wsem.at[s, w])

    @pl.when(i == 0)
    def _():
        stage_ref[0] = 0
        for w in range(len(weights)):
            weight_copy(w, expert, 0).start()

    @pl.when((i < n_used) & ((i == 0) | (expert != be_ref[jnp.maximum(i - 1, 0)])))
    def _():
        s = stage_ref[0]
        nxt = succ_ref[expert]
        for w, (_, st_ref, wb_ref) in enumerate(weights):
            weight_copy(w, 0, s).wait()
            wb_ref[...] = st_ref[s].astype(BF16)

        @pl.when(nxt >= 0)
        def _():
            for w in range(len(weights)):
                weight_copy(w, nxt, 1 - s).start(priority=1)

        stage_ref[0] = 1 - s

    def gather_copy(src_row, to_slot, t):
        src = h_hbm.at[pl.ds(pl.multiple_of(src_row, ROW_SUB), ROW_SUB)]
        return pltpu.make_async_copy(src, _row_slab(xbuf_ref.at[to_slot], t), gsem.at[to_slot])

    def gather(idx_ref, to_slot):
        for t in range(MOE_BLOCK):
            gather_copy(idx_ref[t], to_slot, t).start()

    @pl.when(i == 0)
    def _():
        for a in range(EXPERT_AHEAD):
            gather(src_refs[a], a)

    @pl.when(i + EXPERT_AHEAD < n_used)
    def _():
        gather(src_refs[EXPERT_AHEAD], (i + EXPERT_AHEAD) % nbuf)

    @pl.when(i < n_used)
    def _():
        for t in range(MOE_BLOCK):
            gather_copy(0, slot, 0).wait()
        x_ref = xbuf_ref.at[slot]
        halves = [_unpack_cols(x_ref[_col_block(MOE_BLOCK, c), :]) for c in range(ROW_SUB)]
        x = jnp.concatenate([lo for lo, _ in halves] + [hi for _, hi in halves], axis=1).astype(BF16)
        hid = (_silu(_dot(x, wgb_ref[...])) * _dot(x, wub_ref[...])).astype(BF16)
        y = _dot(hid, wdb_ref[...])
        for c in range(ROW_SUB):
            o_ref[_col_block(MOE_BLOCK, c), :] = _pack_cols(y, c)

    @pl.when(i >= n_used)
    def _():
        o_ref[...] = jnp.zeros(o_ref.shape, o_ref.dtype)


def _combine_kernel(dest_ref, dnext_ref, h_ref, w_ref, g_ref, b_ref, yrows_hbm, of_ref, ob_ref,
                    ybuf_ref, acc_ref, sem):
    i = pl.program_id(0)
    n = pl.num_programs(0)
    slot = i % 2

    def row_copy(d, to_slot, k, t):
        return pltpu.make_async_copy(_row_slab(yrows_hbm, d), _row_slab(ybuf_ref.at[to_slot, k], t), sem.at[to_slot])

    def gather(d_ref, to_slot):
        def start(t, carry):
            for k in range(TOP_K):
                row_copy(d_ref[TOP_K * t + k], to_slot, k, t).start(priority=k)
            return carry

        lax.fori_loop(0, COMBINE_TB, start, 0, unroll=8)

    @pl.when(i == 0)
    def _():
        gather(dest_ref, 0)

    @pl.when(i + 1 < n)
    def _():
        gather(dnext_ref, 1 - slot)

    for t in range(COMBINE_TB):
        for k in range(TOP_K):
            row_copy(0, slot, k, 0).wait()
    w = w_ref[...]
    y0_ref = ybuf_ref.at[slot, 0]
    y1_ref = ybuf_ref.at[slot, 1]
    for c in range(ROW_SUB):
        cb = _col_block(COMBINE_TB, c)
        lo0, hi0 = _unpack_cols(y0_ref[cb, :])
        lo1, hi1 = _unpack_cols(y1_ref[cb, :])
        acc_ref[:, c * LANES:(c + 1) * LANES] = lo0 * w[:, 0:1] + lo1 * w[:, 1:2]
        acc_ref[:, (c + ROW_SUB) * LANES:(c + ROW_SUB + 1) * LANES] = hi0 * w[:, 0:1] + hi1 * w[:, 1:2]
    y = _layernorm_rows(ALPHA * h_ref[...] + acc_ref[...], g_ref[...], b_ref[...])
    of_ref[...] = y
    ob_ref[...] = y.astype(BF16)


def _moe_layer(hf, hp, w_rg, b_rg, w_re, b_re, w_e_gate, w_e_up, w_e_down, layer, ln_g, ln_b):
    T, D = hf.shape
    n_rows = T * TOP_K + N_EXPERTS * MOE_BLOCK
    n_blocks = n_rows // MOE_BLOCK
    w_r = jnp.concatenate([w_re, w_rg, jnp.zeros((D, LANES - N_EXPERTS - N_GROUPS), F32)], axis=-1)
    b_r = jnp.concatenate([b_re, b_rg, jnp.zeros((LANES - N_EXPERTS - N_GROUPS,), F32)]).reshape(1, LANES)
    w_hi = w_r.astype(BF16)
    w_lo = (w_r - w_hi.astype(F32)).astype(BF16)
    const = lambda i: (0, 0)
    rows = lambda i: (i, 0)
    eid, wts, rank, cnt = pl.pallas_call(
        _router_kernel,
        grid=(T // R_TM,),
        in_specs=[pl.BlockSpec((R_TM, D), rows), pl.BlockSpec((D, LANES), const),
                  pl.BlockSpec((D, LANES), const), pl.BlockSpec((1, LANES), const)],
        out_specs=[pl.BlockSpec((R_TM, LANES), rows), pl.BlockSpec((R_TM, LANES), rows),
                   pl.BlockSpec((R_TM, LANES), rows), pl.BlockSpec((8, LANES), const)],
        out_shape=[jax.ShapeDtypeStruct((T, LANES), jnp.int32), jax.ShapeDtypeStruct((T, LANES), F32),
                   jax.ShapeDtypeStruct((T, LANES), jnp.int32), jax.ShapeDtypeStruct((8, LANES), jnp.int32)],
        scratch_shapes=[pltpu.VMEM((8, LANES), F32)],
        compiler_params=_cp("arbitrary"),
        name="moe_router",
    )(hf, w_hi, w_lo, b_r)
    counts = cnt[0, :N_EXPERTS]
    padded = (counts + MOE_BLOCK - 1) // MOE_BLOCK * MOE_BLOCK
    pends = jnp.cumsum(padded)
    pstarts = pends - padded
    blk_start = jnp.arange(n_blocks, dtype=jnp.int32) * MOE_BLOCK
    blk_exp = jnp.minimum(jnp.sum((pends[None, :] <= blk_start[:, None]).astype(jnp.int32), axis=1), N_EXPERTS - 1)
    sel = eid[:, :TOP_K, None] == jnp.arange(N_EXPERTS, dtype=jnp.int32)
    dest = (jnp.sum(jnp.where(sel, pstarts, 0), axis=-1) + rank[:, :TOP_K]).astype(jnp.int32).reshape(T * TOP_K)
    flat = jnp.full((n_rows,), -1, jnp.int32).at[dest].set(jnp.arange(T * TOP_K, dtype=jnp.int32),
                                                           unique_indices=True)
    row_tok = jnp.where(flat < 0, 0, flat // TOP_K) * ROW_SUB
    assert n_blocks > EXPERT_AHEAD
    any_spec = pl.BlockSpec(memory_space=pl.ANY)
    idx_spec = lambda f: pl.BlockSpec((MOE_BLOCK,), f, memory_space=pltpu.SMEM)
    ids = jnp.arange(N_EXPERTS, dtype=jnp.int32)
    later = (ids[None, :] > ids[:, None]) & (counts[None, :] > 0)
    succ = jnp.min(jnp.where(later, ids[None, :], N_EXPERTS), axis=1)
    succ = jnp.where(succ == N_EXPERTS, -1, succ).astype(jnp.int32)
    yrows = pl.pallas_call(
        functools.partial(_expert_kernel, layer=layer),
        grid_spec=pltpu.PrefetchScalarGridSpec(
            num_scalar_prefetch=2,
            grid=(n_blocks,),
            in_specs=[idx_spec(lambda i, *_, a=a: (jnp.minimum(i + a, n_blocks - 1),)) for a in range(EXPERT_AHEAD + 1)]
            + [any_spec] * 4,
            out_specs=pl.BlockSpec((MOE_BLOCK * ROW_SUB, LANES), lambda i, *_: (i, 0)),
            scratch_shapes=[pltpu.VMEM((EXPERT_AHEAD + 1, MOE_BLOCK * ROW_SUB, LANES), ROW_DT),
                            pltpu.VMEM((2, D, D_FF), F32), pltpu.VMEM((2, D, D_FF), F32), pltpu.VMEM((2, D_FF, D), F32),
                            pltpu.VMEM((D, D_FF), BF16), pltpu.VMEM((D, D_FF), BF16), pltpu.VMEM((D_FF, D), BF16),
                            pltpu.SMEM((1,), jnp.int32),
                            pltpu.SemaphoreType.DMA((EXPERT_AHEAD + 1,)), pltpu.SemaphoreType.DMA((2, 3))]),
        out_shape=jax.ShapeDtypeStruct((n_rows * ROW_SUB, LANES), ROW_DT),
        compiler_params=_cp("arbitrary"),
        name="moe_experts",
    )(jnp.concatenate([blk_exp, pends[-1:] // MOE_BLOCK]).astype(jnp.int32), succ,
      *([row_tok] * (EXPERT_AHEAD + 1)), hp, w_e_gate, w_e_up, w_e_down)
    n_steps = T // COMBINE_TB
    dspec = lambda f: pl.BlockSpec((TOP_K * COMBINE_TB,), f, memory_space=pltpu.SMEM)
    return pl.pallas_call(
        _combine_kernel,
        grid=(n_steps,),
        in_specs=[dspec(lambda i: (i,)), dspec(lambda i: (jnp.minimum(i + 1, n_steps - 1),)),
                  pl.BlockSpec((COMBINE_TB, D), rows), pl.BlockSpec((COMBINE_TB, LANES), rows),
                  pl.BlockSpec((1, D), const), pl.BlockSpec((1, D), const), any_spec],
        out_specs=[pl.BlockSpec((COMBINE_TB, D), rows), pl.BlockSpec((COMBINE_TB, D), rows)],
        out_shape=[jax.ShapeDtypeStruct((T, D), F32), jax.ShapeDtypeStruct((T, D), BF16)],
        scratch_shapes=[pltpu.VMEM((2, TOP_K, COMBINE_TB * ROW_SUB, LANES), ROW_DT),
                        pltpu.VMEM((COMBINE_TB, D), F32), pltpu.SemaphoreType.DMA((2,))],
        compiler_params=_cp("arbitrary"),
        name="moe_combine_ln2",
    )(dest, dest, hf, wts, ln_g.reshape(1, D), ln_b.reshape(1, D), yrows)


def kernel(x, ln_in_g, ln_in_b, rel_bias, w_in, g_cq, w_uq, g_ckv, w_ukv, w_dw_c, b_dw_c, ln_c_g, ln_c_b,
           w_conv_d, b_conv_d, a_log_f, a_log_b, dt_bias_f, dt_bias_b, d_skip, g_norm_d, w_br, w_gate, b_gate,
           w_out, ln1_g, ln1_b, w_rg, b_rg, w_re, b_re, w_e_gate, w_e_up, w_e_down, ln2_g, ln2_b):
    B, S, D = x.shape
    T = B * S
    hf, hb = _layernorm(x.reshape(T, D), ln_in_g, ln_in_b)
    a_bias = _mixer_a_bias(rel_bias, S)
    for l in range(DEPTH):
        w_a, w_r = _in_proj_weights(w_in[l])
        qkv = _matmul(hb, w_a.astype(BF16), F32, 512, 3 * A_WIDTH, "in_proj_a")
        rest = _matmul(hb, w_r.astype(BF16), F32, 512, R_WIDTH // 2, "in_proj_rest")
        y_a = _mixer_a(qkv, a_bias, B, S)
        y_b = _mixer_b(rest, g_cq[l], w_uq[l], g_ckv[l], w_ukv[l], B, S)
        y_c = _mixer_c(rest, w_dw_c[l], b_dw_c[l], ln_c_g[l], ln_c_b[l], B, S)
        y_d = _mixer_d(rest, w_conv_d[l], b_conv_d[l], a_log_f[l], a_log_b[l], dt_bias_f[l], dt_bias_b[l],
                       d_skip[l], g_norm_d[l], B, S)
        h1f, h1p = _merge(hb, hf, (y_a, y_b, y_c, y_d), w_gate[l], b_gate[l], w_br[l], w_out[l], ln1_g[l], ln1_b[l])
        hf, hb = _moe_layer(h1f, h1p, w_rg[l], b_rg[l], w_re[l], b_re[l], w_e_gate, w_e_up, w_e_down, l,
                            ln2_g[l], ln2_b[l])
    return hf.reshape(B, S, D)
```
